```python
import jax, jax.numpy as jnp
from jax import lax
import numpy as np

D_MODEL = 1024
BATCH = 2
SEQ = 8192
DEPTH = 1

HG_HEADS = 8
HG_EXPAND = 128
HG_KEY_DIM = HG_HEADS * HG_EXPAND
HG_VAL_DIM = D_MODEL
HG_HEAD_V = HG_VAL_DIM // HG_HEADS
HG_CHUNK = 64
ATT_Q_HEADS = 16
ATT_KV_HEADS = 4
ATT_HEAD_DIM = 64
ATT_GROUP = ATT_Q_HEADS // ATT_KV_HEADS
ATT_WIDTH = ATT_Q_HEADS * ATT_HEAD_DIM
KV_WIDTH = ATT_KV_HEADS * ATT_HEAD_DIM
WINDOW = 128
ATT_BLOCK = WINDOW
EPS = 1e-6

IN_SIZES = [
    HG_KEY_DIM,
    HG_KEY_DIM,
    HG_VAL_DIM,
    HG_VAL_DIM,
    ATT_WIDTH,
    KV_WIDTH,
    KV_WIDTH,
    ATT_WIDTH,
    D_MODEL,
    D_MODEL,
]
D_IN = sum(IN_SIZES)
SPLIT_POINTS = [int(s) for s in np.cumsum(IN_SIZES)[:-1]]

kernel_name = "hybrid_hgrn2_swa_sink_gated_block"


def rms_norm(x, w):
    xf = x.astype(jnp.float32)
    xf = xf * lax.rsqrt(jnp.mean(xf * xf, axis=-1, keepdims=True) + EPS)
    return xf.astype(x.dtype) * w


def hgrn2_chunkwise(q, k, v, log_f):
    B, T, H, dk = q.shape
    dv = v.shape[-1]
    n = T // HG_CHUNK

    def to_chunks(a):
        return a.reshape(B, n, HG_CHUNK, H, a.shape[-1]).transpose(1, 0, 3, 2, 4)

    qc, kc, vc, gc = to_chunks(q), to_chunks(k), to_chunks(v), to_chunks(log_f)
    causal = jnp.tril(jnp.ones((HG_CHUNK, HG_CHUNK), dtype=bool))

    def step(S, inp):
        q_, k_, v_, g_ = inp
        b = jnp.cumsum(g_, axis=2)
        o_inter = jnp.einsum('bhtd,bhdv->bhtv', q_ * jnp.exp(b), S)
        diff = b[:, :, :, None, :] - b[:, :, None, :, :]
        decay = jnp.exp(jnp.where(causal[:, :, None], diff, -jnp.inf))
        scores = jnp.einsum('bhtd,bhsd,bhtsd->bhts', q_, k_, decay)
        o_intra = jnp.einsum('bhts,bhsv->bhtv', scores, v_)
        b_last = b[:, :, -1:, :]
        S_new = jnp.exp(b_last[:, :, 0, :])[..., None] * S + jnp.einsum(
            'bhsd,bhsv->bhdv', k_ * jnp.exp(b_last - b), v_)
        return S_new, o_inter + o_intra

    S0 = jnp.zeros((B, H, dk, dv), jnp.float32)
    _, o = lax.scan(step, S0, (qc, kc, vc, gc))
    return o.transpose(1, 0, 3, 2, 4).reshape(B, T, H, dv)


def sliding_window_attention_with_sinks(q, k, v, sinks):
    B, T = q.shape[0], q.shape[1]
    n = T // ATT_BLOCK
    qb = q.reshape(B, n, ATT_BLOCK, ATT_KV_HEADS, ATT_GROUP, ATT_HEAD_DIM)

    def banded(a):
        ab = a.reshape(B, n, ATT_BLOCK, ATT_KV_HEADS, ATT_HEAD_DIM)
        prev = jnp.pad(ab[:, :-1], ((0, 0), (1, 0), (0, 0), (0, 0), (0, 0)))
        return jnp.concatenate([prev, ab], axis=2)

    keys, vals = banded(k), banded(v)
    scores = jnp.einsum('bnqhgd,bnkhd->bnhgqk', qb, keys).astype(jnp.float32) * (ATT_HEAD_DIM ** -0.5)
    qi = jnp.arange(ATT_BLOCK)[:, None]
    kj = jnp.arange(2 * ATT_BLOCK)[None, :]
    rel = qi + ATT_BLOCK - kj
    band = (rel >= 0) & (rel < WINDOW)
    pad_keys = (jnp.arange(n) == 0)[:, None, None] & (kj < ATT_BLOCK)[None]
    valid = band[None] & ~pad_keys
    scores = jnp.where(valid[None, :, None, None], scores, -jnp.inf)
    sink = sinks.astype(jnp.float32).reshape(ATT_KV_HEADS, ATT_GROUP)[None, None, :, :, None, None]
    m = jnp.maximum(jnp.max(scores, axis=-1, keepdims=True), sink)
    p = jnp.exp(scores - m)
    probs = p / (jnp.sum(p, axis=-1, keepdims=True) + jnp.exp(sink - m))
    out = jnp.einsum('bnhgqk,bnkhd->bnqhgd', probs.astype(v.dtype), vals)
    return out.reshape(B, T, ATT_WIDTH)


def setup_inputs(seed: int = 0) -> dict:
    key = jax.random.key(seed)
    ks = jax.random.split(key, 12)
    f32 = jnp.float32
    return {
        "x": jax.random.normal(ks[0], (BATCH, SEQ, D_MODEL), f32),
        "norm_w": 1.0 + 0.02 * jax.random.normal(ks[1], (DEPTH, D_MODEL), f32),
        "w_in": jax.random.normal(ks[2], (DEPTH, D_MODEL, D_IN), f32) * D_MODEL ** -0.5,
        "hgrn_lower_bound": 0.1 * jax.random.normal(ks[3], (DEPTH + 1, HG_KEY_DIM), f32),
        "hgrn_norm_w": 1.0 + 0.02 * jax.random.normal(ks[4], (DEPTH, HG_VAL_DIM), f32),
        "w_branch_hgrn": jax.random.normal(ks[5], (DEPTH, HG_VAL_DIM, D_MODEL), f32) * HG_VAL_DIM ** -0.5,
        "attn_sinks": 0.5 * jax.random.normal(ks[6], (DEPTH, ATT_Q_HEADS), f32),
        "w_branch_attn": jax.random.normal(ks[7], (DEPTH, ATT_WIDTH, D_MODEL), f32) * ATT_WIDTH ** -0.5,
        "w_out": jax.random.normal(ks[8], (DEPTH, D_MODEL, D_MODEL), f32) * D_MODEL ** -0.5,
        "final_norm_w": 1.0 + 0.02 * jax.random.normal(ks[9], (D_MODEL,), f32),
    }


def reference(x, norm_w, w_in, hgrn_lower_bound, hgrn_norm_w, w_branch_hgrn, attn_sinks,
              w_branch_attn, w_out, final_norm_w):
    B, T, _ = x.shape
    lb_all = jnp.cumsum(jax.nn.softmax(hgrn_lower_bound.astype(jnp.float32), axis=0), axis=0)
    for l in range(DEPTH):
        xn = rms_norm(x, norm_w[l])
        proj = xn @ w_in[l]
        hq, hf, hi, hg, aq, ak, av, ag, mh, ma = jnp.split(proj, SPLIT_POINTS, axis=-1)

        lb = lb_all[l]
        f = lb + (1.0 - lb) * jax.nn.sigmoid(hf.astype(jnp.float32))
        log_f = jnp.log(f).reshape(B, T, HG_HEADS, HG_EXPAND)
        k_h = (1.0 - f).reshape(B, T, HG_HEADS, HG_EXPAND)
        q_h = jax.nn.silu(hq.astype(jnp.float32)).reshape(B, T, HG_HEADS, HG_EXPAND)
        v_h = hi.astype(jnp.float32).reshape(B, T, HG_HEADS, HG_HEAD_V)
        o_h = hgrn2_chunkwise(q_h, k_h, v_h, log_f)
        o_h = rms_norm(o_h, hgrn_norm_w[l].reshape(HG_HEADS, HG_HEAD_V)).reshape(B, T, HG_VAL_DIM)
        y_h = (o_h.astype(x.dtype) * jax.nn.silu(hg)) @ w_branch_hgrn[l]

        o_a = sliding_window_attention_with_sinks(
            aq.reshape(B, T, ATT_Q_HEADS, ATT_HEAD_DIM),
            ak.reshape(B, T, ATT_KV_HEADS, ATT_HEAD_DIM),
            av.reshape(B, T, ATT_KV_HEADS, ATT_HEAD_DIM),
            attn_sinks[l])
        y_a = (o_a * jax.nn.silu(ag)) @ w_branch_attn[l]

        merged = jax.nn.sigmoid(mh) * y_h + jax.nn.sigmoid(ma) * y_a
        x = x + merged @ w_out[l]
    return rms_norm(x, final_norm_w)
```

```python
import functools

import jax
import jax.numpy as jnp
from jax import lax
from jax.experimental import pallas as pl
from jax.experimental.pallas import tpu as pltpu

F32 = jnp.float32
BF16 = jnp.bfloat16

D_MODEL = 1024
HG_HEADS = 8
HG_DK = 128
HG_DV = 128
ATT_Q_HEADS = 16
ATT_KV_HEADS = 4
ATT_GROUP = ATT_Q_HEADS // ATT_KV_HEADS
ATT_DH = 64
KV_WIDTH = ATT_KV_HEADS * ATT_DH
WINDOW = 128
EPS = 1e-6
D_IN = 8 * D_MODEL + 2 * KV_WIDTH

SEC_HQ, SEC_HF, SEC_HI, SEC_HG, SEC_AQ, SEC_AG, SEC_MH, SEC_MA = range(8)
KV_BLOCK_K = 8 * D_MODEL // KV_WIDTH
KV_BLOCK_V = KV_BLOCK_K + 1

VMEM_LIMIT = 56 * 1024 * 1024

IN_TM = 512
IN_NCH = 512
HG_C = 64
HG_SUB = 16
HG_TT = 2048
ATT_TQ = 512
OUT_TM = 512


def _nt(a, b):
    return lax.dot_general(a.astype(BF16), b.astype(BF16), (((1,), (1,)), ((), ())),
                           preferred_element_type=F32)


def _sigmoid(x):
    return 1.0 / (1.0 + jnp.exp(-x))


def _inproj_kernel(x_ref, nw_ref, w_ref, o_ref):
    x = x_ref[...]
    ms = jnp.mean(x * x, axis=-1, keepdims=True)
    xn = (x * lax.rsqrt(ms + EPS) * nw_ref[...]).astype(BF16)
    for n in range(D_IN // IN_NCH):
        cols = slice(n * IN_NCH, (n + 1) * IN_NCH)
        o_ref[:, cols] = jnp.dot(xn, w_ref[:, cols], preferred_element_type=F32).astype(BF16)


def _inproj(x2d, norm_w, w_bf16):
    m = x2d.shape[0]
    return pl.pallas_call(
        _inproj_kernel,
        grid=(m // IN_TM,),
        in_specs=[
            pl.BlockSpec((IN_TM, D_MODEL), lambda i: (i, 0)),
            pl.BlockSpec((1, D_MODEL), lambda i: (0, 0)),
            pl.BlockSpec((D_MODEL, D_IN), lambda i: (0, 0), pipeline_mode=pl.Buffered(1)),
        ],
        out_specs=pl.BlockSpec((IN_TM, D_IN), lambda i: (i, 0)),
        out_shape=jax.ShapeDtypeStruct((m, D_IN), BF16),
        compiler_params=pltpu.CompilerParams(
            dimension_semantics=("arbitrary",), vmem_limit_bytes=VMEM_LIMIT),
        name="inproj",
    )(x2d, norm_w, w_bf16)


def _hgrn_kernel(hq_ref, hf_ref, hi_ref, hg_ref, lb_ref, nw_ref, o_ref, st_ref):
    C, SUB = HG_C, HG_SUB

    @pl.when(pl.program_id(2) == 0)
    def _():
        st_ref[...] = jnp.zeros_like(st_ref)

    lb = lb_ref[...]
    nw = nw_ref[...]
    row = lax.broadcasted_iota(jnp.int32, (C, C), 0)
    col = lax.broadcasted_iota(jnp.int32, (C, C), 1)
    tri = (col <= row).astype(F32)
    rb, cb = row >> 4, col >> 4
    m0 = (rb == cb) & (col <= row)
    m1 = ((rb & 1) == 1) & (cb == rb - 1)
    m2 = (row >= 2 * SUB) & (col < 2 * SUB)

    def bc(r, n=SUB):
        return jnp.broadcast_to(r, (n, HG_DK))

    def body(c, carry):
        r = pl.multiple_of(c * C, C)
        xq = hq_ref[pl.ds(r, C), :].astype(F32)
        xf = hf_ref[pl.ds(r, C), :].astype(F32)
        v = hi_ref[pl.ds(r, C), :]
        xg = hg_ref[pl.ds(r, C), :].astype(F32)

        f = lb + (1.0 - lb) * _sigmoid(xf)
        g = jnp.log(f)
        k = 1.0 - f
        q = xq * _sigmoid(xq)

        b = jnp.dot(tri, g, precision=lax.Precision.HIGHEST, preferred_element_type=F32)
        b15, b31, b47, b63 = b[15:16], b[31:32], b[47:48], b[63:64]
        blk_start = jnp.concatenate([jnp.zeros((SUB, HG_DK), F32), bc(b15), bc(b31), bc(b47)], axis=0)
        blk_end = jnp.concatenate([bc(b15), bc(b31), bc(b47), bc(b63)], axis=0)
        blk_mid = 0.5 * (blk_start + blk_end)
        b31b = bc(b31, C)
        b63b = bc(b63, C)

        s0 = _nt(q * jnp.exp(b - blk_mid), k * jnp.exp(blk_mid - b))
        s1 = _nt(q * jnp.exp(b - blk_start), k * jnp.exp(blk_end - b))
        s2 = _nt(q * jnp.exp(jnp.minimum(b - b31b, 0.0)), k * jnp.exp(jnp.minimum(b31b - b, 0.0)))
        scores = jnp.where(m0, s0, jnp.where(m1, s1, jnp.where(m2, s2, 0.0)))

        st = st_ref[...]
        o = jnp.dot(scores.astype(BF16), v, preferred_element_type=F32)
        o = o + _nt(q * jnp.exp(b), st)
        kb = (k * jnp.exp(b63b - b)).astype(BF16)
        st_ref[...] = st * jnp.exp(b63) + lax.dot_general(
            v, kb, (((0,), (0,)), ((), ())), preferred_element_type=F32)

        ms = jnp.mean(o * o, axis=-1, keepdims=True)
        on = o * lax.rsqrt(ms + EPS) * nw
        o_ref[pl.ds(r, C), :] = (on * (xg * _sigmoid(xg))).astype(BF16)
        return carry

    lax.fori_loop(0, HG_TT // C, body, 0)


def _hgrn(proj, lb, hnw, batch, seq):
    nt = seq // HG_TT

    def sec(s):
        return pl.BlockSpec((HG_TT, HG_DK), lambda b, h, t: (b * nt + t, s * HG_HEADS + h))

    vec = pl.BlockSpec((None, 1, HG_DK), lambda b, h, t: (h, 0, 0))
    return pl.pallas_call(
        _hgrn_kernel,
        grid=(batch, HG_HEADS, nt),
        in_specs=[sec(SEC_HQ), sec(SEC_HF), sec(SEC_HI), sec(SEC_HG), vec, vec],
        out_specs=pl.BlockSpec((HG_TT, HG_DV), lambda b, h, t: (b * nt + t, h)),
        out_shape=jax.ShapeDtypeStruct((batch * seq, HG_HEADS * HG_DV), BF16),
        scratch_shapes=[pltpu.VMEM((HG_DV, HG_DK), F32)],
        compiler_params=pltpu.CompilerParams(
            dimension_semantics=("arbitrary", "arbitrary", "arbitrary"),
            vmem_limit_bytes=VMEM_LIMIT),
        name="hgrn",
    )(proj, proj, proj, proj, lb, hnw)


def _attn_kernel(sink_ref, q_ref, kp_ref, vp_ref, k_ref, v_ref, ag_ref, o_ref, kc_ref, vc_ref):
    W = WINDOW
    kc_ref[0:W, :] = kp_ref[...]
    kc_ref[W:, :] = k_ref[...]
    vc_ref[0:W, :] = vp_ref[...]
    vc_ref[W:, :] = v_ref[...]

    first_tile = pl.program_id(1) == 0
    qi = lax.broadcasted_iota(jnp.int32, (W, 2 * W), 0)
    kj = lax.broadcasted_iota(jnp.int32, (W, 2 * W), 1)
    band = (kj > qi) & (kj <= qi + W)

    def body(n, carry):
        r = pl.multiple_of(n * W, W)
        lo = jnp.where(first_tile & (n == 0), W, 0)
        valid = band & (kj >= lo)
        for h in range(ATT_KV_HEADS):
            kk = kc_ref[pl.ds(r, 2 * W), h * ATT_DH:(h + 1) * ATT_DH]
            vv = vc_ref[pl.ds(r, 2 * W), h * ATT_DH:(h + 1) * ATT_DH]
            outs = []
            for jj in range(ATT_GROUP):
                j = h * ATT_GROUP + jj
                qh = q_ref[pl.ds(r, W), j * ATT_DH:(j + 1) * ATT_DH]
                s = _nt(qh, kk) * (ATT_DH ** -0.5)
                s = jnp.where(valid, s, -jnp.inf)
                sink = sink_ref[j]
                m = jnp.maximum(jnp.max(s, axis=-1, keepdims=True), sink)
                p = jnp.exp(s - m)
                den = jnp.sum(p, axis=-1, keepdims=True) + jnp.exp(sink - m)
                outs.append(jnp.dot(p.astype(BF16), vv, preferred_element_type=F32) / den)
            og = jnp.concatenate(outs, axis=1)
            gw = ATT_GROUP * ATT_DH
            ag = ag_ref[pl.ds(r, W), h * gw:(h + 1) * gw].astype(F32)
            o_ref[pl.ds(r, W), h * gw:(h + 1) * gw] = (og * (ag * _sigmoid(ag))).astype(BF16)
        return carry

    lax.fori_loop(0, ATT_TQ // W, body, 0)


def _attn(proj, sinks, batch, seq):
    nt = seq // ATT_TQ
    per = ATT_TQ // WINDOW

    def cur(blk):
        return pl.BlockSpec((ATT_TQ, KV_WIDTH), lambda b, t, s: (b * nt + t, blk))

    def prev(blk):
        return pl.BlockSpec((WINDOW, KV_WIDTH),
                            lambda b, t, s: (jnp.maximum((b * nt + t) * per - 1, 0), blk))

    def sec(sid):
        return pl.BlockSpec((ATT_TQ, D_MODEL), lambda b, t, s: (b * nt + t, sid))

    grid_spec = pltpu.PrefetchScalarGridSpec(
        num_scalar_prefetch=1,
        grid=(batch, nt),
        in_specs=[sec(SEC_AQ), prev(KV_BLOCK_K), prev(KV_BLOCK_V), cur(KV_BLOCK_K), cur(KV_BLOCK_V),
                  sec(SEC_AG)],
        out_specs=pl.BlockSpec((ATT_TQ, D_MODEL), lambda b, t, s: (b * nt + t, 0)),
        scratch_shapes=[pltpu.VMEM((ATT_TQ + WINDOW, KV_WIDTH), BF16),
                        pltpu.VMEM((ATT_TQ + WINDOW, KV_WIDTH), BF16)],
    )
    return pl.pallas_call(
        _attn_kernel,
        grid_spec=grid_spec,
        out_shape=jax.ShapeDtypeStruct((batch * seq, D_MODEL), BF16),
        compiler_params=pltpu.CompilerParams(
            dimension_semantics=("arbitrary", "arbitrary"), vmem_limit_bytes=VMEM_LIMIT),
        name="swattn",
    )(sinks, proj, proj, proj, proj, proj, proj)


def _out_kernel(gh_ref, ga_ref, mh_ref, ma_ref, x_ref, wbh_ref, wba_ref, wo_ref, fnw_ref, o_ref):
    yh = jnp.dot(gh_ref[...], wbh_ref[...], preferred_element_type=F32)
    ya = jnp.dot(ga_ref[...], wba_ref[...], preferred_element_type=F32)
    merged = _sigmoid(mh_ref[...].astype(F32)) * yh + _sigmoid(ma_ref[...].astype(F32)) * ya
    xo = x_ref[...] + jnp.dot(merged.astype(BF16), wo_ref[...], preferred_element_type=F32)
    ms = jnp.mean(xo * xo, axis=-1, keepdims=True)
    o_ref[...] = xo * lax.rsqrt(ms + EPS) * fnw_ref[...]


def _out(gh, ga, proj, x2d, wbh, wba, wo, fnw):
    m = x2d.shape[0]
    tile = lambda c: pl.BlockSpec((OUT_TM, D_MODEL), lambda i: (i, c))
    wspec = pl.BlockSpec((D_MODEL, D_MODEL), lambda i: (0, 0))
    return pl.pallas_call(
        _out_kernel,
        grid=(m // OUT_TM,),
        in_specs=[tile(0), tile(0), tile(SEC_MH), tile(SEC_MA), tile(0), wspec, wspec, wspec,
                  pl.BlockSpec((1, D_MODEL), lambda i: (0, 0))],
        out_specs=tile(0),
        out_shape=jax.ShapeDtypeStruct((m, D_MODEL), F32),
        compiler_params=pltpu.CompilerParams(
            dimension_semantics=("arbitrary",), vmem_limit_bytes=VMEM_LIMIT),
        name="merge_out",
    )(gh, ga, proj, proj, x2d, wbh, wba, wo, fnw)


def kernel(x, norm_w, w_in, hgrn_lower_bound, hgrn_norm_w, w_branch_hgrn, attn_sinks,
           w_branch_attn, w_out, final_norm_w):
    batch, seq, _ = x.shape
    depth = norm_w.shape[0]
    assert depth == 1, "the output kernel fuses the final RMSNorm into the single layer"
    lb_all = jnp.cumsum(jax.nn.softmax(hgrn_lower_bound.astype(F32), axis=0), axis=0)
    kv0 = 5 * D_MODEL
    kv1 = kv0 + 2 * KV_WIDTH
    for l in range(depth):
        w = w_in[l]
        w_perm = jnp.concatenate([w[:, :kv0], w[:, kv1:], w[:, kv0:kv1]], axis=1).astype(BF16)
        x2d = x.reshape(batch * seq, D_MODEL)
        proj = _inproj(x2d, norm_w[l].reshape(1, D_MODEL), w_perm)
        gh = _hgrn(proj, lb_all[l].reshape(HG_HEADS, 1, HG_DK),
                   hgrn_norm_w[l].reshape(HG_HEADS, 1, HG_DV), batch, seq)
        ga = _attn(proj, attn_sinks[l].astype(F32), batch, seq)
        xo = _out(gh, ga, proj, x2d, w_branch_hgrn[l].astype(BF16), w_branch_attn[l].astype(BF16),
                  w_out[l].astype(BF16), final_norm_w.reshape(1, D_MODEL))
        x = xo.reshape(batch, seq, D_MODEL)
    return x
```

```python
import functools

import jax
import jax.numpy as jnp
from jax import lax
from jax.experimental import pallas as pl
from jax.experimental.pallas import tpu as pltpu

F32 = jnp.float32
BF16 = jnp.bfloat16

D_MODEL = 1024
HG_HEADS = 8
HG_DK = 128
HG_DV = 128
ATT_Q_HEADS = 16
ATT_KV_HEADS = 4
ATT_GROUP = ATT_Q_HEADS // ATT_KV_HEADS
ATT_DH = 64
KV_WIDTH = ATT_KV_HEADS * ATT_DH
WINDOW = 128
EPS = 1e-6
D_IN = 8 * D_MODEL + 2 * KV_WIDTH

SEC_HQ, SEC_HF, SEC_HI, SEC_HG, SEC_AQ, SEC_AG, SEC_MH, SEC_MA = range(8)
KV_BLOCK_K = 8 * D_MODEL // KV_WIDTH
KV_BLOCK_V = KV_BLOCK_K + 1

VMEM_LIMIT = 56 * 1024 * 1024

IN_TM = 512
IN_NCH = 512
HG_C = 64
HG_SUB = 16
HG_TT = 2048
HG_GROUP = 8
ATT_TQ = 512
OUT_TM = 512


def _nt(a, b):
    return lax.dot_general(a.astype(BF16), b.astype(BF16), (((1,), (1,)), ((), ())),
                           preferred_element_type=F32)


def _sigmoid(x):
    return 1.0 / (1.0 + jnp.exp(-x))


def _inproj_kernel(x_ref, nw_ref, w_ref, o_ref):
    x = x_ref[...]
    ms = jnp.mean(x * x, axis=-1, keepdims=True)
    xn = (x * lax.rsqrt(ms + EPS) * nw_ref[...]).astype(BF16)
    for n in range(D_IN // IN_NCH):
        cols = slice(n * IN_NCH, (n + 1) * IN_NCH)
        o_ref[:, cols] = jnp.dot(xn, w_ref[:, cols], preferred_element_type=F32).astype(BF16)


def _inproj(x2d, norm_w, w_bf16):
    m = x2d.shape[0]
    return pl.pallas_call(
        _inproj_kernel,
        grid=(m // IN_TM,),
        in_specs=[
            pl.BlockSpec((IN_TM, D_MODEL), lambda i: (i, 0)),
            pl.BlockSpec((1, D_MODEL), lambda i: (0, 0)),
            pl.BlockSpec((D_MODEL, D_IN), lambda i: (0, 0), pipeline_mode=pl.Buffered(1)),
        ],
        out_specs=pl.BlockSpec((IN_TM, D_IN), lambda i: (i, 0)),
        out_shape=jax.ShapeDtypeStruct((m, D_IN), BF16),
        compiler_params=pltpu.CompilerParams(
            dimension_semantics=("arbitrary",), vmem_limit_bytes=VMEM_LIMIT),
        name="inproj",
    )(x2d, norm_w, w_bf16)


def _hgrn_kernel(hq_ref, hf_ref, hi_ref, hg_ref, lb_ref, nw_ref, o_ref, st_ref):
    C, SUB, NB = HG_C, HG_SUB, HG_C // HG_SUB

    @pl.when(pl.program_id(2) == 0)
    def _():
        st_ref[...] = jnp.zeros_like(st_ref)

    lb = lb_ref[...]
    nw = nw_ref[...]
    row = lax.broadcasted_iota(jnp.int32, (C, C), 0)
    col = lax.broadcasted_iota(jnp.int32, (C, C), 1)
    tri = (col <= row).astype(BF16)
    tri2 = jnp.concatenate([tri, tri], axis=1)
    rb, cb = row >> 4, col >> 4
    m0 = (rb == cb) & (col <= row)
    m1 = ((rb & 1) == 1) & (cb == rb - 1)
    m2 = (row >= 2 * SUB) & (col < 2 * SUB)

    def cat(blocks):
        return jnp.concatenate(blocks, axis=0).astype(BF16)

    def group(gi, carry):
        r0 = pl.multiple_of(gi * (HG_GROUP * C), HG_GROUP * C)
        rows = [pl.ds(r0 + c * C, C) for c in range(HG_GROUP)]

        q, k, v, b = [], [], [], []
        for c in range(HG_GROUP):
            xq = hq_ref[rows[c], :].astype(F32)
            xf = hf_ref[rows[c], :].astype(F32)
            v.append(hi_ref[rows[c], :])
            f = lb + (1.0 - lb) * _sigmoid(xf)
            g = jnp.log(f)
            k.append(1.0 - f)
            q.append(xq * _sigmoid(xq))
            g_hi = g.astype(BF16)
            g_lo = (g - g_hi.astype(F32)).astype(BF16)
            b.append(jnp.dot(tri2, jnp.concatenate([g_hi, g_lo], axis=0),
                             preferred_element_type=F32))

        p_all, qb_all, kb_all, dec_all = [], [], [], []
        for c in range(HG_GROUP):
            ends = [b[c][(i + 1) * SUB - 1:(i + 1) * SUB] for i in range(NB)]
            starts = [jnp.zeros_like(ends[0])] + ends[:-1]
            last = ends[-1]
            q1, k1, qd, kd, qb, kb = [], [], [], [], [], []
            for i in range(NB):
                sl = slice(i * SUB, (i + 1) * SUB)
                bi = b[c][sl]
                q1i = q[c][sl] * jnp.exp(bi - starts[i])
                k1i = k[c][sl] * jnp.exp(ends[i] - bi)
                half = jnp.exp(0.5 * (starts[i] - ends[i]))
                q1.append(q1i)
                k1.append(k1i)
                qd.append(q1i * half)
                kd.append(k1i * half)
                qb.append(q1i * jnp.exp(starts[i]))
                kb.append(k1i * jnp.exp(last - ends[i]))
            q2 = q1[:3] + [q1[3] * jnp.exp(ends[2] - ends[1])]
            k2 = [k1[0] * jnp.exp(ends[1] - ends[0])] + k1[1:]
            s0 = _nt(cat(qd), cat(kd))
            s1 = _nt(cat(q1), cat(k1))
            s2 = _nt(cat(q2), cat(k2))
            p_all.append(jnp.where(m0, s0, jnp.where(m1, s1, jnp.where(m2, s2, 0.0))).astype(BF16))
            qb_all.append(cat(qb))
            kb_all.append(cat(kb))
            dec_all.append(jnp.exp(last))

        o_intra = [jnp.dot(p_all[c], v[c], preferred_element_type=F32) for c in range(HG_GROUP)]
        kv = [lax.dot_general(v[c], kb_all[c], (((0,), (0,)), ((), ())), preferred_element_type=F32)
              for c in range(HG_GROUP)]

        st = st_ref[...]
        o = []
        for c in range(HG_GROUP):
            o.append(o_intra[c] + _nt(qb_all[c], st))
            st = st * dec_all[c] + kv[c]
        st_ref[...] = st

        for c in range(HG_GROUP):
            xg = hg_ref[rows[c], :].astype(F32)
            ms = jnp.mean(o[c] * o[c], axis=-1, keepdims=True)
            on = o[c] * lax.rsqrt(ms + EPS) * nw
            o_ref[rows[c], :] = (on * (xg * _sigmoid(xg))).astype(BF16)
        return carry

    lax.fori_loop(0, HG_TT // (HG_GROUP * C), group, 0)


def _hgrn(proj, lb, hnw, batch, seq):
    nt = seq // HG_TT

    def sec(s):
        return pl.BlockSpec((HG_TT, HG_DK), lambda b, h, t: (b * nt + t, s * HG_HEADS + h))

    vec = pl.BlockSpec((None, 1, HG_DK), lambda b, h, t: (h, 0, 0))
    return pl.pallas_call(
        _hgrn_kernel,
        grid=(batch, HG_HEADS, nt),
        in_specs=[sec(SEC_HQ), sec(SEC_HF), sec(SEC_HI), sec(SEC_HG), vec, vec],
        out_specs=pl.BlockSpec((HG_TT, HG_DV), lambda b, h, t: (b * nt + t, h)),
        out_shape=jax.ShapeDtypeStruct((batch * seq, HG_HEADS * HG_DV), BF16),
        scratch_shapes=[pltpu.VMEM((HG_DV, HG_DK), F32)],
        compiler_params=pltpu.CompilerParams(
            dimension_semantics=("arbitrary", "arbitrary", "arbitrary"),
            vmem_limit_bytes=VMEM_LIMIT),
        name="hgrn",
    )(proj, proj, proj, proj, lb, hnw)


def _attn_kernel(sink_ref, q_ref, kp_ref, vp_ref, k_ref, v_ref, ag_ref, o_ref, kc_ref, vc_ref):
    W = WINDOW
    kc_ref[0:W, :] = kp_ref[...]
    kc_ref[W:, :] = k_ref[...]
    vc_ref[0:W, :] = vp_ref[...]
    vc_ref[W:, :] = v_ref[...]

    first_tile = pl.program_id(1) == 0
    qi = lax.broadcasted_iota(jnp.int32, (W, 2 * W), 0)
    kj = lax.broadcasted_iota(jnp.int32, (W, 2 * W), 1)
    band = (kj > qi) & (kj <= qi + W)

    def body(n, carry):
        r = pl.multiple_of(n * W, W)
        lo = jnp.where(first_tile & (n == 0), W, 0)
        valid = band & (kj >= lo)
        for h in range(ATT_KV_HEADS):
            kk = kc_ref[pl.ds(r, 2 * W), h * ATT_DH:(h + 1) * ATT_DH]
            vv = vc_ref[pl.ds(r, 2 * W), h * ATT_DH:(h + 1) * ATT_DH]
            outs = []
            for jj in range(ATT_GROUP):
                j = h * ATT_GROUP + jj
                qh = q_ref[pl.ds(r, W), j * ATT_DH:(j + 1) * ATT_DH]
                s = _nt(qh, kk) * (ATT_DH ** -0.5)
                s = jnp.where(valid, s, -jnp.inf)
                sink = sink_ref[j]
                m = jnp.maximum(jnp.max(s, axis=-1, keepdims=True), sink)
                p = jnp.exp(s - m)
                den = jnp.sum(p, axis=-1, keepdims=True) + jnp.exp(sink - m)
                outs.append(jnp.dot(p.astype(BF16), vv, preferred_element_type=F32) / den)
            og = jnp.concatenate(outs, axis=1)
            gw = ATT_GROUP * ATT_DH
            ag = ag_ref[pl.ds(r, W), h * gw:(h + 1) * gw].astype(F32)
            o_ref[pl.ds(r, W), h * gw:(h + 1) * gw] = (og * (ag * _sigmoid(ag))).astype(BF16)
        return carry

    lax.fori_loop(0, ATT_TQ // W, body, 0)


def _attn(proj, sinks, batch, seq):
    nt = seq // ATT_TQ
    per = ATT_TQ // WINDOW

    def cur(blk):
        return pl.BlockSpec((ATT_TQ, KV_WIDTH), lambda b, t, s: (b * nt + t, blk))

    def prev(blk):
        return pl.BlockSpec((WINDOW, KV_WIDTH),
                            lambda b, t, s: (jnp.maximum((b * nt + t) * per - 1, 0), blk))

    def sec(sid):
        return pl.BlockSpec((ATT_TQ, D_MODEL), lambda b, t, s: (b * nt + t, sid))

    grid_spec = pltpu.PrefetchScalarGridSpec(
        num_scalar_prefetch=1,
        grid=(batch, nt),
        in_specs=[sec(SEC_AQ), prev(KV_BLOCK_K), prev(KV_BLOCK_V), cur(KV_BLOCK_K), cur(KV_BLOCK_V),
                  sec(SEC_AG)],
        out_specs=pl.BlockSpec((ATT_TQ, D_MODEL), lambda b, t, s: (b * nt + t, 0)),
        scratch_shapes=[pltpu.VMEM((ATT_TQ + WINDOW, KV_WIDTH), BF16),
                        pltpu.VMEM((ATT_TQ + WINDOW, KV_WIDTH), BF16)],
    )
    return pl.pallas_call(
        _attn_kernel,
        grid_spec=grid_spec,
        out_shape=jax.ShapeDtypeStruct((batch * seq, D_MODEL), BF16),
        compiler_params=pltpu.CompilerParams(
            dimension_semantics=("arbitrary", "arbitrary"), vmem_limit_bytes=VMEM_LIMIT),
        name="swattn",
    )(sinks, proj, proj, proj, proj, proj, proj)


def _out_kernel(gh_ref, ga_ref, mh_ref, ma_ref, x_ref, wbh_ref, wba_ref, wo_ref, fnw_ref, o_ref):
    yh = jnp.dot(gh_ref[...], wbh_ref[...], preferred_element_type=F32)
    ya = jnp.dot(ga_ref[...], wba_ref[...], preferred_element_type=F32)
    merged = _sigmoid(mh_ref[...].astype(F32)) * yh + _sigmoid(ma_ref[...].astype(F32)) * ya
    xo = x_ref[...] + jnp.dot(merged.astype(BF16), wo_ref[...], preferred_element_type=F32)
    ms = jnp.mean(xo * xo, axis=-1, keepdims=True)
    o_ref[...] = xo * lax.rsqrt(ms + EPS) * fnw_ref[...]


def _out(gh, ga, proj, x2d, wbh, wba, wo, fnw):
    m = x2d.shape[0]
    tile = lambda c: pl.BlockSpec((OUT_TM, D_MODEL), lambda i: (i, c))
    wspec = pl.BlockSpec((D_MODEL, D_MODEL), lambda i: (0, 0))
    return pl.pallas_call(
        _out_kernel,
        grid=(m // OUT_TM,),
        in_specs=[tile(0), tile(0), tile(SEC_MH), tile(SEC_MA), tile(0), wspec, wspec, wspec,
                  pl.BlockSpec((1, D_MODEL), lambda i: (0, 0))],
        out_specs=tile(0),
        out_shape=jax.ShapeDtypeStruct((m, D_MODEL), F32),
        compiler_params=pltpu.CompilerParams(
            dimension_semantics=("arbitrary",), vmem_limit_bytes=VMEM_LIMIT),
        name="merge_out",
    )(gh, ga, proj, proj, x2d, wbh, wba, wo, fnw)


def kernel(x, norm_w, w_in, hgrn_lower_bound, hgrn_norm_w, w_branch_hgrn, attn_sinks,
           w_branch_attn, w_out, final_norm_w):
    batch, seq, _ = x.shape
    depth = norm_w.shape[0]
    assert depth == 1, "the output kernel fuses the final RMSNorm into the single layer"
    lb_all = jnp.cumsum(jax.nn.softmax(hgrn_lower_bound.astype(F32), axis=0), axis=0)
    kv0 = 5 * D_MODEL
    kv1 = kv0 + 2 * KV_WIDTH
    for l in range(depth):
        w = w_in[l]
        w_perm = jnp.concatenate([w[:, :kv0], w[:, kv1:], w[:, kv0:kv1]], axis=1).astype(BF16)
        x2d = x.reshape(batch * seq, D_MODEL)
        proj = _inproj(x2d, norm_w[l].reshape(1, D_MODEL), w_perm)
        gh = _hgrn(proj, lb_all[l].reshape(HG_HEADS, 1, HG_DK),
                   hgrn_norm_w[l].reshape(HG_HEADS, 1, HG_DV), batch, seq)
        ga = _attn(proj, attn_sinks[l].astype(F32), batch, seq)
        xo = _out(gh, ga, proj, x2d, w_branch_hgrn[l].astype(BF16), w_branch_attn[l].astype(BF16),
                  w_out[l].astype(BF16), final_norm_w.reshape(1, D_MODEL))
        x = xo.reshape(batch, seq, D_MODEL)
    return x
```

```python
import functools

import jax
import jax.numpy as jnp
from jax import lax
from jax.experimental import pallas as pl
from jax.experimental.pallas import tpu as pltpu

F32 = jnp.float32
BF16 = jnp.bfloat16

D_MODEL = 1024
HG_HEADS = 8
HG_DK = 128
HG_DV = 128
ATT_Q_HEADS = 16
ATT_KV_HEADS = 4
ATT_GROUP = ATT_Q_HEADS // ATT_KV_HEADS
ATT_DH = 64
KV_WIDTH = ATT_KV_HEADS * ATT_DH
ATT_SCALE = ATT_DH ** -0.5
assert ATT_SCALE == 0.125
WINDOW = 128
EPS = 1e-6
D_IN = 8 * D_MODEL + 2 * KV_WIDTH

SEC_HQ, SEC_HF, SEC_HI, SEC_HG, SEC_AQ, SEC_AG, SEC_MH, SEC_MA = range(8)
KV_BLOCK_K = 8 * D_MODEL // KV_WIDTH
KV_BLOCK_V = KV_BLOCK_K + 1

VMEM_LIMIT = 56 * 1024 * 1024

IN_TM = 512
IN_NCH = 512
HG_C = 64
HG_SUB = 16
HG_TT = 2048
HG_GROUP = 8
ATT_TQ = 512
OUT_TM = 512


def _nt(a, b):
    return lax.dot_general(a.astype(BF16), b.astype(BF16), (((1,), (1,)), ((), ())),
                           preferred_element_type=F32)


def _sigmoid(x):
    return 1.0 / (1.0 + jnp.exp(-x))


def _inproj_kernel(x_ref, nw_ref, w_ref, o_ref):
    x = x_ref[...]
    ms = jnp.mean(x * x, axis=-1, keepdims=True)
    xn = (x * lax.rsqrt(ms + EPS) * nw_ref[...]).astype(BF16)
    for n in range(D_IN // IN_NCH):
        cols = slice(n * IN_NCH, (n + 1) * IN_NCH)
        acc = jnp.dot(xn, w_ref[:, cols], preferred_element_type=F32)
        if n * IN_NCH // D_MODEL == SEC_AQ:
            acc = acc * ATT_SCALE
        o_ref[:, cols] = acc.astype(BF16)


def _inproj(x2d, norm_w, w_bf16):
    m = x2d.shape[0]
    return pl.pallas_call(
        _inproj_kernel,
        grid=(m // IN_TM,),
        in_specs=[
            pl.BlockSpec((IN_TM, D_MODEL), lambda i: (i, 0)),
            pl.BlockSpec((1, D_MODEL), lambda i: (0, 0)),
            pl.BlockSpec((D_MODEL, D_IN), lambda i: (0, 0), pipeline_mode=pl.Buffered(1)),
        ],
        out_specs=pl.BlockSpec((IN_TM, D_IN), lambda i: (i, 0)),
        out_shape=jax.ShapeDtypeStruct((m, D_IN), BF16),
        compiler_params=pltpu.CompilerParams(
            dimension_semantics=("arbitrary",), vmem_limit_bytes=VMEM_LIMIT),
        name="inproj",
    )(x2d, norm_w, w_bf16)


def _hgrn_kernel(hq_ref, hf_ref, hi_ref, hg_ref, lb_ref, nw_ref, o_ref, st_ref):
    C, SUB, NB = HG_C, HG_SUB, HG_C // HG_SUB

    @pl.when(pl.program_id(2) == 0)
    def _():
        st_ref[...] = jnp.zeros_like(st_ref)

    lb = lb_ref[...]
    nw = nw_ref[...]
    row = lax.broadcasted_iota(jnp.int32, (C, C), 0)
    col = lax.broadcasted_iota(jnp.int32, (C, C), 1)
    tri = (col <= row).astype(BF16)
    tri2 = jnp.concatenate([tri, tri], axis=1)
    rb, cb = row >> 4, col >> 4
    m0 = (rb == cb) & (col <= row)
    m1 = ((rb & 1) == 1) & (cb == rb - 1)
    m2 = (row >= 2 * SUB) & (col < 2 * SUB)

    def cat(blocks):
        return jnp.concatenate(blocks, axis=0).astype(BF16)

    def group(gi, carry):
        r0 = pl.multiple_of(gi * (HG_GROUP * C), HG_GROUP * C)
        rows = [pl.ds(r0 + c * C, C) for c in range(HG_GROUP)]

        q, k, v, b = [], [], [], []
        for c in range(HG_GROUP):
            xq = hq_ref[rows[c], :].astype(F32)
            xf = hf_ref[rows[c], :].astype(F32)
            v.append(hi_ref[rows[c], :])
            f = lb + (1.0 - lb) * _sigmoid(xf)
            g = jnp.log(f)
            k.append(1.0 - f)
            q.append(xq * _sigmoid(xq))
            g_hi = g.astype(BF16)
            g_lo = (g - g_hi.astype(F32)).astype(BF16)
            b.append(jnp.dot(tri2, jnp.concatenate([g_hi, g_lo], axis=0),
                             preferred_element_type=F32))

        p_all, qb_all, kb_all, dec_all = [], [], [], []
        for c in range(HG_GROUP):
            ends = [b[c][(i + 1) * SUB - 1:(i + 1) * SUB] for i in range(NB)]
            starts = [jnp.zeros_like(ends[0])] + ends[:-1]
            last = ends[-1]
            q1, k1, qd, kd, qb, kb = [], [], [], [], [], []
            for i in range(NB):
                sl = slice(i * SUB, (i + 1) * SUB)
                bi = b[c][sl]
                q1i = q[c][sl] * jnp.exp(bi - starts[i])
                k1i = k[c][sl] * jnp.exp(ends[i] - bi)
                half = jnp.exp(0.5 * (starts[i] - ends[i]))
                q1.append(q1i)
                k1.append(k1i)
                qd.append(q1i * half)
                kd.append(k1i * half)
                qb.append(q1i * jnp.exp(starts[i]))
                kb.append(k1i * jnp.exp(last - ends[i]))
            q2 = q1[:3] + [q1[3] * jnp.exp(ends[2] - ends[1])]
            k2 = [k1[0] * jnp.exp(ends[1] - ends[0])] + k1[1:]
            s0 = _nt(cat(qd), cat(kd))
            s1 = _nt(cat(q1), cat(k1))
            s2 = _nt(cat(q2), cat(k2))
            p_all.append(jnp.where(m0, s0, jnp.where(m1, s1, jnp.where(m2, s2, 0.0))).astype(BF16))
            qb_all.append(cat(qb))
            kb_all.append(cat(kb))
            dec_all.append(jnp.exp(last))

        o_intra = [jnp.dot(p_all[c], v[c], preferred_element_type=F32) for c in range(HG_GROUP)]
        kv = [lax.dot_general(v[c], kb_all[c], (((0,), (0,)), ((), ())), preferred_element_type=F32)
              for c in range(HG_GROUP)]

        st = st_ref[...]
        o = []
        for c in range(HG_GROUP):
            o.append(o_intra[c] + _nt(qb_all[c], st))
            st = st * dec_all[c] + kv[c]
        st_ref[...] = st

        for c in range(HG_GROUP):
            xg = hg_ref[rows[c], :].astype(F32)
            ms = jnp.mean(o[c] * o[c], axis=-1, keepdims=True)
            on = o[c] * lax.rsqrt(ms + EPS) * nw
            o_ref[rows[c], :] = (on * (xg * _sigmoid(xg))).astype(BF16)
        return carry

    lax.fori_loop(0, HG_TT // (HG_GROUP * C), group, 0)


def _hgrn(proj, lb, hnw, batch, seq):
    nt = seq // HG_TT

    def sec(s):
        return pl.BlockSpec((HG_TT, HG_DK), lambda b, h, t: (b * nt + t, s * HG_HEADS + h))

    vec = pl.BlockSpec((None, 1, HG_DK), lambda b, h, t: (h, 0, 0))
    return pl.pallas_call(
        _hgrn_kernel,
        grid=(batch, HG_HEADS, nt),
        in_specs=[sec(SEC_HQ), sec(SEC_HF), sec(SEC_HI), sec(SEC_HG), vec, vec],
        out_specs=pl.BlockSpec((HG_TT, HG_DV), lambda b, h, t: (b * nt + t, h)),
        out_shape=jax.ShapeDtypeStruct((batch * seq, HG_HEADS * HG_DV), BF16),
        scratch_shapes=[pltpu.VMEM((HG_DV, HG_DK), F32)],
        compiler_params=pltpu.CompilerParams(
            dimension_semantics=("arbitrary", "arbitrary", "arbitrary"),
            vmem_limit_bytes=VMEM_LIMIT),
        name="hgrn",
    )(proj, proj, proj, proj, lb, hnw)


def _attn_kernel(sink_ref, q_ref, kp_ref, vp_ref, k_ref, v_ref, ag_ref, o_ref, kc_ref, vx_ref):
    W, DH = WINDOW, ATT_DH
    kc_ref[0:W, :] = kp_ref[...]
    kc_ref[W:, :] = k_ref[...]
    lane = lax.broadcasted_iota(jnp.int32, (1, 2 * DH), 1)
    low = lane < DH
    for rows, src in ((slice(0, W), vp_ref), (slice(W, None), v_ref)):
        for slab in range(ATT_KV_HEADS // 2):
            x = src[:, slab * 2 * DH:(slab + 1) * 2 * DH]
            xr = jnp.concatenate([x[:, DH:], x[:, :DH]], axis=1)
            zero = jnp.zeros_like(x)
            for i, piece in enumerate((jnp.where(low, x, zero), jnp.where(low, zero, xr),
                                       jnp.where(low, xr, zero), jnp.where(low, zero, x))):
                vx_ref[rows, (4 * slab + i) * 2 * DH:(4 * slab + i + 1) * 2 * DH] = piece
    ones_lo = jnp.broadcast_to(jnp.where(low, 1.0, 0.0).astype(BF16), (2 * W, 2 * DH))
    ones_hi = jnp.broadcast_to(jnp.where(low, 0.0, 1.0).astype(BF16), (2 * W, 2 * DH))
    low_rows = jnp.broadcast_to(low, (W, 2 * DH))

    first_tile = pl.program_id(1) == 0
    qi = lax.broadcasted_iota(jnp.int32, (W, W), 0)
    kj = lax.broadcasted_iota(jnp.int32, (W, W), 1)
    upper = kj > qi

    def block(n, carry):
        r = pl.multiple_of(n * W, W)
        prev_bias = jnp.where(first_tile & (n == 0), -jnp.inf, 0.0)

        def scores(h):
            kk = kc_ref[pl.ds(r, 2 * W), h * DH:(h + 1) * DH]
            qs = jnp.concatenate(
                [q_ref[pl.ds(r, W), j * DH:(j + 1) * DH]
                 for j in range(h * ATT_GROUP, (h + 1) * ATT_GROUP)], axis=0)
            return _nt(qs, kk)

        s_next = scores(0)
        for h in range(ATT_KV_HEADS):
            s = s_next
            if h + 1 < ATT_KV_HEADS:
                s_next = scores(h + 1)
            probs, sink_terms = [], []
            for jj, j in enumerate(range(h * ATT_GROUP, (h + 1) * ATT_GROUP)):
                sj = s[jj * W:(jj + 1) * W]
                c = jnp.where(upper, sj[:, :W] + prev_bias, sj[:, W:])
                sink = sink_ref[j]
                m = jnp.maximum(jnp.max(c, axis=-1, keepdims=True), sink)
                p = jnp.exp(c - m)
                sink_terms.append(jnp.exp(sink - m))
                probs.append(jnp.concatenate(
                    [jnp.where(upper, p, 0.0), jnp.where(upper, 0.0, p)], axis=1).astype(BF16))
            w_lo = jnp.concatenate(
                [vx_ref[pl.ds(r, 2 * W), (2 * h) * 2 * DH:(2 * h + 1) * 2 * DH], ones_lo], axis=1)
            w_hi = jnp.concatenate(
                [vx_ref[pl.ds(r, 2 * W), (2 * h + 1) * 2 * DH:(2 * h + 2) * 2 * DH], ones_hi], axis=1)
            res = (jnp.dot(jnp.concatenate(probs[0::2], axis=0), w_lo, preferred_element_type=F32)
                   + jnp.dot(jnp.concatenate(probs[1::2], axis=0), w_hi, preferred_element_type=F32))
            for pair in range(ATT_GROUP // 2):
                rp = res[pair * W:(pair + 1) * W]
                den = rp[:, 2 * DH:] + jnp.where(low_rows, sink_terms[2 * pair], sink_terms[2 * pair + 1])
                cols = slice((h * ATT_GROUP + 2 * pair) * DH, (h * ATT_GROUP + 2 * pair + 2) * DH)
                ag = ag_ref[pl.ds(r, W), cols].astype(F32)
                o_ref[pl.ds(r, W), cols] = (rp[:, :2 * DH] / den * (ag * _sigmoid(ag))).astype(BF16)
        return carry

    lax.fori_loop(0, ATT_TQ // W, block, 0)


def _attn(proj, sinks, batch, seq):
    nt = seq // ATT_TQ
    per = ATT_TQ // WINDOW

    def cur(blk):
        return pl.BlockSpec((ATT_TQ, KV_WIDTH), lambda b, t, s: (b * nt + t, blk))

    def prev(blk):
        return pl.BlockSpec((WINDOW, KV_WIDTH),
                            lambda b, t, s: (jnp.maximum((b * nt + t) * per - 1, 0), blk))

    def sec(sid):
        return pl.BlockSpec((ATT_TQ, D_MODEL), lambda b, t, s: (b * nt + t, sid))

    grid_spec = pltpu.PrefetchScalarGridSpec(
        num_scalar_prefetch=1,
        grid=(batch, nt),
        in_specs=[sec(SEC_AQ), prev(KV_BLOCK_K), prev(KV_BLOCK_V), cur(KV_BLOCK_K), cur(KV_BLOCK_V),
                  sec(SEC_AG)],
        out_specs=pl.BlockSpec((ATT_TQ, D_MODEL), lambda b, t, s: (b * nt + t, 0)),
        scratch_shapes=[pltpu.VMEM((ATT_TQ + WINDOW, KV_WIDTH), BF16),
                        pltpu.VMEM((ATT_TQ + WINDOW, ATT_KV_HEADS * 4 * ATT_DH), BF16)],
    )
    return pl.pallas_call(
        _attn_kernel,
        grid_spec=grid_spec,
        out_shape=jax.ShapeDtypeStruct((batch * seq, D_MODEL), BF16),
        compiler_params=pltpu.CompilerParams(
            dimension_semantics=("arbitrary", "arbitrary"), vmem_limit_bytes=VMEM_LIMIT),
        name="swattn",
    )(sinks, proj, proj, proj, proj, proj, proj)


def _out_kernel(gh_ref, ga_ref, mh_ref, ma_ref, x_ref, wbh_ref, wba_ref, wo_ref, fnw_ref, o_ref):
    yh = jnp.dot(gh_ref[...], wbh_ref[...], preferred_element_type=F32)
    ya = jnp.dot(ga_ref[...], wba_ref[...], preferred_element_type=F32)
    merged = _sigmoid(mh_ref[...].astype(F32)) * yh + _sigmoid(ma_ref[...].astype(F32)) * ya
    xo = x_ref[...] + jnp.dot(merged.astype(BF16), wo_ref[...], preferred_element_type=F32)
    ms = jnp.mean(xo * xo, axis=-1, keepdims=True)
    o_ref[...] = xo * lax.rsqrt(ms + EPS) * fnw_ref[...]


def _out(gh, ga, proj, x2d, wbh, wba, wo, fnw):
    m = x2d.shape[0]
    tile = lambda c: pl.BlockSpec((OUT_TM, D_MODEL), lambda i: (i, c))
    wspec = pl.BlockSpec((D_MODEL, D_MODEL), lambda i: (0, 0))
    return pl.pallas_call(
        _out_kernel,
        grid=(m // OUT_TM,),
        in_specs=[tile(0), tile(0), tile(SEC_MH), tile(SEC_MA), tile(0), wspec, wspec, wspec,
                  pl.BlockSpec((1, D_MODEL), lambda i: (0, 0))],
        out_specs=tile(0),
        out_shape=jax.ShapeDtypeStruct((m, D_MODEL), F32),
        compiler_params=pltpu.CompilerParams(
            dimension_semantics=("arbitrary",), vmem_limit_bytes=VMEM_LIMIT),
        name="merge_out",
    )(gh, ga, proj, proj, x2d, wbh, wba, wo, fnw)


def kernel(x, norm_w, w_in, hgrn_lower_bound, hgrn_norm_w, w_branch_hgrn, attn_sinks,
           w_branch_attn, w_out, final_norm_w):
    batch, seq, _ = x.shape
    depth = norm_w.shape[0]
    assert depth == 1, "the output kernel fuses the final RMSNorm into the single layer"
    lb_all = jnp.cumsum(jax.nn.softmax(hgrn_lower_bound.astype(F32), axis=0), axis=0)
    kv0 = 5 * D_MODEL
    kv1 = kv0 + 2 * KV_WIDTH
    for l in range(depth):
        w = w_in[l]
        w_perm = jnp.concatenate([w[:, :kv0], w[:, kv1:], w[:, kv0:kv1]], axis=1).astype(BF16)
        x2d = x.reshape(batch * seq, D_MODEL)
        proj = _inproj(x2d, norm_w[l].reshape(1, D_MODEL), w_perm)
        gh = _hgrn(proj, lb_all[l].reshape(HG_HEADS, 1, HG_DK),
                   hgrn_norm_w[l].reshape(HG_HEADS, 1, HG_DV), batch, seq)
        ga = _attn(proj, attn_sinks[l].astype(F32), batch, seq)
        xo = _out(gh, ga, proj, x2d, w_branch_hgrn[l].astype(BF16), w_branch_attn[l].astype(BF16),
                  w_out[l].astype(BF16), final_norm_w.reshape(1, D_MODEL))
        x = xo.reshape(batch, seq, D_MODEL)
    return x
```

```python
import functools

import jax
import jax.numpy as jnp
from jax import lax
from jax.experimental import pallas as pl
from jax.experimental.pallas import tpu as pltpu

F32 = jnp.float32
BF16 = jnp.bfloat16

D_MODEL = 1024
HG_HEADS = 8
HG_DK = 128
HG_DV = 128
ATT_Q_HEADS = 16
ATT_KV_HEADS = 4
ATT_GROUP = ATT_Q_HEADS // ATT_KV_HEADS
ATT_DH = 64
KV_WIDTH = ATT_KV_HEADS * ATT_DH
ATT_SCALE = ATT_DH ** -0.5
assert ATT_SCALE == 0.125
WINDOW = 128
EPS = 1e-6
D_IN = 8 * D_MODEL + 2 * KV_WIDTH

NCH = 512
N_CHUNKS = D_IN // NCH
N_HG_CHUNKS = 4 * D_MODEL // NCH
N_PJ_CHUNKS = N_CHUNKS - N_HG_CHUNKS
PJ_AQ, PJ_AG, PJ_MH, PJ_MA, PJ_KV = 0, 2, 4, 6, 8
SLABS_PER_CHUNK = NCH // HG_DK
SEC_HQ, SEC_HF, SEC_HI, SEC_HG = range(4)

VMEM_LIMIT = 56 * 1024 * 1024

HG_C = 64
HG_SUB = 16
FR_TT = 512
HG_GROUP = FR_TT // HG_C
ATT_TQ = 512
OUT_TM = 512


def _nt(a, b):
    return lax.dot_general(a.astype(BF16), b.astype(BF16), (((1,), (1,)), ((), ())),
                           preferred_element_type=F32)


def _sigmoid(x):
    return 1.0 / (1.0 + jnp.exp(-x))


def _front_kernel(x_ref, nw_ref, w_ref, lb_ref, hnw_ref, pj_ref, gh_ref, xn_ref, hb0_ref, hb1_ref,
                  st_ref, *, tiles_per_seq):
    C, SUB, NB = HG_C, HG_SUB, HG_C // HG_SUB
    i = pl.program_id(0)
    even = lax.rem(i, 2) == 0

    @pl.when(i == 0)
    def _():
        hb1_ref[...] = jnp.zeros_like(hb1_ref)

    @pl.when((i == 0) | (lax.rem(jnp.maximum(i - 1, 0), tiles_per_seq) == 0))
    def _():
        st_ref[...] = jnp.zeros_like(st_ref)

    x = x_ref[...]
    ms = jnp.mean(x * x, axis=-1, keepdims=True)
    xn_ref[...] = (x * lax.rsqrt(ms + EPS) * nw_ref[...]).astype(BF16)
    pj_ref[N_PJ_CHUNKS - 1] = jnp.dot(
        xn_ref[...], w_ref[N_CHUNKS - 1], preferred_element_type=F32).astype(BF16)

    row = lax.broadcasted_iota(jnp.int32, (C, C), 0)
    col = lax.broadcasted_iota(jnp.int32, (C, C), 1)
    tri = (col <= row).astype(BF16)
    tri2 = jnp.concatenate([tri, tri], axis=1)
    rb, cb = row >> 4, col >> 4
    m0 = (rb == cb) & (col <= row)
    m1 = ((rb & 1) == 1) & (cb == rb - 1)
    m2 = (row >= 2 * SUB) & (col < 2 * SUB)

    def cat(blocks):
        return jnp.concatenate(blocks, axis=0).astype(BF16)

    def body(h, to_scratch, wbuf, rbuf):
        def project(part):
            chunk = 2 * h + part // 2
            half = part % 2
            acc = jnp.dot(xn_ref[...], w_ref[chunk, :, half * (NCH // 2):(half + 1) * (NCH // 2)],
                          preferred_element_type=F32).astype(BF16)
            if to_scratch:
                for s in range(SLABS_PER_CHUNK // 2):
                    slab = chunk * SLABS_PER_CHUNK + half * (SLABS_PER_CHUNK // 2) + s
                    wbuf[slab] = acc[:, s * HG_DK:(s + 1) * HG_DK]
            else:
                pj_ref[chunk - N_HG_CHUNKS, :, half * (NCH // 2):(half + 1) * (NCH // 2)] = acc

        def src(sec):
            return rbuf.at[sec * HG_HEADS + h]

        lb = lb_ref[h]
        nw = hnw_ref[h]
        rows = [slice(c * C, (c + 1) * C) for c in range(HG_GROUP)]

        project(0)

        q, k, v, b = [], [], [], []
        for c in range(HG_GROUP):
            xq = src(SEC_HQ)[rows[c], :].astype(F32)
            xf = src(SEC_HF)[rows[c], :].astype(F32)
            v.append(src(SEC_HI)[rows[c], :])
            f = lb + (1.0 - lb) * _sigmoid(xf)
            g = jnp.log(f)
            k.append(1.0 - f)
            q.append(xq * _sigmoid(xq))
            g_hi = g.astype(BF16)
            g_lo = (g - g_hi.astype(F32)).astype(BF16)
            b.append(jnp.dot(tri2, jnp.concatenate([g_hi, g_lo], axis=0),
                             preferred_element_type=F32))

        project(1)

        p_all, qb_all, kb_all, dec_all = [], [], [], []
        for c in range(HG_GROUP):
            ends = [b[c][(j + 1) * SUB - 1:(j + 1) * SUB] for j in range(NB)]
            starts = [jnp.zeros_like(ends[0])] + ends[:-1]
            last = ends[-1]
            q1, k1, qd, kd, qb, kb = [], [], [], [], [], []
            for j in range(NB):
                sl = slice(j * SUB, (j + 1) * SUB)
                bj = b[c][sl]
                q1j = q[c][sl] * jnp.exp(bj - starts[j])
                k1j = k[c][sl] * jnp.exp(ends[j] - bj)
                half = jnp.exp(0.5 * (starts[j] - ends[j]))
                q1.append(q1j)
                k1.append(k1j)
                qd.append(q1j * half)
                kd.append(k1j * half)
                qb.append(q1j * jnp.exp(starts[j]))
                kb.append(k1j * jnp.exp(last - ends[j]))
            q2 = q1[:3] + [q1[3] * jnp.exp(ends[2] - ends[1])]
            k2 = [k1[0] * jnp.exp(ends[1] - ends[0])] + k1[1:]
            s0 = _nt(cat(qd), cat(kd))
            s1 = _nt(cat(q1), cat(k1))
            s2 = _nt(cat(q2), cat(k2))
            p_all.append(jnp.where(m0, s0, jnp.where(m1, s1, jnp.where(m2, s2, 0.0))).astype(BF16))
            qb_all.append(cat(qb))
            kb_all.append(cat(kb))
            dec_all.append(jnp.exp(last))

        project(2)

        o_intra = [jnp.dot(p_all[c], v[c], preferred_element_type=F32) for c in range(HG_GROUP)]
        kv = [lax.dot_general(v[c], kb_all[c], (((0,), (0,)), ((), ())), preferred_element_type=F32)
              for c in range(HG_GROUP)]

        project(3)

        st = st_ref[h]
        o = []
        for c in range(HG_GROUP):
            o.append(o_intra[c] + _nt(qb_all[c], st))
            st = st * dec_all[c] + kv[c]
        st_ref[h] = st

        for c in range(HG_GROUP):
            xg = src(SEC_HG)[rows[c], :].astype(F32)
            ms_o = jnp.mean(o[c] * o[c], axis=-1, keepdims=True)
            on = o[c] * lax.rsqrt(ms_o + EPS) * nw
            gh_ref[h, rows[c], :] = (on * (xg * _sigmoid(xg))).astype(BF16)

    def run(wbuf, rbuf):
        def loop(lo, hi, to_scratch):
            def step(h, carry):
                body(h, to_scratch, wbuf, rbuf)
                return carry
            lax.fori_loop(lo, hi, step, 0)

        loop(0, N_HG_CHUNKS // 2, True)
        loop(N_HG_CHUNKS // 2, HG_HEADS, False)

    @pl.when(even)
    def _():
        run(hb0_ref, hb1_ref)

    @pl.when(jnp.logical_not(even))
    def _():
        run(hb1_ref, hb0_ref)


def _front(x2d, norm_w, w3, lb, hnw, seq):
    m = x2d.shape[0]
    nt = m // FR_TT
    last = nt - 1
    return pl.pallas_call(
        functools.partial(_front_kernel, tiles_per_seq=seq // FR_TT),
        grid=(nt + 1,),
        in_specs=[
            pl.BlockSpec((FR_TT, D_MODEL), lambda i: (jnp.minimum(i, last), 0)),
            pl.BlockSpec((1, D_MODEL), lambda i: (0, 0)),
            pl.BlockSpec((N_CHUNKS, D_MODEL, NCH), lambda i: (0, 0, 0), pipeline_mode=pl.Buffered(1)),
            pl.BlockSpec((HG_HEADS, 1, HG_DK), lambda i: (0, 0, 0)),
            pl.BlockSpec((HG_HEADS, 1, HG_DV), lambda i: (0, 0, 0)),
        ],
        out_specs=[
            pl.BlockSpec((N_PJ_CHUNKS, FR_TT, NCH), lambda i: (0, jnp.minimum(i, last), 0)),
            pl.BlockSpec((HG_HEADS, FR_TT, HG_DV), lambda i: (0, jnp.maximum(i - 1, 0), 0)),
        ],
        out_shape=[jax.ShapeDtypeStruct((N_PJ_CHUNKS, m, NCH), BF16),
                   jax.ShapeDtypeStruct((HG_HEADS, m, HG_DV), BF16)],
        scratch_shapes=[pltpu.VMEM((FR_TT, D_MODEL), BF16),
                        pltpu.VMEM((4 * HG_HEADS, FR_TT, HG_DK), BF16),
                        pltpu.VMEM((4 * HG_HEADS, FR_TT, HG_DK), BF16),
                        pltpu.VMEM((HG_HEADS, HG_DV, HG_DK), F32)],
        compiler_params=pltpu.CompilerParams(
            dimension_semantics=("arbitrary",), vmem_limit_bytes=VMEM_LIMIT),
        name="front",
    )(x2d, norm_w, w3, lb, hnw)


def _attn_kernel(sink_ref, q0_ref, q1_ref, kvp_ref, kv_ref, ag0_ref, ag1_ref, o_ref, kc_ref, vx_ref):
    W, DH = WINDOW, ATT_DH
    half_heads = ATT_Q_HEADS // 2
    kc_ref[0:W, :] = kvp_ref[:, :KV_WIDTH]
    kc_ref[W:, :] = kv_ref[:, :KV_WIDTH]
    lane = lax.broadcasted_iota(jnp.int32, (1, 2 * DH), 1)
    low = lane < DH
    for rows, src in ((slice(0, W), kvp_ref), (slice(W, None), kv_ref)):
        for slab in range(ATT_KV_HEADS // 2):
            x = src[:, KV_WIDTH + slab * 2 * DH:KV_WIDTH + (slab + 1) * 2 * DH]
            xr = jnp.concatenate([x[:, DH:], x[:, :DH]], axis=1)
            zero = jnp.zeros_like(x)
            for t, piece in enumerate((jnp.where(low, x, zero), jnp.where(low, zero, xr),
                                       jnp.where(low, xr, zero), jnp.where(low, zero, x))):
                vx_ref[rows, (4 * slab + t) * 2 * DH:(4 * slab + t + 1) * 2 * DH] = piece
    ones_lo = jnp.broadcast_to(jnp.where(low, 1.0, 0.0).astype(BF16), (2 * W, 2 * DH))
    ones_hi = jnp.broadcast_to(jnp.where(low, 0.0, 1.0).astype(BF16), (2 * W, 2 * DH))
    low_rows = jnp.broadcast_to(low, (W, 2 * DH))

    first_tile = pl.program_id(1) == 0
    qi = lax.broadcasted_iota(jnp.int32, (W, W), 0)
    kj = lax.broadcasted_iota(jnp.int32, (W, W), 1)
    upper = kj > qi

    def head_cols(ref_pair, j, width):
        ref = ref_pair[j // half_heads]
        j = j % half_heads
        return ref, slice(j * DH, (j + width) * DH)

    def block(n, carry):
        r = pl.multiple_of(n * W, W)
        prev_bias = jnp.where(first_tile & (n == 0), -jnp.inf, 0.0)

        def scores(h):
            kk = kc_ref[pl.ds(r, 2 * W), h * DH:(h + 1) * DH]
            parts = []
            for j in range(h * ATT_GROUP, (h + 1) * ATT_GROUP):
                ref, cols = head_cols((q0_ref, q1_ref), j, 1)
                parts.append(ref[pl.ds(r, W), cols])
            return _nt(jnp.concatenate(parts, axis=0), kk)

        s_next = scores(0)
        for h in range(ATT_KV_HEADS):
            s = s_next
            if h + 1 < ATT_KV_HEADS:
                s_next = scores(h + 1)
            probs, sink_terms = [], []
            for jj, j in enumerate(range(h * ATT_GROUP, (h + 1) * ATT_GROUP)):
                sj = s[jj * W:(jj + 1) * W]
                c = jnp.where(upper, sj[:, :W] + prev_bias, sj[:, W:])
                sink = sink_ref[j]
                m = jnp.maximum(jnp.max(c, axis=-1, keepdims=True), sink)
                p = jnp.exp(c - m)
                sink_terms.append(jnp.exp(sink - m))
                probs.append(jnp.concatenate(
                    [jnp.where(upper, p, 0.0), jnp.where(upper, 0.0, p)], axis=1).astype(BF16))
            w_lo = jnp.concatenate(
                [vx_ref[pl.ds(r, 2 * W), (2 * h) * 2 * DH:(2 * h + 1) * 2 * DH], ones_lo], axis=1)
            w_hi = jnp.concatenate(
                [vx_ref[pl.ds(r, 2 * W), (2 * h + 1) * 2 * DH:(2 * h + 2) * 2 * DH], ones_hi], axis=1)
            res = (jnp.dot(jnp.concatenate(probs[0::2], axis=0), w_lo, preferred_element_type=F32)
                   + jnp.dot(jnp.concatenate(probs[1::2], axis=0), w_hi, preferred_element_type=F32))
            for pair in range(ATT_GROUP // 2):
                rp = res[pair * W:(pair + 1) * W]
                den = rp[:, 2 * DH:] + jnp.where(low_rows, sink_terms[2 * pair], sink_terms[2 * pair + 1])
                j0 = h * ATT_GROUP + 2 * pair
                ag_ref, cols = head_cols((ag0_ref, ag1_ref), j0, 2)
                ag = ag_ref[pl.ds(r, W), cols].astype(F32)
                o_ref[pl.ds(r, W), j0 * DH:(j0 + 2) * DH] = (
                    rp[:, :2 * DH] / den * (ag * _sigmoid(ag))).astype(BF16)
        return carry

    lax.fori_loop(0, ATT_TQ // W, block, 0)


def _attn(pj, sinks, batch, seq):
    nt = seq // ATT_TQ
    per = ATT_TQ // WINDOW

    def chunk(c):
        return pl.BlockSpec((None, ATT_TQ, NCH), lambda b, t, s: (c, b * nt + t, 0))

    prev_kv = pl.BlockSpec((None, WINDOW, NCH),
                           lambda b, t, s: (PJ_KV, jnp.maximum((b * nt + t) * per - 1, 0), 0))
    grid_spec = pltpu.PrefetchScalarGridSpec(
        num_scalar_prefetch=1,
        grid=(batch, nt),
        in_specs=[chunk(PJ_AQ), chunk(PJ_AQ + 1), prev_kv, chunk(PJ_KV), chunk(PJ_AG), chunk(PJ_AG + 1)],
        out_specs=pl.BlockSpec((ATT_TQ, D_MODEL), lambda b, t, s: (b * nt + t, 0)),
        scratch_shapes=[pltpu.VMEM((ATT_TQ + WINDOW, KV_WIDTH), BF16),
                        pltpu.VMEM((ATT_TQ + WINDOW, ATT_KV_HEADS * 4 * ATT_DH), BF16)],
    )
    return pl.pallas_call(
        _attn_kernel,
        grid_spec=grid_spec,
        out_shape=jax.ShapeDtypeStruct((batch * seq, D_MODEL), BF16),
        compiler_params=pltpu.CompilerParams(
            dimension_semantics=("arbitrary", "arbitrary"), vmem_limit_bytes=VMEM_LIMIT),
        name="swattn",
    )(sinks, pj, pj, pj, pj, pj, pj)


def _out_kernel(gh_ref, ga_ref, mh0_ref, mh1_ref, ma0_ref, ma1_ref, x_ref, wbh_ref, wba_ref, wo_ref,
                fnw_ref, o_ref):
    gh = jnp.concatenate([gh_ref[h] for h in range(HG_HEADS)], axis=1)
    yh = jnp.dot(gh, wbh_ref[...], preferred_element_type=F32)
    ya = jnp.dot(ga_ref[...], wba_ref[...], preferred_element_type=F32)
    mh = jnp.concatenate([mh0_ref[...], mh1_ref[...]], axis=1).astype(F32)
    ma = jnp.concatenate([ma0_ref[...], ma1_ref[...]], axis=1).astype(F32)
    merged = _sigmoid(mh) * yh + _sigmoid(ma) * ya
    xo = x_ref[...] + jnp.dot(merged.astype(BF16), wo_ref[...], preferred_element_type=F32)
    ms = jnp.mean(xo * xo, axis=-1, keepdims=True)
    o_ref[...] = xo * lax.rsqrt(ms + EPS) * fnw_ref[...]


def _out(gh3, ga, pj, x2d, wbh, wba, wo, fnw):
    m = x2d.shape[0]
    tile = pl.BlockSpec((OUT_TM, D_MODEL), lambda i: (i, 0))
    wspec = pl.BlockSpec((D_MODEL, D_MODEL), lambda i: (0, 0))

    def chunk(c):
        return pl.BlockSpec((None, OUT_TM, NCH), lambda i: (c, i, 0))

    return pl.pallas_call(
        _out_kernel,
        grid=(m // OUT_TM,),
        in_specs=[pl.BlockSpec((HG_HEADS, OUT_TM, HG_DV), lambda i: (0, i, 0)), tile,
                  chunk(PJ_MH), chunk(PJ_MH + 1), chunk(PJ_MA), chunk(PJ_MA + 1), tile,
                  wspec, wspec, wspec, pl.BlockSpec((1, D_MODEL), lambda i: (0, 0))],
        out_specs=tile,
        out_shape=jax.ShapeDtypeStruct((m, D_MODEL), F32),
        compiler_params=pltpu.CompilerParams(
            dimension_semantics=("arbitrary",), vmem_limit_bytes=VMEM_LIMIT),
        name="merge_out",
    )(gh3, ga, pj, pj, pj, pj, x2d, wbh, wba, wo, fnw)


def kernel(x, norm_w, w_in, hgrn_lower_bound, hgrn_norm_w, w_branch_hgrn, attn_sinks,
           w_branch_attn, w_out, final_norm_w):
    batch, seq, _ = x.shape
    depth = norm_w.shape[0]
    assert depth == 1, "the output kernel fuses the final RMSNorm into the single layer"
    assert seq % FR_TT == 0 and seq % ATT_TQ == 0
    lb_all = jnp.cumsum(jax.nn.softmax(hgrn_lower_bound.astype(F32), axis=0), axis=0)
    sec = D_MODEL
    w = w_in[0]
    w_perm = jnp.concatenate(
        [w[:, :4 * sec],
         w[:, 4 * sec:5 * sec] * ATT_SCALE,
         w[:, 5 * sec + 2 * KV_WIDTH:],
         w[:, 5 * sec:5 * sec + 2 * KV_WIDTH]],
        axis=1).astype(BF16)
    w3 = w_perm.reshape(D_MODEL, N_CHUNKS, NCH).transpose(1, 0, 2)
    x2d = x.reshape(batch * seq, D_MODEL)
    pj, gh3 = _front(x2d, norm_w[0].reshape(1, D_MODEL), w3,
                     lb_all[0].reshape(HG_HEADS, 1, HG_DK),
                     hgrn_norm_w[0].reshape(HG_HEADS, 1, HG_DV), seq)
    ga = _attn(pj, attn_sinks[0].astype(F32), batch, seq)
    xo = _out(gh3, ga, pj, x2d, w_branch_hgrn[0].astype(BF16), w_branch_attn[0].astype(BF16),
              w_out[0].astype(BF16), final_norm_w.reshape(1, D_MODEL))
    return xo.reshape(batch, seq, D_MODEL)
```

```python
import functools

import jax
import jax.numpy as jnp
from jax import lax
from jax.experimental import pallas as pl
from jax.experimental.pallas import tpu as pltpu

F32 = jnp.float32
BF16 = jnp.bfloat16

D_MODEL = 1024
HG_HEADS = 8
HG_DK = 128
HG_DV = 128
ATT_Q_HEADS = 16
ATT_KV_HEADS = 4
ATT_GROUP = ATT_Q_HEADS // ATT_KV_HEADS
ATT_DH = 64
KV_WIDTH = ATT_KV_HEADS * ATT_DH
ATT_SCALE = ATT_DH ** -0.5
assert ATT_SCALE == 0.125
WINDOW = 128
EPS = 1e-6
D_IN = 8 * D_MODEL + 2 * KV_WIDTH

NCH = 512
N_CHUNKS = D_IN // NCH
N_HG_CHUNKS = 4 * D_MODEL // NCH
N_PJ_CHUNKS = N_CHUNKS - N_HG_CHUNKS
PJ_AQ, PJ_AG, PJ_MH, PJ_MA, PJ_KV = 0, 2, 4, 6, 8
SLABS_PER_CHUNK = NCH // HG_DK
SEC_HQ, SEC_HF, SEC_HI, SEC_HG = range(4)

VMEM_LIMIT = 56 * 1024 * 1024

HG_C = 64
HG_SUB = 16
FR_TT = 512
HG_GROUP = FR_TT // HG_C
ATT_TQ = 512
OUT_TM = 512


def _nt(a, b):
    return lax.dot_general(a.astype(BF16), b.astype(BF16), (((1,), (1,)), ((), ())),
                           preferred_element_type=F32)


def _sigmoid(x):
    return 1.0 / (1.0 + jnp.exp(-x))


def _wprep_kernel(w_ref, o_ref):
    scale = jnp.where(pl.program_id(0) // 2 == N_HG_CHUNKS // 2, ATT_SCALE, 1.0)
    o_ref[...] = (w_ref[...] * scale).astype(BF16)


def _wprep(w):
    kv_src = N_HG_CHUNKS + 2

    def src_chunk(c):
        return jnp.where(c < kv_src, c, jnp.where(c == N_CHUNKS - 1, kv_src, c + 1))

    return pl.pallas_call(
        _wprep_kernel,
        grid=(N_CHUNKS,),
        in_specs=[pl.BlockSpec((D_MODEL, NCH), lambda c: (0, src_chunk(c)))],
        out_specs=pl.BlockSpec((None, D_MODEL, NCH), lambda c: (c, 0, 0)),
        out_shape=jax.ShapeDtypeStruct((N_CHUNKS, D_MODEL, NCH), BF16),
        compiler_params=pltpu.CompilerParams(dimension_semantics=("arbitrary",)),
        name="wprep",
    )(w)


def _front_kernel(x_ref, nw_ref, w_ref, lb_ref, hnw_ref, pj_ref, gh_ref, xn_ref, hb0_ref, hb1_ref,
                  st_ref, p_sc, qb_sc, kb_sc, dec_sc, *, tiles_per_seq):
    C, SUB, NB = HG_C, HG_SUB, HG_C // HG_SUB
    HALF = NCH // 2
    i = pl.program_id(0)
    even = lax.rem(i, 2) == 0

    @pl.when(i == 0)
    def _():
        hb1_ref[...] = jnp.zeros_like(hb1_ref)

    @pl.when((i == 0) | (lax.rem(jnp.maximum(i - 1, 0), tiles_per_seq) == 0))
    def _():
        st_ref[...] = jnp.zeros_like(st_ref)

    x = x_ref[...]
    ms = jnp.mean(x * x, axis=-1, keepdims=True)
    xn_ref[...] = (x * lax.rsqrt(ms + EPS) * nw_ref[...]).astype(BF16)

    row = lax.broadcasted_iota(jnp.int32, (C, C), 0)
    col = lax.broadcasted_iota(jnp.int32, (C, C), 1)
    tri = (col <= row).astype(BF16)
    tri2 = jnp.concatenate([tri, tri], axis=1)
    rb, cb = row >> 4, col >> 4
    m0 = (rb == cb) & (col <= row)
    m1 = ((rb & 1) == 1) & (cb == rb - 1)
    m2 = (row >= 2 * SUB) & (col < 2 * SUB)
    rows = [slice(c * C, (c + 1) * C) for c in range(HG_GROUP)]

    def cat(blocks):
        return jnp.concatenate(blocks, axis=0).astype(BF16)

    def slot(head_a, head_b, chunks, wbuf, rbuf):
        pieces = [(chunk, half) for chunk in chunks for half in range(2)]

        def project():
            if not pieces:
                return
            chunk, half = pieces.pop(0)
            acc = jnp.dot(xn_ref[...], w_ref[chunk, :, half * HALF:(half + 1) * HALF],
                          preferred_element_type=F32).astype(BF16)
            if wbuf is None:
                pj_ref[chunk - N_HG_CHUNKS, :, half * HALF:(half + 1) * HALF] = acc
            else:
                for s in range(SLABS_PER_CHUNK // 2):
                    slab = chunk * SLABS_PER_CHUNK + half * (SLABS_PER_CHUNK // 2) + s
                    wbuf[slab] = acc[:, s * HG_DK:(s + 1) * HG_DK]

        def src(sec, head):
            return rbuf.at[sec * HG_HEADS + head]

        if head_b is not None:
            v_b = [src(SEC_HI, head_b)[rows[c], :] for c in range(HG_GROUP)]
            p_b = [p_sc[c] for c in range(HG_GROUP)]
            qb_b = [qb_sc[c] for c in range(HG_GROUP)]
            kb_b = [kb_sc[c] for c in range(HG_GROUP)]
            dec_b = [dec_sc[c] for c in range(HG_GROUP)]

        project()

        if head_a is not None:
            lb = lb_ref[head_a]
            q, k, b = [], [], []
            for c in range(HG_GROUP):
                xq = src(SEC_HQ, head_a)[rows[c], :].astype(F32)
                xf = src(SEC_HF, head_a)[rows[c], :].astype(F32)
                f = lb + (1.0 - lb) * _sigmoid(xf)
                g = jnp.log(f)
                k.append(1.0 - f)
                q.append(xq * _sigmoid(xq))
                g_hi = g.astype(BF16)
                g_lo = (g - g_hi.astype(F32)).astype(BF16)
                b.append(jnp.dot(tri2, jnp.concatenate([g_hi, g_lo], axis=0),
                                 preferred_element_type=F32))

        if head_b is not None:
            o_intra = [jnp.dot(p_b[c], v_b[c], preferred_element_type=F32) for c in range(HG_GROUP)]
            kv = [lax.dot_general(v_b[c], kb_b[c], (((0,), (0,)), ((), ())),
                                  preferred_element_type=F32) for c in range(HG_GROUP)]

        project()

        if head_b is not None:
            st = st_ref[head_b]
            o = []
            for c in range(HG_GROUP):
                o.append(o_intra[c] + _nt(qb_b[c], st))
                st = st * dec_b[c] + kv[c]
            st_ref[head_b] = st

        project()

        if head_a is not None:
            p_a, qb_a, kb_a, dec_a = [], [], [], []
            for c in range(HG_GROUP):
                ends = [b[c][(j + 1) * SUB - 1:(j + 1) * SUB] for j in range(NB)]
                starts = [jnp.zeros_like(ends[0])] + ends[:-1]
                last = ends[-1]
                q1, k1, qd, kd, qb, kb = [], [], [], [], [], []
                for j in range(NB):
                    sl = slice(j * SUB, (j + 1) * SUB)
                    bj = b[c][sl]
                    q1j = q[c][sl] * jnp.exp(bj - starts[j])
                    k1j = k[c][sl] * jnp.exp(ends[j] - bj)
                    half = jnp.exp(0.5 * (starts[j] - ends[j]))
                    q1.append(q1j)
                    k1.append(k1j)
                    qd.append(q1j * half)
                    kd.append(k1j * half)
                    qb.append(q1j * jnp.exp(starts[j]))
                    kb.append(k1j * jnp.exp(last - ends[j]))
                q2 = q1[:3] + [q1[3] * jnp.exp(ends[2] - ends[1])]
                k2 = [k1[0] * jnp.exp(ends[1] - ends[0])] + k1[1:]
                s0 = _nt(cat(qd), cat(kd))
                s1 = _nt(cat(q1), cat(k1))
                s2 = _nt(cat(q2), cat(k2))
                p_a.append(
                    jnp.where(m0, s0, jnp.where(m1, s1, jnp.where(m2, s2, 0.0))).astype(BF16))
                qb_a.append(cat(qb))
                kb_a.append(cat(kb))
                dec_a.append(jnp.exp(last))

        project()

        if head_b is not None:
            nw = hnw_ref[head_b]
            for c in range(HG_GROUP):
                xg = src(SEC_HG, head_b)[rows[c], :].astype(F32)
                ms_o = jnp.mean(o[c] * o[c], axis=-1, keepdims=True)
                on = o[c] * lax.rsqrt(ms_o + EPS) * nw
                gh_ref[head_b, rows[c], :] = (on * (xg * _sigmoid(xg))).astype(BF16)

        if head_a is not None:
            for c in range(HG_GROUP):
                p_sc[c] = p_a[c]
                qb_sc[c] = qb_a[c]
                kb_sc[c] = kb_a[c]
                dec_sc[c] = dec_a[c]

    def run(wbuf, rbuf):
        slot(0, None, [N_CHUNKS - 1], None, rbuf)

        def mid(s, carry, to_scratch):
            slot(s, s - 1, [2 * (s - 1), 2 * (s - 1) + 1], wbuf if to_scratch else None, rbuf)
            return carry

        n_scratch_slots = N_HG_CHUNKS // 2
        lax.fori_loop(1, 1 + n_scratch_slots, functools.partial(mid, to_scratch=True), 0)
        lax.fori_loop(1 + n_scratch_slots, HG_HEADS, functools.partial(mid, to_scratch=False), 0)
        slot(None, HG_HEADS - 1, [2 * (HG_HEADS - 1), 2 * (HG_HEADS - 1) + 1], None, rbuf)

    @pl.when(even)
    def _():
        run(hb0_ref, hb1_ref)

    @pl.when(jnp.logical_not(even))
    def _():
        run(hb1_ref, hb0_ref)


def _front(x2d, norm_w, w3, lb, hnw, seq):
    m = x2d.shape[0]
    nt = m // FR_TT
    last = nt - 1
    return pl.pallas_call(
        functools.partial(_front_kernel, tiles_per_seq=seq // FR_TT),
        grid=(nt + 1,),
        in_specs=[
            pl.BlockSpec((FR_TT, D_MODEL), lambda i: (jnp.minimum(i, last), 0)),
            pl.BlockSpec((1, D_MODEL), lambda i: (0, 0)),
            pl.BlockSpec((N_CHUNKS, D_MODEL, NCH), lambda i: (0, 0, 0), pipeline_mode=pl.Buffered(1)),
            pl.BlockSpec((HG_HEADS, 1, HG_DK), lambda i: (0, 0, 0)),
            pl.BlockSpec((HG_HEADS, 1, HG_DV), lambda i: (0, 0, 0)),
        ],
        out_specs=[
            pl.BlockSpec((N_PJ_CHUNKS, FR_TT, NCH), lambda i: (0, jnp.minimum(i, last), 0)),
            pl.BlockSpec((HG_HEADS, FR_TT, HG_DV), lambda i: (0, jnp.maximum(i - 1, 0), 0)),
        ],
        out_shape=[jax.ShapeDtypeStruct((N_PJ_CHUNKS, m, NCH), BF16),
                   jax.ShapeDtypeStruct((HG_HEADS, m, HG_DV), BF16)],
        scratch_shapes=[pltpu.VMEM((FR_TT, D_MODEL), BF16),
                        pltpu.VMEM((4 * HG_HEADS, FR_TT, HG_DK), BF16),
                        pltpu.VMEM((4 * HG_HEADS, FR_TT, HG_DK), BF16),
                        pltpu.VMEM((HG_HEADS, HG_DV, HG_DK), F32),
                        pltpu.VMEM((HG_GROUP, HG_C, HG_C), BF16),
                        pltpu.VMEM((HG_GROUP, HG_C, HG_DK), BF16),
                        pltpu.VMEM((HG_GROUP, HG_C, HG_DK), BF16),
                        pltpu.VMEM((HG_GROUP, 1, HG_DK), F32)],
        compiler_params=pltpu.CompilerParams(
            dimension_semantics=("arbitrary",), vmem_limit_bytes=VMEM_LIMIT),
        name="front",
    )(x2d, norm_w, w3, lb, hnw)


def _attn_kernel(sink_ref, q0_ref, q1_ref, kvp_ref, kv_ref, ag0_ref, ag1_ref, o_ref, kc_ref, vx_ref):
    W, DH = WINDOW, ATT_DH
    half_heads = ATT_Q_HEADS // 2
    kc_ref[0:W, :] = kvp_ref[:, :KV_WIDTH]
    kc_ref[W:, :] = kv_ref[:, :KV_WIDTH]
    lane = lax.broadcasted_iota(jnp.int32, (1, 2 * DH), 1)
    low = lane < DH
    for rows, src in ((slice(0, W), kvp_ref), (slice(W, None), kv_ref)):
        for slab in range(ATT_KV_HEADS // 2):
            x = src[:, KV_WIDTH + slab * 2 * DH:KV_WIDTH + (slab + 1) * 2 * DH]
            xr = jnp.concatenate([x[:, DH:], x[:, :DH]], axis=1)
            zero = jnp.zeros_like(x)
            for t, piece in enumerate((jnp.where(low, x, zero), jnp.where(low, zero, xr),
                                       jnp.where(low, xr, zero), jnp.where(low, zero, x))):
                vx_ref[rows, (4 * slab + t) * 2 * DH:(4 * slab + t + 1) * 2 * DH] = piece
    ones_lo = jnp.broadcast_to(jnp.where(low, 1.0, 0.0).astype(BF16), (2 * W, 2 * DH))
    ones_hi = jnp.broadcast_to(jnp.where(low, 0.0, 1.0).astype(BF16), (2 * W, 2 * DH))
    low_rows = jnp.broadcast_to(low, (W, 2 * DH))

    first_tile = pl.program_id(1) == 0
    qi = lax.broadcasted_iota(jnp.int32, (W, W), 0)
    kj = lax.broadcasted_iota(jnp.int32, (W, W), 1)
    upper = kj > qi

    def head_cols(ref_pair, j, width):
        ref = ref_pair[j // half_heads]
        j = j % half_heads
        return ref, slice(j * DH, (j + width) * DH)

    def block(n, carry):
        r = pl.multiple_of(n * W, W)
        prev_bias = jnp.where(first_tile & (n == 0), -jnp.inf, 0.0)

        def scores(h):
            kk = kc_ref[pl.ds(r, 2 * W), h * DH:(h + 1) * DH]
            parts = []
            for j in range(h * ATT_GROUP, (h + 1) * ATT_GROUP):
                ref, cols = head_cols((q0_ref, q1_ref), j, 1)
                parts.append(ref[pl.ds(r, W), cols])
            return _nt(jnp.concatenate(parts, axis=0), kk)

        s_next = scores(0)
        for h in range(ATT_KV_HEADS):
            s = s_next
            if h + 1 < ATT_KV_HEADS:
                s_next = scores(h + 1)
            probs, sink_terms = [], []
            for jj, j in enumerate(range(h * ATT_GROUP, (h + 1) * ATT_GROUP)):
                sj = s[jj * W:(jj + 1) * W]
                c = jnp.where(upper, sj[:, :W] + prev_bias, sj[:, W:])
                sink = sink_ref[j]
                m = jnp.maximum(jnp.max(c, axis=-1, keepdims=True), sink)
                p = jnp.exp(c - m)
                sink_terms.append(jnp.exp(sink - m))
                probs.append(jnp.concatenate(
                    [jnp.where(upper, p, 0.0), jnp.where(upper, 0.0, p)], axis=1).astype(BF16))
            w_lo = jnp.concatenate(
                [vx_ref[pl.ds(r, 2 * W), (2 * h) * 2 * DH:(2 * h + 1) * 2 * DH], ones_lo], axis=1)
            w_hi = jnp.concatenate(
                [vx_ref[pl.ds(r, 2 * W), (2 * h + 1) * 2 * DH:(2 * h + 2) * 2 * DH], ones_hi], axis=1)
            res = (jnp.dot(jnp.concatenate(probs[0::2], axis=0), w_lo, preferred_element_type=F32)
                   + jnp.dot(jnp.concatenate(probs[1::2], axis=0), w_hi, preferred_element_type=F32))
            for pair in range(ATT_GROUP // 2):
                rp = res[pair * W:(pair + 1) * W]
                den = rp[:, 2 * DH:] + jnp.where(low_rows, sink_terms[2 * pair], sink_terms[2 * pair + 1])
                j0 = h * ATT_GROUP + 2 * pair
                ag_ref, cols = head_cols((ag0_ref, ag1_ref), j0, 2)
                ag = ag_ref[pl.ds(r, W), cols].astype(F32)
                o_ref[pl.ds(r, W), j0 * DH:(j0 + 2) * DH] = (
                    rp[:, :2 * DH] / den * (ag * _sigmoid(ag))).astype(BF16)
        return carry

    lax.fori_loop(0, ATT_TQ // W, block, 0)


def _attn(pj, sinks, batch, seq):
    nt = seq // ATT_TQ
    per = ATT_TQ // WINDOW

    def chunk(c):
        return pl.BlockSpec((None, ATT_TQ, NCH), lambda b, t, s: (c, b * nt + t, 0))

    prev_kv = pl.BlockSpec((None, WINDOW, NCH),
                           lambda b, t, s: (PJ_KV, jnp.maximum((b * nt + t) * per - 1, 0), 0))
    grid_spec = pltpu.PrefetchScalarGridSpec(
        num_scalar_prefetch=1,
        grid=(batch, nt),
        in_specs=[chunk(PJ_AQ), chunk(PJ_AQ + 1), prev_kv, chunk(PJ_KV), chunk(PJ_AG), chunk(PJ_AG + 1)],
        out_specs=pl.BlockSpec((ATT_TQ, D_MODEL), lambda b, t, s: (b * nt + t, 0)),
        scratch_shapes=[pltpu.VMEM((ATT_TQ + WINDOW, KV_WIDTH), BF16),
                        pltpu.VMEM((ATT_TQ + WINDOW, ATT_KV_HEADS * 4 * ATT_DH), BF16)],
    )
    return pl.pallas_call(
        _attn_kernel,
        grid_spec=grid_spec,
        out_shape=jax.ShapeDtypeStruct((batch * seq, D_MODEL), BF16),
        compiler_params=pltpu.CompilerParams(
            dimension_semantics=("arbitrary", "arbitrary"), vmem_limit_bytes=VMEM_LIMIT),
        name="swattn",
    )(sinks, pj, pj, pj, pj, pj, pj)


def _out_kernel(gh_ref, ga_ref, mh0_ref, mh1_ref, ma0_ref, ma1_ref, x_ref, wbh32_ref, wba32_ref,
                wo32_ref, fnw_ref, o_ref, wbh_ref, wba_ref, wo_ref):
    @pl.when(pl.program_id(0) == 0)
    def _():
        wbh_ref[...] = wbh32_ref[...].astype(BF16)
        wba_ref[...] = wba32_ref[...].astype(BF16)
        wo_ref[...] = wo32_ref[...].astype(BF16)

    gh = jnp.concatenate([gh_ref[h] for h in range(HG_HEADS)], axis=1)
    yh = jnp.dot(gh, wbh_ref[...], preferred_element_type=F32)
    ya = jnp.dot(ga_ref[...], wba_ref[...], preferred_element_type=F32)
    mh = jnp.concatenate([mh0_ref[...], mh1_ref[...]], axis=1).astype(F32)
    ma = jnp.concatenate([ma0_ref[...], ma1_ref[...]], axis=1).astype(F32)
    merged = _sigmoid(mh) * yh + _sigmoid(ma) * ya
    xo = x_ref[...] + jnp.dot(merged.astype(BF16), wo_ref[...], preferred_element_type=F32)
    ms = jnp.mean(xo * xo, axis=-1, keepdims=True)
    o_ref[...] = xo * lax.rsqrt(ms + EPS) * fnw_ref[...]


def _out(gh3, ga, pj, x2d, wbh, wba, wo, fnw):
    m = x2d.shape[0]
    tile = pl.BlockSpec((OUT_TM, D_MODEL), lambda i: (i, 0))
    wspec = pl.BlockSpec((D_MODEL, D_MODEL), lambda i: (0, 0), pipeline_mode=pl.Buffered(1))

    def chunk(c):
        return pl.BlockSpec((None, OUT_TM, NCH), lambda i: (c, i, 0))

    return pl.pallas_call(
        _out_kernel,
        grid=(m // OUT_TM,),
        in_specs=[pl.BlockSpec((HG_HEADS, OUT_TM, HG_DV), lambda i: (0, i, 0)), tile,
                  chunk(PJ_MH), chunk(PJ_MH + 1), chunk(PJ_MA), chunk(PJ_MA + 1), tile,
                  wspec, wspec, wspec, pl.BlockSpec((1, D_MODEL), lambda i: (0, 0))],
        out_specs=tile,
        out_shape=jax.ShapeDtypeStruct((m, D_MODEL), F32),
        scratch_shapes=[pltpu.VMEM((D_MODEL, D_MODEL), BF16)] * 3,
        compiler_params=pltpu.CompilerParams(
            dimension_semantics=("arbitrary",), vmem_limit_bytes=VMEM_LIMIT),
        name="merge_out",
    )(gh3, ga, pj, pj, pj, pj, x2d, wbh, wba, wo, fnw)


def kernel(x, norm_w, w_in, hgrn_lower_bound, hgrn_norm_w, w_branch_hgrn, attn_sinks,
           w_branch_attn, w_out, final_norm_w):
    batch, seq, _ = x.shape
    depth = norm_w.shape[0]
    assert depth == 1, "the output kernel fuses the final RMSNorm into the single layer"
    assert seq % FR_TT == 0 and seq % ATT_TQ == 0
    lb_all = jnp.cumsum(jax.nn.softmax(hgrn_lower_bound.astype(F32), axis=0), axis=0)
    w3 = _wprep(w_in[0])
    x2d = x.reshape(batch * seq, D_MODEL)
    pj, gh3 = _front(x2d, norm_w[0].reshape(1, D_MODEL), w3,
                     lb_all[0].reshape(HG_HEADS, 1, HG_DK),
                     hgrn_norm_w[0].reshape(HG_HEADS, 1, HG_DV), seq)
    ga = _attn(pj, attn_sinks[0].astype(F32), batch, seq)
    xo = _out(gh3, ga, pj, x2d, w_branch_hgrn[0], w_branch_attn[0], w_out[0],
              final_norm_w.reshape(1, D_MODEL))
    return xo.reshape(batch, seq, D_MODEL)
```

```python
import functools

import jax
import jax.numpy as jnp
from jax import lax
from jax.experimental import pallas as pl
from jax.experimental.pallas import tpu as pltpu

F32 = jnp.float32
BF16 = jnp.bfloat16

D_MODEL = 1024
HG_HEADS = 8
HG_DK = 128
HG_DV = 128
ATT_Q_HEADS = 16
ATT_KV_HEADS = 4
ATT_GROUP = ATT_Q_HEADS // ATT_KV_HEADS
ATT_DH = 64
KV_WIDTH = ATT_KV_HEADS * ATT_DH
ATT_SCALE = ATT_DH ** -0.5
assert ATT_SCALE == 0.125
WINDOW = 128
EPS = 1e-6
D_IN = 8 * D_MODEL + 2 * KV_WIDTH

NCH = 512
N_CHUNKS = D_IN // NCH
N_HG_CHUNKS = 4 * D_MODEL // NCH
N_PJ_CHUNKS = N_CHUNKS - N_HG_CHUNKS
PJ_AQ, PJ_AG, PJ_MH, PJ_MA, PJ_KV = 0, 2, 4, 6, 8
SLABS_PER_CHUNK = NCH // HG_DK
SEC_HQ, SEC_HF, SEC_HI, SEC_HG = range(4)

VMEM_LIMIT = 56 * 1024 * 1024

HG_C = 64
HG_SUB = 16
FR_TT = 512
HG_GROUP = FR_TT // HG_C
BK_TT = 512


def _nt(a, b):
    return lax.dot_general(a.astype(BF16), b.astype(BF16), (((1,), (1,)), ((), ())),
                           preferred_element_type=F32)


def _sigmoid(x):
    return 1.0 / (1.0 + jnp.exp(-x))


def _wprep_kernel(w_ref, o_ref):
    scale = jnp.where(pl.program_id(0) // 2 == N_HG_CHUNKS // 2, ATT_SCALE, 1.0)
    o_ref[...] = (w_ref[...] * scale).astype(BF16)


def _wprep(w):
    kv_src = N_HG_CHUNKS + 2

    def src_chunk(c):
        return jnp.where(c < kv_src, c, jnp.where(c == N_CHUNKS - 1, kv_src, c + 1))

    return pl.pallas_call(
        _wprep_kernel,
        grid=(N_CHUNKS,),
        in_specs=[pl.BlockSpec((D_MODEL, NCH), lambda c: (0, src_chunk(c)))],
        out_specs=pl.BlockSpec((None, D_MODEL, NCH), lambda c: (c, 0, 0)),
        out_shape=jax.ShapeDtypeStruct((N_CHUNKS, D_MODEL, NCH), BF16),
        compiler_params=pltpu.CompilerParams(dimension_semantics=("arbitrary",)),
        name="wprep",
    )(w)


def _front_kernel(x_ref, nw_ref, w_ref, lb_ref, hnw_ref, pj_ref, gh_ref, xn_ref, hb0_ref, hb1_ref,
                  st_ref, p_sc, qb_sc, kb_sc, dec_sc, *, tiles_per_seq):
    C, SUB, NB = HG_C, HG_SUB, HG_C // HG_SUB
    HALF = NCH // 2
    i = pl.program_id(0)
    even = lax.rem(i, 2) == 0

    @pl.when(i == 0)
    def _():
        hb1_ref[...] = jnp.zeros_like(hb1_ref)
        p_sc[...] = jnp.zeros_like(p_sc)
        qb_sc[...] = jnp.zeros_like(qb_sc)
        kb_sc[...] = jnp.zeros_like(kb_sc)
        dec_sc[...] = jnp.zeros_like(dec_sc)

    @pl.when((i == 0) | (lax.rem(jnp.maximum(i - 1, 0), tiles_per_seq) == 0))
    def _():
        st_ref[...] = jnp.zeros_like(st_ref)

    x = x_ref[...]
    ms = jnp.mean(x * x, axis=-1, keepdims=True)
    xn_ref[...] = (x * lax.rsqrt(ms + EPS) * nw_ref[...]).astype(BF16)

    row = lax.broadcasted_iota(jnp.int32, (C, C), 0)
    col = lax.broadcasted_iota(jnp.int32, (C, C), 1)
    tri = (col <= row).astype(BF16)
    tri2 = jnp.concatenate([tri, tri], axis=1)
    rb, cb = row >> 4, col >> 4
    m0 = (rb == cb) & (col <= row)
    m1 = ((rb & 1) == 1) & (cb == rb - 1)
    m2 = (row >= 2 * SUB) & (col < 2 * SUB)
    rows = [slice(c * C, (c + 1) * C) for c in range(HG_GROUP)]

    def cat(blocks):
        return jnp.concatenate(blocks, axis=0).astype(BF16)

    def slot(head_a, abuf, head_b, chunks, wbuf, rbuf):
        pieces = [(chunk, half) for chunk in chunks for half in range(2)]

        def project(flush=False):
            if not pieces:
                return
            chunk, half = pieces.pop(0)
            acc = jnp.dot(xn_ref[...], w_ref[chunk, :, half * HALF:(half + 1) * HALF],
                          preferred_element_type=F32).astype(BF16)
            if wbuf is None:
                pj_ref[chunk - N_HG_CHUNKS, :, half * HALF:(half + 1) * HALF] = acc
            else:
                for s in range(SLABS_PER_CHUNK // 2):
                    slab = chunk * SLABS_PER_CHUNK + half * (SLABS_PER_CHUNK // 2) + s
                    wbuf[slab] = acc[:, s * HG_DK:(s + 1) * HG_DK]
            if flush:
                project(True)

        def src(sec, head, buf=rbuf):
            return buf.at[sec * HG_HEADS + head]

        v_b = [src(SEC_HI, head_b)[rows[c], :] for c in range(HG_GROUP)]
        p_b = [p_sc[c] for c in range(HG_GROUP)]
        qb_b = [qb_sc[c] for c in range(HG_GROUP)]
        kb_b = [kb_sc[c] for c in range(HG_GROUP)]
        dec_b = [dec_sc[c] for c in range(HG_GROUP)]

        project()

        lb = lb_ref[head_a]
        q, k, b = [], [], []
        for c in range(HG_GROUP):
            xq = src(SEC_HQ, head_a, abuf)[rows[c], :].astype(F32)
            xf = src(SEC_HF, head_a, abuf)[rows[c], :].astype(F32)
            f = lb + (1.0 - lb) * _sigmoid(xf)
            g = jnp.log(f)
            k.append(1.0 - f)
            q.append(xq * _sigmoid(xq))
            g_hi = g.astype(BF16)
            g_lo = (g - g_hi.astype(F32)).astype(BF16)
            b.append(jnp.dot(tri2, jnp.concatenate([g_hi, g_lo], axis=0),
                             preferred_element_type=F32))

        o_intra = [jnp.dot(p_b[c], v_b[c], preferred_element_type=F32) for c in range(HG_GROUP)]
        kv = [lax.dot_general(v_b[c], kb_b[c], (((0,), (0,)), ((), ())),
                              preferred_element_type=F32) for c in range(HG_GROUP)]

        project()

        st = st_ref[head_b]
        o = []
        for c in range(HG_GROUP):
            o.append(o_intra[c] + _nt(qb_b[c], st))
            st = st * dec_b[c] + kv[c]
        st_ref[head_b] = st

        project()

        p_a, qb_a, kb_a, dec_a = [], [], [], []
        for c in range(HG_GROUP):
            ends = [b[c][(j + 1) * SUB - 1:(j + 1) * SUB] for j in range(NB)]
            starts = [jnp.zeros_like(ends[0])] + ends[:-1]
            last = ends[-1]
            q1, k1, qd, kd, qb, kb = [], [], [], [], [], []
            for j in range(NB):
                sl = slice(j * SUB, (j + 1) * SUB)
                bj = b[c][sl]
                q1j = q[c][sl] * jnp.exp(bj - starts[j])
                k1j = k[c][sl] * jnp.exp(ends[j] - bj)
                half = jnp.exp(0.5 * (starts[j] - ends[j]))
                q1.append(q1j)
                k1.append(k1j)
                qd.append(q1j * half)
                kd.append(k1j * half)
                qb.append(q1j * jnp.exp(starts[j]))
                kb.append(k1j * jnp.exp(last - ends[j]))
            q2 = q1[:3] + [q1[3] * jnp.exp(ends[2] - ends[1])]
            k2 = [k1[0] * jnp.exp(ends[1] - ends[0])] + k1[1:]
            s0 = _nt(cat(qd), cat(kd))
            s1 = _nt(cat(q1), cat(k1))
            s2 = _nt(cat(q2), cat(k2))
            p_a.append(jnp.where(m0, s0, jnp.where(m1, s1, jnp.where(m2, s2, 0.0))).astype(BF16))
            qb_a.append(cat(qb))
            kb_a.append(cat(kb))
            dec_a.append(jnp.exp(last))

        project(flush=True)

        nw = hnw_ref[head_b]
        for c in range(HG_GROUP):
            xg = src(SEC_HG, head_b)[rows[c], :].astype(F32)
            ms_o = jnp.mean(o[c] * o[c], axis=-1, keepdims=True)
            on = o[c] * lax.rsqrt(ms_o + EPS) * nw
            gh_ref[head_b, rows[c], :] = (on * (xg * _sigmoid(xg))).astype(BF16)

        for c in range(HG_GROUP):
            p_sc[c] = p_a[c]
            qb_sc[c] = qb_a[c]
            kb_sc[c] = kb_a[c]
            dec_sc[c] = dec_a[c]

    def run(wbuf, rbuf):
        def mid(s, carry, to_scratch):
            slot(s, rbuf, s - 1, [2 * (s - 1), 2 * (s - 1) + 1], wbuf if to_scratch else None, rbuf)
            return carry

        n_scratch_slots = N_HG_CHUNKS // 2
        lax.fori_loop(1, 1 + n_scratch_slots, functools.partial(mid, to_scratch=True), 0)
        lax.fori_loop(1 + n_scratch_slots, HG_HEADS, functools.partial(mid, to_scratch=False), 0)
        slot(0, wbuf, HG_HEADS - 1, list(range(2 * (HG_HEADS - 1), N_CHUNKS)), None, rbuf)

    @pl.when(even)
    def _():
        run(hb0_ref, hb1_ref)

    @pl.when(jnp.logical_not(even))
    def _():
        run(hb1_ref, hb0_ref)


def _front(x2d, norm_w, w3, lb, hnw, seq):
    m = x2d.shape[0]
    nt = m // FR_TT
    last = nt - 1
    return pl.pallas_call(
        functools.partial(_front_kernel, tiles_per_seq=seq // FR_TT),
        grid=(nt + 1,),
        in_specs=[
            pl.BlockSpec((FR_TT, D_MODEL), lambda i: (jnp.minimum(i, last), 0)),
            pl.BlockSpec((1, D_MODEL), lambda i: (0, 0)),
            pl.BlockSpec((N_CHUNKS, D_MODEL, NCH), lambda i: (0, 0, 0), pipeline_mode=pl.Buffered(1)),
            pl.BlockSpec((HG_HEADS, 1, HG_DK), lambda i: (0, 0, 0)),
            pl.BlockSpec((HG_HEADS, 1, HG_DV), lambda i: (0, 0, 0)),
        ],
        out_specs=[
            pl.BlockSpec((N_PJ_CHUNKS, FR_TT, NCH), lambda i: (0, jnp.minimum(i, last), 0)),
            pl.BlockSpec((HG_HEADS, FR_TT, HG_DV), lambda i: (0, jnp.maximum(i - 1, 0), 0)),
        ],
        out_shape=[jax.ShapeDtypeStruct((N_PJ_CHUNKS, m, NCH), BF16),
                   jax.ShapeDtypeStruct((HG_HEADS, m, HG_DV), BF16)],
        scratch_shapes=[pltpu.VMEM((FR_TT, D_MODEL), BF16),
                        pltpu.VMEM((4 * HG_HEADS, FR_TT, HG_DK), BF16),
                        pltpu.VMEM((4 * HG_HEADS, FR_TT, HG_DK), BF16),
                        pltpu.VMEM((HG_HEADS, HG_DV, HG_DK), F32),
                        pltpu.VMEM((HG_GROUP, HG_C, HG_C), BF16),
                        pltpu.VMEM((HG_GROUP, HG_C, HG_DK), BF16),
                        pltpu.VMEM((HG_GROUP, HG_C, HG_DK), BF16),
                        pltpu.VMEM((HG_GROUP, 1, HG_DK), F32)],
        compiler_params=pltpu.CompilerParams(
            dimension_semantics=("arbitrary",), vmem_limit_bytes=VMEM_LIMIT),
        name="front",
    )(x2d, norm_w, w3, lb, hnw)


def _back_kernel(sink_ref, q0_ref, q1_ref, kvp_ref, kv_ref, ag0_ref, ag1_ref, gh_ref, mh_ref, ma_ref,
                 x_ref, wbh32_ref, wba32_ref, wo32_ref, fnw_ref, o_ref,
                 kc_ref, vx_ref, ga_ref, gap_ref, wbh_ref, wba_ref, wo_ref, acc_ref, mg_ref,
                 *, n_tiles, tiles_per_seq):
    W, DH = WINDOW, ATT_DH
    half_heads = ATT_Q_HEADS // 2
    HALF = D_MODEL // 2
    QUARTER = D_MODEL // 4
    j = pl.program_id(0)

    @pl.when(j == 0)
    def _():
        for hh in range(2):
            wbh_ref[hh] = wbh32_ref[:, hh * HALF:(hh + 1) * HALF].astype(BF16)
            wba_ref[hh] = wba32_ref[:, hh * HALF:(hh + 1) * HALF].astype(BF16)
            wo_ref[hh] = wo32_ref[hh * HALF:(hh + 1) * HALF, :].astype(BF16)
        ga_ref[...] = jnp.zeros_like(ga_ref)

    gap_ref[...] = ga_ref[...]
    acc_ref[...] = x_ref[...]

    kc_ref[0:W, :] = kvp_ref[:, :KV_WIDTH]
    kc_ref[W:, :] = kv_ref[:, :KV_WIDTH]
    lane = lax.broadcasted_iota(jnp.int32, (1, 2 * DH), 1)
    low = lane < DH
    for rows, src in ((slice(0, W), kvp_ref), (slice(W, None), kv_ref)):
        for slab in range(ATT_KV_HEADS // 2):
            x = src[:, KV_WIDTH + slab * 2 * DH:KV_WIDTH + (slab + 1) * 2 * DH]
            xr = jnp.concatenate([x[:, DH:], x[:, :DH]], axis=1)
            zero = jnp.zeros_like(x)
            for t, piece in enumerate((jnp.where(low, x, zero), jnp.where(low, zero, xr),
                                       jnp.where(low, xr, zero), jnp.where(low, zero, x))):
                vx_ref[rows, (4 * slab + t) * 2 * DH:(4 * slab + t + 1) * 2 * DH] = piece
    ones_lo = jnp.broadcast_to(jnp.where(low, 1.0, 0.0).astype(BF16), (2 * W, 2 * DH))
    ones_hi = jnp.broadcast_to(jnp.where(low, 0.0, 1.0).astype(BF16), (2 * W, 2 * DH))
    low_rows = jnp.broadcast_to(low, (W, 2 * DH))

    first_tile = lax.rem(jnp.minimum(j, n_tiles - 1), tiles_per_seq) == 0
    qi = lax.broadcasted_iota(jnp.int32, (W, W), 0)
    kj = lax.broadcasted_iota(jnp.int32, (W, W), 1)
    upper = kj > qi

    def head_cols(ref_pair, hd, width):
        ref = ref_pair[hd // half_heads]
        hd = hd % half_heads
        return ref, slice(hd * DH, (hd + width) * DH)

    def half(hh, carry):
        gh = jnp.concatenate([gh_ref[h] for h in range(HG_HEADS)], axis=1)
        y_parts = {}

        def branch_piece(name, lhs, w_ref, k):
            y_parts[name, k] = jnp.dot(lhs, w_ref[hh, :, k * QUARTER:(k + 1) * QUARTER],
                                       preferred_element_type=F32)

        def gate():
            yh = jnp.concatenate([y_parts["h", 0], y_parts["h", 1]], axis=1)
            ya = jnp.concatenate([y_parts["a", 0], y_parts["a", 1]], axis=1)
            mg_ref[...] = (_sigmoid(mh_ref[hh].astype(F32)) * yh
                           + _sigmoid(ma_ref[hh].astype(F32)) * ya).astype(BF16)

        def out_piece(k):
            cols = slice(k * QUARTER, (k + 1) * QUARTER)
            acc_ref[:, cols] += jnp.dot(mg_ref[...], wo_ref[hh, :, cols], preferred_element_type=F32)

        pieces = [lambda: branch_piece("h", gh, wbh_ref, 0), lambda: branch_piece("h", gh, wbh_ref, 1),
                  lambda: branch_piece("a", gap_ref[...], wba_ref, 0),
                  lambda: branch_piece("a", gap_ref[...], wba_ref, 1),
                  lambda: (gate(), out_piece(0)), lambda: out_piece(1), lambda: out_piece(2),
                  lambda: out_piece(3)]

        units = [(bi, h) for bi in range(2) for h in range(ATT_KV_HEADS)]

        def scores(bi, h):
            r = pl.multiple_of((2 * hh + bi) * W, W)
            kk = kc_ref[pl.ds(r, 2 * W), h * DH:(h + 1) * DH]
            parts = []
            for hd in range(h * ATT_GROUP, (h + 1) * ATT_GROUP):
                ref, cols = head_cols((q0_ref, q1_ref), hd, 1)
                parts.append(ref[pl.ds(r, W), cols])
            return _nt(jnp.concatenate(parts, axis=0), kk)

        s_next = scores(*units[0])
        for idx, (bi, h) in enumerate(units):
            r = pl.multiple_of((2 * hh + bi) * W, W)
            s = s_next
            if idx + 1 < len(units):
                s_next = scores(*units[idx + 1])
            pieces[idx]()
            prev_bias = jnp.where(first_tile & (2 * hh + bi == 0), -jnp.inf, 0.0)
            probs, sink_terms = [], []
            for jj, hd in enumerate(range(h * ATT_GROUP, (h + 1) * ATT_GROUP)):
                sj = s[jj * W:(jj + 1) * W]
                c = jnp.where(upper, sj[:, :W] + prev_bias, sj[:, W:])
                sink = sink_ref[hd]
                m = jnp.maximum(jnp.max(c, axis=-1, keepdims=True), sink)
                p = jnp.exp(c - m)
                sink_terms.append(jnp.exp(sink - m))
                probs.append(jnp.concatenate(
                    [jnp.where(upper, p, 0.0), jnp.where(upper, 0.0, p)], axis=1).astype(BF16))
            w_lo = jnp.concatenate(
                [vx_ref[pl.ds(r, 2 * W), (2 * h) * 2 * DH:(2 * h + 1) * 2 * DH], ones_lo], axis=1)
            w_hi = jnp.concatenate(
                [vx_ref[pl.ds(r, 2 * W), (2 * h + 1) * 2 * DH:(2 * h + 2) * 2 * DH], ones_hi], axis=1)
            res = (jnp.dot(jnp.concatenate(probs[0::2], axis=0), w_lo, preferred_element_type=F32)
                   + jnp.dot(jnp.concatenate(probs[1::2], axis=0), w_hi, preferred_element_type=F32))
            for pair in range(ATT_GROUP // 2):
                rp = res[pair * W:(pair + 1) * W]
                den = rp[:, 2 * DH:] + jnp.where(low_rows, sink_terms[2 * pair], sink_terms[2 * pair + 1])
                j0 = h * ATT_GROUP + 2 * pair
                ag_ref, cols = head_cols((ag0_ref, ag1_ref), j0, 2)
                ag = ag_ref[pl.ds(r, W), cols].astype(F32)
                ga_ref[pl.ds(r, W), j0 * DH:(j0 + 2) * DH] = (
                    rp[:, :2 * DH] / den * (ag * _sigmoid(ag))).astype(BF16)
        return carry

    lax.fori_loop(0, 2, half, 0)

    xo = acc_ref[...]
    ms = jnp.mean(xo * xo, axis=-1, keepdims=True)
    o_ref[...] = xo * lax.rsqrt(ms + EPS) * fnw_ref[...]


def _back(pj, gh3, x2d, sinks, wbh, wba, wo, fnw, seq):
    m = x2d.shape[0]
    nt = m // BK_TT
    last = nt - 1
    per = BK_TT // WINDOW

    def cur(jj):
        return jnp.minimum(jj, last)

    def prv(jj):
        return jnp.maximum(jj - 1, 0)

    def chunk(c):
        return pl.BlockSpec((None, BK_TT, NCH), lambda jj, s: (c, cur(jj), 0))

    def pair(c):
        return pl.BlockSpec((2, BK_TT, NCH), lambda jj, s: (c // 2, prv(jj), 0))

    prev_kv = pl.BlockSpec((None, WINDOW, NCH),
                           lambda jj, s: (PJ_KV, jnp.maximum(cur(jj) * per - 1, 0), 0))
    tile = pl.BlockSpec((BK_TT, D_MODEL), lambda jj, s: (prv(jj), 0))
    wspec = pl.BlockSpec((D_MODEL, D_MODEL), lambda jj, s: (0, 0), pipeline_mode=pl.Buffered(1))
    half = D_MODEL // 2
    grid_spec = pltpu.PrefetchScalarGridSpec(
        num_scalar_prefetch=1,
        grid=(nt + 1,),
        in_specs=[chunk(PJ_AQ), chunk(PJ_AQ + 1), prev_kv, chunk(PJ_KV), chunk(PJ_AG), chunk(PJ_AG + 1),
                  pl.BlockSpec((HG_HEADS, BK_TT, HG_DV), lambda jj, s: (0, prv(jj), 0)),
                  pair(PJ_MH), pair(PJ_MA), tile, wspec, wspec, wspec,
                  pl.BlockSpec((1, D_MODEL), lambda jj, s: (0, 0))],
        out_specs=tile,
        scratch_shapes=[pltpu.VMEM((BK_TT + WINDOW, KV_WIDTH), BF16),
                        pltpu.VMEM((BK_TT + WINDOW, ATT_KV_HEADS * 4 * ATT_DH), BF16),
                        pltpu.VMEM((BK_TT, D_MODEL), BF16),
                        pltpu.VMEM((BK_TT, D_MODEL), BF16),
                        pltpu.VMEM((2, D_MODEL, half), BF16),
                        pltpu.VMEM((2, D_MODEL, half), BF16),
                        pltpu.VMEM((2, half, D_MODEL), BF16),
                        pltpu.VMEM((BK_TT, D_MODEL), F32),
                        pltpu.VMEM((BK_TT, half), BF16)],
    )
    return pl.pallas_call(
        functools.partial(_back_kernel, n_tiles=nt, tiles_per_seq=seq // BK_TT),
        grid_spec=grid_spec,
        out_shape=jax.ShapeDtypeStruct((m, D_MODEL), F32),
        compiler_params=pltpu.CompilerParams(
            dimension_semantics=("arbitrary",), vmem_limit_bytes=VMEM_LIMIT),
        name="back",
    )(sinks, pj, pj, pj, pj, pj, pj, gh3, pj, pj, x2d, wbh, wba, wo, fnw)


def kernel(x, norm_w, w_in, hgrn_lower_bound, hgrn_norm_w, w_branch_hgrn, attn_sinks,
           w_branch_attn, w_out, final_norm_w):
    batch, seq, _ = x.shape
    depth = norm_w.shape[0]
    assert depth == 1, "the back kernel fuses the final RMSNorm into the single layer"
    assert seq % FR_TT == 0 and seq % BK_TT == 0
    lb_all = jnp.cumsum(jax.nn.softmax(hgrn_lower_bound.astype(F32), axis=0), axis=0)
    w3 = _wprep(w_in[0])
    x2d = x.reshape(batch * seq, D_MODEL)
    pj, gh3 = _front(x2d, norm_w[0].reshape(1, D_MODEL), w3,
                     lb_all[0].reshape(HG_HEADS, 1, HG_DK),
                     hgrn_norm_w[0].reshape(HG_HEADS, 1, HG_DV), seq)
    xo = _back(pj, gh3, x2d, attn_sinks[0].astype(F32), w_branch_hgrn[0], w_branch_attn[0], w_out[0],
               final_norm_w.reshape(1, D_MODEL), seq)
    return xo.reshape(batch, seq, D_MODEL)
```

```python
import functools

import jax
import jax.numpy as jnp
from jax import lax
from jax.experimental import pallas as pl
from jax.experimental.pallas import tpu as pltpu

F32 = jnp.float32
BF16 = jnp.bfloat16

D_MODEL = 1024
HG_HEADS = 8
HG_DK = 128
HG_DV = 128
ATT_Q_HEADS = 16
ATT_KV_HEADS = 4
ATT_GROUP = ATT_Q_HEADS // ATT_KV_HEADS
ATT_DH = 64
KV_WIDTH = ATT_KV_HEADS * ATT_DH
ATT_SCALE = ATT_DH ** -0.5
assert ATT_SCALE == 0.125
WINDOW = 128
EPS = 1e-6
D_IN = 8 * D_MODEL + 2 * KV_WIDTH

NCH = 512
N_CHUNKS = D_IN // NCH
N_HG_CHUNKS = 4 * D_MODEL // NCH
N_PJ_CHUNKS = N_CHUNKS - N_HG_CHUNKS
PJ_AQ, PJ_AG, PJ_MH, PJ_MA, PJ_KV = 0, 2, 4, 6, 8
SLABS_PER_CHUNK = NCH // HG_DK
SEC_HQ, SEC_HF, SEC_HI, SEC_HG = range(4)

VMEM_LIMIT = 56 * 1024 * 1024

HG_C = 64
HG_SUB = 16
FR_TT = 512
HG_GROUP = FR_TT // HG_C
HG_UNITS = HG_GROUP // 2
SUBLANES = 8
BK_TT = 512


def _nt(a, b):
    return lax.dot_general(a.astype(BF16), b.astype(BF16), (((1,), (1,)), ((), ())),
                           preferred_element_type=F32)


def _sigmoid(x):
    return 1.0 / (1.0 + jnp.exp(-x))


def _wprep_kernel(w_ref, o_ref):
    scale = jnp.where(pl.program_id(0) // 2 == N_HG_CHUNKS // 2, ATT_SCALE, 1.0)
    o_ref[...] = (w_ref[...] * scale).astype(BF16)


def _wprep(w):
    kv_src = N_HG_CHUNKS + 2

    def src_chunk(c):
        return jnp.where(c < kv_src, c, jnp.where(c == N_CHUNKS - 1, kv_src, c + 1))

    return pl.pallas_call(
        _wprep_kernel,
        grid=(N_CHUNKS,),
        in_specs=[pl.BlockSpec((D_MODEL, NCH), lambda c: (0, src_chunk(c)))],
        out_specs=pl.BlockSpec((None, D_MODEL, NCH), lambda c: (c, 0, 0)),
        out_shape=jax.ShapeDtypeStruct((N_CHUNKS, D_MODEL, NCH), BF16),
        compiler_params=pltpu.CompilerParams(dimension_semantics=("arbitrary",)),
        name="wprep",
    )(w)


def _front_kernel(x_ref, nw_ref, w_ref, lb_ref, hnw_ref, pj_ref, gh_ref, xn_ref, hb0_ref, hb1_ref,
                  st_ref, p_sc, qb_sc, kb_sc, dec_sc, *, tiles_per_seq):
    C, SUB, NB = HG_C, HG_SUB, HG_C // HG_SUB
    HALF = NCH // 2
    i = pl.program_id(0)
    even = lax.rem(i, 2) == 0

    @pl.when(i == 0)
    def _():
        hb1_ref[...] = jnp.zeros_like(hb1_ref)
        p_sc[...] = jnp.zeros_like(p_sc)
        qb_sc[...] = jnp.zeros_like(qb_sc)
        kb_sc[...] = jnp.zeros_like(kb_sc)
        dec_sc[...] = jnp.zeros_like(dec_sc)

    @pl.when((i == 0) | (lax.rem(jnp.maximum(i - 1, 0), tiles_per_seq) == 0))
    def _():
        st_ref[...] = jnp.zeros_like(st_ref)

    x = x_ref[...]
    ms = jnp.mean(x * x, axis=-1, keepdims=True)
    xn_ref[...] = (x * lax.rsqrt(ms + EPS) * nw_ref[...]).astype(BF16)

    row = lax.broadcasted_iota(jnp.int32, (C, C), 0)
    col = lax.broadcasted_iota(jnp.int32, (C, C), 1)
    rb, cb = row >> 4, col >> 4
    m0 = (rb == cb) & (col <= row)
    m1 = ((rb & 1) == 1) & (cb == rb - 1)
    m2 = (row >= 2 * SUB) & (col < 2 * SUB)
    rows = [slice(c * C, (c + 1) * C) for c in range(HG_GROUP)]
    urows = [slice(u * 2 * C, (u + 1) * 2 * C) for u in range(HG_UNITS)]
    sub_row = lax.broadcasted_iota(jnp.int32, (SUBLANES, HG_DK), 0)

    def cat(blocks):
        return jnp.concatenate(blocks, axis=0).astype(BF16)

    def cumsum_rows(g):
        groups = []
        for v in range(C // SUBLANES):
            xg = g[v * SUBLANES:(v + 1) * SUBLANES]
            shift = 1
            while shift < SUBLANES:
                xg = xg + jnp.where(sub_row >= shift, pltpu.roll(xg, shift, axis=0), 0.0)
                shift *= 2
            groups.append(xg)
        out, carry = [groups[0]], groups[0][SUBLANES - 1:SUBLANES]
        for xg in groups[1:]:
            out.append(xg + carry)
            carry = carry + xg[SUBLANES - 1:SUBLANES]
        return jnp.concatenate(out, axis=0)

    def slot(head_a, abuf, head_b, chunks, wbuf, rbuf):
        pieces = [(chunk, half) for chunk in chunks for half in range(2)]

        def project(flush=False):
            if not pieces:
                return
            chunk, half = pieces.pop(0)
            acc = jnp.dot(xn_ref[...], w_ref[chunk, :, half * HALF:(half + 1) * HALF],
                          preferred_element_type=F32).astype(BF16)
            if wbuf is None:
                pj_ref[chunk - N_HG_CHUNKS, :, half * HALF:(half + 1) * HALF] = acc
            else:
                for s in range(SLABS_PER_CHUNK // 2):
                    slab = chunk * SLABS_PER_CHUNK + half * (SLABS_PER_CHUNK // 2) + s
                    wbuf[slab] = acc[:, s * HG_DK:(s + 1) * HG_DK]
            if flush:
                project(True)

        def src(sec, head, buf=rbuf):
            return buf.at[sec * HG_HEADS + head]

        v_b = [src(SEC_HI, head_b)[urows[u], :] for u in range(HG_UNITS)]
        p_b = [p_sc[u] for u in range(HG_UNITS)]
        qb_b = [qb_sc[u] for u in range(HG_UNITS)]
        kb_b = [kb_sc[u] for u in range(HG_UNITS)]
        dec_b = [dec_sc[u] for u in range(HG_UNITS)]

        project()

        lb = lb_ref[head_a]
        q, k, b = [], [], []
        for c in range(HG_GROUP):
            xq = src(SEC_HQ, head_a, abuf)[rows[c], :].astype(F32)
            xf = src(SEC_HF, head_a, abuf)[rows[c], :].astype(F32)
            f = lb + (1.0 - lb) * _sigmoid(xf)
            k.append(1.0 - f)
            q.append(xq * _sigmoid(xq))
            b.append(cumsum_rows(jnp.log(f)))

        o_intra = [jnp.dot(p_b[u], v_b[u], preferred_element_type=F32) for u in range(HG_UNITS)]
        kv = [lax.dot_general(v_b[u], kb_b[u], (((0,), (0,)), ((), ())),
                              preferred_element_type=F32) for u in range(HG_UNITS)]

        project()

        st = st_ref[head_b]
        o = []
        for u in range(HG_UNITS):
            o.append(o_intra[u] + _nt(qb_b[u], st))
            st = st * dec_b[u] + kv[u]
        st_ref[head_b] = st

        project()

        p_c, qb_c, kb_c, last_c = [], [], [], []
        for c in range(HG_GROUP):
            ends = [b[c][(j + 1) * SUB - 1:(j + 1) * SUB] for j in range(NB)]
            starts = [jnp.zeros_like(ends[0])] + ends[:-1]
            last = ends[-1]
            q1, k1, qd, kd, qb, kb = [], [], [], [], [], []
            for j in range(NB):
                sl = slice(j * SUB, (j + 1) * SUB)
                bj = b[c][sl]
                q1j = q[c][sl] * jnp.exp(bj - starts[j])
                k1j = k[c][sl] * jnp.exp(ends[j] - bj)
                half = jnp.exp(0.5 * (starts[j] - ends[j]))
                q1.append(q1j)
                k1.append(k1j)
                qd.append(q1j * half)
                kd.append(k1j * half)
                qb.append(q1j * jnp.exp(starts[j]))
                kb.append(k1j * jnp.exp(last - ends[j]))
            q2 = q1[:3] + [q1[3] * jnp.exp(ends[2] - ends[1])]
            k2 = [k1[0] * jnp.exp(ends[1] - ends[0])] + k1[1:]
            s0 = _nt(cat(qd), cat(kd))
            s1 = _nt(cat(q1), cat(k1))
            s2 = _nt(cat(q2), cat(k2))
            p_c.append(jnp.where(m0, s0, jnp.where(m1, s1, jnp.where(m2, s2, 0.0))).astype(BF16))
            qb_c.append(jnp.concatenate(qb, axis=0))
            kb_c.append(jnp.concatenate(kb, axis=0))
            last_c.append(last)

        p_a, qb_a, kb_a, dec_a = [], [], [], []
        for u in range(HG_UNITS):
            c0, c1 = 2 * u, 2 * u + 1
            cross = _nt(qb_c[c1], kb_c[c0]).astype(BF16)
            p_a.append(jnp.concatenate(
                [jnp.concatenate([p_c[c0], jnp.zeros((C, C), BF16)], axis=1),
                 jnp.concatenate([cross, p_c[c1]], axis=1)], axis=0))
            qb_a.append(cat([qb_c[c0], qb_c[c1] * jnp.exp(last_c[c0])]))
            kb_a.append(cat([kb_c[c0] * jnp.exp(last_c[c1]), kb_c[c1]]))
            dec_a.append(jnp.exp(last_c[c0] + last_c[c1]))

        project(flush=True)

        nw = hnw_ref[head_b]
        for u in range(HG_UNITS):
            xg = src(SEC_HG, head_b)[urows[u], :].astype(F32)
            ms_o = jnp.mean(o[u] * o[u], axis=-1, keepdims=True)
            on = o[u] * lax.rsqrt(ms_o + EPS) * nw
            gh_ref[head_b, urows[u], :] = (on * (xg * _sigmoid(xg))).astype(BF16)

        for u in range(HG_UNITS):
            p_sc[u] = p_a[u]
            qb_sc[u] = qb_a[u]
            kb_sc[u] = kb_a[u]
            dec_sc[u] = dec_a[u]

    def run(wbuf, rbuf):
        def mid(s, carry, to_scratch):
            slot(s, rbuf, s - 1, [2 * (s - 1), 2 * (s - 1) + 1], wbuf if to_scratch else None, rbuf)
            return carry

        n_scratch_slots = N_HG_CHUNKS // 2
        lax.fori_loop(1, 1 + n_scratch_slots, functools.partial(mid, to_scratch=True), 0)
        lax.fori_loop(1 + n_scratch_slots, HG_HEADS, functools.partial(mid, to_scratch=False), 0)
        slot(0, wbuf, HG_HEADS - 1, list(range(2 * (HG_HEADS - 1), N_CHUNKS)), None, rbuf)

    @pl.when(even)
    def _():
        run(hb0_ref, hb1_ref)

    @pl.when(jnp.logical_not(even))
    def _():
        run(hb1_ref, hb0_ref)


def _front(x2d, norm_w, w3, lb, hnw, seq):
    m = x2d.shape[0]
    nt = m // FR_TT
    last = nt - 1
    return pl.pallas_call(
        functools.partial(_front_kernel, tiles_per_seq=seq // FR_TT),
        grid=(nt + 1,),
        in_specs=[
            pl.BlockSpec((FR_TT, D_MODEL), lambda i: (jnp.minimum(i, last), 0)),
            pl.BlockSpec((1, D_MODEL), lambda i: (0, 0)),
            pl.BlockSpec((N_CHUNKS, D_MODEL, NCH), lambda i: (0, 0, 0), pipeline_mode=pl.Buffered(1)),
            pl.BlockSpec((HG_HEADS, 1, HG_DK), lambda i: (0, 0, 0)),
            pl.BlockSpec((HG_HEADS, 1, HG_DV), lambda i: (0, 0, 0)),
        ],
        out_specs=[
            pl.BlockSpec((N_PJ_CHUNKS, FR_TT, NCH), lambda i: (0, jnp.minimum(i, last), 0)),
            pl.BlockSpec((HG_HEADS, FR_TT, HG_DV), lambda i: (0, jnp.maximum(i - 1, 0), 0)),
        ],
        out_shape=[jax.ShapeDtypeStruct((N_PJ_CHUNKS, m, NCH), BF16),
                   jax.ShapeDtypeStruct((HG_HEADS, m, HG_DV), BF16)],
        scratch_shapes=[pltpu.VMEM((FR_TT, D_MODEL), BF16),
                        pltpu.VMEM((4 * HG_HEADS, FR_TT, HG_DK), BF16),
                        pltpu.VMEM((4 * HG_HEADS, FR_TT, HG_DK), BF16),
                        pltpu.VMEM((HG_HEADS, HG_DV, HG_DK), F32),
                        pltpu.VMEM((HG_UNITS, 2 * HG_C, 2 * HG_C), BF16),
                        pltpu.VMEM((HG_UNITS, 2 * HG_C, HG_DK), BF16),
                        pltpu.VMEM((HG_UNITS, 2 * HG_C, HG_DK), BF16),
                        pltpu.VMEM((HG_UNITS, 1, HG_DK), F32)],
        compiler_params=pltpu.CompilerParams(
            dimension_semantics=("arbitrary",), vmem_limit_bytes=VMEM_LIMIT),
        name="front",
    )(x2d, norm_w, w3, lb, hnw)


def _back_kernel(sink_ref, q0_ref, q1_ref, kvp_ref, kv_ref, ag0_ref, ag1_ref, gh_ref, mh_ref, ma_ref,
                 x_ref, wbh32_ref, wba32_ref, wo32_ref, fnw_ref, o_ref,
                 kc_ref, vx_ref, ga_ref, gap_ref, wbh_ref, wba_ref, wo_ref, acc_ref, mg_ref,
                 *, n_tiles, tiles_per_seq):
    W, DH = WINDOW, ATT_DH
    half_heads = ATT_Q_HEADS // 2
    HALF = D_MODEL // 2
    QUARTER = D_MODEL // 4
    j = pl.program_id(0)

    @pl.when(j == 0)
    def _():
        for hh in range(2):
            wbh_ref[hh] = wbh32_ref[:, hh * HALF:(hh + 1) * HALF].astype(BF16)
            wba_ref[hh] = wba32_ref[:, hh * HALF:(hh + 1) * HALF].astype(BF16)
            wo_ref[hh] = wo32_ref[hh * HALF:(hh + 1) * HALF, :].astype(BF16)
        ga_ref[...] = jnp.zeros_like(ga_ref)

    gap_ref[...] = ga_ref[...]
    acc_ref[...] = x_ref[...]

    kc_ref[0:W, :] = kvp_ref[:, :KV_WIDTH]
    kc_ref[W:, :] = kv_ref[:, :KV_WIDTH]
    lane = lax.broadcasted_iota(jnp.int32, (1, 2 * DH), 1)
    low = lane < DH
    for rows, src in ((slice(0, W), kvp_ref), (slice(W, None), kv_ref)):
        for slab in range(ATT_KV_HEADS // 2):
            x = src[:, KV_WIDTH + slab * 2 * DH:KV_WIDTH + (slab + 1) * 2 * DH]
            xr = jnp.concatenate([x[:, DH:], x[:, :DH]], axis=1)
            zero = jnp.zeros_like(x)
            for t, piece in enumerate((jnp.where(low, x, zero), jnp.where(low, zero, xr),
                                       jnp.where(low, xr, zero), jnp.where(low, zero, x))):
                vx_ref[rows, (4 * slab + t) * 2 * DH:(4 * slab + t + 1) * 2 * DH] = piece
    ones_lo = jnp.broadcast_to(jnp.where(low, 1.0, 0.0).astype(BF16), (2 * W, 2 * DH))
    ones_hi = jnp.broadcast_to(jnp.where(low, 0.0, 1.0).astype(BF16), (2 * W, 2 * DH))
    low_rows = jnp.broadcast_to(low, (W, 2 * DH))

    first_tile = lax.rem(jnp.minimum(j, n_tiles - 1), tiles_per_seq) == 0
    qi = lax.broadcasted_iota(jnp.int32, (W, W), 0)
    kj = lax.broadcasted_iota(jnp.int32, (W, W), 1)
    upper = kj > qi

    def head_cols(ref_pair, hd, width):
        ref = ref_pair[hd // half_heads]
        hd = hd % half_heads
        return ref, slice(hd * DH, (hd + width) * DH)

    def half(hh, carry):
        gh = jnp.concatenate([gh_ref[h] for h in range(HG_HEADS)], axis=1)
        y_parts = {}

        def branch_piece(name, lhs, w_ref, k):
            y_parts[name, k] = jnp.dot(lhs, w_ref[hh, :, k * QUARTER:(k + 1) * QUARTER],
                                       preferred_element_type=F32)

        def gate():
            yh = jnp.concatenate([y_parts["h", 0], y_parts["h", 1]], axis=1)
            ya = jnp.concatenate([y_parts["a", 0], y_parts["a", 1]], axis=1)
            mg_ref[...] = (_sigmoid(mh_ref[hh].astype(F32)) * yh
                           + _sigmoid(ma_ref[hh].astype(F32)) * ya).astype(BF16)

        def out_piece(k):
            cols = slice(k * QUARTER, (k + 1) * QUARTER)
            acc_ref[:, cols] += jnp.dot(mg_ref[...], wo_ref[hh, :, cols], preferred_element_type=F32)

        pieces = [lambda: branch_piece("h", gh, wbh_ref, 0), lambda: branch_piece("h", gh, wbh_ref, 1),
                  lambda: branch_piece("a", gap_ref[...], wba_ref, 0),
                  lambda: branch_piece("a", gap_ref[...], wba_ref, 1),
                  lambda: (gate(), out_piece(0)), lambda: out_piece(1), lambda: out_piece(2),
                  lambda: out_piece(3)]

        units = [(bi, h) for bi in range(2) for h in range(ATT_KV_HEADS)]

        def scores(bi, h):
            r = pl.multiple_of((2 * hh + bi) * W, W)
            kk = kc_ref[pl.ds(r, 2 * W), h * DH:(h + 1) * DH]
            parts = []
            for hd in range(h * ATT_GROUP, (h + 1) * ATT_GROUP):
                ref, cols = head_cols((q0_ref, q1_ref), hd, 1)
                parts.append(ref[pl.ds(r, W), cols])
            return _nt(jnp.concatenate(parts, axis=0), kk)

        s_next = scores(*units[0])
        for idx, (bi, h) in enumerate(units):
            r = pl.multiple_of((2 * hh + bi) * W, W)
            s = s_next
            if idx + 1 < len(units):
                s_next = scores(*units[idx + 1])
            pieces[idx]()
            prev_bias = jnp.where(first_tile & (2 * hh + bi == 0), -jnp.inf, 0.0)
            probs, sink_terms = [], []
            for jj, hd in enumerate(range(h * ATT_GROUP, (h + 1) * ATT_GROUP)):
                sj = s[jj * W:(jj + 1) * W]
                c = jnp.where(upper, sj[:, :W] + prev_bias, sj[:, W:])
                sink = sink_ref[hd]
                m = jnp.maximum(jnp.max(c, axis=-1, keepdims=True), sink)
                p = jnp.exp(c - m)
                sink_terms.append(jnp.exp(sink - m))
                probs.append(jnp.concatenate(
                    [jnp.where(upper, p, 0.0), jnp.where(upper, 0.0, p)], axis=1).astype(BF16))
            w_lo = jnp.concatenate(
                [vx_ref[pl.ds(r, 2 * W), (2 * h) * 2 * DH:(2 * h + 1) * 2 * DH], ones_lo], axis=1)
            w_hi = jnp.concatenate(
                [vx_ref[pl.ds(r, 2 * W), (2 * h + 1) * 2 * DH:(2 * h + 2) * 2 * DH], ones_hi], axis=1)
            res = (jnp.dot(jnp.concatenate(probs[0::2], axis=0), w_lo, preferred_element_type=F32)
                   + jnp.dot(jnp.concatenate(probs[1::2], axis=0), w_hi, preferred_element_type=F32))
            for pair in range(ATT_GROUP // 2):
                rp = res[pair * W:(pair + 1) * W]
                den = rp[:, 2 * DH:] + jnp.where(low_rows, sink_terms[2 * pair], sink_terms[2 * pair + 1])
                j0 = h * ATT_GROUP + 2 * pair
                ag_ref, cols = head_cols((ag0_ref, ag1_ref), j0, 2)
                ag = ag_ref[pl.ds(r, W), cols].astype(F32)
                ga_ref[pl.ds(r, W), j0 * DH:(j0 + 2) * DH] = (
                    rp[:, :2 * DH] / den * (ag * _sigmoid(ag))).astype(BF16)
        return carry

    lax.fori_loop(0, 2, half, 0)

    xo = acc_ref[...]
    ms = jnp.mean(xo * xo, axis=-1, keepdims=True)
    o_ref[...] = xo * lax.rsqrt(ms + EPS) * fnw_ref[...]


def _back(pj, gh3, x2d, sinks, wbh, wba, wo, fnw, seq):
    m = x2d.shape[0]
    nt = m // BK_TT
    last = nt - 1
    per = BK_TT // WINDOW

    def cur(jj):
        return jnp.minimum(jj, last)

    def prv(jj):
        return jnp.maximum(jj - 1, 0)

    def chunk(c):
        return pl.BlockSpec((None, BK_TT, NCH), lambda jj, s: (c, cur(jj), 0))

    def pair(c):
        return pl.BlockSpec((2, BK_TT, NCH), lambda jj, s: (c // 2, prv(jj), 0))

    prev_kv = pl.BlockSpec((None, WINDOW, NCH),
                           lambda jj, s: (PJ_KV, jnp.maximum(cur(jj) * per - 1, 0), 0))
    tile = pl.BlockSpec((BK_TT, D_MODEL), lambda jj, s: (prv(jj), 0))
    wspec = pl.BlockSpec((D_MODEL, D_MODEL), lambda jj, s: (0, 0), pipeline_mode=pl.Buffered(1))
    half = D_MODEL // 2
    grid_spec = pltpu.PrefetchScalarGridSpec(
        num_scalar_prefetch=1,
        grid=(nt + 1,),
        in_specs=[chunk(PJ_AQ), chunk(PJ_AQ + 1), prev_kv, chunk(PJ_KV), chunk(PJ_AG), chunk(PJ_AG + 1),
                  pl.BlockSpec((HG_HEADS, BK_TT, HG_DV), lambda jj, s: (0, prv(jj), 0)),
                  pair(PJ_MH), pair(PJ_MA), tile, wspec, wspec, wspec,
                  pl.BlockSpec((1, D_MODEL), lambda jj, s: (0, 0))],
        out_specs=tile,
        scratch_shapes=[pltpu.VMEM((BK_TT + WINDOW, KV_WIDTH), BF16),
                        pltpu.VMEM((BK_TT + WINDOW, ATT_KV_HEADS * 4 * ATT_DH), BF16),
                        pltpu.VMEM((BK_TT, D_MODEL), BF16),
                        pltpu.VMEM((BK_TT, D_MODEL), BF16),
                        pltpu.VMEM((2, D_MODEL, half), BF16),
                        pltpu.VMEM((2, D_MODEL, half), BF16),
                        pltpu.VMEM((2, half, D_MODEL), BF16),
                        pltpu.VMEM((BK_TT, D_MODEL), F32),
                        pltpu.VMEM((BK_TT, half), BF16)],
    )
    return pl.pallas_call(
        functools.partial(_back_kernel, n_tiles=nt, tiles_per_seq=seq // BK_TT),
        grid_spec=grid_spec,
        out_shape=jax.ShapeDtypeStruct((m, D_MODEL), F32),
        compiler_params=pltpu.CompilerParams(
            dimension_semantics=("arbitrary",), vmem_limit_bytes=VMEM_LIMIT),
        name="back",
    )(sinks, pj, pj, pj, pj, pj, pj, gh3, pj, pj, x2d, wbh, wba, wo, fnw)


def kernel(x, norm_w, w_in, hgrn_lower_bound, hgrn_norm_w, w_branch_hgrn, attn_sinks,
           w_branch_attn, w_out, final_norm_w):
    batch, seq, _ = x.shape
    depth = norm_w.shape[0]
    assert depth == 1, "the back kernel fuses the final RMSNorm into the single layer"
    assert seq % FR_TT == 0 and seq % BK_TT == 0
    lb_all = jnp.cumsum(jax.nn.softmax(hgrn_lower_bound.astype(F32), axis=0), axis=0)
    w3 = _wprep(w_in[0])
    x2d = x.reshape(batch * seq, D_MODEL)
    pj, gh3 = _front(x2d, norm_w[0].reshape(1, D_MODEL), w3,
                     lb_all[0].reshape(HG_HEADS, 1, HG_DK),
                     hgrn_norm_w[0].reshape(HG_HEADS, 1, HG_DV), seq)
    xo = _back(pj, gh3, x2d, attn_sinks[0].astype(F32), w_branch_hgrn[0], w_branch_attn[0], w_out[0],
               final_norm_w.reshape(1, D_MODEL), seq)
    return xo.reshape(batch, seq, D_MODEL)
```

```python
import functools

import jax
import jax.numpy as jnp
from jax import lax
from jax.experimental import pallas as pl
from jax.experimental.pallas import tpu as pltpu

F32 = jnp.float32
BF16 = jnp.bfloat16

D_MODEL = 1024
HG_HEADS = 8
HG_DK = 128
HG_DV = 128
ATT_Q_HEADS = 16
ATT_KV_HEADS = 4
ATT_GROUP = ATT_Q_HEADS // ATT_KV_HEADS
ATT_DH = 64
KV_WIDTH = ATT_KV_HEADS * ATT_DH
ATT_SCALE = ATT_DH ** -0.5
assert ATT_SCALE == 0.125
WINDOW = 128
EPS = 1e-6
D_IN = 8 * D_MODEL + 2 * KV_WIDTH

NCH = 512
N_CHUNKS = D_IN // NCH
N_HG_CHUNKS = 4 * D_MODEL // NCH
N_PJ_CHUNKS = N_CHUNKS - N_HG_CHUNKS
PJ_AQ, PJ_AG, PJ_MH, PJ_MA, PJ_KV = 0, 2, 4, 6, 8
SLABS_PER_CHUNK = NCH // HG_DK
SEC_HQ, SEC_HF, SEC_HI, SEC_HG = range(4)

VMEM_LIMIT = 56 * 1024 * 1024

HG_C = 64
HG_SUB = 16
FR_TT = 512
HG_GROUP = FR_TT // HG_C
HG_UNITS = HG_GROUP // 2
SUBLANES = 8
BK_TT = 512


def _nt(a, b):
    return lax.dot_general(a.astype(BF16), b.astype(BF16), (((1,), (1,)), ((), ())),
                           preferred_element_type=F32)


def _sigmoid(x):
    return 1.0 / (1.0 + jnp.exp(-x))


def _wprep_kernel(w_ref, o_ref):
    scale = jnp.where(pl.program_id(0) // 2 == N_HG_CHUNKS // 2, ATT_SCALE, 1.0)
    o_ref[...] = (w_ref[...] * scale).astype(BF16)


def _wprep(w):
    kv_src = N_HG_CHUNKS + 2

    def src_chunk(c):
        return jnp.where(c < kv_src, c, jnp.where(c == N_CHUNKS - 1, kv_src, c + 1))

    return pl.pallas_call(
        _wprep_kernel,
        grid=(N_CHUNKS,),
        in_specs=[pl.BlockSpec((D_MODEL, NCH), lambda c: (0, src_chunk(c)))],
        out_specs=pl.BlockSpec((None, D_MODEL, NCH), lambda c: (c, 0, 0)),
        out_shape=jax.ShapeDtypeStruct((N_CHUNKS, D_MODEL, NCH), BF16),
        compiler_params=pltpu.CompilerParams(dimension_semantics=("arbitrary",)),
        name="wprep",
    )(w)


def _front_kernel(x_ref, xnext_ref, nw_ref, w_ref, lb_ref, hnw_ref, pj_ref, gh_ref, xn0_ref, xn1_ref,
                  hb0_ref, hb1_ref, st_ref, p_sc, qb_sc, kb_sc, dec_sc, *, tiles_per_seq):
    C, SUB, NB = HG_C, HG_SUB, HG_C // HG_SUB
    HALF = NCH // 2
    i = pl.program_id(0)
    even = lax.rem(i, 2) == 0

    @pl.when(i == 0)
    def _():
        hb1_ref[...] = jnp.zeros_like(hb1_ref)
        p_sc[...] = jnp.zeros_like(p_sc)
        qb_sc[...] = jnp.zeros_like(qb_sc)
        kb_sc[...] = jnp.zeros_like(kb_sc)
        dec_sc[...] = jnp.zeros_like(dec_sc)

    @pl.when((i == 0) | (lax.rem(jnp.maximum(i - 1, 0), tiles_per_seq) == 0))
    def _():
        st_ref[...] = jnp.zeros_like(st_ref)

    def normalise(src_ref, dst_ref):
        x = src_ref[...]
        ms = jnp.mean(x * x, axis=-1, keepdims=True)
        dst_ref[...] = (x * lax.rsqrt(ms + EPS) * nw_ref[...]).astype(BF16)

    @pl.when(i == 0)
    def _():
        normalise(x_ref, xn0_ref)

    row = lax.broadcasted_iota(jnp.int32, (C, C), 0)
    col = lax.broadcasted_iota(jnp.int32, (C, C), 1)
    rb, cb = row >> 4, col >> 4
    m0 = (rb == cb) & (col <= row)
    m1 = ((rb & 1) == 1) & (cb == rb - 1)
    m2 = (row >= 2 * SUB) & (col < 2 * SUB)
    rows = [slice(c * C, (c + 1) * C) for c in range(HG_GROUP)]
    urows = [slice(u * 2 * C, (u + 1) * 2 * C) for u in range(HG_UNITS)]
    sub_row = lax.broadcasted_iota(jnp.int32, (SUBLANES, HG_DK), 0)

    def cat(blocks):
        return jnp.concatenate(blocks, axis=0).astype(BF16)

    def cumsum_rows(g):
        groups = []
        for v in range(C // SUBLANES):
            xg = g[v * SUBLANES:(v + 1) * SUBLANES]
            shift = 1
            while shift < SUBLANES:
                xg = xg + jnp.where(sub_row >= shift, pltpu.roll(xg, shift, axis=0), 0.0)
                shift *= 2
            groups.append(xg)
        out, carry = [groups[0]], groups[0][SUBLANES - 1:SUBLANES]
        for xg in groups[1:]:
            out.append(xg + carry)
            carry = carry + xg[SUBLANES - 1:SUBLANES]
        return jnp.concatenate(out, axis=0)

    def slot(head_a, abuf, head_b, chunks, wbuf, rbuf, xn_ref):
        pieces = [(chunk, half) for chunk in chunks for half in range(2)]

        def project(flush=False):
            if not pieces:
                return
            chunk, half = pieces.pop(0)
            acc = jnp.dot(xn_ref[...], w_ref[chunk, :, half * HALF:(half + 1) * HALF],
                          preferred_element_type=F32).astype(BF16)
            if wbuf is None:
                pj_ref[chunk - N_HG_CHUNKS, :, half * HALF:(half + 1) * HALF] = acc
            else:
                for s in range(SLABS_PER_CHUNK // 2):
                    slab = chunk * SLABS_PER_CHUNK + half * (SLABS_PER_CHUNK // 2) + s
                    wbuf[slab] = acc[:, s * HG_DK:(s + 1) * HG_DK]
            if flush:
                project(True)

        def src(sec, head, buf=rbuf):
            return buf.at[sec * HG_HEADS + head]

        v_b = [src(SEC_HI, head_b)[urows[u], :] for u in range(HG_UNITS)]
        p_b = [p_sc[u] for u in range(HG_UNITS)]
        qb_b = [qb_sc[u] for u in range(HG_UNITS)]
        kb_b = [kb_sc[u] for u in range(HG_UNITS)]
        dec_b = [dec_sc[u] for u in range(HG_UNITS)]

        project()

        lb = lb_ref[head_a]
        q, k, b = [], [], []
        for c in range(HG_GROUP):
            xq = src(SEC_HQ, head_a, abuf)[rows[c], :].astype(F32)
            xf = src(SEC_HF, head_a, abuf)[rows[c], :].astype(F32)
            f = lb + (1.0 - lb) * _sigmoid(xf)
            k.append(1.0 - f)
            q.append(xq * _sigmoid(xq))
            b.append(cumsum_rows(jnp.log(f)))

        o_intra = [jnp.dot(p_b[u], v_b[u], preferred_element_type=F32) for u in range(HG_UNITS)]
        kv = [lax.dot_general(v_b[u], kb_b[u], (((0,), (0,)), ((), ())),
                              preferred_element_type=F32) for u in range(HG_UNITS)]

        project()

        st = st_ref[head_b]
        o = []
        for u in range(HG_UNITS):
            o.append(o_intra[u] + _nt(qb_b[u], st))
            st = st * dec_b[u] + kv[u]
        st_ref[head_b] = st

        project()

        p_c, qb_c, kb_c, last_c = [], [], [], []
        for c in range(HG_GROUP):
            ends = [b[c][(j + 1) * SUB - 1:(j + 1) * SUB] for j in range(NB)]
            starts = [jnp.zeros_like(ends[0])] + ends[:-1]
            last = ends[-1]
            q1, k1, qd, kd, qb, kb = [], [], [], [], [], []
            for j in range(NB):
                sl = slice(j * SUB, (j + 1) * SUB)
                bj = b[c][sl]
                q1j = q[c][sl] * jnp.exp(bj - starts[j])
                k1j = k[c][sl] * jnp.exp(ends[j] - bj)
                half = jnp.exp(0.5 * (starts[j] - ends[j]))
                q1.append(q1j)
                k1.append(k1j)
                qd.append(q1j * half)
                kd.append(k1j * half)
                qb.append(q1j * jnp.exp(starts[j]))
                kb.append(k1j * jnp.exp(last - ends[j]))
            q2 = q1[:3] + [q1[3] * jnp.exp(ends[2] - ends[1])]
            k2 = [k1[0] * jnp.exp(ends[1] - ends[0])] + k1[1:]
            s0 = _nt(cat(qd), cat(kd))
            s1 = _nt(cat(q1), cat(k1))
            s2 = _nt(cat(q2), cat(k2))
            p_c.append(jnp.where(m0, s0, jnp.where(m1, s1, jnp.where(m2, s2, 0.0))).astype(BF16))
            qb_c.append(jnp.concatenate(qb, axis=0))
            kb_c.append(jnp.concatenate(kb, axis=0))
            last_c.append(last)

        p_a, qb_a, kb_a, dec_a = [], [], [], []
        for u in range(HG_UNITS):
            c0, c1 = 2 * u, 2 * u + 1
            cross = _nt(qb_c[c1], kb_c[c0]).astype(BF16)
            p_a.append(jnp.concatenate(
                [jnp.concatenate([p_c[c0], jnp.zeros((C, C), BF16)], axis=1),
                 jnp.concatenate([cross, p_c[c1]], axis=1)], axis=0))
            qb_a.append(cat([qb_c[c0], qb_c[c1] * jnp.exp(last_c[c0])]))
            kb_a.append(cat([kb_c[c0] * jnp.exp(last_c[c1]), kb_c[c1]]))
            dec_a.append(jnp.exp(last_c[c0] + last_c[c1]))

        project(flush=True)

        nw = hnw_ref[head_b]
        for u in range(HG_UNITS):
            xg = src(SEC_HG, head_b)[urows[u], :].astype(F32)
            ms_o = jnp.mean(o[u] * o[u], axis=-1, keepdims=True)
            on = o[u] * lax.rsqrt(ms_o + EPS) * nw
            gh_ref[head_b, urows[u], :] = (on * (xg * _sigmoid(xg))).astype(BF16)

        for u in range(HG_UNITS):
            p_sc[u] = p_a[u]
            qb_sc[u] = qb_a[u]
            kb_sc[u] = kb_a[u]
            dec_sc[u] = dec_a[u]

    def run(wbuf, rbuf, xn_ref, xn_next_ref):
        def mid(s, carry, to_scratch):
            slot(s, rbuf, s - 1, [2 * (s - 1), 2 * (s - 1) + 1], wbuf if to_scratch else None, rbuf,
                 xn_ref)
            return carry

        n_scratch_slots = N_HG_CHUNKS // 2
        lax.fori_loop(1, 1 + n_scratch_slots, functools.partial(mid, to_scratch=True), 0)
        lax.fori_loop(1 + n_scratch_slots, HG_HEADS, functools.partial(mid, to_scratch=False), 0)
        normalise(xnext_ref, xn_next_ref)
        slot(0, wbuf, HG_HEADS - 1, list(range(2 * (HG_HEADS - 1), N_CHUNKS)), None, rbuf, xn_ref)

    @pl.when(even)
    def _():
        run(hb0_ref, hb1_ref, xn0_ref, xn1_ref)

    @pl.when(jnp.logical_not(even))
    def _():
        run(hb1_ref, hb0_ref, xn1_ref, xn0_ref)


def _front(x2d, norm_w, w3, lb, hnw, seq):
    m = x2d.shape[0]
    nt = m // FR_TT
    last = nt - 1
    return pl.pallas_call(
        functools.partial(_front_kernel, tiles_per_seq=seq // FR_TT),
        grid=(nt + 1,),
        in_specs=[
            pl.BlockSpec((FR_TT, D_MODEL), lambda i: (jnp.minimum(i, last), 0)),
            pl.BlockSpec((FR_TT, D_MODEL), lambda i: (jnp.minimum(i + 1, last), 0)),
            pl.BlockSpec((1, D_MODEL), lambda i: (0, 0)),
            pl.BlockSpec((N_CHUNKS, D_MODEL, NCH), lambda i: (0, 0, 0), pipeline_mode=pl.Buffered(1)),
            pl.BlockSpec((HG_HEADS, 1, HG_DK), lambda i: (0, 0, 0)),
            pl.BlockSpec((HG_HEADS, 1, HG_DV), lambda i: (0, 0, 0)),
        ],
        out_specs=[
            pl.BlockSpec((N_PJ_CHUNKS, FR_TT, NCH), lambda i: (0, jnp.minimum(i, last), 0)),
            pl.BlockSpec((HG_HEADS, FR_TT, HG_DV), lambda i: (0, jnp.maximum(i - 1, 0), 0)),
        ],
        out_shape=[jax.ShapeDtypeStruct((N_PJ_CHUNKS, m, NCH), BF16),
                   jax.ShapeDtypeStruct((HG_HEADS, m, HG_DV), BF16)],
        scratch_shapes=[pltpu.VMEM((FR_TT, D_MODEL), BF16),
                        pltpu.VMEM((FR_TT, D_MODEL), BF16),
                        pltpu.VMEM((4 * HG_HEADS, FR_TT, HG_DK), BF16),
                        pltpu.VMEM((4 * HG_HEADS, FR_TT, HG_DK), BF16),
                        pltpu.VMEM((HG_HEADS, HG_DV, HG_DK), F32),
                        pltpu.VMEM((HG_UNITS, 2 * HG_C, 2 * HG_C), BF16),
                        pltpu.VMEM((HG_UNITS, 2 * HG_C, HG_DK), BF16),
                        pltpu.VMEM((HG_UNITS, 2 * HG_C, HG_DK), BF16),
                        pltpu.VMEM((HG_UNITS, 1, HG_DK), F32)],
        compiler_params=pltpu.CompilerParams(
            dimension_semantics=("arbitrary",), vmem_limit_bytes=VMEM_LIMIT),
        name="front",
    )(x2d, x2d, norm_w, w3, lb, hnw)


def _back_kernel(sink_ref, q0_ref, q1_ref, kvp_ref, kv_ref, ag0_ref, ag1_ref, gh_ref, mh_ref, ma_ref,
                 x_ref, wbh32_ref, wba32_ref, wo32_ref, fnw_ref, o_ref,
                 kc_ref, vx_ref, ga0_ref, ga1_ref, wbh_ref, wba_ref, wo_ref, acc_ref, mg_ref,
                 *, n_tiles, tiles_per_seq):
    W, DH = WINDOW, ATT_DH
    half_heads = ATT_Q_HEADS // 2
    HALF = D_MODEL // 2
    QUARTER = D_MODEL // 4
    j = pl.program_id(0)

    @pl.when(j == 0)
    def _():
        for hh in range(2):
            wbh_ref[hh] = wbh32_ref[:, hh * HALF:(hh + 1) * HALF].astype(BF16)
            wba_ref[hh] = wba32_ref[:, hh * HALF:(hh + 1) * HALF].astype(BF16)
            wo_ref[hh] = wo32_ref[hh * HALF:(hh + 1) * HALF, :].astype(BF16)
        ga1_ref[...] = jnp.zeros_like(ga1_ref)
        acc_ref[...] = jnp.zeros_like(acc_ref)

    kc_ref[0:W, :] = kvp_ref[:, :KV_WIDTH]
    kc_ref[W:, :] = kv_ref[:, :KV_WIDTH]
    lane = lax.broadcasted_iota(jnp.int32, (1, 2 * DH), 1)
    low = lane < DH
    for rows, src in ((slice(0, W), kvp_ref), (slice(W, None), kv_ref)):
        for slab in range(ATT_KV_HEADS // 2):
            x = src[:, KV_WIDTH + slab * 2 * DH:KV_WIDTH + (slab + 1) * 2 * DH]
            xr = jnp.concatenate([x[:, DH:], x[:, :DH]], axis=1)
            zero = jnp.zeros_like(x)
            for t, piece in enumerate((jnp.where(low, x, zero), jnp.where(low, zero, xr),
                                       jnp.where(low, xr, zero), jnp.where(low, zero, x))):
                vx_ref[rows, (4 * slab + t) * 2 * DH:(4 * slab + t + 1) * 2 * DH] = piece
    ones_lo = jnp.broadcast_to(jnp.where(low, 1.0, 0.0).astype(BF16), (2 * W, 2 * DH))
    ones_hi = jnp.broadcast_to(jnp.where(low, 0.0, 1.0).astype(BF16), (2 * W, 2 * DH))
    low_rows = jnp.broadcast_to(low, (W, 2 * DH))

    first_tile = lax.rem(jnp.minimum(j, n_tiles - 1), tiles_per_seq) == 0
    qi = lax.broadcasted_iota(jnp.int32, (W, W), 0)
    kj = lax.broadcasted_iota(jnp.int32, (W, W), 1)
    upper = kj > qi

    def head_cols(ref_pair, hd, width):
        ref = ref_pair[hd // half_heads]
        hd = hd % half_heads
        return ref, slice(hd * DH, (hd + width) * DH)

    def half(hh, carry, ga_ref, gap_ref):
        gh = jnp.concatenate([gh_ref[h] for h in range(HG_HEADS)], axis=1)
        y_parts = {}

        def branch_piece(name, lhs, w_ref, k):
            y_parts[name, k] = jnp.dot(lhs, w_ref[hh, :, k * QUARTER:(k + 1) * QUARTER],
                                       preferred_element_type=F32)

        def gate():
            yh = jnp.concatenate([y_parts["h", 0], y_parts["h", 1]], axis=1)
            ya = jnp.concatenate([y_parts["a", 0], y_parts["a", 1]], axis=1)
            mg_ref[...] = (_sigmoid(mh_ref[hh].astype(F32)) * yh
                           + _sigmoid(ma_ref[hh].astype(F32)) * ya).astype(BF16)

        def out_piece(k):
            cols = slice(k * QUARTER, (k + 1) * QUARTER)
            acc_ref[:, cols] = jnp.where(hh == 0, 0.0, acc_ref[:, cols]) + jnp.dot(
                mg_ref[...], wo_ref[hh, :, cols], preferred_element_type=F32)

        pieces = [lambda: branch_piece("h", gh, wbh_ref, 0), lambda: branch_piece("h", gh, wbh_ref, 1),
                  lambda: branch_piece("a", gap_ref[...], wba_ref, 0),
                  lambda: branch_piece("a", gap_ref[...], wba_ref, 1),
                  lambda: (gate(), out_piece(0)), lambda: out_piece(1), lambda: out_piece(2),
                  lambda: out_piece(3)]

        units = [(bi, h) for bi in range(2) for h in range(ATT_KV_HEADS)]

        def scores(bi, h):
            r = pl.multiple_of((2 * hh + bi) * W, W)
            kk = kc_ref[pl.ds(r, 2 * W), h * DH:(h + 1) * DH]
            parts = []
            for hd in range(h * ATT_GROUP, (h + 1) * ATT_GROUP):
                ref, cols = head_cols((q0_ref, q1_ref), hd, 1)
                parts.append(ref[pl.ds(r, W), cols])
            return _nt(jnp.concatenate(parts, axis=0), kk)

        s_next = scores(*units[0])
        for idx, (bi, h) in enumerate(units):
            r = pl.multiple_of((2 * hh + bi) * W, W)
            s = s_next
            if idx + 1 < len(units):
                s_next = scores(*units[idx + 1])
            pieces[idx]()
            prev_bias = jnp.where(first_tile & (2 * hh + bi == 0), -jnp.inf, 0.0)
            probs, sink_terms = [], []
            for jj, hd in enumerate(range(h * ATT_GROUP, (h + 1) * ATT_GROUP)):
                sj = s[jj * W:(jj + 1) * W]
                c = jnp.where(upper, sj[:, :W] + prev_bias, sj[:, W:])
                sink = sink_ref[hd]
                m = jnp.maximum(jnp.max(c, axis=-1, keepdims=True), sink)
                p = jnp.exp(c - m)
                sink_terms.append(jnp.exp(sink - m))
                probs.append(jnp.concatenate(
                    [jnp.where(upper, p, 0.0), jnp.where(upper, 0.0, p)], axis=1).astype(BF16))
            w_lo = jnp.concatenate(
                [vx_ref[pl.ds(r, 2 * W), (2 * h) * 2 * DH:(2 * h + 1) * 2 * DH], ones_lo], axis=1)
            w_hi = jnp.concatenate(
                [vx_ref[pl.ds(r, 2 * W), (2 * h + 1) * 2 * DH:(2 * h + 2) * 2 * DH], ones_hi], axis=1)
            res = (jnp.dot(jnp.concatenate(probs[0::2], axis=0), w_lo, preferred_element_type=F32)
                   + jnp.dot(jnp.concatenate(probs[1::2], axis=0), w_hi, preferred_element_type=F32))
            for pair in range(ATT_GROUP // 2):
                rp = res[pair * W:(pair + 1) * W]
                den = rp[:, 2 * DH:] + jnp.where(low_rows, sink_terms[2 * pair], sink_terms[2 * pair + 1])
                j0 = h * ATT_GROUP + 2 * pair
                ag_ref, cols = head_cols((ag0_ref, ag1_ref), j0, 2)
                ag = ag_ref[pl.ds(r, W), cols].astype(F32)
                ga_ref[pl.ds(r, W), j0 * DH:(j0 + 2) * DH] = (
                    rp[:, :2 * DH] / den * (ag * _sigmoid(ag))).astype(BF16)
        return carry

    @pl.when(lax.rem(j, 2) == 0)
    def _():
        lax.fori_loop(0, 2, functools.partial(half, ga_ref=ga0_ref, gap_ref=ga1_ref), 0)

    @pl.when(lax.rem(j, 2) == 1)
    def _():
        lax.fori_loop(0, 2, functools.partial(half, ga_ref=ga1_ref, gap_ref=ga0_ref), 0)

    xo = x_ref[...] + acc_ref[...]
    ms = jnp.mean(xo * xo, axis=-1, keepdims=True)
    o_ref[...] = xo * lax.rsqrt(ms + EPS) * fnw_ref[...]


def _back(pj, gh3, x2d, sinks, wbh, wba, wo, fnw, seq):
    m = x2d.shape[0]
    nt = m // BK_TT
    last = nt - 1
    per = BK_TT // WINDOW

    def cur(jj):
        return jnp.minimum(jj, last)

    def prv(jj):
        return jnp.maximum(jj - 1, 0)

    def chunk(c):
        return pl.BlockSpec((None, BK_TT, NCH), lambda jj, s: (c, cur(jj), 0))

    def pair(c):
        return pl.BlockSpec((2, BK_TT, NCH), lambda jj, s: (c // 2, prv(jj), 0))

    prev_kv = pl.BlockSpec((None, WINDOW, NCH),
                           lambda jj, s: (PJ_KV, jnp.maximum(cur(jj) * per - 1, 0), 0))
    tile = pl.BlockSpec((BK_TT, D_MODEL), lambda jj, s: (prv(jj), 0))
    wspec = pl.BlockSpec((D_MODEL, D_MODEL), lambda jj, s: (0, 0), pipeline_mode=pl.Buffered(1))
    half = D_MODEL // 2
    grid_spec = pltpu.PrefetchScalarGridSpec(
        num_scalar_prefetch=1,
        grid=(nt + 1,),
        in_specs=[chunk(PJ_AQ), chunk(PJ_AQ + 1), prev_kv, chunk(PJ_KV), chunk(PJ_AG), chunk(PJ_AG + 1),
                  pl.BlockSpec((HG_HEADS, BK_TT, HG_DV), lambda jj, s: (0, prv(jj), 0)),
                  pair(PJ_MH), pair(PJ_MA), tile, wspec, wspec, wspec,
                  pl.BlockSpec((1, D_MODEL), lambda jj, s: (0, 0))],
        out_specs=tile,
        scratch_shapes=[pltpu.VMEM((BK_TT + WINDOW, KV_WIDTH), BF16),
                        pltpu.VMEM((BK_TT + WINDOW, ATT_KV_HEADS * 4 * ATT_DH), BF16),
                        pltpu.VMEM((BK_TT, D_MODEL), BF16),
                        pltpu.VMEM((BK_TT, D_MODEL), BF16),
                        pltpu.VMEM((2, D_MODEL, half), BF16),
                        pltpu.VMEM((2, D_MODEL, half), BF16),
                        pltpu.VMEM((2, half, D_MODEL), BF16),
                        pltpu.VMEM((BK_TT, D_MODEL), F32),
                        pltpu.VMEM((BK_TT, half), BF16)],
    )
    return pl.pallas_call(
        functools.partial(_back_kernel, n_tiles=nt, tiles_per_seq=seq // BK_TT),
        grid_spec=grid_spec,
        out_shape=jax.ShapeDtypeStruct((m, D_MODEL), F32),
        compiler_params=pltpu.CompilerParams(
            dimension_semantics=("arbitrary",), vmem_limit_bytes=VMEM_LIMIT),
        name="back",
    )(sinks, pj, pj, pj, pj, pj, pj, gh3, pj, pj, x2d, wbh, wba, wo, fnw)


def kernel(x, norm_w, w_in, hgrn_lower_bound, hgrn_norm_w, w_branch_hgrn, attn_sinks,
           w_branch_attn, w_out, final_norm_w):
    batch, seq, _ = x.shape
    depth = norm_w.shape[0]
    assert depth == 1, "the back kernel fuses the final RMSNorm into the single layer"
    assert seq % FR_TT == 0 and seq % BK_TT == 0
    lb_all = jnp.cumsum(jax.nn.softmax(hgrn_lower_bound.astype(F32), axis=0), axis=0)
    w3 = _wprep(w_in[0])
    x2d = x.reshape(batch * seq, D_MODEL)
    pj, gh3 = _front(x2d, norm_w[0].reshape(1, D_MODEL), w3,
                     lb_all[0].reshape(HG_HEADS, 1, HG_DK),
                     hgrn_norm_w[0].reshape(HG_HEADS, 1, HG_DV), seq)
    xo = _back(pj, gh3, x2d, attn_sinks[0].astype(F32), w_branch_hgrn[0], w_branch_attn[0], w_out[0],
               final_norm_w.reshape(1, D_MODEL), seq)
    return xo.reshape(batch, seq, D_MODEL)
```

```python
import functools

import jax
import jax.numpy as jnp
from jax import lax
from jax.experimental import pallas as pl
from jax.experimental.pallas import tpu as pltpu

F32 = jnp.float32
BF16 = jnp.bfloat16

D_MODEL = 1024
HG_HEADS = 8
HG_DK = 128
HG_DV = 128
ATT_Q_HEADS = 16
ATT_KV_HEADS = 4
ATT_GROUP = ATT_Q_HEADS // ATT_KV_HEADS
ATT_DH = 64
KV_WIDTH = ATT_KV_HEADS * ATT_DH
ATT_SCALE = ATT_DH ** -0.5
assert ATT_SCALE == 0.125
WINDOW = 128
EPS = 1e-6
D_IN = 8 * D_MODEL + 2 * KV_WIDTH

NCH = 512
N_CHUNKS = D_IN // NCH
N_HG_CHUNKS = 4 * D_MODEL // NCH
N_PJ_CHUNKS = N_CHUNKS - N_HG_CHUNKS
PJ_AQ, PJ_AG, PJ_MH, PJ_MA, PJ_KV = 0, 2, 4, 6, 8
SLABS_PER_CHUNK = NCH // HG_DK
SEC_HQ, SEC_HF, SEC_HI, SEC_HG = range(4)

VMEM_LIMIT = 56 * 1024 * 1024

HG_C = 64
HG_SUB = 16
FR_TT = 512
HG_GROUP = FR_TT // HG_C
HG_UNITS = HG_GROUP // 2
SUBLANES = 8
BK_TT = 512


def _nt(a, b):
    return lax.dot_general(a.astype(BF16), b.astype(BF16), (((1,), (1,)), ((), ())),
                           preferred_element_type=F32)


def _sigmoid(x):
    return 1.0 / (1.0 + jnp.exp(-x))


def _wprep_kernel(w_ref, o_ref):
    scale = jnp.where(pl.program_id(0) // 2 == N_HG_CHUNKS // 2, ATT_SCALE, 1.0)
    o_ref[...] = (w_ref[...] * scale).astype(BF16)


def _wprep(w):
    kv_src = N_HG_CHUNKS + 2

    def src_chunk(c):
        return jnp.where(c < kv_src, c, jnp.where(c == N_CHUNKS - 1, kv_src, c + 1))

    return pl.pallas_call(
        _wprep_kernel,
        grid=(N_CHUNKS,),
        in_specs=[pl.BlockSpec((D_MODEL, NCH), lambda c: (0, src_chunk(c)))],
        out_specs=pl.BlockSpec((None, D_MODEL, NCH), lambda c: (c, 0, 0)),
        out_shape=jax.ShapeDtypeStruct((N_CHUNKS, D_MODEL, NCH), BF16),
        compiler_params=pltpu.CompilerParams(dimension_semantics=("arbitrary",)),
        name="wprep",
    )(w)


def _front_kernel(x_ref, xnext_ref, nw_ref, w_ref, lb_ref, hnw_ref, pj_ref, gh_ref, xn0_ref, xn1_ref,
                  hb0_ref, hb1_ref, st_ref, p_sc, qb_sc, kb_sc, dec_sc, *, tiles_per_seq):
    C, SUB, NB = HG_C, HG_SUB, HG_C // HG_SUB
    HALF = NCH // 2
    i = pl.program_id(0)
    even = lax.rem(i, 2) == 0

    @pl.when(i == 0)
    def _():
        hb1_ref[...] = jnp.zeros_like(hb1_ref)
        p_sc[...] = jnp.zeros_like(p_sc)
        qb_sc[...] = jnp.zeros_like(qb_sc)
        kb_sc[...] = jnp.zeros_like(kb_sc)
        dec_sc[...] = jnp.zeros_like(dec_sc)

    @pl.when((i == 0) | (lax.rem(jnp.maximum(i - 1, 0), tiles_per_seq) == 0))
    def _():
        st_ref[...] = jnp.zeros_like(st_ref)

    def normalise(src_ref, dst_ref):
        x = src_ref[...]
        ms = jnp.mean(x * x, axis=-1, keepdims=True)
        dst_ref[...] = (x * lax.rsqrt(ms + EPS) * nw_ref[...]).astype(BF16)

    @pl.when(i == 0)
    def _():
        normalise(x_ref, xn0_ref)

    row = lax.broadcasted_iota(jnp.int32, (C, C), 0)
    col = lax.broadcasted_iota(jnp.int32, (C, C), 1)
    rb, cb = row >> 4, col >> 4
    m0 = (rb == cb) & (col <= row)
    m1 = ((rb & 1) == 1) & (cb == rb - 1)
    m2 = (row >= 2 * SUB) & (col < 2 * SUB)
    rows = [slice(c * C, (c + 1) * C) for c in range(HG_GROUP)]
    urows = [slice(u * 2 * C, (u + 1) * 2 * C) for u in range(HG_UNITS)]
    sub_row = lax.broadcasted_iota(jnp.int32, (SUBLANES, HG_DK), 0)

    def cat(blocks):
        return jnp.concatenate(blocks, axis=0).astype(BF16)

    def _mt(a, b_blocks):
        bt = jnp.concatenate(b_blocks, axis=0).T.astype(BF16)
        return jnp.dot(a, bt, preferred_element_type=F32)

    def cumsum_rows(g):
        groups = []
        for v in range(C // SUBLANES):
            xg = g[v * SUBLANES:(v + 1) * SUBLANES]
            shift = 1
            while shift < SUBLANES:
                xg = xg + jnp.where(sub_row >= shift, pltpu.roll(xg, shift, axis=0), 0.0)
                shift *= 2
            groups.append(xg)
        out, carry = [groups[0]], groups[0][SUBLANES - 1:SUBLANES]
        for xg in groups[1:]:
            out.append(xg + carry)
            carry = carry + xg[SUBLANES - 1:SUBLANES]
        return jnp.concatenate(out, axis=0)

    def slot(head_a, abuf, head_b, chunks, wbuf, rbuf, xn_ref):
        pieces = [(chunk, half) for chunk in chunks for half in range(2)]

        def project(flush=False):
            if not pieces:
                return
            chunk, half = pieces.pop(0)
            acc = jnp.dot(xn_ref[...], w_ref[chunk, :, half * HALF:(half + 1) * HALF],
                          preferred_element_type=F32).astype(BF16)
            if wbuf is None:
                pj_ref[chunk - N_HG_CHUNKS, :, half * HALF:(half + 1) * HALF] = acc
            else:
                for s in range(SLABS_PER_CHUNK // 2):
                    slab = chunk * SLABS_PER_CHUNK + half * (SLABS_PER_CHUNK // 2) + s
                    wbuf[slab] = acc[:, s * HG_DK:(s + 1) * HG_DK]
            if flush:
                project(True)

        def src(sec, head, buf=rbuf):
            return buf.at[sec * HG_HEADS + head]

        v_b = [src(SEC_HI, head_b)[urows[u], :] for u in range(HG_UNITS)]
        p_b = [p_sc[u] for u in range(HG_UNITS)]
        qb_b = [qb_sc[u] for u in range(HG_UNITS)]
        kb_b = [kb_sc[u] for u in range(HG_UNITS)]
        dec_b = [dec_sc[u] for u in range(HG_UNITS)]

        project()

        lb = lb_ref[head_a]
        q, k, b = [], [], []
        for c in range(HG_GROUP):
            xq = src(SEC_HQ, head_a, abuf)[rows[c], :].astype(F32)
            xf = src(SEC_HF, head_a, abuf)[rows[c], :].astype(F32)
            f = lb + (1.0 - lb) * _sigmoid(xf)
            k.append(1.0 - f)
            q.append(xq * _sigmoid(xq))
            b.append(cumsum_rows(jnp.log(f)))

        o_intra = [jnp.dot(p_b[u], v_b[u], preferred_element_type=F32) for u in range(HG_UNITS)]
        kv = [lax.dot_general(v_b[u], kb_b[u], (((0,), (0,)), ((), ())),
                              preferred_element_type=F32) for u in range(HG_UNITS)]

        project()

        st = st_ref[head_b]
        o = []
        for u in range(HG_UNITS):
            o.append(o_intra[u] + _nt(qb_b[u], st))
            st = st * dec_b[u] + kv[u]
        st_ref[head_b] = st

        project()

        p_c, qb_c, kb_c, last_c = [], [], [], []
        for c in range(HG_GROUP):
            ends = [b[c][(j + 1) * SUB - 1:(j + 1) * SUB] for j in range(NB)]
            starts = [jnp.zeros_like(ends[0])] + ends[:-1]
            last = ends[-1]
            q1, k1, qd, kd, qb, kb = [], [], [], [], [], []
            for j in range(NB):
                sl = slice(j * SUB, (j + 1) * SUB)
                bj = b[c][sl]
                q1j = q[c][sl] * jnp.exp(bj - starts[j])
                k1j = k[c][sl] * jnp.exp(ends[j] - bj)
                half = jnp.exp(0.5 * (starts[j] - ends[j]))
                q1.append(q1j)
                k1.append(k1j)
                qd.append(q1j * half)
                kd.append(k1j * half)
                qb.append(q1j * jnp.exp(starts[j]))
                kb.append(k1j * jnp.exp(last - ends[j]))
            q2 = q1[:3] + [q1[3] * jnp.exp(ends[2] - ends[1])]
            k2 = [k1[0] * jnp.exp(ends[1] - ends[0])] + k1[1:]
            s0 = _mt(cat(qd), kd)
            s1 = _mt(cat(q1), k1)
            s2 = _mt(cat(q2), k2)
            p_c.append(jnp.where(m0, s0, jnp.where(m1, s1, jnp.where(m2, s2, 0.0))).astype(BF16))
            qb_c.append(jnp.concatenate(qb, axis=0))
            kb_c.append(jnp.concatenate(kb, axis=0))
            last_c.append(last)

        p_a, qb_a, kb_a, dec_a = [], [], [], []
        for u in range(HG_UNITS):
            c0, c1 = 2 * u, 2 * u + 1
            cross = _mt(qb_c[c1].astype(BF16), [kb_c[c0]]).astype(BF16)
            p_a.append(jnp.concatenate(
                [jnp.concatenate([p_c[c0], jnp.zeros((C, C), BF16)], axis=1),
                 jnp.concatenate([cross, p_c[c1]], axis=1)], axis=0))
            qb_a.append(cat([qb_c[c0], qb_c[c1] * jnp.exp(last_c[c0])]))
            kb_a.append(cat([kb_c[c0] * jnp.exp(last_c[c1]), kb_c[c1]]))
            dec_a.append(jnp.exp(last_c[c0] + last_c[c1]))

        project(flush=True)

        nw = hnw_ref[head_b]
        for u in range(HG_UNITS):
            xg = src(SEC_HG, head_b)[urows[u], :].astype(F32)
            ms_o = jnp.mean(o[u] * o[u], axis=-1, keepdims=True)
            on = o[u] * lax.rsqrt(ms_o + EPS) * nw
            gh_ref[head_b, urows[u], :] = (on * (xg * _sigmoid(xg))).astype(BF16)

        for u in range(HG_UNITS):
            p_sc[u] = p_a[u]
            qb_sc[u] = qb_a[u]
            kb_sc[u] = kb_a[u]
            dec_sc[u] = dec_a[u]

    def run(wbuf, rbuf, xn_ref, xn_next_ref):
        def mid(s, carry, to_scratch):
            slot(s, rbuf, s - 1, [2 * (s - 1), 2 * (s - 1) + 1], wbuf if to_scratch else None, rbuf,
                 xn_ref)
            return carry

        n_scratch_slots = N_HG_CHUNKS // 2
        lax.fori_loop(1, 1 + n_scratch_slots, functools.partial(mid, to_scratch=True), 0)
        lax.fori_loop(1 + n_scratch_slots, HG_HEADS, functools.partial(mid, to_scratch=False), 0)
        normalise(xnext_ref, xn_next_ref)
        slot(0, wbuf, HG_HEADS - 1, list(range(2 * (HG_HEADS - 1), N_CHUNKS)), None, rbuf, xn_ref)

    @pl.when(even)
    def _():
        run(hb0_ref, hb1_ref, xn0_ref, xn1_ref)

    @pl.when(jnp.logical_not(even))
    def _():
        run(hb1_ref, hb0_ref, xn1_ref, xn0_ref)


def _front(x2d, norm_w, w3, lb, hnw, seq):
    m = x2d.shape[0]
    nt = m // FR_TT
    last = nt - 1
    return pl.pallas_call(
        functools.partial(_front_kernel, tiles_per_seq=seq // FR_TT),
        grid=(nt + 1,),
        in_specs=[
            pl.BlockSpec((FR_TT, D_MODEL), lambda i: (jnp.minimum(i, last), 0)),
            pl.BlockSpec((FR_TT, D_MODEL), lambda i: (jnp.minimum(i + 1, last), 0)),
            pl.BlockSpec((1, D_MODEL), lambda i: (0, 0)),
            pl.BlockSpec((N_CHUNKS, D_MODEL, NCH), lambda i: (0, 0, 0), pipeline_mode=pl.Buffered(1)),
            pl.BlockSpec((HG_HEADS, 1, HG_DK), lambda i: (0, 0, 0)),
            pl.BlockSpec((HG_HEADS, 1, HG_DV), lambda i: (0, 0, 0)),
        ],
        out_specs=[
            pl.BlockSpec((N_PJ_CHUNKS, FR_TT, NCH), lambda i: (0, jnp.minimum(i, last), 0)),
            pl.BlockSpec((HG_HEADS, FR_TT, HG_DV), lambda i: (0, jnp.maximum(i - 1, 0), 0)),
        ],
        out_shape=[jax.ShapeDtypeStruct((N_PJ_CHUNKS, m, NCH), BF16),
                   jax.ShapeDtypeStruct((HG_HEADS, m, HG_DV), BF16)],
        scratch_shapes=[pltpu.VMEM((FR_TT, D_MODEL), BF16),
                        pltpu.VMEM((FR_TT, D_MODEL), BF16),
                        pltpu.VMEM((4 * HG_HEADS, FR_TT, HG_DK), BF16),
                        pltpu.VMEM((4 * HG_HEADS, FR_TT, HG_DK), BF16),
                        pltpu.VMEM((HG_HEADS, HG_DV, HG_DK), F32),
                        pltpu.VMEM((HG_UNITS, 2 * HG_C, 2 * HG_C), BF16),
                        pltpu.VMEM((HG_UNITS, 2 * HG_C, HG_DK), BF16),
                        pltpu.VMEM((HG_UNITS, 2 * HG_C, HG_DK), BF16),
                        pltpu.VMEM((HG_UNITS, 1, HG_DK), F32)],
        compiler_params=pltpu.CompilerParams(
            dimension_semantics=("arbitrary",), vmem_limit_bytes=VMEM_LIMIT),
        name="front",
    )(x2d, x2d, norm_w, w3, lb, hnw)


def _back_kernel(sink_ref, q0_ref, q1_ref, kvp_ref, kv_ref, ag0_ref, ag1_ref, gh_ref, mh_ref, ma_ref,
                 x_ref, wbh32_ref, wba32_ref, wo32_ref, fnw_ref, o_ref,
                 kc_ref, vx_ref, ga0_ref, ga1_ref, wbh_ref, wba_ref, wo_ref, acc_ref, mg_ref,
                 *, n_tiles, tiles_per_seq):
    W, DH = WINDOW, ATT_DH
    half_heads = ATT_Q_HEADS // 2
    HALF = D_MODEL // 2
    QUARTER = D_MODEL // 4
    j = pl.program_id(0)

    @pl.when(j == 0)
    def _():
        for hh in range(2):
            wbh_ref[hh] = wbh32_ref[:, hh * HALF:(hh + 1) * HALF].astype(BF16)
            wba_ref[hh] = wba32_ref[:, hh * HALF:(hh + 1) * HALF].astype(BF16)
            wo_ref[hh] = wo32_ref[hh * HALF:(hh + 1) * HALF, :].astype(BF16)
        ga1_ref[...] = jnp.zeros_like(ga1_ref)
        acc_ref[...] = jnp.zeros_like(acc_ref)

    kc_ref[0:W, :] = kvp_ref[:, :KV_WIDTH]
    kc_ref[W:, :] = kv_ref[:, :KV_WIDTH]
    lane = lax.broadcasted_iota(jnp.int32, (1, 2 * DH), 1)
    low = lane < DH
    for rows, src in ((slice(0, W), kvp_ref), (slice(W, None), kv_ref)):
        for slab in range(ATT_KV_HEADS // 2):
            x = src[:, KV_WIDTH + slab * 2 * DH:KV_WIDTH + (slab + 1) * 2 * DH]
            xr = jnp.concatenate([x[:, DH:], x[:, :DH]], axis=1)
            zero = jnp.zeros_like(x)
            for t, piece in enumerate((jnp.where(low, x, zero), jnp.where(low, zero, xr),
                                       jnp.where(low, xr, zero), jnp.where(low, zero, x))):
                vx_ref[rows, (4 * slab + t) * 2 * DH:(4 * slab + t + 1) * 2 * DH] = piece
    ones_lo = jnp.broadcast_to(jnp.where(low, 1.0, 0.0).astype(BF16), (2 * W, 2 * DH))
    ones_hi = jnp.broadcast_to(jnp.where(low, 0.0, 1.0).astype(BF16), (2 * W, 2 * DH))
    low_rows = jnp.broadcast_to(low, (W, 2 * DH))

    first_tile = lax.rem(jnp.minimum(j, n_tiles - 1), tiles_per_seq) == 0
    qi = lax.broadcasted_iota(jnp.int32, (W, W), 0)
    kj = lax.broadcasted_iota(jnp.int32, (W, W), 1)
    upper = kj > qi

    def head_cols(ref_pair, hd, width):
        ref = ref_pair[hd // half_heads]
        hd = hd % half_heads
        return ref, slice(hd * DH, (hd + width) * DH)

    def half(hh, carry, ga_ref, gap_ref):
        gh = jnp.concatenate([gh_ref[h] for h in range(HG_HEADS)], axis=1)
        y_parts = {}

        def branch_piece(name, lhs, w_ref, k):
            y_parts[name, k] = jnp.dot(lhs, w_ref[hh, :, k * QUARTER:(k + 1) * QUARTER],
                                       preferred_element_type=F32)

        def gate():
            yh = jnp.concatenate([y_parts["h", 0], y_parts["h", 1]], axis=1)
            ya = jnp.concatenate([y_parts["a", 0], y_parts["a", 1]], axis=1)
            mg_ref[...] = (_sigmoid(mh_ref[hh].astype(F32)) * yh
                           + _sigmoid(ma_ref[hh].astype(F32)) * ya).astype(BF16)

        def out_piece(k):
            cols = slice(k * QUARTER, (k + 1) * QUARTER)
            acc_ref[:, cols] = jnp.where(hh == 0, 0.0, acc_ref[:, cols]) + jnp.dot(
                mg_ref[...], wo_ref[hh, :, cols], preferred_element_type=F32)

        pieces = [lambda: branch_piece("h", gh, wbh_ref, 0), lambda: branch_piece("h", gh, wbh_ref, 1),
                  lambda: branch_piece("a", gap_ref[...], wba_ref, 0),
                  lambda: branch_piece("a", gap_ref[...], wba_ref, 1),
                  lambda: (gate(), out_piece(0)), lambda: out_piece(1), lambda: out_piece(2),
                  lambda: out_piece(3)]

        units = [(bi, h) for bi in range(2) for h in range(ATT_KV_HEADS)]

        def scores(bi, h):
            r = pl.multiple_of((2 * hh + bi) * W, W)
            kk = kc_ref[pl.ds(r, 2 * W), h * DH:(h + 1) * DH]
            parts = []
            for hd in range(h * ATT_GROUP, (h + 1) * ATT_GROUP):
                ref, cols = head_cols((q0_ref, q1_ref), hd, 1)
                parts.append(ref[pl.ds(r, W), cols])
            return _nt(jnp.concatenate(parts, axis=0), kk)

        s_next = scores(*units[0])
        for idx, (bi, h) in enumerate(units):
            r = pl.multiple_of((2 * hh + bi) * W, W)
            s = s_next
            if idx + 1 < len(units):
                s_next = scores(*units[idx + 1])
            pieces[idx]()
            prev_bias = jnp.where(first_tile & (2 * hh + bi == 0), -jnp.inf, 0.0)
            probs, sink_terms = [], []
            for jj, hd in enumerate(range(h * ATT_GROUP, (h + 1) * ATT_GROUP)):
                sj = s[jj * W:(jj + 1) * W]
                c = jnp.where(upper, sj[:, :W] + prev_bias, sj[:, W:])
                sink = sink_ref[hd]
                m = jnp.maximum(jnp.max(c, axis=-1, keepdims=True), sink)
                p = jnp.exp(c - m)
                sink_terms.append(jnp.exp(sink - m))
                probs.append(jnp.concatenate(
                    [jnp.where(upper, p, 0.0), jnp.where(upper, 0.0, p)], axis=1).astype(BF16))
            w_lo = jnp.concatenate(
                [vx_ref[pl.ds(r, 2 * W), (2 * h) * 2 * DH:(2 * h + 1) * 2 * DH], ones_lo], axis=1)
            w_hi = jnp.concatenate(
                [vx_ref[pl.ds(r, 2 * W), (2 * h + 1) * 2 * DH:(2 * h + 2) * 2 * DH], ones_hi], axis=1)
            res = (jnp.dot(jnp.concatenate(probs[0::2], axis=0), w_lo, preferred_element_type=F32)
                   + jnp.dot(jnp.concatenate(probs[1::2], axis=0), w_hi, preferred_element_type=F32))
            for pair in range(ATT_GROUP // 2):
                rp = res[pair * W:(pair + 1) * W]
                den = rp[:, 2 * DH:] + jnp.where(low_rows, sink_terms[2 * pair], sink_terms[2 * pair + 1])
                j0 = h * ATT_GROUP + 2 * pair
                ag_ref, cols = head_cols((ag0_ref, ag1_ref), j0, 2)
                ag = ag_ref[pl.ds(r, W), cols].astype(F32)
                ga_ref[pl.ds(r, W), j0 * DH:(j0 + 2) * DH] = (
                    rp[:, :2 * DH] / den * (ag * _sigmoid(ag))).astype(BF16)
        return carry

    @pl.when(lax.rem(j, 2) == 0)
    def _():
        lax.fori_loop(0, 2, functools.partial(half, ga_ref=ga0_ref, gap_ref=ga1_ref), 0)

    @pl.when(lax.rem(j, 2) == 1)
    def _():
        lax.fori_loop(0, 2, functools.partial(half, ga_ref=ga1_ref, gap_ref=ga0_ref), 0)

    xo = x_ref[...] + acc_ref[...]
    ms = jnp.mean(xo * xo, axis=-1, keepdims=True)
    o_ref[...] = xo * lax.rsqrt(ms + EPS) * fnw_ref[...]


def _back(pj, gh3, x2d, sinks, wbh, wba, wo, fnw, seq):
    m = x2d.shape[0]
    nt = m // BK_TT
    last = nt - 1
    per = BK_TT // WINDOW

    def cur(jj):
        return jnp.minimum(jj, last)

    def prv(jj):
        return jnp.maximum(jj - 1, 0)

    def chunk(c):
        return pl.BlockSpec((None, BK_TT, NCH), lambda jj, s: (c, cur(jj), 0))

    def pair(c):
        return pl.BlockSpec((2, BK_TT, NCH), lambda jj, s: (c // 2, prv(jj), 0))

    prev_kv = pl.BlockSpec((None, WINDOW, NCH),
                           lambda jj, s: (PJ_KV, jnp.maximum(cur(jj) * per - 1, 0), 0))
    tile = pl.BlockSpec((BK_TT, D_MODEL), lambda jj, s: (prv(jj), 0))
    wspec = pl.BlockSpec((D_MODEL, D_MODEL), lambda jj, s: (0, 0), pipeline_mode=pl.Buffered(1))
    half = D_MODEL // 2
    grid_spec = pltpu.PrefetchScalarGridSpec(
        num_scalar_prefetch=1,
        grid=(nt + 1,),
        in_specs=[chunk(PJ_AQ), chunk(PJ_AQ + 1), prev_kv, chunk(PJ_KV), chunk(PJ_AG), chunk(PJ_AG + 1),
                  pl.BlockSpec((HG_HEADS, BK_TT, HG_DV), lambda jj, s: (0, prv(jj), 0)),
                  pair(PJ_MH), pair(PJ_MA), tile, wspec, wspec, wspec,
                  pl.BlockSpec((1, D_MODEL), lambda jj, s: (0, 0))],
        out_specs=tile,
        scratch_shapes=[pltpu.VMEM((BK_TT + WINDOW, KV_WIDTH), BF16),
                        pltpu.VMEM((BK_TT + WINDOW, ATT_KV_HEADS * 4 * ATT_DH), BF16),
                        pltpu.VMEM((BK_TT, D_MODEL), BF16),
                        pltpu.VMEM((BK_TT, D_MODEL), BF16),
                        pltpu.VMEM((2, D_MODEL, half), BF16),
                        pltpu.VMEM((2, D_MODEL, half), BF16),
                        pltpu.VMEM((2, half, D_MODEL), BF16),
                        pltpu.VMEM((BK_TT, D_MODEL), F32),
                        pltpu.VMEM((BK_TT, half), BF16)],
    )
    return pl.pallas_call(
        functools.partial(_back_kernel, n_tiles=nt, tiles_per_seq=seq // BK_TT),
        grid_spec=grid_spec,
        out_shape=jax.ShapeDtypeStruct((m, D_MODEL), F32),
        compiler_params=pltpu.CompilerParams(
            dimension_semantics=("arbitrary",), vmem_limit_bytes=VMEM_LIMIT),
        name="back",
    )(sinks, pj, pj, pj, pj, pj, pj, gh3, pj, pj, x2d, wbh, wba, wo, fnw)


def kernel(x, norm_w, w_in, hgrn_lower_bound, hgrn_norm_w, w_branch_hgrn, attn_sinks,
           w_branch_attn, w_out, final_norm_w):
    batch, seq, _ = x.shape
    depth = norm_w.shape[0]
    assert depth == 1, "the back kernel fuses the final RMSNorm into the single layer"
    assert seq % FR_TT == 0 and seq % BK_TT == 0
    lb_all = jnp.cumsum(jax.nn.softmax(hgrn_lower_bound.astype(F32), axis=0), axis=0)
    w3 = _wprep(w_in[0])
    x2d = x.reshape(batch * seq, D_MODEL)
    pj, gh3 = _front(x2d, norm_w[0].reshape(1, D_MODEL), w3,
                     lb_all[0].reshape(HG_HEADS, 1, HG_DK),
                     hgrn_norm_w[0].reshape(HG_HEADS, 1, HG_DV), seq)
    xo = _back(pj, gh3, x2d, attn_sinks[0].astype(F32), w_branch_hgrn[0], w_branch_attn[0], w_out[0],
               final_norm_w.reshape(1, D_MODEL), seq)
    return xo.reshape(batch, seq, D_MODEL)
```

```python
import functools

import jax
import jax.numpy as jnp
from jax import lax
from jax.experimental import pallas as pl
from jax.experimental.pallas import tpu as pltpu

F32 = jnp.float32
BF16 = jnp.bfloat16

D_MODEL = 1024
HG_HEADS = 8
HG_DK = 128
HG_DV = 128
ATT_Q_HEADS = 16
ATT_KV_HEADS = 4
ATT_GROUP = ATT_Q_HEADS // ATT_KV_HEADS
ATT_DH = 64
KV_WIDTH = ATT_KV_HEADS * ATT_DH
ATT_SCALE = ATT_DH ** -0.5
assert ATT_SCALE == 0.125
WINDOW = 128
EPS = 1e-6
D_IN = 8 * D_MODEL + 2 * KV_WIDTH

NCH = 512
N_CHUNKS = D_IN // NCH
N_HG_CHUNKS = 4 * D_MODEL // NCH
N_PJ_CHUNKS = N_CHUNKS - N_HG_CHUNKS
PJ_AQ, PJ_AG, PJ_MH, PJ_MA, PJ_KV = 0, 2, 4, 6, 8
SLABS_PER_CHUNK = NCH // HG_DK
SEC_HQ, SEC_HF, SEC_HI, SEC_HG = range(4)

VMEM_LIMIT = 56 * 1024 * 1024

HG_C = 64
HG_SUB = 16
FR_TT = 512
HG_GROUP = FR_TT // HG_C
HG_UNITS = HG_GROUP // 2
SUBLANES = 8
BK_TT = 512


def _nt(a, b):
    return lax.dot_general(a.astype(BF16), b.astype(BF16), (((1,), (1,)), ((), ())),
                           preferred_element_type=F32)


def _sigmoid(x):
    return 1.0 / (1.0 + jnp.exp(-x))


def _wprep_kernel(w_ref, o_ref):
    scale = jnp.where(pl.program_id(0) // 2 == N_HG_CHUNKS // 2, ATT_SCALE, 1.0)
    o_ref[...] = (w_ref[...] * scale).astype(BF16)


def _wprep(w):
    kv_src = N_HG_CHUNKS + 2

    def src_chunk(c):
        return jnp.where(c < kv_src, c, jnp.where(c == N_CHUNKS - 1, kv_src, c + 1))

    return pl.pallas_call(
        _wprep_kernel,
        grid=(N_CHUNKS,),
        in_specs=[pl.BlockSpec((D_MODEL, NCH), lambda c: (0, src_chunk(c)))],
        out_specs=pl.BlockSpec((None, D_MODEL, NCH), lambda c: (c, 0, 0)),
        out_shape=jax.ShapeDtypeStruct((N_CHUNKS, D_MODEL, NCH), BF16),
        compiler_params=pltpu.CompilerParams(dimension_semantics=("arbitrary",)),
        name="wprep",
    )(w)


def _front_kernel(x_ref, xnext_ref, nw_ref, w_ref, lb_ref, hnw_ref, pj_ref, gh_ref, xn0_ref, xn1_ref,
                  hb0_ref, hb1_ref, st_ref, p_sc, qb_sc, kb_sc, dec_sc, *, n_tiles, tiles_per_seq):
    C, SUB, NB = HG_C, HG_SUB, HG_C // HG_SUB
    HALF = NCH // 2
    i = pl.program_id(0)

    @pl.when(lax.rem(jnp.maximum(i - 1, 0), tiles_per_seq) == 0)
    def _():
        st_ref[...] = jnp.zeros_like(st_ref)

    def normalise(src_ref, dst_ref):
        x = src_ref[...]
        ms = jnp.mean(x * x, axis=-1, keepdims=True)
        dst_ref[...] = (x * lax.rsqrt(ms + EPS) * nw_ref[...]).astype(BF16)

    @pl.when(i == 0)
    def _():
        normalise(x_ref, xn0_ref)

    row = lax.broadcasted_iota(jnp.int32, (C, C), 0)
    col = lax.broadcasted_iota(jnp.int32, (C, C), 1)
    rb, cb = row >> 4, col >> 4
    m0 = (rb == cb) & (col <= row)
    m1 = ((rb & 1) == 1) & (cb == rb - 1)
    m2 = (row >= 2 * SUB) & (col < 2 * SUB)
    rows = [slice(c * C, (c + 1) * C) for c in range(HG_GROUP)]
    urows = [slice(u * 2 * C, (u + 1) * 2 * C) for u in range(HG_UNITS)]
    sub_row = lax.broadcasted_iota(jnp.int32, (SUBLANES, HG_DK), 0)

    def cat(blocks):
        return jnp.concatenate(blocks, axis=0).astype(BF16)

    def cumsum_rows(g):
        groups = []
        for v in range(C // SUBLANES):
            xg = g[v * SUBLANES:(v + 1) * SUBLANES]
            shift = 1
            while shift < SUBLANES:
                xg = xg + jnp.where(sub_row >= shift, pltpu.roll(xg, shift, axis=0), 0.0)
                shift *= 2
            groups.append(xg)
        out, carry = [groups[0]], groups[0][SUBLANES - 1:SUBLANES]
        for xg in groups[1:]:
            out.append(xg + carry)
            carry = carry + xg[SUBLANES - 1:SUBLANES]
        return jnp.concatenate(out, axis=0)

    def slot(head_a, abuf, head_b, chunks, wbuf, rbuf, xn_ref):
        pieces = [(chunk, half) for chunk in chunks for half in range(2)]

        def project(flush=False):
            if not pieces:
                return
            chunk, half = pieces.pop(0)
            acc = jnp.dot(xn_ref[...], w_ref[chunk, :, half * HALF:(half + 1) * HALF],
                          preferred_element_type=F32).astype(BF16)
            if wbuf is None:
                pj_ref[chunk - N_HG_CHUNKS, :, half * HALF:(half + 1) * HALF] = acc
            else:
                for s in range(SLABS_PER_CHUNK // 2):
                    slab = chunk * SLABS_PER_CHUNK + half * (SLABS_PER_CHUNK // 2) + s
                    wbuf[slab] = acc[:, s * HG_DK:(s + 1) * HG_DK]
            if flush:
                project(True)

        def src(sec, head, buf=rbuf):
            return buf.at[sec * HG_HEADS + head]

        first_half, second_half = head_a is not None, head_b is not None

        if second_half:
            v_b = [src(SEC_HI, head_b)[urows[u], :] for u in range(HG_UNITS)]
            p_b = [p_sc[u] for u in range(HG_UNITS)]
            qb_b = [qb_sc[u] for u in range(HG_UNITS)]
            kb_b = [kb_sc[u] for u in range(HG_UNITS)]
            dec_b = [dec_sc[u] for u in range(HG_UNITS)]

        project()

        if first_half:
            lb = lb_ref[head_a]
            q, k, b = [], [], []
            for c in range(HG_GROUP):
                xq = src(SEC_HQ, head_a, abuf)[rows[c], :].astype(F32)
                xf = src(SEC_HF, head_a, abuf)[rows[c], :].astype(F32)
                f = lb + (1.0 - lb) * _sigmoid(xf)
                k.append(1.0 - f)
                q.append(xq * _sigmoid(xq))
                b.append(cumsum_rows(jnp.log(f)))

        if second_half:
            o_intra = [jnp.dot(p_b[u], v_b[u], preferred_element_type=F32) for u in range(HG_UNITS)]
            kv = [lax.dot_general(v_b[u], kb_b[u], (((0,), (0,)), ((), ())),
                                  preferred_element_type=F32) for u in range(HG_UNITS)]

        project()

        if second_half:
            st = st_ref[head_b]
            o = []
            for u in range(HG_UNITS):
                o.append(o_intra[u] + _nt(qb_b[u], st))
                st = st * dec_b[u] + kv[u]
            st_ref[head_b] = st

        project()

        if first_half:
            first_half_tail(q, k, b)

        project(flush=True)

        if second_half:
            nw = hnw_ref[head_b]
            for u in range(HG_UNITS):
                xg = src(SEC_HG, head_b)[urows[u], :].astype(F32)
                ms_o = jnp.mean(o[u] * o[u], axis=-1, keepdims=True)
                on = o[u] * lax.rsqrt(ms_o + EPS) * nw
                gh_ref[head_b, urows[u], :] = (on * (xg * _sigmoid(xg))).astype(BF16)

    def first_half_tail(q, k, b):
        p_c, qb_c, kb_c, last_c = [], [], [], []
        for c in range(HG_GROUP):
            ends = [b[c][(j + 1) * SUB - 1:(j + 1) * SUB] for j in range(NB)]
            starts = [jnp.zeros_like(ends[0])] + ends[:-1]
            last = ends[-1]
            q1, k1, qd, kd, qb, kb = [], [], [], [], [], []
            for j in range(NB):
                sl = slice(j * SUB, (j + 1) * SUB)
                bj = b[c][sl]
                q1j = q[c][sl] * jnp.exp(bj - starts[j])
                k1j = k[c][sl] * jnp.exp(ends[j] - bj)
                half = jnp.exp(0.5 * (starts[j] - ends[j]))
                q1.append(q1j)
                k1.append(k1j)
                qd.append(q1j * half)
                kd.append(k1j * half)
                qb.append(q1j * jnp.exp(starts[j]))
                kb.append(k1j * jnp.exp(last - ends[j]))
            q2 = q1[:3] + [q1[3] * jnp.exp(ends[2] - ends[1])]
            k2 = [k1[0] * jnp.exp(ends[1] - ends[0])] + k1[1:]
            s0 = _nt(cat(qd), cat(kd))
            s1 = _nt(cat(q1), cat(k1))
            s2 = _nt(cat(q2), cat(k2))
            p_c.append(jnp.where(m0, s0, jnp.where(m1, s1, jnp.where(m2, s2, 0.0))).astype(BF16))
            qb_c.append(jnp.concatenate(qb, axis=0))
            kb_c.append(jnp.concatenate(kb, axis=0))
            last_c.append(last)

        p_a, qb_a, kb_a, dec_a = [], [], [], []
        for u in range(HG_UNITS):
            c0, c1 = 2 * u, 2 * u + 1
            cross = _nt(qb_c[c1], kb_c[c0]).astype(BF16)
            p_a.append(jnp.concatenate(
                [jnp.concatenate([p_c[c0], jnp.zeros((C, C), BF16)], axis=1),
                 jnp.concatenate([cross, p_c[c1]], axis=1)], axis=0))
            qb_a.append(cat([qb_c[c0], qb_c[c1] * jnp.exp(last_c[c0])]))
            kb_a.append(cat([kb_c[c0] * jnp.exp(last_c[c1]), kb_c[c1]]))
            dec_a.append(jnp.exp(last_c[c0] + last_c[c1]))

        for u in range(HG_UNITS):
            p_sc[u] = p_a[u]
            qb_sc[u] = qb_a[u]
            kb_sc[u] = kb_a[u]
            dec_sc[u] = dec_a[u]

    def run(wbuf, rbuf, xn_ref, xn_next_ref, recur, proj):
        def mid(s, carry, to_scratch):
            slot(s if recur else None, rbuf, s - 1 if recur else None,
                 [2 * (s - 1), 2 * (s - 1) + 1] if proj else [], wbuf if to_scratch else None, rbuf,
                 xn_ref)
            return carry

        n_scratch_slots = N_HG_CHUNKS // 2
        lax.fori_loop(1, 1 + n_scratch_slots, functools.partial(mid, to_scratch=True), 0)
        lax.fori_loop(1 + n_scratch_slots, HG_HEADS, functools.partial(mid, to_scratch=False), 0)
        if proj:
            normalise(xnext_ref, xn_next_ref)
        slot(0 if proj else None, wbuf, HG_HEADS - 1 if recur else None,
             list(range(2 * (HG_HEADS - 1), N_CHUNKS)) if proj else [], None, rbuf, xn_ref)

    for parity, bufs in ((0, (hb0_ref, hb1_ref, xn0_ref, xn1_ref)), (1, (hb1_ref, hb0_ref, xn1_ref, xn0_ref))):
        on_parity = lax.rem(i, 2) == parity
        if parity == 0:
            pl.when(i == 0)(functools.partial(run, *bufs, recur=False, proj=True))
        if (n_tiles % 2) == parity:
            pl.when(i == n_tiles)(functools.partial(run, *bufs, recur=True, proj=False))
        pl.when(on_parity & (i > 0) & (i < n_tiles))(functools.partial(run, *bufs, recur=True, proj=True))


def _front(x2d, norm_w, w3, lb, hnw, seq):
    m = x2d.shape[0]
    nt = m // FR_TT
    last = nt - 1
    return pl.pallas_call(
        functools.partial(_front_kernel, n_tiles=nt, tiles_per_seq=seq // FR_TT),
        grid=(nt + 1,),
        in_specs=[
            pl.BlockSpec((FR_TT, D_MODEL), lambda i: (jnp.minimum(i, last), 0)),
            pl.BlockSpec((FR_TT, D_MODEL), lambda i: (jnp.minimum(i + 1, last), 0)),
            pl.BlockSpec((1, D_MODEL), lambda i: (0, 0)),
            pl.BlockSpec((N_CHUNKS, D_MODEL, NCH), lambda i: (0, 0, 0), pipeline_mode=pl.Buffered(1)),
            pl.BlockSpec((HG_HEADS, 1, HG_DK), lambda i: (0, 0, 0)),
            pl.BlockSpec((HG_HEADS, 1, HG_DV), lambda i: (0, 0, 0)),
        ],
        out_specs=[
            pl.BlockSpec((N_PJ_CHUNKS, FR_TT, NCH), lambda i: (0, jnp.minimum(i, last), 0)),
            pl.BlockSpec((HG_HEADS, FR_TT, HG_DV), lambda i: (0, jnp.maximum(i - 1, 0), 0)),
        ],
        out_shape=[jax.ShapeDtypeStruct((N_PJ_CHUNKS, m, NCH), BF16),
                   jax.ShapeDtypeStruct((HG_HEADS, m, HG_DV), BF16)],
        scratch_shapes=[pltpu.VMEM((FR_TT, D_MODEL), BF16),
                        pltpu.VMEM((FR_TT, D_MODEL), BF16),
                        pltpu.VMEM((4 * HG_HEADS, FR_TT, HG_DK), BF16),
                        pltpu.VMEM((4 * HG_HEADS, FR_TT, HG_DK), BF16),
                        pltpu.VMEM((HG_HEADS, HG_DV, HG_DK), F32),
                        pltpu.VMEM((HG_UNITS, 2 * HG_C, 2 * HG_C), BF16),
                        pltpu.VMEM((HG_UNITS, 2 * HG_C, HG_DK), BF16),
                        pltpu.VMEM((HG_UNITS, 2 * HG_C, HG_DK), BF16),
                        pltpu.VMEM((HG_UNITS, 1, HG_DK), F32)],
        compiler_params=pltpu.CompilerParams(
            dimension_semantics=("arbitrary",), vmem_limit_bytes=VMEM_LIMIT),
        name="front",
    )(x2d, x2d, norm_w, w3, lb, hnw)


def _back_kernel(sink_ref, q0_ref, q1_ref, kvp_ref, kv_ref, ag0_ref, ag1_ref, gh_ref, mh_ref, ma_ref,
                 x_ref, wbh32_ref, wba32_ref, wo32_ref, fnw_ref, o_ref,
                 kc_ref, vx_ref, ga0_ref, ga1_ref, wbh_ref, wba_ref, wo_ref, acc_ref, mg_ref,
                 *, n_tiles, tiles_per_seq):
    W, DH = WINDOW, ATT_DH
    half_heads = ATT_Q_HEADS // 2
    HALF = D_MODEL // 2
    QUARTER = D_MODEL // 4
    j = pl.program_id(0)

    @pl.when(j == 0)
    def _():
        for hh in range(2):
            wbh_ref[hh] = wbh32_ref[:, hh * HALF:(hh + 1) * HALF].astype(BF16)
            wba_ref[hh] = wba32_ref[:, hh * HALF:(hh + 1) * HALF].astype(BF16)
            wo_ref[hh] = wo32_ref[hh * HALF:(hh + 1) * HALF, :].astype(BF16)
        acc_ref[...] = jnp.zeros_like(acc_ref)

    kc_ref[0:W, :] = kvp_ref[:, :KV_WIDTH]
    kc_ref[W:, :] = kv_ref[:, :KV_WIDTH]
    lane = lax.broadcasted_iota(jnp.int32, (1, 2 * DH), 1)
    low = lane < DH
    for rows, src in ((slice(0, W), kvp_ref), (slice(W, None), kv_ref)):
        for slab in range(ATT_KV_HEADS // 2):
            x = src[:, KV_WIDTH + slab * 2 * DH:KV_WIDTH + (slab + 1) * 2 * DH]
            xr = jnp.concatenate([x[:, DH:], x[:, :DH]], axis=1)
            zero = jnp.zeros_like(x)
            for t, piece in enumerate((jnp.where(low, x, zero), jnp.where(low, zero, xr),
                                       jnp.where(low, xr, zero), jnp.where(low, zero, x))):
                vx_ref[rows, (4 * slab + t) * 2 * DH:(4 * slab + t + 1) * 2 * DH] = piece
    ones_lo = jnp.broadcast_to(jnp.where(low, 1.0, 0.0).astype(BF16), (2 * W, 2 * DH))
    ones_hi = jnp.broadcast_to(jnp.where(low, 0.0, 1.0).astype(BF16), (2 * W, 2 * DH))
    low_rows = jnp.broadcast_to(low, (W, 2 * DH))

    first_tile = lax.rem(jnp.minimum(j, n_tiles - 1), tiles_per_seq) == 0
    qi = lax.broadcasted_iota(jnp.int32, (W, W), 0)
    kj = lax.broadcasted_iota(jnp.int32, (W, W), 1)
    upper = kj > qi

    def head_cols(ref_pair, hd, width):
        ref = ref_pair[hd // half_heads]
        hd = hd % half_heads
        return ref, slice(hd * DH, (hd + width) * DH)

    def half(hh, carry, ga_ref, gap_ref, attend, merge):
        gh = jnp.concatenate([gh_ref[h] for h in range(HG_HEADS)], axis=1) if merge else None
        y_parts = {}

        def branch_piece(name, lhs, w_ref, k):
            y_parts[name, k] = jnp.dot(lhs, w_ref[hh, :, k * QUARTER:(k + 1) * QUARTER],
                                       preferred_element_type=F32)

        def gate():
            yh = jnp.concatenate([y_parts["h", 0], y_parts["h", 1]], axis=1)
            ya = jnp.concatenate([y_parts["a", 0], y_parts["a", 1]], axis=1)
            mg_ref[...] = (_sigmoid(mh_ref[hh].astype(F32)) * yh
                           + _sigmoid(ma_ref[hh].astype(F32)) * ya).astype(BF16)

        def out_piece(k):
            cols = slice(k * QUARTER, (k + 1) * QUARTER)
            acc_ref[:, cols] = jnp.where(hh == 0, 0.0, acc_ref[:, cols]) + jnp.dot(
                mg_ref[...], wo_ref[hh, :, cols], preferred_element_type=F32)

        pieces = [lambda: branch_piece("h", gh, wbh_ref, 0), lambda: branch_piece("h", gh, wbh_ref, 1),
                  lambda: branch_piece("a", gap_ref[...], wba_ref, 0),
                  lambda: branch_piece("a", gap_ref[...], wba_ref, 1),
                  lambda: (gate(), out_piece(0)), lambda: out_piece(1), lambda: out_piece(2),
                  lambda: out_piece(3)]

        units = [(bi, h) for bi in range(2) for h in range(ATT_KV_HEADS)]

        def scores(bi, h):
            r = pl.multiple_of((2 * hh + bi) * W, W)
            kk = kc_ref[pl.ds(r, 2 * W), h * DH:(h + 1) * DH]
            parts = []
            for hd in range(h * ATT_GROUP, (h + 1) * ATT_GROUP):
                ref, cols = head_cols((q0_ref, q1_ref), hd, 1)
                parts.append(ref[pl.ds(r, W), cols])
            return _nt(jnp.concatenate(parts, axis=0), kk)

        s_next = scores(*units[0]) if attend else None
        for idx, (bi, h) in enumerate(units):
            if attend:
                s = s_next
                if idx + 1 < len(units):
                    s_next = scores(*units[idx + 1])
            if merge:
                pieces[idx]()
            if not attend:
                continue
            r = pl.multiple_of((2 * hh + bi) * W, W)
            prev_bias = jnp.where(first_tile & (2 * hh + bi == 0), -jnp.inf, 0.0)
            probs, sink_terms = [], []
            for jj, hd in enumerate(range(h * ATT_GROUP, (h + 1) * ATT_GROUP)):
                sj = s[jj * W:(jj + 1) * W]
                c = jnp.where(upper, sj[:, :W] + prev_bias, sj[:, W:])
                sink = sink_ref[hd]
                m = jnp.maximum(jnp.max(c, axis=-1, keepdims=True), sink)
                p = jnp.exp(c - m)
                sink_terms.append(jnp.exp(sink - m))
                probs.append(jnp.concatenate(
                    [jnp.where(upper, p, 0.0), jnp.where(upper, 0.0, p)], axis=1).astype(BF16))
            w_lo = jnp.concatenate(
                [vx_ref[pl.ds(r, 2 * W), (2 * h) * 2 * DH:(2 * h + 1) * 2 * DH], ones_lo], axis=1)
            w_hi = jnp.concatenate(
                [vx_ref[pl.ds(r, 2 * W), (2 * h + 1) * 2 * DH:(2 * h + 2) * 2 * DH], ones_hi], axis=1)
            res = (jnp.dot(jnp.concatenate(probs[0::2], axis=0), w_lo, preferred_element_type=F32)
                   + jnp.dot(jnp.concatenate(probs[1::2], axis=0), w_hi, preferred_element_type=F32))
            for pair in range(ATT_GROUP // 2):
                rp = res[pair * W:(pair + 1) * W]
                den = rp[:, 2 * DH:] + jnp.where(low_rows, sink_terms[2 * pair], sink_terms[2 * pair + 1])
                j0 = h * ATT_GROUP + 2 * pair
                ag_ref, cols = head_cols((ag0_ref, ag1_ref), j0, 2)
                ag = ag_ref[pl.ds(r, W), cols].astype(F32)
                ga_ref[pl.ds(r, W), j0 * DH:(j0 + 2) * DH] = (
                    rp[:, :2 * DH] / den * (ag * _sigmoid(ag))).astype(BF16)
        return carry

    def run(ga_ref, gap_ref, attend, merge):
        lax.fori_loop(0, 2, functools.partial(half, ga_ref=ga_ref, gap_ref=gap_ref, attend=attend,
                                              merge=merge), 0)
        if merge:
            xo = x_ref[...] + acc_ref[...]
            ms = jnp.mean(xo * xo, axis=-1, keepdims=True)
            o_ref[...] = xo * lax.rsqrt(ms + EPS) * fnw_ref[...]

    for parity, bufs in ((0, (ga0_ref, ga1_ref)), (1, (ga1_ref, ga0_ref))):
        if parity == 0:
            pl.when(j == 0)(functools.partial(run, *bufs, attend=True, merge=False))
        if (n_tiles % 2) == parity:
            pl.when(j == n_tiles)(functools.partial(run, *bufs, attend=False, merge=True))
        pl.when((lax.rem(j, 2) == parity) & (j > 0) & (j < n_tiles))(
            functools.partial(run, *bufs, attend=True, merge=True))


def _back(pj, gh3, x2d, sinks, wbh, wba, wo, fnw, seq):
    m = x2d.shape[0]
    nt = m // BK_TT
    last = nt - 1
    per = BK_TT // WINDOW

    def cur(jj):
        return jnp.minimum(jj, last)

    def prv(jj):
        return jnp.maximum(jj - 1, 0)

    def chunk(c):
        return pl.BlockSpec((None, BK_TT, NCH), lambda jj, s: (c, cur(jj), 0))

    def pair(c):
        return pl.BlockSpec((2, BK_TT, NCH), lambda jj, s: (c // 2, prv(jj), 0))

    prev_kv = pl.BlockSpec((None, WINDOW, NCH),
                           lambda jj, s: (PJ_KV, jnp.maximum(cur(jj) * per - 1, 0), 0))
    tile = pl.BlockSpec((BK_TT, D_MODEL), lambda jj, s: (prv(jj), 0))
    wspec = pl.BlockSpec((D_MODEL, D_MODEL), lambda jj, s: (0, 0), pipeline_mode=pl.Buffered(1))
    half = D_MODEL // 2
    grid_spec = pltpu.PrefetchScalarGridSpec(
        num_scalar_prefetch=1,
        grid=(nt + 1,),
        in_specs=[chunk(PJ_AQ), chunk(PJ_AQ + 1), prev_kv, chunk(PJ_KV), chunk(PJ_AG), chunk(PJ_AG + 1),
                  pl.BlockSpec((HG_HEADS, BK_TT, HG_DV), lambda jj, s: (0, prv(jj), 0)),
                  pair(PJ_MH), pair(PJ_MA), tile, wspec, wspec, wspec,
                  pl.BlockSpec((1, D_MODEL), lambda jj, s: (0, 0))],
        out_specs=tile,
        scratch_shapes=[pltpu.VMEM((BK_TT + WINDOW, KV_WIDTH), BF16),
                        pltpu.VMEM((BK_TT + WINDOW, ATT_KV_HEADS * 4 * ATT_DH), BF16),
                        pltpu.VMEM((BK_TT, D_MODEL), BF16),
                        pltpu.VMEM((BK_TT, D_MODEL), BF16),
                        pltpu.VMEM((2, D_MODEL, half), BF16),
                        pltpu.VMEM((2, D_MODEL, half), BF16),
                        pltpu.VMEM((2, half, D_MODEL), BF16),
                        pltpu.VMEM((BK_TT, D_MODEL), F32),
                        pltpu.VMEM((BK_TT, half), BF16)],
    )
    return pl.pallas_call(
        functools.partial(_back_kernel, n_tiles=nt, tiles_per_seq=seq // BK_TT),
        grid_spec=grid_spec,
        out_shape=jax.ShapeDtypeStruct((m, D_MODEL), F32),
        compiler_params=pltpu.CompilerParams(
            dimension_semantics=("arbitrary",), vmem_limit_bytes=VMEM_LIMIT),
        name="back",
    )(sinks, pj, pj, pj, pj, pj, pj, gh3, pj, pj, x2d, wbh, wba, wo, fnw)


def kernel(x, norm_w, w_in, hgrn_lower_bound, hgrn_norm_w, w_branch_hgrn, attn_sinks,
           w_branch_attn, w_out, final_norm_w):
    batch, seq, _ = x.shape
    depth = norm_w.shape[0]
    assert depth == 1, "the back kernel fuses the final RMSNorm into the single layer"
    assert seq % FR_TT == 0 and seq % BK_TT == 0
    lb_all = jnp.cumsum(jax.nn.softmax(hgrn_lower_bound.astype(F32), axis=0), axis=0)
    w3 = _wprep(w_in[0])
    x2d = x.reshape(batch * seq, D_MODEL)
    pj, gh3 = _front(x2d, norm_w[0].reshape(1, D_MODEL), w3,
                     lb_all[0].reshape(HG_HEADS, 1, HG_DK),
                     hgrn_norm_w[0].reshape(HG_HEADS, 1, HG_DV), seq)
    xo = _back(pj, gh3, x2d, attn_sinks[0].astype(F32), w_branch_hgrn[0], w_branch_attn[0], w_out[0],
               final_norm_w.reshape(1, D_MODEL), seq)
    return xo.reshape(batch, seq, D_MODEL)
```

```python
import functools

import jax
import jax.numpy as jnp
from jax import lax
from jax.experimental import pallas as pl
from jax.experimental.pallas import tpu as pltpu

F32 = jnp.float32
BF16 = jnp.bfloat16

D_MODEL = 1024
HG_HEADS = 8
HG_DK = 128
HG_DV = 128
ATT_Q_HEADS = 16
ATT_KV_HEADS = 4
ATT_GROUP = ATT_Q_HEADS // ATT_KV_HEADS
ATT_DH = 64
KV_WIDTH = ATT_KV_HEADS * ATT_DH
ATT_SCALE = ATT_DH ** -0.5
assert ATT_SCALE == 0.125
WINDOW = 128
EPS = 1e-6
D_IN = 8 * D_MODEL + 2 * KV_WIDTH

NCH = 512
N_CHUNKS = D_IN // NCH
N_HG_CHUNKS = 4 * D_MODEL // NCH
N_PJ_CHUNKS = N_CHUNKS - N_HG_CHUNKS
PJ_AQ, PJ_AG, PJ_MH, PJ_MA, PJ_KV = 0, 2, 4, 6, 8
SLABS_PER_CHUNK = NCH // HG_DK
SEC_HQ, SEC_HF, SEC_HI, SEC_HG = range(4)

V7X_VMEM_BYTES = 64 * 1024 * 1024
VMEM_LIMIT = V7X_VMEM_BYTES * 7 // 8

HG_C = 64
HG_SUB = 16
FR_TT = 512
HG_GROUP = FR_TT // HG_C
HG_UNITS = HG_GROUP // 2
SUBLANES = 8
BK_TT = 512


def _nt(a, b):
    return lax.dot_general(a.astype(BF16), b.astype(BF16), (((1,), (1,)), ((), ())),
                           preferred_element_type=F32)


def _sigmoid(x):
    return 1.0 / (1.0 + jnp.exp(-x))


def _wprep_kernel(w_ref, o_ref):
    scale = jnp.where(pl.program_id(0) // 2 == N_HG_CHUNKS // 2, ATT_SCALE, 1.0)
    o_ref[...] = (w_ref[...] * scale).astype(BF16)


def _wprep(w):
    kv_src = N_HG_CHUNKS + 2

    def src_chunk(c):
        return jnp.where(c < kv_src, c, jnp.where(c == N_CHUNKS - 1, kv_src, c + 1))

    return pl.pallas_call(
        _wprep_kernel,
        grid=(N_CHUNKS,),
        in_specs=[pl.BlockSpec((D_MODEL, NCH), lambda c: (0, src_chunk(c)))],
        out_specs=pl.BlockSpec((None, D_MODEL, NCH), lambda c: (c, 0, 0)),
        out_shape=jax.ShapeDtypeStruct((N_CHUNKS, D_MODEL, NCH), BF16),
        compiler_params=pltpu.CompilerParams(dimension_semantics=("arbitrary",)),
        name="wprep",
    )(w)


def _front_kernel(x0_ref, xnext_ref, nw_ref, w_ref, lb_ref, hnw_ref, pj_ref, gh_ref, xn0_ref, xn1_ref,
                  hb0_ref, hb1_ref, st_ref, p_sc, qb_sc, kb_sc, dec_sc, *, n_tiles, tiles_per_seq):
    C, SUB, NB = HG_C, HG_SUB, HG_C // HG_SUB
    assert NB == 4, "the three score levels below are written for four sub-blocks per chunk"
    sub_shift = SUB.bit_length() - 1
    HALF = NCH // 2
    i = pl.program_id(0)

    @pl.when(lax.rem(jnp.maximum(i - 1, 0), tiles_per_seq) == 0)
    def _():
        st_ref[...] = jnp.zeros_like(st_ref)

    def normalise(src_ref, dst_ref):
        x = src_ref[...]
        ms = jnp.mean(x * x, axis=-1, keepdims=True)
        dst_ref[...] = (x * lax.rsqrt(ms + EPS) * nw_ref[...]).astype(BF16)

    @pl.when(i == 0)
    def _():
        normalise(x0_ref, xn0_ref)

    row = lax.broadcasted_iota(jnp.int32, (C, C), 0)
    col = lax.broadcasted_iota(jnp.int32, (C, C), 1)
    rb, cb = row >> sub_shift, col >> sub_shift
    m0 = (rb == cb) & (col <= row)
    m1 = ((rb & 1) == 1) & (cb == rb - 1)
    m2 = (row >= 2 * SUB) & (col < 2 * SUB)
    rows = [slice(c * C, (c + 1) * C) for c in range(HG_GROUP)]
    urows = [slice(u * 2 * C, (u + 1) * 2 * C) for u in range(HG_UNITS)]
    sub_row = lax.broadcasted_iota(jnp.int32, (SUBLANES, HG_DK), 0)

    def cat(blocks):
        return jnp.concatenate(blocks, axis=0).astype(BF16)

    def cumsum_rows(g):
        groups = []
        for v in range(C // SUBLANES):
            xg = g[v * SUBLANES:(v + 1) * SUBLANES]
            shift = 1
            while shift < SUBLANES:
                xg = xg + jnp.where(sub_row >= shift, pltpu.roll(xg, shift, axis=0), 0.0)
                shift *= 2
            groups.append(xg)
        out, carry = [groups[0]], groups[0][SUBLANES - 1:SUBLANES]
        for xg in groups[1:]:
            out.append(xg + carry)
            carry = carry + xg[SUBLANES - 1:SUBLANES]
        return jnp.concatenate(out, axis=0)

    def slot(head_a, abuf, head_b, chunks, wbuf, rbuf, xn_ref):
        pieces = [(chunk, half) for chunk in chunks for half in range(2)]

        def project(flush=False):
            if not pieces:
                return
            chunk, half = pieces.pop(0)
            acc = jnp.dot(xn_ref[...], w_ref[chunk, :, half * HALF:(half + 1) * HALF],
                          preferred_element_type=F32).astype(BF16)
            if wbuf is None:
                pj_ref[chunk - N_HG_CHUNKS, :, half * HALF:(half + 1) * HALF] = acc
            else:
                for s in range(SLABS_PER_CHUNK // 2):
                    slab = chunk * SLABS_PER_CHUNK + half * (SLABS_PER_CHUNK // 2) + s
                    wbuf[slab] = acc[:, s * HG_DK:(s + 1) * HG_DK]
            if flush:
                project(True)

        def src(sec, head, buf=rbuf):
            return buf.at[sec * HG_HEADS + head]

        first_half, second_half = head_a is not None, head_b is not None

        if second_half:
            v_b = [src(SEC_HI, head_b)[urows[u], :] for u in range(HG_UNITS)]
            p_b = [p_sc[u] for u in range(HG_UNITS)]
            qb_b = [qb_sc[u] for u in range(HG_UNITS)]
            kb_b = [kb_sc[u] for u in range(HG_UNITS)]
            dec_b = [dec_sc[u] for u in range(HG_UNITS)]

        project()

        if first_half:
            lb = lb_ref[head_a]
            q, k, b = [], [], []
            for c in range(HG_GROUP):
                xq = src(SEC_HQ, head_a, abuf)[rows[c], :].astype(F32)
                xf = src(SEC_HF, head_a, abuf)[rows[c], :].astype(F32)
                f = lb + (1.0 - lb) * _sigmoid(xf)
                k.append(1.0 - f)
                q.append(xq * _sigmoid(xq))
                b.append(cumsum_rows(jnp.log(f)))

        if second_half:
            o_intra = [jnp.dot(p_b[u], v_b[u], preferred_element_type=F32) for u in range(HG_UNITS)]
            kv = [lax.dot_general(v_b[u], kb_b[u], (((0,), (0,)), ((), ())),
                                  preferred_element_type=F32) for u in range(HG_UNITS)]

        project()

        if second_half:
            st = st_ref[head_b]
            o = []
            for u in range(HG_UNITS):
                o.append(o_intra[u] + _nt(qb_b[u], st))
                st = st * dec_b[u] + kv[u]
            st_ref[head_b] = st

        project()

        if first_half:
            first_half_tail(q, k, b)

        project(flush=True)

        if second_half:
            nw = hnw_ref[head_b]
            for u in range(HG_UNITS):
                xg = src(SEC_HG, head_b)[urows[u], :].astype(F32)
                ms_o = jnp.mean(o[u] * o[u], axis=-1, keepdims=True)
                on = o[u] * lax.rsqrt(ms_o + EPS) * nw
                gh_ref[head_b, urows[u], :] = (on * (xg * _sigmoid(xg))).astype(BF16)

    def first_half_tail(q, k, b):
        p_c, qb_c, kb_c, last_c = [], [], [], []
        for c in range(HG_GROUP):
            ends = [b[c][(j + 1) * SUB - 1:(j + 1) * SUB] for j in range(NB)]
            starts = [jnp.zeros_like(ends[0])] + ends[:-1]
            last = ends[-1]
            q1, k1, qd, kd, qb, kb = [], [], [], [], [], []
            for j in range(NB):
                sl = slice(j * SUB, (j + 1) * SUB)
                bj = b[c][sl]
                q1j = q[c][sl] * jnp.exp(bj - starts[j])
                k1j = k[c][sl] * jnp.exp(ends[j] - bj)
                half = jnp.exp(0.5 * (starts[j] - ends[j]))
                q1.append(q1j)
                k1.append(k1j)
                qd.append(q1j * half)
                kd.append(k1j * half)
                qb.append(q1j * jnp.exp(starts[j]))
                kb.append(k1j * jnp.exp(last - ends[j]))
            q2 = q1[:3] + [q1[3] * jnp.exp(ends[2] - ends[1])]
            k2 = [k1[0] * jnp.exp(ends[1] - ends[0])] + k1[1:]
            s0 = _nt(cat(qd), cat(kd))
            s1 = _nt(cat(q1), cat(k1))
            s2 = _nt(cat(q2), cat(k2))
            p_c.append(jnp.where(m0, s0, jnp.where(m1, s1, jnp.where(m2, s2, 0.0))).astype(BF16))
            qb_c.append(jnp.concatenate(qb, axis=0))
            kb_c.append(jnp.concatenate(kb, axis=0))
            last_c.append(last)

        p_a, qb_a, kb_a, dec_a = [], [], [], []
        for u in range(HG_UNITS):
            c0, c1 = 2 * u, 2 * u + 1
            cross = _nt(qb_c[c1], kb_c[c0]).astype(BF16)
            p_a.append(jnp.concatenate(
                [jnp.concatenate([p_c[c0], jnp.zeros((C, C), BF16)], axis=1),
                 jnp.concatenate([cross, p_c[c1]], axis=1)], axis=0))
            qb_a.append(cat([qb_c[c0], qb_c[c1] * jnp.exp(last_c[c0])]))
            kb_a.append(cat([kb_c[c0] * jnp.exp(last_c[c1]), kb_c[c1]]))
            dec_a.append(jnp.exp(last_c[c0] + last_c[c1]))

        for u in range(HG_UNITS):
            p_sc[u] = p_a[u]
            qb_sc[u] = qb_a[u]
            kb_sc[u] = kb_a[u]
            dec_sc[u] = dec_a[u]

    def run(wbuf, rbuf, xn_ref, xn_next_ref, recur, proj):
        def mid(s, carry, to_scratch):
            slot(s if recur else None, rbuf, s - 1 if recur else None,
                 [2 * (s - 1), 2 * (s - 1) + 1] if proj else [], wbuf if to_scratch else None, rbuf,
                 xn_ref)
            return carry

        n_scratch_slots = N_HG_CHUNKS // 2
        lax.fori_loop(1, 1 + n_scratch_slots, functools.partial(mid, to_scratch=True), 0)
        lax.fori_loop(1 + n_scratch_slots, HG_HEADS, functools.partial(mid, to_scratch=False), 0)
        if proj:
            normalise(xnext_ref, xn_next_ref)
        slot(0 if proj else None, wbuf, HG_HEADS - 1 if recur else None,
             list(range(2 * (HG_HEADS - 1), N_CHUNKS)) if proj else [], None, rbuf, xn_ref)

    for parity, bufs in ((0, (hb0_ref, hb1_ref, xn0_ref, xn1_ref)), (1, (hb1_ref, hb0_ref, xn1_ref, xn0_ref))):
        on_parity = lax.rem(i, 2) == parity
        if parity == 0:
            pl.when(i == 0)(functools.partial(run, *bufs, recur=False, proj=True))
        if (n_tiles % 2) == parity:
            pl.when(i == n_tiles)(functools.partial(run, *bufs, recur=True, proj=False))
        pl.when(on_parity & (i > 0) & (i < n_tiles))(functools.partial(run, *bufs, recur=True, proj=True))


def _front(x2d, norm_w, w3, lb, hnw, seq):
    m = x2d.shape[0]
    nt = m // FR_TT
    last = nt - 1
    return pl.pallas_call(
        functools.partial(_front_kernel, n_tiles=nt, tiles_per_seq=seq // FR_TT),
        grid=(nt + 1,),
        in_specs=[
            pl.BlockSpec((FR_TT, D_MODEL), lambda i: (0, 0), pipeline_mode=pl.Buffered(1)),
            pl.BlockSpec((FR_TT, D_MODEL), lambda i: (jnp.minimum(i + 1, last), 0)),
            pl.BlockSpec((1, D_MODEL), lambda i: (0, 0)),
            pl.BlockSpec((N_CHUNKS, D_MODEL, NCH), lambda i: (0, 0, 0), pipeline_mode=pl.Buffered(1)),
            pl.BlockSpec((HG_HEADS, 1, HG_DK), lambda i: (0, 0, 0)),
            pl.BlockSpec((HG_HEADS, 1, HG_DV), lambda i: (0, 0, 0)),
        ],
        out_specs=[
            pl.BlockSpec((N_PJ_CHUNKS, FR_TT, NCH), lambda i: (0, jnp.minimum(i, last), 0)),
            pl.BlockSpec((HG_HEADS, FR_TT, HG_DV), lambda i: (0, jnp.maximum(i - 1, 0), 0)),
        ],
        out_shape=[jax.ShapeDtypeStruct((N_PJ_CHUNKS, m, NCH), BF16),
                   jax.ShapeDtypeStruct((HG_HEADS, m, HG_DV), BF16)],
        scratch_shapes=[pltpu.VMEM((FR_TT, D_MODEL), BF16),
                        pltpu.VMEM((FR_TT, D_MODEL), BF16),
                        pltpu.VMEM((4 * HG_HEADS, FR_TT, HG_DK), BF16),
                        pltpu.VMEM((4 * HG_HEADS, FR_TT, HG_DK), BF16),
                        pltpu.VMEM((HG_HEADS, HG_DV, HG_DK), F32),
                        pltpu.VMEM((HG_UNITS, 2 * HG_C, 2 * HG_C), BF16),
                        pltpu.VMEM((HG_UNITS, 2 * HG_C, HG_DK), BF16),
                        pltpu.VMEM((HG_UNITS, 2 * HG_C, HG_DK), BF16),
                        pltpu.VMEM((HG_UNITS, 1, HG_DK), F32)],
        compiler_params=pltpu.CompilerParams(
            dimension_semantics=("arbitrary",), vmem_limit_bytes=VMEM_LIMIT),
        name="front",
    )(x2d, x2d, norm_w, w3, lb, hnw)


def _back_kernel(sink_ref, q0_ref, q1_ref, kvp_ref, kv_ref, ag0_ref, ag1_ref, gh_ref, mh_ref, ma_ref,
                 x_ref, wbh32_ref, wba32_ref, wo32_ref, fnw_ref, o_ref,
                 kc_ref, vx_ref, ga0_ref, ga1_ref, wbh_ref, wba_ref, wo_ref, acc_ref, mg_ref,
                 *, n_tiles, tiles_per_seq):
    W, DH = WINDOW, ATT_DH
    half_heads = ATT_Q_HEADS // 2
    HALF = D_MODEL // 2
    QUARTER = D_MODEL // 4
    j = pl.program_id(0)

    @pl.when(j == 0)
    def _():
        for hh in range(2):
            wbh_ref[hh] = wbh32_ref[:, hh * HALF:(hh + 1) * HALF].astype(BF16)
            wba_ref[hh] = wba32_ref[:, hh * HALF:(hh + 1) * HALF].astype(BF16)
            wo_ref[hh] = wo32_ref[hh * HALF:(hh + 1) * HALF, :].astype(BF16)
        acc_ref[...] = jnp.zeros_like(acc_ref)

    kc_ref[0:W, :] = kvp_ref[:, :KV_WIDTH]
    kc_ref[W:, :] = kv_ref[:, :KV_WIDTH]
    lane = lax.broadcasted_iota(jnp.int32, (1, 2 * DH), 1)
    low = lane < DH
    for rows, src in ((slice(0, W), kvp_ref), (slice(W, None), kv_ref)):
        for slab in range(ATT_KV_HEADS // 2):
            x = src[:, KV_WIDTH + slab * 2 * DH:KV_WIDTH + (slab + 1) * 2 * DH]
            xr = jnp.concatenate([x[:, DH:], x[:, :DH]], axis=1)
            zero = jnp.zeros_like(x)
            for t, piece in enumerate((jnp.where(low, x, zero), jnp.where(low, zero, xr),
                                       jnp.where(low, xr, zero), jnp.where(low, zero, x))):
                vx_ref[rows, (4 * slab + t) * 2 * DH:(4 * slab + t + 1) * 2 * DH] = piece
    ones_lo = jnp.broadcast_to(jnp.where(low, 1.0, 0.0).astype(BF16), (2 * W, 2 * DH))
    ones_hi = jnp.broadcast_to(jnp.where(low, 0.0, 1.0).astype(BF16), (2 * W, 2 * DH))
    low_rows = jnp.broadcast_to(low, (W, 2 * DH))

    first_tile = lax.rem(jnp.minimum(j, n_tiles - 1), tiles_per_seq) == 0
    qi = lax.broadcasted_iota(jnp.int32, (W, W), 0)
    kj = lax.broadcasted_iota(jnp.int32, (W, W), 1)
    upper = kj > qi

    def head_cols(ref_pair, hd, width):
        ref = ref_pair[hd // half_heads]
        hd = hd % half_heads
        return ref, slice(hd * DH, (hd + width) * DH)

    def half(hh, carry, ga_ref, gap_ref, attend, merge):
        gh = jnp.concatenate([gh_ref[h] for h in range(HG_HEADS)], axis=1) if merge else None
        y_parts = {}

        def branch_piece(name, lhs, w_ref, k):
            y_parts[name, k] = jnp.dot(lhs, w_ref[hh, :, k * QUARTER:(k + 1) * QUARTER],
                                       preferred_element_type=F32)

        def gate():
            yh = jnp.concatenate([y_parts["h", 0], y_parts["h", 1]], axis=1)
            ya = jnp.concatenate([y_parts["a", 0], y_parts["a", 1]], axis=1)
            mg_ref[...] = (_sigmoid(mh_ref[hh].astype(F32)) * yh
                           + _sigmoid(ma_ref[hh].astype(F32)) * ya).astype(BF16)

        def out_piece(k):
            cols = slice(k * QUARTER, (k + 1) * QUARTER)
            acc_ref[:, cols] = jnp.where(hh == 0, 0.0, acc_ref[:, cols]) + jnp.dot(
                mg_ref[...], wo_ref[hh, :, cols], preferred_element_type=F32)

        pieces = [lambda: branch_piece("h", gh, wbh_ref, 0), lambda: branch_piece("h", gh, wbh_ref, 1),
                  lambda: branch_piece("a", gap_ref[...], wba_ref, 0),
                  lambda: branch_piece("a", gap_ref[...], wba_ref, 1),
                  lambda: (gate(), out_piece(0)), lambda: out_piece(1), lambda: out_piece(2),
                  lambda: out_piece(3)]

        units = [(bi, h) for bi in range(2) for h in range(ATT_KV_HEADS)]

        def scores(bi, h):
            r = pl.multiple_of((2 * hh + bi) * W, W)
            kk = kc_ref[pl.ds(r, 2 * W), h * DH:(h + 1) * DH]
            parts = []
            for hd in range(h * ATT_GROUP, (h + 1) * ATT_GROUP):
                ref, cols = head_cols((q0_ref, q1_ref), hd, 1)
                parts.append(ref[pl.ds(r, W), cols])
            return _nt(jnp.concatenate(parts, axis=0), kk)

        s_next = scores(*units[0]) if attend else None
        for idx, (bi, h) in enumerate(units):
            if attend:
                s = s_next
                if idx + 1 < len(units):
                    s_next = scores(*units[idx + 1])
            if merge:
                pieces[idx]()
            if not attend:
                continue
            r = pl.multiple_of((2 * hh + bi) * W, W)
            prev_bias = jnp.where(first_tile & (2 * hh + bi == 0), -jnp.inf, 0.0)
            probs, sink_terms = [], []
            for jj, hd in enumerate(range(h * ATT_GROUP, (h + 1) * ATT_GROUP)):
                sj = s[jj * W:(jj + 1) * W]
                c = jnp.where(upper, sj[:, :W] + prev_bias, sj[:, W:])
                sink = sink_ref[hd]
                m = jnp.maximum(jnp.max(c, axis=-1, keepdims=True), sink)
                p = jnp.exp(c - m)
                sink_terms.append(jnp.exp(sink - m))
                probs.append(jnp.concatenate(
                    [jnp.where(upper, p, 0.0), jnp.where(upper, 0.0, p)], axis=1).astype(BF16))
            w_lo = jnp.concatenate(
                [vx_ref[pl.ds(r, 2 * W), (2 * h) * 2 * DH:(2 * h + 1) * 2 * DH], ones_lo], axis=1)
            w_hi = jnp.concatenate(
                [vx_ref[pl.ds(r, 2 * W), (2 * h + 1) * 2 * DH:(2 * h + 2) * 2 * DH], ones_hi], axis=1)
            res = (jnp.dot(jnp.concatenate(probs[0::2], axis=0), w_lo, preferred_element_type=F32)
                   + jnp.dot(jnp.concatenate(probs[1::2], axis=0), w_hi, preferred_element_type=F32))
            for pair in range(ATT_GROUP // 2):
                rp = res[pair * W:(pair + 1) * W]
                den = rp[:, 2 * DH:] + jnp.where(low_rows, sink_terms[2 * pair], sink_terms[2 * pair + 1])
                j0 = h * ATT_GROUP + 2 * pair
                ag_ref, cols = head_cols((ag0_ref, ag1_ref), j0, 2)
                ag = ag_ref[pl.ds(r, W), cols].astype(F32)
                ga_ref[pl.ds(r, W), j0 * DH:(j0 + 2) * DH] = (
                    rp[:, :2 * DH] / den * (ag * _sigmoid(ag))).astype(BF16)
        return carry

    def run(ga_ref, gap_ref, attend, merge):
        lax.fori_loop(0, 2, functools.partial(half, ga_ref=ga_ref, gap_ref=gap_ref, attend=attend,
                                              merge=merge), 0)
        if merge:
            xo = x_ref[...] + acc_ref[...]
            ms = jnp.mean(xo * xo, axis=-1, keepdims=True)
            o_ref[...] = xo * lax.rsqrt(ms + EPS) * fnw_ref[...]

    for parity, bufs in ((0, (ga0_ref, ga1_ref)), (1, (ga1_ref, ga0_ref))):
        if parity == 0:
            pl.when(j == 0)(functools.partial(run, *bufs, attend=True, merge=False))
        if (n_tiles % 2) == parity:
            pl.when(j == n_tiles)(functools.partial(run, *bufs, attend=False, merge=True))
        pl.when((lax.rem(j, 2) == parity) & (j > 0) & (j < n_tiles))(
            functools.partial(run, *bufs, attend=True, merge=True))


def _back(pj, gh3, x2d, sinks, wbh, wba, wo, fnw, seq):
    m = x2d.shape[0]
    nt = m // BK_TT
    last = nt - 1
    per = BK_TT // WINDOW

    def cur(jj):
        return jnp.minimum(jj, last)

    def prv(jj):
        return jnp.maximum(jj - 1, 0)

    def chunk(c):
        return pl.BlockSpec((None, BK_TT, NCH), lambda jj, s: (c, cur(jj), 0))

    def pair(c):
        return pl.BlockSpec((2, BK_TT, NCH), lambda jj, s: (c // 2, prv(jj), 0))

    prev_kv = pl.BlockSpec((None, WINDOW, NCH),
                           lambda jj, s: (PJ_KV, jnp.maximum(cur(jj) * per - 1, 0), 0))
    tile = pl.BlockSpec((BK_TT, D_MODEL), lambda jj, s: (prv(jj), 0))
    wspec = pl.BlockSpec((D_MODEL, D_MODEL), lambda jj, s: (0, 0), pipeline_mode=pl.Buffered(1))
    half = D_MODEL // 2
    grid_spec = pltpu.PrefetchScalarGridSpec(
        num_scalar_prefetch=1,
        grid=(nt + 1,),
        in_specs=[chunk(PJ_AQ), chunk(PJ_AQ + 1), prev_kv, chunk(PJ_KV), chunk(PJ_AG), chunk(PJ_AG + 1),
                  pl.BlockSpec((HG_HEADS, BK_TT, HG_DV), lambda jj, s: (0, prv(jj), 0)),
                  pair(PJ_MH), pair(PJ_MA), tile, wspec, wspec, wspec,
                  pl.BlockSpec((1, D_MODEL), lambda jj, s: (0, 0))],
        out_specs=tile,
        scratch_shapes=[pltpu.VMEM((BK_TT + WINDOW, KV_WIDTH), BF16),
                        pltpu.VMEM((BK_TT + WINDOW, ATT_KV_HEADS * 4 * ATT_DH), BF16),
                        pltpu.VMEM((BK_TT, D_MODEL), BF16),
                        pltpu.VMEM((BK_TT, D_MODEL), BF16),
                        pltpu.VMEM((2, D_MODEL, half), BF16),
                        pltpu.VMEM((2, D_MODEL, half), BF16),
                        pltpu.VMEM((2, half, D_MODEL), BF16),
                        pltpu.VMEM((BK_TT, D_MODEL), F32),
                        pltpu.VMEM((BK_TT, half), BF16)],
    )
    return pl.pallas_call(
        functools.partial(_back_kernel, n_tiles=nt, tiles_per_seq=seq // BK_TT),
        grid_spec=grid_spec,
        out_shape=jax.ShapeDtypeStruct((m, D_MODEL), F32),
        compiler_params=pltpu.CompilerParams(
            dimension_semantics=("arbitrary",), vmem_limit_bytes=VMEM_LIMIT),
        name="back",
    )(sinks, pj, pj, pj, pj, pj, pj, gh3, pj, pj, x2d, wbh, wba, wo, fnw)


def kernel(x, norm_w, w_in, hgrn_lower_bound, hgrn_norm_w, w_branch_hgrn, attn_sinks,
           w_branch_attn, w_out, final_norm_w):
    batch, seq, _ = x.shape
    depth = norm_w.shape[0]
    assert depth == 1, "the back kernel fuses the final RMSNorm into the single layer"
    assert seq % FR_TT == 0 and seq % BK_TT == 0
    lb_all = jnp.cumsum(jax.nn.softmax(hgrn_lower_bound.astype(F32), axis=0), axis=0)
    w3 = _wprep(w_in[0])
    x2d = x.reshape(batch * seq, D_MODEL)
    pj, gh3 = _front(x2d, norm_w[0].reshape(1, D_MODEL), w3,
                     lb_all[0].reshape(HG_HEADS, 1, HG_DK),
                     hgrn_norm_w[0].reshape(HG_HEADS, 1, HG_DV), seq)
    xo = _back(pj, gh3, x2d, attn_sinks[0].astype(F32), w_branch_hgrn[0], w_branch_attn[0], w_out[0],
               final_norm_w.reshape(1, D_MODEL), seq)
    return xo.reshape(batch, seq, D_MODEL)
```

```python
import functools

import jax
import jax.numpy as jnp
from jax import lax
from jax.experimental import pallas as pl
from jax.experimental.pallas import tpu as pltpu

F32 = jnp.float32
BF16 = jnp.bfloat16

D_MODEL = 1024
HG_HEADS = 8
HG_DK = 128
HG_DV = 128
ATT_Q_HEADS = 16
ATT_KV_HEADS = 4
ATT_GROUP = ATT_Q_HEADS // ATT_KV_HEADS
ATT_DH = 64
KV_WIDTH = ATT_KV_HEADS * ATT_DH
ATT_SCALE = ATT_DH ** -0.5
assert ATT_SCALE == 0.125
WINDOW = 128
EPS = 1e-6
D_IN = 8 * D_MODEL + 2 * KV_WIDTH

NCH = 512
N_CHUNKS = D_IN // NCH
N_HG_CHUNKS = 4 * D_MODEL // NCH
N_PJ_CHUNKS = N_CHUNKS - N_HG_CHUNKS
PJ_AQ, PJ_AG, PJ_MH, PJ_MA, PJ_KV = 0, 2, 4, 6, 8
SLABS_PER_CHUNK = NCH // HG_DK
SEC_HQ, SEC_HF, SEC_HI, SEC_HG = range(4)

V7X_VMEM_BYTES = 64 * 1024 * 1024
VMEM_LIMIT = V7X_VMEM_BYTES * 7 // 8

HG_C = 64
HG_SUB = 16
FR_TT = 512
HG_GROUP = FR_TT // HG_C
HG_UNITS = HG_GROUP // 2
SUBLANES = 8
BK_TT = 512


def _nt(a, b):
    return lax.dot_general(a.astype(BF16), b.astype(BF16), (((1,), (1,)), ((), ())),
                           preferred_element_type=F32)


def _sigmoid(x):
    return 1.0 / (1.0 + jnp.exp(-x))


WP_ROWS = 256


def _wprep_kernel(w_ref, o_ref):
    kv_src = N_HG_CHUNKS + 2
    for c in range(N_CHUNKS):
        src = c if c < kv_src else (kv_src if c == N_CHUNKS - 1 else c + 1)
        blk = w_ref[:, src * NCH:(src + 1) * NCH]
        if c // 2 == N_HG_CHUNKS // 2:
            blk = blk * ATT_SCALE
        o_ref[c] = blk.astype(BF16)


def _wprep(w):
    return pl.pallas_call(
        _wprep_kernel,
        grid=(D_MODEL // WP_ROWS,),
        in_specs=[pl.BlockSpec((WP_ROWS, D_IN), lambda r: (r, 0))],
        out_specs=pl.BlockSpec((N_CHUNKS, WP_ROWS, NCH), lambda r: (0, r, 0)),
        out_shape=jax.ShapeDtypeStruct((N_CHUNKS, D_MODEL, NCH), BF16),
        compiler_params=pltpu.CompilerParams(
            dimension_semantics=("arbitrary",), vmem_limit_bytes=VMEM_LIMIT),
        name="wprep",
    )(w)


def _front_kernel(x0_ref, xnext_ref, nw_ref, w_ref, lb_ref, hnw_ref, pj_ref, gh_ref, xn0_ref, xn1_ref,
                  hb0_ref, hb1_ref, st_ref, p_sc, qb_sc, kb_sc, dec_sc, *, n_tiles, tiles_per_seq):
    C, SUB, NB = HG_C, HG_SUB, HG_C // HG_SUB
    assert NB == 4, "the three score levels below are written for four sub-blocks per chunk"
    sub_shift = SUB.bit_length() - 1
    HALF = NCH // 2
    i = pl.program_id(0)

    @pl.when(lax.rem(jnp.maximum(i - 1, 0), tiles_per_seq) == 0)
    def _():
        st_ref[...] = jnp.zeros_like(st_ref)

    def normalise(src_ref, dst_ref):
        x = src_ref[...]
        ms = jnp.mean(x * x, axis=-1, keepdims=True)
        dst_ref[...] = (x * lax.rsqrt(ms + EPS) * nw_ref[...]).astype(BF16)

    @pl.when(i == 0)
    def _():
        normalise(x0_ref, xn0_ref)

    row = lax.broadcasted_iota(jnp.int32, (C, C), 0)
    col = lax.broadcasted_iota(jnp.int32, (C, C), 1)
    rb, cb = row >> sub_shift, col >> sub_shift
    m0 = (rb == cb) & (col <= row)
    m1 = ((rb & 1) == 1) & (cb == rb - 1)
    m2 = (row >= 2 * SUB) & (col < 2 * SUB)
    rows = [slice(c * C, (c + 1) * C) for c in range(HG_GROUP)]
    urows = [slice(u * 2 * C, (u + 1) * 2 * C) for u in range(HG_UNITS)]
    sub_row = lax.broadcasted_iota(jnp.int32, (SUBLANES, HG_DK), 0)

    def cat(blocks):
        return jnp.concatenate(blocks, axis=0).astype(BF16)

    def cumsum_rows(g):
        groups = []
        for v in range(C // SUBLANES):
            xg = g[v * SUBLANES:(v + 1) * SUBLANES]
            shift = 1
            while shift < SUBLANES:
                xg = xg + jnp.where(sub_row >= shift, pltpu.roll(xg, shift, axis=0), 0.0)
                shift *= 2
            groups.append(xg)
        out, carry = [groups[0]], groups[0][SUBLANES - 1:SUBLANES]
        for xg in groups[1:]:
            out.append(xg + carry)
            carry = carry + xg[SUBLANES - 1:SUBLANES]
        return jnp.concatenate(out, axis=0)

    def slot(head_a, abuf, head_b, chunks, wbuf, rbuf, xn_ref):
        pieces = [(chunk, half) for chunk in chunks for half in range(2)]

        def project(flush=False):
            if not pieces:
                return
            chunk, half = pieces.pop(0)
            acc = jnp.dot(xn_ref[...], w_ref[chunk, :, half * HALF:(half + 1) * HALF],
                          preferred_element_type=F32).astype(BF16)
            if wbuf is None:
                pj_ref[chunk - N_HG_CHUNKS, :, half * HALF:(half + 1) * HALF] = acc
            else:
                for s in range(SLABS_PER_CHUNK // 2):
                    slab = chunk * SLABS_PER_CHUNK + half * (SLABS_PER_CHUNK // 2) + s
                    wbuf[slab] = acc[:, s * HG_DK:(s + 1) * HG_DK]
            if flush:
                project(True)

        def src(sec, head, buf=rbuf):
            return buf.at[sec * HG_HEADS + head]

        first_half, second_half = head_a is not None, head_b is not None

        if second_half:
            v_b = [src(SEC_HI, head_b)[urows[u], :] for u in range(HG_UNITS)]
            p_b = [p_sc[u] for u in range(HG_UNITS)]
            qb_b = [qb_sc[u] for u in range(HG_UNITS)]
            kb_b = [kb_sc[u] for u in range(HG_UNITS)]
            dec_b = [dec_sc[u] for u in range(HG_UNITS)]

        project()

        if first_half:
            lb = lb_ref[head_a]
            q, k, b = [], [], []
            for c in range(HG_GROUP):
                xq = src(SEC_HQ, head_a, abuf)[rows[c], :].astype(F32)
                xf = src(SEC_HF, head_a, abuf)[rows[c], :].astype(F32)
                f = lb + (1.0 - lb) * _sigmoid(xf)
                k.append(1.0 - f)
                q.append(xq * _sigmoid(xq))
                b.append(cumsum_rows(jnp.log(f)))

        if second_half:
            o_intra = [jnp.dot(p_b[u], v_b[u], preferred_element_type=F32) for u in range(HG_UNITS)]
            kv = [lax.dot_general(v_b[u], kb_b[u], (((0,), (0,)), ((), ())),
                                  preferred_element_type=F32) for u in range(HG_UNITS)]

        project()

        if second_half:
            st = st_ref[head_b]
            o = []
            for u in range(HG_UNITS):
                o.append(o_intra[u] + _nt(qb_b[u], st))
                st = st * dec_b[u] + kv[u]
            st_ref[head_b] = st

        project()

        if first_half:
            first_half_tail(q, k, b)

        project(flush=True)

        if second_half:
            nw = hnw_ref[head_b]
            for u in range(HG_UNITS):
                xg = src(SEC_HG, head_b)[urows[u], :].astype(F32)
                ms_o = jnp.mean(o[u] * o[u], axis=-1, keepdims=True)
                on = o[u] * lax.rsqrt(ms_o + EPS) * nw
                gh_ref[head_b, urows[u], :] = (on * (xg * _sigmoid(xg))).astype(BF16)

    def first_half_tail(q, k, b):
        p_c, qb_c, kb_c, last_c = [], [], [], []
        for c in range(HG_GROUP):
            ends = [b[c][(j + 1) * SUB - 1:(j + 1) * SUB] for j in range(NB)]
            starts = [jnp.zeros_like(ends[0])] + ends[:-1]
            last = ends[-1]
            q1, k1, qd, kd, qb, kb = [], [], [], [], [], []
            for j in range(NB):
                sl = slice(j * SUB, (j + 1) * SUB)
                bj = b[c][sl]
                q1j = q[c][sl] * jnp.exp(bj - starts[j])
                k1j = k[c][sl] * jnp.exp(ends[j] - bj)
                half = jnp.exp(0.5 * (starts[j] - ends[j]))
                q1.append(q1j)
                k1.append(k1j)
                qd.append(q1j * half)
                kd.append(k1j * half)
                qb.append(q1j * jnp.exp(starts[j]))
                kb.append(k1j * jnp.exp(last - ends[j]))
            q2 = q1[:3] + [q1[3] * jnp.exp(ends[2] - ends[1])]
            k2 = [k1[0] * jnp.exp(ends[1] - ends[0])] + k1[1:]
            s0 = _nt(cat(qd), cat(kd))
            s1 = _nt(cat(q1), cat(k1))
            s2 = _nt(cat(q2), cat(k2))
            p_c.append(jnp.where(m0, s0, jnp.where(m1, s1, jnp.where(m2, s2, 0.0))).astype(BF16))
            qb_c.append(jnp.concatenate(qb, axis=0))
            kb_c.append(jnp.concatenate(kb, axis=0))
            last_c.append(last)

        p_a, qb_a, kb_a, dec_a = [], [], [], []
        for u in range(HG_UNITS):
            c0, c1 = 2 * u, 2 * u + 1
            cross = _nt(qb_c[c1], kb_c[c0]).astype(BF16)
            p_a.append(jnp.concatenate(
                [jnp.concatenate([p_c[c0], jnp.zeros((C, C), BF16)], axis=1),
                 jnp.concatenate([cross, p_c[c1]], axis=1)], axis=0))
            qb_a.append(cat([qb_c[c0], qb_c[c1] * jnp.exp(last_c[c0])]))
            kb_a.append(cat([kb_c[c0] * jnp.exp(last_c[c1]), kb_c[c1]]))
            dec_a.append(jnp.exp(last_c[c0] + last_c[c1]))

        for u in range(HG_UNITS):
            p_sc[u] = p_a[u]
            qb_sc[u] = qb_a[u]
            kb_sc[u] = kb_a[u]
            dec_sc[u] = dec_a[u]

    def run(wbuf, rbuf, xn_ref, xn_next_ref, recur, proj):
        def mid(s, carry, to_scratch):
            slot(s if recur else None, rbuf, s - 1 if recur else None,
                 [2 * (s - 1), 2 * (s - 1) + 1] if proj else [], wbuf if to_scratch else None, rbuf,
                 xn_ref)
            return carry

        n_scratch_slots = N_HG_CHUNKS // 2
        lax.fori_loop(1, 1 + n_scratch_slots, functools.partial(mid, to_scratch=True), 0)
        lax.fori_loop(1 + n_scratch_slots, HG_HEADS, functools.partial(mid, to_scratch=False), 0)
        if proj:
            normalise(xnext_ref, xn_next_ref)
        slot(0 if proj else None, wbuf, HG_HEADS - 1 if recur else None,
             list(range(2 * (HG_HEADS - 1), N_CHUNKS)) if proj else [], None, rbuf, xn_ref)

    for parity, bufs in ((0, (hb0_ref, hb1_ref, xn0_ref, xn1_ref)), (1, (hb1_ref, hb0_ref, xn1_ref, xn0_ref))):
        on_parity = lax.rem(i, 2) == parity
        if parity == 0:
            pl.when(i == 0)(functools.partial(run, *bufs, recur=False, proj=True))
        if (n_tiles % 2) == parity:
            pl.when(i == n_tiles)(functools.partial(run, *bufs, recur=True, proj=False))
        pl.when(on_parity & (i > 0) & (i < n_tiles))(functools.partial(run, *bufs, recur=True, proj=True))


def _front(x2d, norm_w, w3, lb, hnw, seq):
    m = x2d.shape[0]
    nt = m // FR_TT
    last = nt - 1
    return pl.pallas_call(
        functools.partial(_front_kernel, n_tiles=nt, tiles_per_seq=seq // FR_TT),
        grid=(nt + 1,),
        in_specs=[
            pl.BlockSpec((FR_TT, D_MODEL), lambda i: (0, 0), pipeline_mode=pl.Buffered(1)),
            pl.BlockSpec((FR_TT, D_MODEL), lambda i: (jnp.minimum(i + 1, last), 0)),
            pl.BlockSpec((1, D_MODEL), lambda i: (0, 0)),
            pl.BlockSpec((N_CHUNKS, D_MODEL, NCH), lambda i: (0, 0, 0), pipeline_mode=pl.Buffered(1)),
            pl.BlockSpec((HG_HEADS, 1, HG_DK), lambda i: (0, 0, 0)),
            pl.BlockSpec((HG_HEADS, 1, HG_DV), lambda i: (0, 0, 0)),
        ],
        out_specs=[
            pl.BlockSpec((N_PJ_CHUNKS, FR_TT, NCH), lambda i: (0, jnp.minimum(i, last), 0)),
            pl.BlockSpec((HG_HEADS, FR_TT, HG_DV), lambda i: (0, jnp.maximum(i - 1, 0), 0)),
        ],
        out_shape=[jax.ShapeDtypeStruct((N_PJ_CHUNKS, m, NCH), BF16),
                   jax.ShapeDtypeStruct((HG_HEADS, m, HG_DV), BF16)],
        scratch_shapes=[pltpu.VMEM((FR_TT, D_MODEL), BF16),
                        pltpu.VMEM((FR_TT, D_MODEL), BF16),
                        pltpu.VMEM((4 * HG_HEADS, FR_TT, HG_DK), BF16),
                        pltpu.VMEM((4 * HG_HEADS, FR_TT, HG_DK), BF16),
                        pltpu.VMEM((HG_HEADS, HG_DV, HG_DK), F32),
                        pltpu.VMEM((HG_UNITS, 2 * HG_C, 2 * HG_C), BF16),
                        pltpu.VMEM((HG_UNITS, 2 * HG_C, HG_DK), BF16),
                        pltpu.VMEM((HG_UNITS, 2 * HG_C, HG_DK), BF16),
                        pltpu.VMEM((HG_UNITS, 1, HG_DK), F32)],
        compiler_params=pltpu.CompilerParams(
            dimension_semantics=("arbitrary",), vmem_limit_bytes=VMEM_LIMIT),
        name="front",
    )(x2d, x2d, norm_w, w3, lb, hnw)


def _back_kernel(sink_ref, q0_ref, q1_ref, kvp_ref, kv_ref, ag0_ref, ag1_ref, gh_ref, mh_ref, ma_ref,
                 x_ref, wbh32_ref, wba32_ref, wo32_ref, fnw_ref, o_ref,
                 kc_ref, vx_ref, ga0_ref, ga1_ref, wbh_ref, wba_ref, wo_ref, acc_ref, mg_ref,
                 *, n_tiles, tiles_per_seq):
    W, DH = WINDOW, ATT_DH
    half_heads = ATT_Q_HEADS // 2
    HALF = D_MODEL // 2
    QUARTER = D_MODEL // 4
    j = pl.program_id(0)

    @pl.when(j == 0)
    def _():
        for hh in range(2):
            wbh_ref[hh] = wbh32_ref[:, hh * HALF:(hh + 1) * HALF].astype(BF16)
            wba_ref[hh] = wba32_ref[:, hh * HALF:(hh + 1) * HALF].astype(BF16)
            wo_ref[hh] = wo32_ref[hh * HALF:(hh + 1) * HALF, :].astype(BF16)
        acc_ref[...] = jnp.zeros_like(acc_ref)

    kc_ref[0:W, :] = kvp_ref[:, :KV_WIDTH]
    kc_ref[W:, :] = kv_ref[:, :KV_WIDTH]
    lane = lax.broadcasted_iota(jnp.int32, (1, 2 * DH), 1)
    low = lane < DH
    for rows, src in ((slice(0, W), kvp_ref), (slice(W, None), kv_ref)):
        for slab in range(ATT_KV_HEADS // 2):
            x = src[:, KV_WIDTH + slab * 2 * DH:KV_WIDTH + (slab + 1) * 2 * DH]
            xr = jnp.concatenate([x[:, DH:], x[:, :DH]], axis=1)
            zero = jnp.zeros_like(x)
            for t, piece in enumerate((jnp.where(low, x, zero), jnp.where(low, zero, xr),
                                       jnp.where(low, xr, zero), jnp.where(low, zero, x))):
                vx_ref[rows, (4 * slab + t) * 2 * DH:(4 * slab + t + 1) * 2 * DH] = piece
    ones_lo = jnp.broadcast_to(jnp.where(low, 1.0, 0.0).astype(BF16), (2 * W, 2 * DH))
    ones_hi = jnp.broadcast_to(jnp.where(low, 0.0, 1.0).astype(BF16), (2 * W, 2 * DH))
    low_rows = jnp.broadcast_to(low, (W, 2 * DH))

    first_tile = lax.rem(jnp.minimum(j, n_tiles - 1), tiles_per_seq) == 0
    qi = lax.broadcasted_iota(jnp.int32, (W, W), 0)
    kj = lax.broadcasted_iota(jnp.int32, (W, W), 1)
    upper = kj > qi

    def head_cols(ref_pair, hd, width):
        ref = ref_pair[hd // half_heads]
        hd = hd % half_heads
        return ref, slice(hd * DH, (hd + width) * DH)

    def half(hh, carry, ga_ref, gap_ref, attend, merge):
        gh = jnp.concatenate([gh_ref[h] for h in range(HG_HEADS)], axis=1) if merge else None
        y_parts = {}

        def branch_piece(name, lhs, w_ref, k):
            y_parts[name, k] = jnp.dot(lhs, w_ref[hh, :, k * QUARTER:(k + 1) * QUARTER],
                                       preferred_element_type=F32)

        def gate():
            yh = jnp.concatenate([y_parts["h", 0], y_parts["h", 1]], axis=1)
            ya = jnp.concatenate([y_parts["a", 0], y_parts["a", 1]], axis=1)
            mg_ref[...] = (_sigmoid(mh_ref[hh].astype(F32)) * yh
                           + _sigmoid(ma_ref[hh].astype(F32)) * ya).astype(BF16)

        def out_piece(k):
            cols = slice(k * QUARTER, (k + 1) * QUARTER)
            acc_ref[:, cols] = jnp.where(hh == 0, 0.0, acc_ref[:, cols]) + jnp.dot(
                mg_ref[...], wo_ref[hh, :, cols], preferred_element_type=F32)

        pieces = [lambda: branch_piece("h", gh, wbh_ref, 0), lambda: branch_piece("h", gh, wbh_ref, 1),
                  lambda: branch_piece("a", gap_ref[...], wba_ref, 0),
                  lambda: branch_piece("a", gap_ref[...], wba_ref, 1),
                  lambda: (gate(), out_piece(0)), lambda: out_piece(1), lambda: out_piece(2),
                  lambda: out_piece(3)]

        units = [(bi, h) for bi in range(2) for h in range(ATT_KV_HEADS)]

        def scores(bi, h):
            r = pl.multiple_of((2 * hh + bi) * W, W)
            kk = kc_ref[pl.ds(r, 2 * W), h * DH:(h + 1) * DH]
            parts = []
            for hd in range(h * ATT_GROUP, (h + 1) * ATT_GROUP):
                ref, cols = head_cols((q0_ref, q1_ref), hd, 1)
                parts.append(ref[pl.ds(r, W), cols])
            return _nt(jnp.concatenate(parts, axis=0), kk)

        s_next = scores(*units[0]) if attend else None
        for idx, (bi, h) in enumerate(units):
            if attend:
                s = s_next
                if idx + 1 < len(units):
                    s_next = scores(*units[idx + 1])
            if merge:
                pieces[idx]()
            if not attend:
                continue
            r = pl.multiple_of((2 * hh + bi) * W, W)
            prev_bias = jnp.where(first_tile & (2 * hh + bi == 0), -jnp.inf, 0.0)
            probs, sink_terms = [], []
            for jj, hd in enumerate(range(h * ATT_GROUP, (h + 1) * ATT_GROUP)):
                sj = s[jj * W:(jj + 1) * W]
                c = jnp.where(upper, sj[:, :W] + prev_bias, sj[:, W:])
                sink = sink_ref[hd]
                m = jnp.maximum(jnp.max(c, axis=-1, keepdims=True), sink)
                p = jnp.exp(c - m)
                sink_terms.append(jnp.exp(sink - m))
                probs.append(jnp.concatenate(
                    [jnp.where(upper, p, 0.0), jnp.where(upper, 0.0, p)], axis=1).astype(BF16))
            w_lo = jnp.concatenate(
                [vx_ref[pl.ds(r, 2 * W), (2 * h) * 2 * DH:(2 * h + 1) * 2 * DH], ones_lo], axis=1)
            w_hi = jnp.concatenate(
                [vx_ref[pl.ds(r, 2 * W), (2 * h + 1) * 2 * DH:(2 * h + 2) * 2 * DH], ones_hi], axis=1)
            res = (jnp.dot(jnp.concatenate(probs[0::2], axis=0), w_lo, preferred_element_type=F32)
                   + jnp.dot(jnp.concatenate(probs[1::2], axis=0), w_hi, preferred_element_type=F32))
            for pair in range(ATT_GROUP // 2):
                rp = res[pair * W:(pair + 1) * W]
                den = rp[:, 2 * DH:] + jnp.where(low_rows, sink_terms[2 * pair], sink_terms[2 * pair + 1])
                j0 = h * ATT_GROUP + 2 * pair
                ag_ref, cols = head_cols((ag0_ref, ag1_ref), j0, 2)
                ag = ag_ref[pl.ds(r, W), cols].astype(F32)
                ga_ref[pl.ds(r, W), j0 * DH:(j0 + 2) * DH] = (
                    rp[:, :2 * DH] / den * (ag * _sigmoid(ag))).astype(BF16)
        return carry

    def run(ga_ref, gap_ref, attend, merge):
        lax.fori_loop(0, 2, functools.partial(half, ga_ref=ga_ref, gap_ref=gap_ref, attend=attend,
                                              merge=merge), 0)
        if merge:
            xo = x_ref[...] + acc_ref[...]
            ms = jnp.mean(xo * xo, axis=-1, keepdims=True)
            o_ref[...] = xo * lax.rsqrt(ms + EPS) * fnw_ref[...]

    for parity, bufs in ((0, (ga0_ref, ga1_ref)), (1, (ga1_ref, ga0_ref))):
        if parity == 0:
            pl.when(j == 0)(functools.partial(run, *bufs, attend=True, merge=False))
        if (n_tiles % 2) == parity:
            pl.when(j == n_tiles)(functools.partial(run, *bufs, attend=False, merge=True))
        pl.when((lax.rem(j, 2) == parity) & (j > 0) & (j < n_tiles))(
            functools.partial(run, *bufs, attend=True, merge=True))


def _back(pj, gh3, x2d, sinks, wbh, wba, wo, fnw, seq):
    m = x2d.shape[0]
    nt = m // BK_TT
    last = nt - 1
    per = BK_TT // WINDOW

    def cur(jj):
        return jnp.minimum(jj, last)

    def prv(jj):
        return jnp.maximum(jj - 1, 0)

    def chunk(c):
        return pl.BlockSpec((None, BK_TT, NCH), lambda jj, s: (c, cur(jj), 0))

    def pair(c):
        return pl.BlockSpec((2, BK_TT, NCH), lambda jj, s: (c // 2, prv(jj), 0))

    prev_kv = pl.BlockSpec((None, WINDOW, NCH),
                           lambda jj, s: (PJ_KV, jnp.maximum(cur(jj) * per - 1, 0), 0))
    tile = pl.BlockSpec((BK_TT, D_MODEL), lambda jj, s: (prv(jj), 0))
    wspec = pl.BlockSpec((D_MODEL, D_MODEL), lambda jj, s: (0, 0), pipeline_mode=pl.Buffered(1))
    half = D_MODEL // 2
    grid_spec = pltpu.PrefetchScalarGridSpec(
        num_scalar_prefetch=1,
        grid=(nt + 1,),
        in_specs=[chunk(PJ_AQ), chunk(PJ_AQ + 1), prev_kv, chunk(PJ_KV), chunk(PJ_AG), chunk(PJ_AG + 1),
                  pl.BlockSpec((HG_HEADS, BK_TT, HG_DV), lambda jj, s: (0, prv(jj), 0)),
                  pair(PJ_MH), pair(PJ_MA), tile, wspec, wspec, wspec,
                  pl.BlockSpec((1, D_MODEL), lambda jj, s: (0, 0))],
        out_specs=tile,
        scratch_shapes=[pltpu.VMEM((BK_TT + WINDOW, KV_WIDTH), BF16),
                        pltpu.VMEM((BK_TT + WINDOW, ATT_KV_HEADS * 4 * ATT_DH), BF16),
                        pltpu.VMEM((BK_TT, D_MODEL), BF16),
                        pltpu.VMEM((BK_TT, D_MODEL), BF16),
                        pltpu.VMEM((2, D_MODEL, half), BF16),
                        pltpu.VMEM((2, D_MODEL, half), BF16),
                        pltpu.VMEM((2, half, D_MODEL), BF16),
                        pltpu.VMEM((BK_TT, D_MODEL), F32),
                        pltpu.VMEM((BK_TT, half), BF16)],
    )
    return pl.pallas_call(
        functools.partial(_back_kernel, n_tiles=nt, tiles_per_seq=seq // BK_TT),
        grid_spec=grid_spec,
        out_shape=jax.ShapeDtypeStruct((m, D_MODEL), F32),
        compiler_params=pltpu.CompilerParams(
            dimension_semantics=("arbitrary",), vmem_limit_bytes=VMEM_LIMIT),
        name="back",
    )(sinks, pj, pj, pj, pj, pj, pj, gh3, pj, pj, x2d, wbh, wba, wo, fnw)


def kernel(x, norm_w, w_in, hgrn_lower_bound, hgrn_norm_w, w_branch_hgrn, attn_sinks,
           w_branch_attn, w_out, final_norm_w):
    batch, seq, _ = x.shape
    depth = norm_w.shape[0]
    assert depth == 1, "the back kernel fuses the final RMSNorm into the single layer"
    assert seq % FR_TT == 0 and seq % BK_TT == 0
    lb_all = jnp.cumsum(jax.nn.softmax(hgrn_lower_bound.astype(F32), axis=0), axis=0)
    w3 = _wprep(w_in[0])
    x2d = x.reshape(batch * seq, D_MODEL)
    pj, gh3 = _front(x2d, norm_w[0].reshape(1, D_MODEL), w3,
                     lb_all[0].reshape(HG_HEADS, 1, HG_DK),
                     hgrn_norm_w[0].reshape(HG_HEADS, 1, HG_DV), seq)
    xo = _back(pj, gh3, x2d, attn_sinks[0].astype(F32), w_branch_hgrn[0], w_branch_attn[0], w_out[0],
               final_norm_w.reshape(1, D_MODEL), seq)
    return xo.reshape(batch, seq, D_MODEL)
```

```python
import functools

import jax
import jax.numpy as jnp
from jax import lax
from jax.experimental import pallas as pl
from jax.experimental.pallas import tpu as pltpu

F32 = jnp.float32
BF16 = jnp.bfloat16

D_MODEL = 1024
HG_HEADS = 8
HG_DK = 128
HG_DV = 128
ATT_Q_HEADS = 16
ATT_KV_HEADS = 4
ATT_GROUP = ATT_Q_HEADS // ATT_KV_HEADS
ATT_DH = 64
KV_WIDTH = ATT_KV_HEADS * ATT_DH
ATT_SCALE = ATT_DH ** -0.5
assert ATT_SCALE == 0.125
WINDOW = 128
EPS = 1e-6
D_IN = 8 * D_MODEL + 2 * KV_WIDTH

NCH = 512
N_CHUNKS = D_IN // NCH
N_HG_CHUNKS = 4 * D_MODEL // NCH
N_PJ_CHUNKS = N_CHUNKS - N_HG_CHUNKS
PJ_AQ, PJ_AG, PJ_MH, PJ_MA, PJ_KV = 0, 2, 4, 6, 8
SLABS_PER_CHUNK = NCH // HG_DK
SEC_HQ, SEC_HF, SEC_HI, SEC_HG = range(4)

V7X_VMEM_BYTES = 64 * 1024 * 1024
VMEM_LIMIT = V7X_VMEM_BYTES * 7 // 8

HG_C = 64
HG_SUB = 16
FR_TT = 512
HG_GROUP = FR_TT // HG_C
HG_UNITS = HG_GROUP // 2
SUBLANES = 8
BK_TT = 512


def _nt(a, b):
    return lax.dot_general(a.astype(BF16), b.astype(BF16), (((1,), (1,)), ((), ())),
                           preferred_element_type=F32)


def _sigmoid(x):
    return 1.0 / (1.0 + jnp.exp(-x))


WP_ROWS = 256


def _wprep_kernel(w_ref, o_ref):
    kv_src = N_HG_CHUNKS + 2
    for c in range(N_CHUNKS):
        src = c if c < kv_src else (kv_src if c == N_CHUNKS - 1 else c + 1)
        blk = w_ref[:, src * NCH:(src + 1) * NCH]
        if c // 2 == N_HG_CHUNKS // 2:
            blk = blk * ATT_SCALE
        o_ref[c] = blk.astype(BF16)


def _wprep(w):
    return pl.pallas_call(
        _wprep_kernel,
        grid=(D_MODEL // WP_ROWS,),
        in_specs=[pl.BlockSpec((WP_ROWS, D_IN), lambda r: (r, 0))],
        out_specs=pl.BlockSpec((N_CHUNKS, WP_ROWS, NCH), lambda r: (0, r, 0)),
        out_shape=jax.ShapeDtypeStruct((N_CHUNKS, D_MODEL, NCH), BF16),
        compiler_params=pltpu.CompilerParams(
            dimension_semantics=("arbitrary",), vmem_limit_bytes=VMEM_LIMIT),
        name="wprep",
    )(w)


def _front_kernel(x0_ref, xnext_ref, nw_ref, w_ref, lb_ref, hnw_ref, pj_ref, gh_ref, xn0_ref, xn1_ref,
                  hb0_ref, hb1_ref, st_ref, p_sc, qb_sc, kb_sc, dec_sc, *, n_tiles, tiles_per_seq):
    C, SUB, NB = HG_C, HG_SUB, HG_C // HG_SUB
    assert NB == 4, "the three score levels below are written for four sub-blocks per chunk"
    sub_shift = SUB.bit_length() - 1
    HALF = NCH // 2
    i = pl.program_id(0)

    @pl.when(lax.rem(jnp.maximum(i - 1, 0), tiles_per_seq) == 0)
    def _():
        st_ref[...] = jnp.zeros_like(st_ref)

    def normalise(src_ref, dst_ref):
        x = src_ref[...]
        ms = jnp.mean(x * x, axis=-1, keepdims=True)
        dst_ref[...] = (x * lax.rsqrt(ms + EPS) * nw_ref[...]).astype(BF16)

    @pl.when(i == 0)
    def _():
        normalise(x0_ref, xn0_ref)

    row = lax.broadcasted_iota(jnp.int32, (C, C), 0)
    col = lax.broadcasted_iota(jnp.int32, (C, C), 1)
    rb, cb = row >> sub_shift, col >> sub_shift
    m0 = (rb == cb) & (col <= row)
    m1 = ((rb & 1) == 1) & (cb == rb - 1)
    m2 = (row >= 2 * SUB) & (col < 2 * SUB)
    rows = [slice(c * C, (c + 1) * C) for c in range(HG_GROUP)]
    urows = [slice(u * 2 * C, (u + 1) * 2 * C) for u in range(HG_UNITS)]
    sub_row = lax.broadcasted_iota(jnp.int32, (SUBLANES, HG_DK), 0)

    def cat(blocks):
        return jnp.concatenate(blocks, axis=0).astype(BF16)

    def cumsum_rows(g):
        groups = []
        for v in range(C // SUBLANES):
            xg = g[v * SUBLANES:(v + 1) * SUBLANES]
            shift = 1
            while shift < SUBLANES:
                xg = xg + jnp.where(sub_row >= shift, pltpu.roll(xg, shift, axis=0), 0.0)
                shift *= 2
            groups.append(xg)
        out, carry = [groups[0]], groups[0][SUBLANES - 1:SUBLANES]
        for xg in groups[1:]:
            out.append(xg + carry)
            carry = carry + xg[SUBLANES - 1:SUBLANES]
        return jnp.concatenate(out, axis=0)

    def slot(head_a, abuf, head_b, chunks, wbuf, rbuf, xn_ref):
        pieces = [(chunk, half) for chunk in chunks for half in range(2)]

        def project(flush=False):
            if not pieces:
                return
            chunk, half = pieces.pop(0)
            acc = jnp.dot(xn_ref[...], w_ref[chunk, :, half * HALF:(half + 1) * HALF],
                          preferred_element_type=F32).astype(BF16)
            if wbuf is None:
                pj_ref[chunk - N_HG_CHUNKS, :, half * HALF:(half + 1) * HALF] = acc
            else:
                for s in range(SLABS_PER_CHUNK // 2):
                    slab = chunk * SLABS_PER_CHUNK + half * (SLABS_PER_CHUNK // 2) + s
                    wbuf[slab] = acc[:, s * HG_DK:(s + 1) * HG_DK]
            if flush:
                project(True)

        def src(sec, head, buf=rbuf):
            return buf.at[sec * HG_HEADS + head]

        first_half, second_half = head_a is not None, head_b is not None

        if second_half:
            v_b = [src(SEC_HI, head_b)[urows[u], :] for u in range(HG_UNITS)]
            p_b = [p_sc[u] for u in range(HG_UNITS)]
            qb_b = [qb_sc[u] for u in range(HG_UNITS)]
            kb_b = [kb_sc[u] for u in range(HG_UNITS)]
            dec_b = [dec_sc[u] for u in range(HG_UNITS)]

        project()

        if first_half:
            lb = lb_ref[head_a]
            q, k, b = [], [], []
            for c in range(HG_GROUP):
                xq = src(SEC_HQ, head_a, abuf)[rows[c], :].astype(F32)
                xf = src(SEC_HF, head_a, abuf)[rows[c], :].astype(F32)
                f = lb + (1.0 - lb) * _sigmoid(xf)
                k.append(1.0 - f)
                q.append(xq * _sigmoid(xq))
                b.append(cumsum_rows(jnp.log(f)))

        if second_half:
            o_intra = [jnp.dot(p_b[u], v_b[u], preferred_element_type=F32) for u in range(HG_UNITS)]
            kv = [lax.dot_general(v_b[u], kb_b[u], (((0,), (0,)), ((), ())),
                                  preferred_element_type=F32) for u in range(HG_UNITS)]

        project()

        if second_half:
            st = st_ref[head_b]
            o = []
            for u in range(HG_UNITS):
                o.append(o_intra[u] + _nt(qb_b[u], st))
                st = st * dec_b[u] + kv[u]
            st_ref[head_b] = st

        project()

        if first_half:
            unit_handover(*chunk_scores(q, k, b))

        project(flush=True)

        if second_half:
            nw = hnw_ref[head_b]
            for u in range(HG_UNITS):
                xg = src(SEC_HG, head_b)[urows[u], :].astype(F32)
                ms_o = jnp.mean(o[u] * o[u], axis=-1, keepdims=True)
                on = o[u] * lax.rsqrt(ms_o + EPS) * nw
                gh_ref[head_b, urows[u], :] = (on * (xg * _sigmoid(xg))).astype(BF16)

    def chunk_scores(q, k, b):
        p_c, qb_c, kb_c, last_c = [], [], [], []
        for c in range(HG_GROUP):
            ends = [b[c][(j + 1) * SUB - 1:(j + 1) * SUB] for j in range(NB)]
            starts = [jnp.zeros_like(ends[0])] + ends[:-1]
            last = ends[-1]
            q1, k1, qd, kd, qb, kb = [], [], [], [], [], []
            for j in range(NB):
                sl = slice(j * SUB, (j + 1) * SUB)
                bj = b[c][sl]
                q1j = q[c][sl] * jnp.exp(bj - starts[j])
                k1j = k[c][sl] * jnp.exp(ends[j] - bj)
                half = jnp.exp(0.5 * (starts[j] - ends[j]))
                q1.append(q1j)
                k1.append(k1j)
                qd.append(q1j * half)
                kd.append(k1j * half)
                qb.append(q1j * jnp.exp(starts[j]))
                kb.append(k1j * jnp.exp(last - ends[j]))
            q2 = q1[:3] + [q1[3] * jnp.exp(ends[2] - ends[1])]
            k2 = [k1[0] * jnp.exp(ends[1] - ends[0])] + k1[1:]
            s0 = _nt(cat(qd), cat(kd))
            s1 = _nt(cat(q1), cat(k1))
            s2 = _nt(cat(q2), cat(k2))
            p_c.append(jnp.where(m0, s0, jnp.where(m1, s1, jnp.where(m2, s2, 0.0))).astype(BF16))
            qb_c.append(jnp.concatenate(qb, axis=0))
            kb_c.append(jnp.concatenate(kb, axis=0))
            last_c.append(last)
        return p_c, qb_c, kb_c, last_c

    def unit_handover(p_c, qb_c, kb_c, last_c):
        p_a, qb_a, kb_a, dec_a = [], [], [], []
        for u in range(HG_UNITS):
            c0, c1 = 2 * u, 2 * u + 1
            cross = _nt(qb_c[c1], kb_c[c0]).astype(BF16)
            p_a.append(jnp.concatenate(
                [jnp.concatenate([p_c[c0], jnp.zeros((C, C), BF16)], axis=1),
                 jnp.concatenate([cross, p_c[c1]], axis=1)], axis=0))
            qb_a.append(cat([qb_c[c0], qb_c[c1] * jnp.exp(last_c[c0])]))
            kb_a.append(cat([kb_c[c0] * jnp.exp(last_c[c1]), kb_c[c1]]))
            dec_a.append(jnp.exp(last_c[c0] + last_c[c1]))

        for u in range(HG_UNITS):
            p_sc[u] = p_a[u]
            qb_sc[u] = qb_a[u]
            kb_sc[u] = kb_a[u]
            dec_sc[u] = dec_a[u]

    def run(wbuf, rbuf, xn_ref, xn_next_ref, recur, proj):
        def mid(s, carry, to_scratch):
            slot(s if recur else None, rbuf, s - 1 if recur else None,
                 [2 * (s - 1), 2 * (s - 1) + 1] if proj else [], wbuf if to_scratch else None, rbuf,
                 xn_ref)
            return carry

        n_scratch_slots = N_HG_CHUNKS // 2
        lax.fori_loop(1, 1 + n_scratch_slots, functools.partial(mid, to_scratch=True), 0)
        lax.fori_loop(1 + n_scratch_slots, HG_HEADS, functools.partial(mid, to_scratch=False), 0)
        if proj:
            normalise(xnext_ref, xn_next_ref)
        slot(0 if proj else None, wbuf, HG_HEADS - 1 if recur else None,
             list(range(2 * (HG_HEADS - 1), N_CHUNKS)) if proj else [], None, rbuf, xn_ref)

    for parity, bufs in ((0, (hb0_ref, hb1_ref, xn0_ref, xn1_ref)), (1, (hb1_ref, hb0_ref, xn1_ref, xn0_ref))):
        on_parity = lax.rem(i, 2) == parity
        if parity == 0:
            pl.when(i == 0)(functools.partial(run, *bufs, recur=False, proj=True))
        if (n_tiles % 2) == parity:
            pl.when(i == n_tiles)(functools.partial(run, *bufs, recur=True, proj=False))
        pl.when(on_parity & (i > 0) & (i < n_tiles))(functools.partial(run, *bufs, recur=True, proj=True))


def _front(x2d, norm_w, w3, lb, hnw, seq):
    m = x2d.shape[0]
    nt = m // FR_TT
    last = nt - 1
    return pl.pallas_call(
        functools.partial(_front_kernel, n_tiles=nt, tiles_per_seq=seq // FR_TT),
        grid=(nt + 1,),
        in_specs=[
            pl.BlockSpec((FR_TT, D_MODEL), lambda i: (0, 0), pipeline_mode=pl.Buffered(1)),
            pl.BlockSpec((FR_TT, D_MODEL), lambda i: (jnp.minimum(i + 1, last), 0)),
            pl.BlockSpec((1, D_MODEL), lambda i: (0, 0)),
            pl.BlockSpec((N_CHUNKS, D_MODEL, NCH), lambda i: (0, 0, 0), pipeline_mode=pl.Buffered(1)),
            pl.BlockSpec((HG_HEADS, 1, HG_DK), lambda i: (0, 0, 0)),
            pl.BlockSpec((HG_HEADS, 1, HG_DV), lambda i: (0, 0, 0)),
        ],
        out_specs=[
            pl.BlockSpec((N_PJ_CHUNKS, FR_TT, NCH), lambda i: (0, jnp.minimum(i, last), 0)),
            pl.BlockSpec((HG_HEADS, FR_TT, HG_DV), lambda i: (0, jnp.maximum(i - 1, 0), 0)),
        ],
        out_shape=[jax.ShapeDtypeStruct((N_PJ_CHUNKS, m, NCH), BF16),
                   jax.ShapeDtypeStruct((HG_HEADS, m, HG_DV), BF16)],
        scratch_shapes=[pltpu.VMEM((FR_TT, D_MODEL), BF16),
                        pltpu.VMEM((FR_TT, D_MODEL), BF16),
                        pltpu.VMEM((4 * HG_HEADS, FR_TT, HG_DK), BF16),
                        pltpu.VMEM((4 * HG_HEADS, FR_TT, HG_DK), BF16),
                        pltpu.VMEM((HG_HEADS, HG_DV, HG_DK), F32),
                        pltpu.VMEM((HG_UNITS, 2 * HG_C, 2 * HG_C), BF16),
                        pltpu.VMEM((HG_UNITS, 2 * HG_C, HG_DK), BF16),
                        pltpu.VMEM((HG_UNITS, 2 * HG_C, HG_DK), BF16),
                        pltpu.VMEM((HG_UNITS, 1, HG_DK), F32)],
        compiler_params=pltpu.CompilerParams(
            dimension_semantics=("arbitrary",), vmem_limit_bytes=VMEM_LIMIT),
        name="front",
    )(x2d, x2d, norm_w, w3, lb, hnw)


def _back_kernel(sink_ref, q0_ref, q1_ref, kvp_ref, kv_ref, ag0_ref, ag1_ref, gh_ref, mh_ref, ma_ref,
                 x_ref, wbh32_ref, wba32_ref, wo32_ref, fnw_ref, o_ref,
                 kc_ref, vx_ref, ga0_ref, ga1_ref, wbh_ref, wba_ref, wo_ref, acc_ref, mg_ref,
                 *, n_tiles, tiles_per_seq):
    W, DH = WINDOW, ATT_DH
    half_heads = ATT_Q_HEADS // 2
    HALF = D_MODEL // 2
    QUARTER = D_MODEL // 4
    j = pl.program_id(0)

    @pl.when(j == 0)
    def _():
        for hh in range(2):
            wbh_ref[hh] = wbh32_ref[:, hh * HALF:(hh + 1) * HALF].astype(BF16)
            wba_ref[hh] = wba32_ref[:, hh * HALF:(hh + 1) * HALF].astype(BF16)
            wo_ref[hh] = wo32_ref[hh * HALF:(hh + 1) * HALF, :].astype(BF16)
        acc_ref[...] = jnp.zeros_like(acc_ref)

    kc_ref[0:W, :] = kvp_ref[:, :KV_WIDTH]
    kc_ref[W:, :] = kv_ref[:, :KV_WIDTH]
    lane = lax.broadcasted_iota(jnp.int32, (1, 2 * DH), 1)
    low = lane < DH
    for rows, src in ((slice(0, W), kvp_ref), (slice(W, None), kv_ref)):
        for slab in range(ATT_KV_HEADS // 2):
            x = src[:, KV_WIDTH + slab * 2 * DH:KV_WIDTH + (slab + 1) * 2 * DH]
            xr = jnp.concatenate([x[:, DH:], x[:, :DH]], axis=1)
            zero = jnp.zeros_like(x)
            for t, piece in enumerate((jnp.where(low, x, zero), jnp.where(low, zero, xr),
                                       jnp.where(low, xr, zero), jnp.where(low, zero, x))):
                vx_ref[rows, (4 * slab + t) * 2 * DH:(4 * slab + t + 1) * 2 * DH] = piece
    ones_lo = jnp.broadcast_to(jnp.where(low, 1.0, 0.0).astype(BF16), (2 * W, 2 * DH))
    ones_hi = jnp.broadcast_to(jnp.where(low, 0.0, 1.0).astype(BF16), (2 * W, 2 * DH))
    low_rows = jnp.broadcast_to(low, (W, 2 * DH))

    first_tile = lax.rem(jnp.minimum(j, n_tiles - 1), tiles_per_seq) == 0
    qi = lax.broadcasted_iota(jnp.int32, (W, W), 0)
    kj = lax.broadcasted_iota(jnp.int32, (W, W), 1)
    upper = kj > qi

    def head_cols(ref_pair, hd, width):
        ref = ref_pair[hd // half_heads]
        hd = hd % half_heads
        return ref, slice(hd * DH, (hd + width) * DH)

    def half(hh, carry, ga_ref, gap_ref, attend, merge):
        gh = jnp.concatenate([gh_ref[h] for h in range(HG_HEADS)], axis=1) if merge else None
        y_parts = {}

        def branch_piece(name, lhs, w_ref, k):
            y_parts[name, k] = jnp.dot(lhs, w_ref[hh, :, k * QUARTER:(k + 1) * QUARTER],
                                       preferred_element_type=F32)

        def gate():
            yh = jnp.concatenate([y_parts["h", 0], y_parts["h", 1]], axis=1)
            ya = jnp.concatenate([y_parts["a", 0], y_parts["a", 1]], axis=1)
            mg_ref[...] = (_sigmoid(mh_ref[hh].astype(F32)) * yh
                           + _sigmoid(ma_ref[hh].astype(F32)) * ya).astype(BF16)

        def out_piece(k):
            cols = slice(k * QUARTER, (k + 1) * QUARTER)
            acc_ref[:, cols] = jnp.where(hh == 0, 0.0, acc_ref[:, cols]) + jnp.dot(
                mg_ref[...], wo_ref[hh, :, cols], preferred_element_type=F32)

        pieces = [lambda: branch_piece("h", gh, wbh_ref, 0), lambda: branch_piece("h", gh, wbh_ref, 1),
                  lambda: branch_piece("a", gap_ref[...], wba_ref, 0),
                  lambda: branch_piece("a", gap_ref[...], wba_ref, 1),
                  lambda: (gate(), out_piece(0)), lambda: out_piece(1), lambda: out_piece(2),
                  lambda: out_piece(3)]

        units = [(bi, h) for bi in range(2) for h in range(ATT_KV_HEADS)]

        def scores(bi, h):
            r = pl.multiple_of((2 * hh + bi) * W, W)
            kk = kc_ref[pl.ds(r, 2 * W), h * DH:(h + 1) * DH]
            parts = []
            for hd in range(h * ATT_GROUP, (h + 1) * ATT_GROUP):
                ref, cols = head_cols((q0_ref, q1_ref), hd, 1)
                parts.append(ref[pl.ds(r, W), cols])
            return _nt(jnp.concatenate(parts, axis=0), kk)

        s_next = scores(*units[0]) if attend else None
        for idx, (bi, h) in enumerate(units):
            if merge:
                pieces[idx]()
            if attend:
                s = s_next
                if idx + 1 < len(units):
                    s_next = scores(*units[idx + 1])
            if not attend:
                continue
            r = pl.multiple_of((2 * hh + bi) * W, W)
            prev_bias = jnp.where(first_tile & (2 * hh + bi == 0), -jnp.inf, 0.0)
            probs, sink_terms = [], []
            for jj, hd in enumerate(range(h * ATT_GROUP, (h + 1) * ATT_GROUP)):
                sj = s[jj * W:(jj + 1) * W]
                c = jnp.where(upper, sj[:, :W] + prev_bias, sj[:, W:])
                sink = sink_ref[hd]
                m = jnp.maximum(jnp.max(c, axis=-1, keepdims=True), sink)
                p = jnp.exp(c - m)
                sink_terms.append(jnp.exp(sink - m))
                probs.append(jnp.concatenate(
                    [jnp.where(upper, p, 0.0), jnp.where(upper, 0.0, p)], axis=1).astype(BF16))
            w_lo = jnp.concatenate(
                [vx_ref[pl.ds(r, 2 * W), (2 * h) * 2 * DH:(2 * h + 1) * 2 * DH], ones_lo], axis=1)
            w_hi = jnp.concatenate(
                [vx_ref[pl.ds(r, 2 * W), (2 * h + 1) * 2 * DH:(2 * h + 2) * 2 * DH], ones_hi], axis=1)
            res = (jnp.dot(jnp.concatenate(probs[0::2], axis=0), w_lo, preferred_element_type=F32)
                   + jnp.dot(jnp.concatenate(probs[1::2], axis=0), w_hi, preferred_element_type=F32))
            for pair in range(ATT_GROUP // 2):
                rp = res[pair * W:(pair + 1) * W]
                den = rp[:, 2 * DH:] + jnp.where(low_rows, sink_terms[2 * pair], sink_terms[2 * pair + 1])
                j0 = h * ATT_GROUP + 2 * pair
                ag_ref, cols = head_cols((ag0_ref, ag1_ref), j0, 2)
                ag = ag_ref[pl.ds(r, W), cols].astype(F32)
                ga_ref[pl.ds(r, W), j0 * DH:(j0 + 2) * DH] = (
                    rp[:, :2 * DH] / den * (ag * _sigmoid(ag))).astype(BF16)
        return carry

    def run(ga_ref, gap_ref, attend, merge):
        lax.fori_loop(0, 2, functools.partial(half, ga_ref=ga_ref, gap_ref=gap_ref, attend=attend,
                                              merge=merge), 0)
        if merge:
            xo = x_ref[...] + acc_ref[...]
            ms = jnp.mean(xo * xo, axis=-1, keepdims=True)
            o_ref[...] = xo * lax.rsqrt(ms + EPS) * fnw_ref[...]

    for parity, bufs in ((0, (ga0_ref, ga1_ref)), (1, (ga1_ref, ga0_ref))):
        if parity == 0:
            pl.when(j == 0)(functools.partial(run, *bufs, attend=True, merge=False))
        if (n_tiles % 2) == parity:
            pl.when(j == n_tiles)(functools.partial(run, *bufs, attend=False, merge=True))
        pl.when((lax.rem(j, 2) == parity) & (j > 0) & (j < n_tiles))(
            functools.partial(run, *bufs, attend=True, merge=True))


def _back(pj, gh3, x2d, sinks, wbh, wba, wo, fnw, seq):
    m = x2d.shape[0]
    nt = m // BK_TT
    last = nt - 1
    per = BK_TT // WINDOW

    def cur(jj):
        return jnp.minimum(jj, last)

    def prv(jj):
        return jnp.maximum(jj - 1, 0)

    def chunk(c):
        return pl.BlockSpec((None, BK_TT, NCH), lambda jj, s: (c, cur(jj), 0))

    def pair(c):
        return pl.BlockSpec((2, BK_TT, NCH), lambda jj, s: (c // 2, prv(jj), 0))

    prev_kv = pl.BlockSpec((None, WINDOW, NCH),
                           lambda jj, s: (PJ_KV, jnp.maximum(cur(jj) * per - 1, 0), 0))
    tile = pl.BlockSpec((BK_TT, D_MODEL), lambda jj, s: (prv(jj), 0))
    wspec = pl.BlockSpec((D_MODEL, D_MODEL), lambda jj, s: (0, 0), pipeline_mode=pl.Buffered(1))
    half = D_MODEL // 2
    grid_spec = pltpu.PrefetchScalarGridSpec(
        num_scalar_prefetch=1,
        grid=(nt + 1,),
        in_specs=[chunk(PJ_AQ), chunk(PJ_AQ + 1), prev_kv, chunk(PJ_KV), chunk(PJ_AG), chunk(PJ_AG + 1),
                  pl.BlockSpec((HG_HEADS, BK_TT, HG_DV), lambda jj, s: (0, prv(jj), 0)),
                  pair(PJ_MH), pair(PJ_MA), tile, wspec, wspec, wspec,
                  pl.BlockSpec((1, D_MODEL), lambda jj, s: (0, 0))],
        out_specs=tile,
        scratch_shapes=[pltpu.VMEM((BK_TT + WINDOW, KV_WIDTH), BF16),
                        pltpu.VMEM((BK_TT + WINDOW, ATT_KV_HEADS * 4 * ATT_DH), BF16),
                        pltpu.VMEM((BK_TT, D_MODEL), BF16),
                        pltpu.VMEM((BK_TT, D_MODEL), BF16),
                        pltpu.VMEM((2, D_MODEL, half), BF16),
                        pltpu.VMEM((2, D_MODEL, half), BF16),
                        pltpu.VMEM((2, half, D_MODEL), BF16),
                        pltpu.VMEM((BK_TT, D_MODEL), F32),
                        pltpu.VMEM((BK_TT, half), BF16)],
    )
    return pl.pallas_call(
        functools.partial(_back_kernel, n_tiles=nt, tiles_per_seq=seq // BK_TT),
        grid_spec=grid_spec,
        out_shape=jax.ShapeDtypeStruct((m, D_MODEL), F32),
        compiler_params=pltpu.CompilerParams(
            dimension_semantics=("arbitrary",), vmem_limit_bytes=VMEM_LIMIT),
        name="back",
    )(sinks, pj, pj, pj, pj, pj, pj, gh3, pj, pj, x2d, wbh, wba, wo, fnw)


def kernel(x, norm_w, w_in, hgrn_lower_bound, hgrn_norm_w, w_branch_hgrn, attn_sinks,
           w_branch_attn, w_out, final_norm_w):
    batch, seq, _ = x.shape
    depth = norm_w.shape[0]
    assert depth == 1, "the back kernel fuses the final RMSNorm into the single layer"
    assert seq % FR_TT == 0 and seq % BK_TT == 0
    lb_all = jnp.cumsum(jax.nn.softmax(hgrn_lower_bound.astype(F32), axis=0), axis=0)
    w3 = _wprep(w_in[0])
    x2d = x.reshape(batch * seq, D_MODEL)
    pj, gh3 = _front(x2d, norm_w[0].reshape(1, D_MODEL), w3,
                     lb_all[0].reshape(HG_HEADS, 1, HG_DK),
                     hgrn_norm_w[0].reshape(HG_HEADS, 1, HG_DV), seq)
    xo = _back(pj, gh3, x2d, attn_sinks[0].astype(F32), w_branch_hgrn[0], w_branch_attn[0], w_out[0],
               final_norm_w.reshape(1, D_MODEL), seq)
    return xo.reshape(batch, seq, D_MODEL)
```

```python
import functools

import jax
import jax.numpy as jnp
from jax import lax
from jax.experimental import pallas as pl
from jax.experimental.pallas import tpu as pltpu

F32 = jnp.float32
BF16 = jnp.bfloat16

D_MODEL = 1024
HG_HEADS = 8
HG_DK = 128
HG_DV = 128
ATT_Q_HEADS = 16
ATT_KV_HEADS = 4
ATT_GROUP = ATT_Q_HEADS // ATT_KV_HEADS
ATT_DH = 64
KV_WIDTH = ATT_KV_HEADS * ATT_DH
ATT_SCALE = ATT_DH ** -0.5
assert ATT_SCALE == 0.125
WINDOW = 128
EPS = 1e-6
D_IN = 8 * D_MODEL + 2 * KV_WIDTH

NCH = 512
N_CHUNKS = D_IN // NCH
N_HG_CHUNKS = 4 * D_MODEL // NCH
N_PJ_CHUNKS = N_CHUNKS - N_HG_CHUNKS
PJ_AQ, PJ_AG, PJ_MH, PJ_MA, PJ_KV = 0, 2, 4, 6, 8
SLABS_PER_CHUNK = NCH // HG_DK
SEC_HQ, SEC_HF, SEC_HI, SEC_HG = range(4)

V7X_VMEM_BYTES = 64 * 1024 * 1024
VMEM_LIMIT = V7X_VMEM_BYTES * 7 // 8

HG_C = 64
HG_SUB = 16
FR_TT = 512
HG_GROUP = FR_TT // HG_C
HG_UNITS = HG_GROUP // 2
SUBLANES = 8
HG_SLOT_HEADS = 2
BK_TT = 512


def _nt(a, b):
    return lax.dot_general(a.astype(BF16), b.astype(BF16), (((1,), (1,)), ((), ())),
                           preferred_element_type=F32)


def _sigmoid(x):
    return 1.0 / (1.0 + jnp.exp(-x))


WP_ROWS = 256


def _wprep_kernel(w_ref, o_ref):
    kv_src = N_HG_CHUNKS + 2
    for c in range(N_CHUNKS):
        src = c if c < kv_src else (kv_src if c == N_CHUNKS - 1 else c + 1)
        blk = w_ref[:, src * NCH:(src + 1) * NCH]
        if c // 2 == N_HG_CHUNKS // 2:
            blk = blk * ATT_SCALE
        o_ref[c] = blk.astype(BF16)


def _wprep(w):
    return pl.pallas_call(
        _wprep_kernel,
        grid=(D_MODEL // WP_ROWS,),
        in_specs=[pl.BlockSpec((WP_ROWS, D_IN), lambda r: (r, 0))],
        out_specs=pl.BlockSpec((N_CHUNKS, WP_ROWS, NCH), lambda r: (0, r, 0)),
        out_shape=jax.ShapeDtypeStruct((N_CHUNKS, D_MODEL, NCH), BF16),
        compiler_params=pltpu.CompilerParams(
            dimension_semantics=("arbitrary",), vmem_limit_bytes=VMEM_LIMIT),
        name="wprep",
    )(w)


def _front_kernel(x0_ref, xnext_ref, nw_ref, w_ref, lb_ref, hnw_ref, pj_ref, gh_ref, xn0_ref, xn1_ref,
                  hb0_ref, hb1_ref, st_ref, p_sc, qb_sc, kb_sc, dec_sc, *, n_tiles, tiles_per_seq):
    C, SUB, NB = HG_C, HG_SUB, HG_C // HG_SUB
    assert NB == 4, "the three score levels below are written for four sub-blocks per chunk"
    sub_shift = SUB.bit_length() - 1
    HALF = NCH // 2
    i = pl.program_id(0)

    @pl.when(lax.rem(jnp.maximum(i - 1, 0), tiles_per_seq) == 0)
    def _():
        st_ref[...] = jnp.zeros_like(st_ref)

    def normalise(src_ref, dst_ref):
        x = src_ref[...]
        ms = jnp.mean(x * x, axis=-1, keepdims=True)
        dst_ref[...] = (x * lax.rsqrt(ms + EPS) * nw_ref[...]).astype(BF16)

    @pl.when(i == 0)
    def _():
        normalise(x0_ref, xn0_ref)

    row = lax.broadcasted_iota(jnp.int32, (C, C), 0)
    col = lax.broadcasted_iota(jnp.int32, (C, C), 1)
    rb, cb = row >> sub_shift, col >> sub_shift
    m0 = (rb == cb) & (col <= row)
    m1 = ((rb & 1) == 1) & (cb == rb - 1)
    m2 = (row >= 2 * SUB) & (col < 2 * SUB)
    rows = [slice(c * C, (c + 1) * C) for c in range(HG_GROUP)]
    urows = [slice(u * 2 * C, (u + 1) * 2 * C) for u in range(HG_UNITS)]
    sub_row = lax.broadcasted_iota(jnp.int32, (SUBLANES, HG_DK), 0)

    def cat(blocks):
        return jnp.concatenate(blocks, axis=0).astype(BF16)

    def cumsum_rows(g):
        groups = []
        for v in range(C // SUBLANES):
            xg = g[v * SUBLANES:(v + 1) * SUBLANES]
            shift = 1
            while shift < SUBLANES:
                xg = xg + jnp.where(sub_row >= shift, pltpu.roll(xg, shift, axis=0), 0.0)
                shift *= 2
            groups.append(xg)
        out, carry = [groups[0]], groups[0][SUBLANES - 1:SUBLANES]
        for xg in groups[1:]:
            out.append(xg + carry)
            carry = carry + xg[SUBLANES - 1:SUBLANES]
        return jnp.concatenate(out, axis=0)

    def slot(heads_a, abuf, heads_b, chunks, wbuf, rbuf, xn_ref):
        pieces = [(chunk, half) for chunk in chunks for half in range(2)]

        def project(flush=False):
            if not pieces:
                return
            chunk, half = pieces.pop(0)
            acc = jnp.dot(xn_ref[...], w_ref[chunk, :, half * HALF:(half + 1) * HALF],
                          preferred_element_type=F32).astype(BF16)
            if wbuf is None:
                pj_ref[chunk - N_HG_CHUNKS, :, half * HALF:(half + 1) * HALF] = acc
            else:
                for s in range(SLABS_PER_CHUNK // 2):
                    slab = chunk * SLABS_PER_CHUNK + half * (SLABS_PER_CHUNK // 2) + s
                    wbuf[slab] = acc[:, s * HG_DK:(s + 1) * HG_DK]
            if flush:
                project(True)

        def src(sec, head, buf=rbuf):
            return buf.at[sec * HG_HEADS + head]

        loaded = []
        for n, head in enumerate(heads_b):
            base = n * HG_UNITS
            loaded.append(dict(
                v=[src(SEC_HI, head)[urows[u], :] for u in range(HG_UNITS)],
                p=[p_sc[base + u] for u in range(HG_UNITS)],
                qb=[qb_sc[base + u] for u in range(HG_UNITS)],
                kb=[kb_sc[base + u] for u in range(HG_UNITS)],
                dec=[dec_sc[base + u] for u in range(HG_UNITS)]))

        project()

        gates = []
        for head in heads_a:
            lb = lb_ref[head]
            q, k, b = [], [], []
            for c in range(HG_GROUP):
                xq = src(SEC_HQ, head, abuf)[rows[c], :].astype(F32)
                xf = src(SEC_HF, head, abuf)[rows[c], :].astype(F32)
                f = lb + (1.0 - lb) * _sigmoid(xf)
                k.append(1.0 - f)
                q.append(xq * _sigmoid(xq))
                b.append(cumsum_rows(jnp.log(f)))
            gates.append((q, k, b))

        for ld in loaded:
            ld["o_intra"] = [jnp.dot(ld["p"][u], ld["v"][u], preferred_element_type=F32)
                             for u in range(HG_UNITS)]
            ld["kv"] = [lax.dot_general(ld["v"][u], ld["kb"][u], (((0,), (0,)), ((), ())),
                                        preferred_element_type=F32) for u in range(HG_UNITS)]
            project()

        for head, ld in zip(heads_b, loaded):
            st = st_ref[head]
            ld["o"] = []
            for u in range(HG_UNITS):
                ld["o"].append(ld["o_intra"][u] + _nt(ld["qb"][u], st))
                st = st * ld["dec"][u] + ld["kv"][u]
            st_ref[head] = st
            project()

        for n, (q, k, b) in enumerate(gates):
            unit_handover(n * HG_UNITS, *chunk_scores(q, k, b))
            project()

        project(flush=True)

        for head, ld in zip(heads_b, loaded):
            nw = hnw_ref[head]
            for u in range(HG_UNITS):
                xg = src(SEC_HG, head)[urows[u], :].astype(F32)
                o = ld["o"][u]
                ms_o = jnp.mean(o * o, axis=-1, keepdims=True)
                on = o * lax.rsqrt(ms_o + EPS) * nw
                gh_ref[head, urows[u], :] = (on * (xg * _sigmoid(xg))).astype(BF16)

    def chunk_scores(q, k, b):
        p_c, qb_c, kb_c, last_c = [], [], [], []
        for c in range(HG_GROUP):
            ends = [b[c][(j + 1) * SUB - 1:(j + 1) * SUB] for j in range(NB)]
            starts = [jnp.zeros_like(ends[0])] + ends[:-1]
            last = ends[-1]
            q1, k1, qd, kd, qb, kb = [], [], [], [], [], []
            for j in range(NB):
                sl = slice(j * SUB, (j + 1) * SUB)
                bj = b[c][sl]
                q1j = q[c][sl] * jnp.exp(bj - starts[j])
                k1j = k[c][sl] * jnp.exp(ends[j] - bj)
                half = jnp.exp(0.5 * (starts[j] - ends[j]))
                q1.append(q1j)
                k1.append(k1j)
                qd.append(q1j * half)
                kd.append(k1j * half)
                qb.append(q1j * jnp.exp(starts[j]))
                kb.append(k1j * jnp.exp(last - ends[j]))
            q2 = q1[:3] + [q1[3] * jnp.exp(ends[2] - ends[1])]
            k2 = [k1[0] * jnp.exp(ends[1] - ends[0])] + k1[1:]
            s0 = _nt(cat(qd), cat(kd))
            s1 = _nt(cat(q1), cat(k1))
            s2 = _nt(cat(q2), cat(k2))
            p_c.append(jnp.where(m0, s0, jnp.where(m1, s1, jnp.where(m2, s2, 0.0))).astype(BF16))
            qb_c.append(jnp.concatenate(qb, axis=0))
            kb_c.append(jnp.concatenate(kb, axis=0))
            last_c.append(last)
        return p_c, qb_c, kb_c, last_c

    def unit_handover(base, p_c, qb_c, kb_c, last_c):
        p_a, qb_a, kb_a, dec_a = [], [], [], []
        for u in range(HG_UNITS):
            c0, c1 = 2 * u, 2 * u + 1
            cross = _nt(qb_c[c1], kb_c[c0]).astype(BF16)
            p_a.append(jnp.concatenate(
                [jnp.concatenate([p_c[c0], jnp.zeros((C, C), BF16)], axis=1),
                 jnp.concatenate([cross, p_c[c1]], axis=1)], axis=0))
            qb_a.append(cat([qb_c[c0], qb_c[c1] * jnp.exp(last_c[c0])]))
            kb_a.append(cat([kb_c[c0] * jnp.exp(last_c[c1]), kb_c[c1]]))
            dec_a.append(jnp.exp(last_c[c0] + last_c[c1]))

        for u in range(HG_UNITS):
            p_sc[base + u] = p_a[u]
            qb_sc[base + u] = qb_a[u]
            kb_sc[base + u] = kb_a[u]
            dec_sc[base + u] = dec_a[u]

    def run(wbuf, rbuf, xn_ref, xn_next_ref, recur, proj):
        hp = HG_SLOT_HEADS
        per_slot = N_HG_CHUNKS // 2

        def mid(t, carry, to_scratch):
            heads_a = tuple(hp * t + n for n in range(hp)) if recur else ()
            heads_b = tuple(hp * (t - 1) + n for n in range(hp)) if recur else ()
            chunks = [per_slot * (t - 1) + n for n in range(per_slot)] if proj else []
            slot(heads_a, rbuf, heads_b, chunks, wbuf if to_scratch else None, rbuf, xn_ref)
            return carry

        n_slots = HG_HEADS // hp
        n_scratch_slots = N_HG_CHUNKS // per_slot
        lax.fori_loop(1, 1 + n_scratch_slots, functools.partial(mid, to_scratch=True), 0)
        lax.fori_loop(1 + n_scratch_slots, n_slots, functools.partial(mid, to_scratch=False), 0)
        if proj:
            normalise(xnext_ref, xn_next_ref)
        slot(tuple(range(hp)) if proj else (), wbuf,
             tuple(HG_HEADS - hp + n for n in range(hp)) if recur else (),
             list(range(per_slot * (n_slots - 1), N_CHUNKS)) if proj else [], None, rbuf, xn_ref)

    for parity, bufs in ((0, (hb0_ref, hb1_ref, xn0_ref, xn1_ref)), (1, (hb1_ref, hb0_ref, xn1_ref, xn0_ref))):
        on_parity = lax.rem(i, 2) == parity
        if parity == 0:
            pl.when(i == 0)(functools.partial(run, *bufs, recur=False, proj=True))
        if (n_tiles % 2) == parity:
            pl.when(i == n_tiles)(functools.partial(run, *bufs, recur=True, proj=False))
        pl.when(on_parity & (i > 0) & (i < n_tiles))(functools.partial(run, *bufs, recur=True, proj=True))


def _front(x2d, norm_w, w3, lb, hnw, seq):
    m = x2d.shape[0]
    nt = m // FR_TT
    last = nt - 1
    return pl.pallas_call(
        functools.partial(_front_kernel, n_tiles=nt, tiles_per_seq=seq // FR_TT),
        grid=(nt + 1,),
        in_specs=[
            pl.BlockSpec((FR_TT, D_MODEL), lambda i: (0, 0), pipeline_mode=pl.Buffered(1)),
            pl.BlockSpec((FR_TT, D_MODEL), lambda i: (jnp.minimum(i + 1, last), 0)),
            pl.BlockSpec((1, D_MODEL), lambda i: (0, 0)),
            pl.BlockSpec((N_CHUNKS, D_MODEL, NCH), lambda i: (0, 0, 0), pipeline_mode=pl.Buffered(1)),
            pl.BlockSpec((HG_HEADS, 1, HG_DK), lambda i: (0, 0, 0)),
            pl.BlockSpec((HG_HEADS, 1, HG_DV), lambda i: (0, 0, 0)),
        ],
        out_specs=[
            pl.BlockSpec((N_PJ_CHUNKS, FR_TT, NCH), lambda i: (0, jnp.minimum(i, last), 0)),
            pl.BlockSpec((HG_HEADS, FR_TT, HG_DV), lambda i: (0, jnp.maximum(i - 1, 0), 0)),
        ],
        out_shape=[jax.ShapeDtypeStruct((N_PJ_CHUNKS, m, NCH), BF16),
                   jax.ShapeDtypeStruct((HG_HEADS, m, HG_DV), BF16)],
        scratch_shapes=[pltpu.VMEM((FR_TT, D_MODEL), BF16),
                        pltpu.VMEM((FR_TT, D_MODEL), BF16),
                        pltpu.VMEM((4 * HG_HEADS, FR_TT, HG_DK), BF16),
                        pltpu.VMEM((4 * HG_HEADS, FR_TT, HG_DK), BF16),
                        pltpu.VMEM((HG_HEADS, HG_DV, HG_DK), F32),
                        pltpu.VMEM((HG_SLOT_HEADS * HG_UNITS, 2 * HG_C, 2 * HG_C), BF16),
                        pltpu.VMEM((HG_SLOT_HEADS * HG_UNITS, 2 * HG_C, HG_DK), BF16),
                        pltpu.VMEM((HG_SLOT_HEADS * HG_UNITS, 2 * HG_C, HG_DK), BF16),
                        pltpu.VMEM((HG_SLOT_HEADS * HG_UNITS, 1, HG_DK), F32)],
        compiler_params=pltpu.CompilerParams(
            dimension_semantics=("arbitrary",), vmem_limit_bytes=VMEM_LIMIT),
        name="front",
    )(x2d, x2d, norm_w, w3, lb, hnw)


def _back_kernel(sink_ref, q0_ref, q1_ref, kvp_ref, kv_ref, ag0_ref, ag1_ref, gh_ref, mh_ref, ma_ref,
                 x_ref, wbh32_ref, wba32_ref, wo32_ref, fnw_ref, o_ref,
                 kc_ref, vx_ref, ga0_ref, ga1_ref, wbh_ref, wba_ref, wo_ref, acc_ref, mg_ref,
                 *, n_tiles, tiles_per_seq):
    W, DH = WINDOW, ATT_DH
    half_heads = ATT_Q_HEADS // 2
    HALF = D_MODEL // 2
    QUARTER = D_MODEL // 4
    j = pl.program_id(0)

    @pl.when(j == 0)
    def _():
        for hh in range(2):
            wbh_ref[hh] = wbh32_ref[:, hh * HALF:(hh + 1) * HALF].astype(BF16)
            wba_ref[hh] = wba32_ref[:, hh * HALF:(hh + 1) * HALF].astype(BF16)
            wo_ref[hh] = wo32_ref[hh * HALF:(hh + 1) * HALF, :].astype(BF16)
        acc_ref[...] = jnp.zeros_like(acc_ref)

    kc_ref[0:W, :] = kvp_ref[:, :KV_WIDTH]
    kc_ref[W:, :] = kv_ref[:, :KV_WIDTH]
    lane = lax.broadcasted_iota(jnp.int32, (1, 2 * DH), 1)
    low = lane < DH
    for rows, src in ((slice(0, W), kvp_ref), (slice(W, None), kv_ref)):
        for slab in range(ATT_KV_HEADS // 2):
            x = src[:, KV_WIDTH + slab * 2 * DH:KV_WIDTH + (slab + 1) * 2 * DH]
            xr = jnp.concatenate([x[:, DH:], x[:, :DH]], axis=1)
            zero = jnp.zeros_like(x)
            for t, piece in enumerate((jnp.where(low, x, zero), jnp.where(low, zero, xr),
                                       jnp.where(low, xr, zero), jnp.where(low, zero, x))):
                vx_ref[rows, (4 * slab + t) * 2 * DH:(4 * slab + t + 1) * 2 * DH] = piece
    ones_lo = jnp.broadcast_to(jnp.where(low, 1.0, 0.0).astype(BF16), (2 * W, 2 * DH))
    ones_hi = jnp.broadcast_to(jnp.where(low, 0.0, 1.0).astype(BF16), (2 * W, 2 * DH))
    low_rows = jnp.broadcast_to(low, (W, 2 * DH))

    first_tile = lax.rem(jnp.minimum(j, n_tiles - 1), tiles_per_seq) == 0
    qi = lax.broadcasted_iota(jnp.int32, (W, W), 0)
    kj = lax.broadcasted_iota(jnp.int32, (W, W), 1)
    upper = kj > qi

    def head_cols(ref_pair, hd, width):
        ref = ref_pair[hd // half_heads]
        hd = hd % half_heads
        return ref, slice(hd * DH, (hd + width) * DH)

    def half(hh, carry, ga_ref, gap_ref, attend, merge):
        gh = jnp.concatenate([gh_ref[h] for h in range(HG_HEADS)], axis=1) if merge else None
        y_parts = {}

        def branch_piece(name, lhs, w_ref, k):
            y_parts[name, k] = jnp.dot(lhs, w_ref[hh, :, k * QUARTER:(k + 1) * QUARTER],
                                       preferred_element_type=F32)

        def gate():
            yh = jnp.concatenate([y_parts["h", 0], y_parts["h", 1]], axis=1)
            ya = jnp.concatenate([y_parts["a", 0], y_parts["a", 1]], axis=1)
            mg_ref[...] = (_sigmoid(mh_ref[hh].astype(F32)) * yh
                           + _sigmoid(ma_ref[hh].astype(F32)) * ya).astype(BF16)

        def out_piece(k):
            cols = slice(k * QUARTER, (k + 1) * QUARTER)
            acc_ref[:, cols] = jnp.where(hh == 0, 0.0, acc_ref[:, cols]) + jnp.dot(
                mg_ref[...], wo_ref[hh, :, cols], preferred_element_type=F32)

        pieces = [lambda: branch_piece("h", gh, wbh_ref, 0), lambda: branch_piece("h", gh, wbh_ref, 1),
                  lambda: branch_piece("a", gap_ref[...], wba_ref, 0),
                  lambda: branch_piece("a", gap_ref[...], wba_ref, 1),
                  lambda: (gate(), out_piece(0)), lambda: out_piece(1), lambda: out_piece(2),
                  lambda: out_piece(3)]

        units = [(bi, h) for bi in range(2) for h in range(ATT_KV_HEADS)]

        def scores(bi, h):
            r = pl.multiple_of((2 * hh + bi) * W, W)
            kk = kc_ref[pl.ds(r, 2 * W), h * DH:(h + 1) * DH]
            parts = []
            for hd in range(h * ATT_GROUP, (h + 1) * ATT_GROUP):
                ref, cols = head_cols((q0_ref, q1_ref), hd, 1)
                parts.append(ref[pl.ds(r, W), cols])
            return _nt(jnp.concatenate(parts, axis=0), kk)

        s_next = scores(*units[0]) if attend else None
        for idx, (bi, h) in enumerate(units):
            if merge:
                pieces[idx]()
            if attend:
                s = s_next
                if idx + 1 < len(units):
                    s_next = scores(*units[idx + 1])
            if not attend:
                continue
            r = pl.multiple_of((2 * hh + bi) * W, W)
            prev_bias = jnp.where(first_tile & (2 * hh + bi == 0), -jnp.inf, 0.0)
            probs, sink_terms = [], []
            for jj, hd in enumerate(range(h * ATT_GROUP, (h + 1) * ATT_GROUP)):
                sj = s[jj * W:(jj + 1) * W]
                c = jnp.where(upper, sj[:, :W] + prev_bias, sj[:, W:])
                sink = sink_ref[hd]
                m = jnp.maximum(jnp.max(c, axis=-1, keepdims=True), sink)
                p = jnp.exp(c - m)
                sink_terms.append(jnp.exp(sink - m))
                probs.append(jnp.concatenate(
                    [jnp.where(upper, p, 0.0), jnp.where(upper, 0.0, p)], axis=1).astype(BF16))
            w_lo = jnp.concatenate(
                [vx_ref[pl.ds(r, 2 * W), (2 * h) * 2 * DH:(2 * h + 1) * 2 * DH], ones_lo], axis=1)
            w_hi = jnp.concatenate(
                [vx_ref[pl.ds(r, 2 * W), (2 * h + 1) * 2 * DH:(2 * h + 2) * 2 * DH], ones_hi], axis=1)
            res = (jnp.dot(jnp.concatenate(probs[0::2], axis=0), w_lo, preferred_element_type=F32)
                   + jnp.dot(jnp.concatenate(probs[1::2], axis=0), w_hi, preferred_element_type=F32))
            for pair in range(ATT_GROUP // 2):
                rp = res[pair * W:(pair + 1) * W]
                den = rp[:, 2 * DH:] + jnp.where(low_rows, sink_terms[2 * pair], sink_terms[2 * pair + 1])
                j0 = h * ATT_GROUP + 2 * pair
                ag_ref, cols = head_cols((ag0_ref, ag1_ref), j0, 2)
                ag = ag_ref[pl.ds(r, W), cols].astype(F32)
                ga_ref[pl.ds(r, W), j0 * DH:(j0 + 2) * DH] = (
                    rp[:, :2 * DH] / den * (ag * _sigmoid(ag))).astype(BF16)
        return carry

    def run(ga_ref, gap_ref, attend, merge):
        lax.fori_loop(0, 2, functools.partial(half, ga_ref=ga_ref, gap_ref=gap_ref, attend=attend,
                                              merge=merge), 0)
        if merge:
            xo = x_ref[...] + acc_ref[...]
            ms = jnp.mean(xo * xo, axis=-1, keepdims=True)
            o_ref[...] = xo * lax.rsqrt(ms + EPS) * fnw_ref[...]

    for parity, bufs in ((0, (ga0_ref, ga1_ref)), (1, (ga1_ref, ga0_ref))):
        if parity == 0:
            pl.when(j == 0)(functools.partial(run, *bufs, attend=True, merge=False))
        if (n_tiles % 2) == parity:
            pl.when(j == n_tiles)(functools.partial(run, *bufs, attend=False, merge=True))
        pl.when((lax.rem(j, 2) == parity) & (j > 0) & (j < n_tiles))(
            functools.partial(run, *bufs, attend=True, merge=True))


def _back(pj, gh3, x2d, sinks, wbh, wba, wo, fnw, seq):
    m = x2d.shape[0]
    nt = m // BK_TT
    last = nt - 1
    per = BK_TT // WINDOW

    def cur(jj):
        return jnp.minimum(jj, last)

    def prv(jj):
        return jnp.maximum(jj - 1, 0)

    def chunk(c):
        return pl.BlockSpec((None, BK_TT, NCH), lambda jj, s: (c, cur(jj), 0))

    def pair(c):
        return pl.BlockSpec((2, BK_TT, NCH), lambda jj, s: (c // 2, prv(jj), 0))

    prev_kv = pl.BlockSpec((None, WINDOW, NCH),
                           lambda jj, s: (PJ_KV, jnp.maximum(cur(jj) * per - 1, 0), 0))
    tile = pl.BlockSpec((BK_TT, D_MODEL), lambda jj, s: (prv(jj), 0))
    wspec = pl.BlockSpec((D_MODEL, D_MODEL), lambda jj, s: (0, 0), pipeline_mode=pl.Buffered(1))
    half = D_MODEL // 2
    grid_spec = pltpu.PrefetchScalarGridSpec(
        num_scalar_prefetch=1,
        grid=(nt + 1,),
        in_specs=[chunk(PJ_AQ), chunk(PJ_AQ + 1), prev_kv, chunk(PJ_KV), chunk(PJ_AG), chunk(PJ_AG + 1),
                  pl.BlockSpec((HG_HEADS, BK_TT, HG_DV), lambda jj, s: (0, prv(jj), 0)),
                  pair(PJ_MH), pair(PJ_MA), tile, wspec, wspec, wspec,
                  pl.BlockSpec((1, D_MODEL), lambda jj, s: (0, 0))],
        out_specs=tile,
        scratch_shapes=[pltpu.VMEM((BK_TT + WINDOW, KV_WIDTH), BF16),
                        pltpu.VMEM((BK_TT + WINDOW, ATT_KV_HEADS * 4 * ATT_DH), BF16),
                        pltpu.VMEM((BK_TT, D_MODEL), BF16),
                        pltpu.VMEM((BK_TT, D_MODEL), BF16),
                        pltpu.VMEM((2, D_MODEL, half), BF16),
                        pltpu.VMEM((2, D_MODEL, half), BF16),
                        pltpu.VMEM((2, half, D_MODEL), BF16),
                        pltpu.VMEM((BK_TT, D_MODEL), F32),
                        pltpu.VMEM((BK_TT, half), BF16)],
    )
    return pl.pallas_call(
        functools.partial(_back_kernel, n_tiles=nt, tiles_per_seq=seq // BK_TT),
        grid_spec=grid_spec,
        out_shape=jax.ShapeDtypeStruct((m, D_MODEL), F32),
        compiler_params=pltpu.CompilerParams(
            dimension_semantics=("arbitrary",), vmem_limit_bytes=VMEM_LIMIT),
        name="back",
    )(sinks, pj, pj, pj, pj, pj, pj, gh3, pj, pj, x2d, wbh, wba, wo, fnw)


def kernel(x, norm_w, w_in, hgrn_lower_bound, hgrn_norm_w, w_branch_hgrn, attn_sinks,
           w_branch_attn, w_out, final_norm_w):
    batch, seq, _ = x.shape
    depth = norm_w.shape[0]
    assert depth == 1, "the back kernel fuses the final RMSNorm into the single layer"
    assert seq % FR_TT == 0 and seq % BK_TT == 0
    lb_all = jnp.cumsum(jax.nn.softmax(hgrn_lower_bound.astype(F32), axis=0), axis=0)
    w3 = _wprep(w_in[0])
    x2d = x.reshape(batch * seq, D_MODEL)
    pj, gh3 = _front(x2d, norm_w[0].reshape(1, D_MODEL), w3,
                     lb_all[0].reshape(HG_HEADS, 1, HG_DK),
                     hgrn_norm_w[0].reshape(HG_HEADS, 1, HG_DV), seq)
    xo = _back(pj, gh3, x2d, attn_sinks[0].astype(F32), w_branch_hgrn[0], w_branch_attn[0], w_out[0],
               final_norm_w.reshape(1, D_MODEL), seq)
    return xo.reshape(batch, seq, D_MODEL)
```

```python
import functools

import jax
import jax.numpy as jnp
from jax import lax
from jax.experimental import pallas as pl
from jax.experimental.pallas import tpu as pltpu

F32 = jnp.float32
BF16 = jnp.bfloat16

D_MODEL = 1024
HG_HEADS = 8
HG_DK = 128
HG_DV = 128
ATT_Q_HEADS = 16
ATT_KV_HEADS = 4
ATT_GROUP = ATT_Q_HEADS // ATT_KV_HEADS
ATT_DH = 64
KV_WIDTH = ATT_KV_HEADS * ATT_DH
ATT_SCALE = ATT_DH ** -0.5
assert ATT_SCALE == 0.125
WINDOW = 128
EPS = 1e-6
D_IN = 8 * D_MODEL + 2 * KV_WIDTH

NCH = 512
N_CHUNKS = D_IN // NCH
N_HG_CHUNKS = 4 * D_MODEL // NCH
N_PJ_CHUNKS = N_CHUNKS - N_HG_CHUNKS
PJ_AQ, PJ_AG, PJ_MH, PJ_MA, PJ_KV = 0, 2, 4, 6, 8
SLABS_PER_CHUNK = NCH // HG_DK
SEC_HQ, SEC_HF, SEC_HI, SEC_HG = range(4)

V7X_VMEM_BYTES = 64 * 1024 * 1024
VMEM_LIMIT = V7X_VMEM_BYTES * 7 // 8

HG_C = 64
HG_SUB = 16
FR_TT = 512
HG_GROUP = FR_TT // HG_C
HG_UNITS = HG_GROUP // 2
SUBLANES = 8
HG_SLOT_HEADS = 4
BK_TT = 512


def _nt(a, b):
    return lax.dot_general(a.astype(BF16), b.astype(BF16), (((1,), (1,)), ((), ())),
                           preferred_element_type=F32)


def _sigmoid(x):
    return 1.0 / (1.0 + jnp.exp(-x))


WP_ROWS = 256


def _wprep_kernel(w_ref, o_ref):
    kv_src = N_HG_CHUNKS + 2
    for c in range(N_CHUNKS):
        src = c if c < kv_src else (kv_src if c == N_CHUNKS - 1 else c + 1)
        blk = w_ref[:, src * NCH:(src + 1) * NCH]
        if c // 2 == N_HG_CHUNKS // 2:
            blk = blk * ATT_SCALE
        o_ref[c] = blk.astype(BF16)


def _wprep(w):
    return pl.pallas_call(
        _wprep_kernel,
        grid=(D_MODEL // WP_ROWS,),
        in_specs=[pl.BlockSpec((WP_ROWS, D_IN), lambda r: (r, 0))],
        out_specs=pl.BlockSpec((N_CHUNKS, WP_ROWS, NCH), lambda r: (0, r, 0)),
        out_shape=jax.ShapeDtypeStruct((N_CHUNKS, D_MODEL, NCH), BF16),
        compiler_params=pltpu.CompilerParams(
            dimension_semantics=("arbitrary",), vmem_limit_bytes=VMEM_LIMIT),
        name="wprep",
    )(w)


def _front_kernel(x0_ref, xnext_ref, nw_ref, w_ref, lb_ref, hnw_ref, pj_ref, gh_ref, xn0_ref, xn1_ref,
                  hb0_ref, hb1_ref, st_ref, p_sc, qb_sc, kb_sc, dec_sc, *, n_tiles, tiles_per_seq):
    C, SUB, NB = HG_C, HG_SUB, HG_C // HG_SUB
    assert NB == 4, "the three score levels below are written for four sub-blocks per chunk"
    sub_shift = SUB.bit_length() - 1
    HALF = NCH // 2
    i = pl.program_id(0)

    @pl.when(lax.rem(jnp.maximum(i - 1, 0), tiles_per_seq) == 0)
    def _():
        st_ref[...] = jnp.zeros_like(st_ref)

    def normalise(src_ref, dst_ref):
        x = src_ref[...]
        ms = jnp.mean(x * x, axis=-1, keepdims=True)
        dst_ref[...] = (x * lax.rsqrt(ms + EPS) * nw_ref[...]).astype(BF16)

    @pl.when(i == 0)
    def _():
        normalise(x0_ref, xn0_ref)

    row = lax.broadcasted_iota(jnp.int32, (C, C), 0)
    col = lax.broadcasted_iota(jnp.int32, (C, C), 1)
    rb, cb = row >> sub_shift, col >> sub_shift
    m0 = (rb == cb) & (col <= row)
    m1 = ((rb & 1) == 1) & (cb == rb - 1)
    m2 = (row >= 2 * SUB) & (col < 2 * SUB)
    rows = [slice(c * C, (c + 1) * C) for c in range(HG_GROUP)]
    urows = [slice(u * 2 * C, (u + 1) * 2 * C) for u in range(HG_UNITS)]
    sub_row = lax.broadcasted_iota(jnp.int32, (SUBLANES, HG_DK), 0)

    def cat(blocks):
        return jnp.concatenate(blocks, axis=0).astype(BF16)

    def cumsum_rows(g):
        groups = []
        for v in range(C // SUBLANES):
            xg = g[v * SUBLANES:(v + 1) * SUBLANES]
            shift = 1
            while shift < SUBLANES:
                xg = xg + jnp.where(sub_row >= shift, pltpu.roll(xg, shift, axis=0), 0.0)
                shift *= 2
            groups.append(xg)
        out, carry = [groups[0]], groups[0][SUBLANES - 1:SUBLANES]
        for xg in groups[1:]:
            out.append(xg + carry)
            carry = carry + xg[SUBLANES - 1:SUBLANES]
        return jnp.concatenate(out, axis=0)

    def slot(heads_a, abuf, heads_b, chunks, wbuf, rbuf, xn_ref):
        pieces = [(chunk, half) for chunk in chunks for half in range(2)]

        def project(flush=False):
            if not pieces:
                return
            chunk, half = pieces.pop(0)
            acc = jnp.dot(xn_ref[...], w_ref[chunk, :, half * HALF:(half + 1) * HALF],
                          preferred_element_type=F32).astype(BF16)
            if wbuf is None:
                pj_ref[chunk - N_HG_CHUNKS, :, half * HALF:(half + 1) * HALF] = acc
            else:
                for s in range(SLABS_PER_CHUNK // 2):
                    slab = chunk * SLABS_PER_CHUNK + half * (SLABS_PER_CHUNK // 2) + s
                    wbuf[slab] = acc[:, s * HG_DK:(s + 1) * HG_DK]
            if flush:
                project(True)

        def src(sec, head, buf=rbuf):
            return buf.at[sec * HG_HEADS + head]

        loaded = []
        for n, head in enumerate(heads_b):
            base = n * HG_UNITS
            loaded.append(dict(
                v=[src(SEC_HI, head)[urows[u], :] for u in range(HG_UNITS)],
                p=[p_sc[base + u] for u in range(HG_UNITS)],
                qb=[qb_sc[base + u] for u in range(HG_UNITS)],
                kb=[kb_sc[base + u] for u in range(HG_UNITS)],
                dec=[dec_sc[base + u] for u in range(HG_UNITS)]))

        project()

        gates = []
        for head in heads_a:
            lb = lb_ref[head]
            q, k, b = [], [], []
            for c in range(HG_GROUP):
                xq = src(SEC_HQ, head, abuf)[rows[c], :].astype(F32)
                xf = src(SEC_HF, head, abuf)[rows[c], :].astype(F32)
                f = lb + (1.0 - lb) * _sigmoid(xf)
                k.append(1.0 - f)
                q.append(xq * _sigmoid(xq))
                b.append(cumsum_rows(jnp.log(f)))
            gates.append((q, k, b))

        for ld in loaded:
            ld["o_intra"] = [jnp.dot(ld["p"][u], ld["v"][u], preferred_element_type=F32)
                             for u in range(HG_UNITS)]
            ld["kv"] = [lax.dot_general(ld["v"][u], ld["kb"][u], (((0,), (0,)), ((), ())),
                                        preferred_element_type=F32) for u in range(HG_UNITS)]
            project()

        for head, ld in zip(heads_b, loaded):
            st = st_ref[head]
            ld["o"] = []
            for u in range(HG_UNITS):
                ld["o"].append(ld["o_intra"][u] + _nt(ld["qb"][u], st))
                st = st * ld["dec"][u] + ld["kv"][u]
            st_ref[head] = st
            project()

        for n, (q, k, b) in enumerate(gates):
            unit_handover(n * HG_UNITS, *chunk_scores(q, k, b))
            project()

        project(flush=True)

        for head, ld in zip(heads_b, loaded):
            nw = hnw_ref[head]
            for u in range(HG_UNITS):
                xg = src(SEC_HG, head)[urows[u], :].astype(F32)
                o = ld["o"][u]
                ms_o = jnp.mean(o * o, axis=-1, keepdims=True)
                on = o * lax.rsqrt(ms_o + EPS) * nw
                gh_ref[head, urows[u], :] = (on * (xg * _sigmoid(xg))).astype(BF16)

    def chunk_scores(q, k, b):
        p_c, qb_c, kb_c, last_c = [], [], [], []
        for c in range(HG_GROUP):
            ends = [b[c][(j + 1) * SUB - 1:(j + 1) * SUB] for j in range(NB)]
            starts = [jnp.zeros_like(ends[0])] + ends[:-1]
            last = ends[-1]
            q1, k1, qd, kd, qb, kb = [], [], [], [], [], []
            for j in range(NB):
                sl = slice(j * SUB, (j + 1) * SUB)
                bj = b[c][sl]
                q1j = q[c][sl] * jnp.exp(bj - starts[j])
                k1j = k[c][sl] * jnp.exp(ends[j] - bj)
                half = jnp.exp(0.5 * (starts[j] - ends[j]))
                q1.append(q1j)
                k1.append(k1j)
                qd.append(q1j * half)
                kd.append(k1j * half)
                qb.append(q1j * jnp.exp(starts[j]))
                kb.append(k1j * jnp.exp(last - ends[j]))
            q2 = q1[:3] + [q1[3] * jnp.exp(ends[2] - ends[1])]
            k2 = [k1[0] * jnp.exp(ends[1] - ends[0])] + k1[1:]
            s0 = _nt(cat(qd), cat(kd))
            s1 = _nt(cat(q1), cat(k1))
            s2 = _nt(cat(q2), cat(k2))
            p_c.append(jnp.where(m0, s0, jnp.where(m1, s1, jnp.where(m2, s2, 0.0))).astype(BF16))
            qb_c.append(jnp.concatenate(qb, axis=0))
            kb_c.append(jnp.concatenate(kb, axis=0))
            last_c.append(last)
        return p_c, qb_c, kb_c, last_c

    def unit_handover(base, p_c, qb_c, kb_c, last_c):
        p_a, qb_a, kb_a, dec_a = [], [], [], []
        for u in range(HG_UNITS):
            c0, c1 = 2 * u, 2 * u + 1
            cross = _nt(qb_c[c1], kb_c[c0]).astype(BF16)
            p_a.append(jnp.concatenate(
                [jnp.concatenate([p_c[c0], jnp.zeros((C, C), BF16)], axis=1),
                 jnp.concatenate([cross, p_c[c1]], axis=1)], axis=0))
            qb_a.append(cat([qb_c[c0], qb_c[c1] * jnp.exp(last_c[c0])]))
            kb_a.append(cat([kb_c[c0] * jnp.exp(last_c[c1]), kb_c[c1]]))
            dec_a.append(jnp.exp(last_c[c0] + last_c[c1]))

        for u in range(HG_UNITS):
            p_sc[base + u] = p_a[u]
            qb_sc[base + u] = qb_a[u]
            kb_sc[base + u] = kb_a[u]
            dec_sc[base + u] = dec_a[u]

    def run(wbuf, rbuf, xn_ref, xn_next_ref, recur, proj):
        hp = HG_SLOT_HEADS
        per_slot = 2 * hp

        def mid(t, carry, to_scratch):
            heads_a = tuple(hp * t + n for n in range(hp)) if recur else ()
            heads_b = tuple(hp * (t - 1) + n for n in range(hp)) if recur else ()
            chunks = [per_slot * (t - 1) + n for n in range(per_slot)] if proj else []
            slot(heads_a, rbuf, heads_b, chunks, wbuf if to_scratch else None, rbuf, xn_ref)
            return carry

        n_slots = HG_HEADS // hp
        n_scratch_slots = N_HG_CHUNKS // per_slot
        lax.fori_loop(1, 1 + n_scratch_slots, functools.partial(mid, to_scratch=True), 0)
        lax.fori_loop(1 + n_scratch_slots, n_slots, functools.partial(mid, to_scratch=False), 0)
        if proj:
            normalise(xnext_ref, xn_next_ref)
        slot(tuple(range(hp)) if proj else (), wbuf,
             tuple(HG_HEADS - hp + n for n in range(hp)) if recur else (),
             list(range(per_slot * (n_slots - 1), N_CHUNKS)) if proj else [], None, rbuf, xn_ref)

    for parity, bufs in ((0, (hb0_ref, hb1_ref, xn0_ref, xn1_ref)), (1, (hb1_ref, hb0_ref, xn1_ref, xn0_ref))):
        on_parity = lax.rem(i, 2) == parity
        if parity == 0:
            pl.when(i == 0)(functools.partial(run, *bufs, recur=False, proj=True))
        if (n_tiles % 2) == parity:
            pl.when(i == n_tiles)(functools.partial(run, *bufs, recur=True, proj=False))
        pl.when(on_parity & (i > 0) & (i < n_tiles))(functools.partial(run, *bufs, recur=True, proj=True))


def _front(x2d, norm_w, w3, lb, hnw, seq):
    m = x2d.shape[0]
    nt = m // FR_TT
    last = nt - 1
    return pl.pallas_call(
        functools.partial(_front_kernel, n_tiles=nt, tiles_per_seq=seq // FR_TT),
        grid=(nt + 1,),
        in_specs=[
            pl.BlockSpec((FR_TT, D_MODEL), lambda i: (0, 0), pipeline_mode=pl.Buffered(1)),
            pl.BlockSpec((FR_TT, D_MODEL), lambda i: (jnp.minimum(i + 1, last), 0)),
            pl.BlockSpec((1, D_MODEL), lambda i: (0, 0)),
            pl.BlockSpec((N_CHUNKS, D_MODEL, NCH), lambda i: (0, 0, 0), pipeline_mode=pl.Buffered(1)),
            pl.BlockSpec((HG_HEADS, 1, HG_DK), lambda i: (0, 0, 0)),
            pl.BlockSpec((HG_HEADS, 1, HG_DV), lambda i: (0, 0, 0)),
        ],
        out_specs=[
            pl.BlockSpec((N_PJ_CHUNKS, FR_TT, NCH), lambda i: (0, jnp.minimum(i, last), 0)),
            pl.BlockSpec((HG_HEADS, FR_TT, HG_DV), lambda i: (0, jnp.maximum(i - 1, 0), 0)),
        ],
        out_shape=[jax.ShapeDtypeStruct((N_PJ_CHUNKS, m, NCH), BF16),
                   jax.ShapeDtypeStruct((HG_HEADS, m, HG_DV), BF16)],
        scratch_shapes=[pltpu.VMEM((FR_TT, D_MODEL), BF16),
                        pltpu.VMEM((FR_TT, D_MODEL), BF16),
                        pltpu.VMEM((4 * HG_HEADS, FR_TT, HG_DK), BF16),
                        pltpu.VMEM((4 * HG_HEADS, FR_TT, HG_DK), BF16),
                        pltpu.VMEM((HG_HEADS, HG_DV, HG_DK), F32),
                        pltpu.VMEM((HG_SLOT_HEADS * HG_UNITS, 2 * HG_C, 2 * HG_C), BF16),
                        pltpu.VMEM((HG_SLOT_HEADS * HG_UNITS, 2 * HG_C, HG_DK), BF16),
                        pltpu.VMEM((HG_SLOT_HEADS * HG_UNITS, 2 * HG_C, HG_DK), BF16),
                        pltpu.VMEM((HG_SLOT_HEADS * HG_UNITS, 1, HG_DK), F32)],
        compiler_params=pltpu.CompilerParams(
            dimension_semantics=("arbitrary",), vmem_limit_bytes=VMEM_LIMIT),
        name="front",
    )(x2d, x2d, norm_w, w3, lb, hnw)


def _back_kernel(sink_ref, q0_ref, q1_ref, kvp_ref, kv_ref, ag0_ref, ag1_ref, gh_ref, mh_ref, ma_ref,
                 x_ref, wbh32_ref, wba32_ref, wo32_ref, fnw_ref, o_ref,
                 kc_ref, vx_ref, ga0_ref, ga1_ref, wbh_ref, wba_ref, wo_ref, acc_ref, mg_ref,
                 *, n_tiles, tiles_per_seq):
    W, DH = WINDOW, ATT_DH
    half_heads = ATT_Q_HEADS // 2
    HALF = D_MODEL // 2
    QUARTER = D_MODEL // 4
    j = pl.program_id(0)

    @pl.when(j == 0)
    def _():
        for hh in range(2):
            wbh_ref[hh] = wbh32_ref[:, hh * HALF:(hh + 1) * HALF].astype(BF16)
            wba_ref[hh] = wba32_ref[:, hh * HALF:(hh + 1) * HALF].astype(BF16)
            wo_ref[hh] = wo32_ref[hh * HALF:(hh + 1) * HALF, :].astype(BF16)
        acc_ref[...] = jnp.zeros_like(acc_ref)

    kc_ref[0:W, :] = kvp_ref[:, :KV_WIDTH]
    kc_ref[W:, :] = kv_ref[:, :KV_WIDTH]
    lane = lax.broadcasted_iota(jnp.int32, (1, 2 * DH), 1)
    low = lane < DH
    for rows, src in ((slice(0, W), kvp_ref), (slice(W, None), kv_ref)):
        for slab in range(ATT_KV_HEADS // 2):
            x = src[:, KV_WIDTH + slab * 2 * DH:KV_WIDTH + (slab + 1) * 2 * DH]
            xr = jnp.concatenate([x[:, DH:], x[:, :DH]], axis=1)
            zero = jnp.zeros_like(x)
            for t, piece in enumerate((jnp.where(low, x, zero), jnp.where(low, zero, xr),
                                       jnp.where(low, xr, zero), jnp.where(low, zero, x))):
                vx_ref[rows, (4 * slab + t) * 2 * DH:(4 * slab + t + 1) * 2 * DH] = piece
    ones_lo = jnp.broadcast_to(jnp.where(low, 1.0, 0.0).astype(BF16), (2 * W, 2 * DH))
    ones_hi = jnp.broadcast_to(jnp.where(low, 0.0, 1.0).astype(BF16), (2 * W, 2 * DH))
    low_rows = jnp.broadcast_to(low, (W, 2 * DH))

    first_tile = lax.rem(jnp.minimum(j, n_tiles - 1), tiles_per_seq) == 0
    qi = lax.broadcasted_iota(jnp.int32, (W, W), 0)
    kj = lax.broadcasted_iota(jnp.int32, (W, W), 1)
    upper = kj > qi

    def head_cols(ref_pair, hd, width):
        ref = ref_pair[hd // half_heads]
        hd = hd % half_heads
        return ref, slice(hd * DH, (hd + width) * DH)

    def half(hh, carry, ga_ref, gap_ref, attend, merge):
        gh = jnp.concatenate([gh_ref[h] for h in range(HG_HEADS)], axis=1) if merge else None
        y_parts = {}

        def branch_piece(name, lhs, w_ref, k):
            y_parts[name, k] = jnp.dot(lhs, w_ref[hh, :, k * QUARTER:(k + 1) * QUARTER],
                                       preferred_element_type=F32)

        def gate():
            yh = jnp.concatenate([y_parts["h", 0], y_parts["h", 1]], axis=1)
            ya = jnp.concatenate([y_parts["a", 0], y_parts["a", 1]], axis=1)
            mg_ref[...] = (_sigmoid(mh_ref[hh].astype(F32)) * yh
                           + _sigmoid(ma_ref[hh].astype(F32)) * ya).astype(BF16)

        def out_piece(k):
            cols = slice(k * QUARTER, (k + 1) * QUARTER)
            acc_ref[:, cols] = jnp.where(hh == 0, 0.0, acc_ref[:, cols]) + jnp.dot(
                mg_ref[...], wo_ref[hh, :, cols], preferred_element_type=F32)

        pieces = [lambda: branch_piece("h", gh, wbh_ref, 0), lambda: branch_piece("h", gh, wbh_ref, 1),
                  lambda: branch_piece("a", gap_ref[...], wba_ref, 0),
                  lambda: branch_piece("a", gap_ref[...], wba_ref, 1),
                  lambda: (gate(), out_piece(0)), lambda: out_piece(1), lambda: out_piece(2),
                  lambda: out_piece(3)]

        units = [(bi, h) for bi in range(2) for h in range(ATT_KV_HEADS)]

        def scores(bi, h):
            r = pl.multiple_of((2 * hh + bi) * W, W)
            kk = kc_ref[pl.ds(r, 2 * W), h * DH:(h + 1) * DH]
            parts = []
            for hd in range(h * ATT_GROUP, (h + 1) * ATT_GROUP):
                ref, cols = head_cols((q0_ref, q1_ref), hd, 1)
                parts.append(ref[pl.ds(r, W), cols])
            return _nt(jnp.concatenate(parts, axis=0), kk)

        s_next = scores(*units[0]) if attend else None
        for idx, (bi, h) in enumerate(units):
            if merge:
                pieces[idx]()
            if attend:
                s = s_next
                if idx + 1 < len(units):
                    s_next = scores(*units[idx + 1])
            if not attend:
                continue
            r = pl.multiple_of((2 * hh + bi) * W, W)
            prev_bias = jnp.where(first_tile & (2 * hh + bi == 0), -jnp.inf, 0.0)
            probs, sink_terms = [], []
            for jj, hd in enumerate(range(h * ATT_GROUP, (h + 1) * ATT_GROUP)):
                sj = s[jj * W:(jj + 1) * W]
                c = jnp.where(upper, sj[:, :W] + prev_bias, sj[:, W:])
                sink = sink_ref[hd]
                m = jnp.maximum(jnp.max(c, axis=-1, keepdims=True), sink)
                p = jnp.exp(c - m)
                sink_terms.append(jnp.exp(sink - m))
                probs.append(jnp.concatenate(
                    [jnp.where(upper, p, 0.0), jnp.where(upper, 0.0, p)], axis=1).astype(BF16))
            w_lo = jnp.concatenate(
                [vx_ref[pl.ds(r, 2 * W), (2 * h) * 2 * DH:(2 * h + 1) * 2 * DH], ones_lo], axis=1)
            w_hi = jnp.concatenate(
                [vx_ref[pl.ds(r, 2 * W), (2 * h + 1) * 2 * DH:(2 * h + 2) * 2 * DH], ones_hi], axis=1)
            res = (jnp.dot(jnp.concatenate(probs[0::2], axis=0), w_lo, preferred_element_type=F32)
                   + jnp.dot(jnp.concatenate(probs[1::2], axis=0), w_hi, preferred_element_type=F32))
            for pair in range(ATT_GROUP // 2):
                rp = res[pair * W:(pair + 1) * W]
                den = rp[:, 2 * DH:] + jnp.where(low_rows, sink_terms[2 * pair], sink_terms[2 * pair + 1])
                j0 = h * ATT_GROUP + 2 * pair
                ag_ref, cols = head_cols((ag0_ref, ag1_ref), j0, 2)
                ag = ag_ref[pl.ds(r, W), cols].astype(F32)
                ga_ref[pl.ds(r, W), j0 * DH:(j0 + 2) * DH] = (
                    rp[:, :2 * DH] / den * (ag * _sigmoid(ag))).astype(BF16)
        return carry

    def run(ga_ref, gap_ref, attend, merge):
        lax.fori_loop(0, 2, functools.partial(half, ga_ref=ga_ref, gap_ref=gap_ref, attend=attend,
                                              merge=merge), 0)
        if merge:
            xo = x_ref[...] + acc_ref[...]
            ms = jnp.mean(xo * xo, axis=-1, keepdims=True)
            o_ref[...] = xo * lax.rsqrt(ms + EPS) * fnw_ref[...]

    for parity, bufs in ((0, (ga0_ref, ga1_ref)), (1, (ga1_ref, ga0_ref))):
        if parity == 0:
            pl.when(j == 0)(functools.partial(run, *bufs, attend=True, merge=False))
        if (n_tiles % 2) == parity:
            pl.when(j == n_tiles)(functools.partial(run, *bufs, attend=False, merge=True))
        pl.when((lax.rem(j, 2) == parity) & (j > 0) & (j < n_tiles))(
            functools.partial(run, *bufs, attend=True, merge=True))


def _back(pj, gh3, x2d, sinks, wbh, wba, wo, fnw, seq):
    m = x2d.shape[0]
    nt = m // BK_TT
    last = nt - 1
    per = BK_TT // WINDOW

    def cur(jj):
        return jnp.minimum(jj, last)

    def prv(jj):
        return jnp.maximum(jj - 1, 0)

    def chunk(c):
        return pl.BlockSpec((None, BK_TT, NCH), lambda jj, s: (c, cur(jj), 0))

    def pair(c):
        return pl.BlockSpec((2, BK_TT, NCH), lambda jj, s: (c // 2, prv(jj), 0))

    prev_kv = pl.BlockSpec((None, WINDOW, NCH),
                           lambda jj, s: (PJ_KV, jnp.maximum(cur(jj) * per - 1, 0), 0))
    tile = pl.BlockSpec((BK_TT, D_MODEL), lambda jj, s: (prv(jj), 0))
    wspec = pl.BlockSpec((D_MODEL, D_MODEL), lambda jj, s: (0, 0), pipeline_mode=pl.Buffered(1))
    half = D_MODEL // 2
    grid_spec = pltpu.PrefetchScalarGridSpec(
        num_scalar_prefetch=1,
        grid=(nt + 1,),
        in_specs=[chunk(PJ_AQ), chunk(PJ_AQ + 1), prev_kv, chunk(PJ_KV), chunk(PJ_AG), chunk(PJ_AG + 1),
                  pl.BlockSpec((HG_HEADS, BK_TT, HG_DV), lambda jj, s: (0, prv(jj), 0)),
                  pair(PJ_MH), pair(PJ_MA), tile, wspec, wspec, wspec,
                  pl.BlockSpec((1, D_MODEL), lambda jj, s: (0, 0))],
        out_specs=tile,
        scratch_shapes=[pltpu.VMEM((BK_TT + WINDOW, KV_WIDTH), BF16),
                        pltpu.VMEM((BK_TT + WINDOW, ATT_KV_HEADS * 4 * ATT_DH), BF16),
                        pltpu.VMEM((BK_TT, D_MODEL), BF16),
                        pltpu.VMEM((BK_TT, D_MODEL), BF16),
                        pltpu.VMEM((2, D_MODEL, half), BF16),
                        pltpu.VMEM((2, D_MODEL, half), BF16),
                        pltpu.VMEM((2, half, D_MODEL), BF16),
                        pltpu.VMEM((BK_TT, D_MODEL), F32),
                        pltpu.VMEM((BK_TT, half), BF16)],
    )
    return pl.pallas_call(
        functools.partial(_back_kernel, n_tiles=nt, tiles_per_seq=seq // BK_TT),
        grid_spec=grid_spec,
        out_shape=jax.ShapeDtypeStruct((m, D_MODEL), F32),
        compiler_params=pltpu.CompilerParams(
            dimension_semantics=("arbitrary",), vmem_limit_bytes=VMEM_LIMIT),
        name="back",
    )(sinks, pj, pj, pj, pj, pj, pj, gh3, pj, pj, x2d, wbh, wba, wo, fnw)


def kernel(x, norm_w, w_in, hgrn_lower_bound, hgrn_norm_w, w_branch_hgrn, attn_sinks,
           w_branch_attn, w_out, final_norm_w):
    batch, seq, _ = x.shape
    depth = norm_w.shape[0]
    assert depth == 1, "the back kernel fuses the final RMSNorm into the single layer"
    assert seq % FR_TT == 0 and seq % BK_TT == 0
    lb_all = jnp.cumsum(jax.nn.softmax(hgrn_lower_bound.astype(F32), axis=0), axis=0)
    w3 = _wprep(w_in[0])
    x2d = x.reshape(batch * seq, D_MODEL)
    pj, gh3 = _front(x2d, norm_w[0].reshape(1, D_MODEL), w3,
                     lb_all[0].reshape(HG_HEADS, 1, HG_DK),
                     hgrn_norm_w[0].reshape(HG_HEADS, 1, HG_DV), seq)
    xo = _back(pj, gh3, x2d, attn_sinks[0].astype(F32), w_branch_hgrn[0], w_branch_attn[0], w_out[0],
               final_norm_w.reshape(1, D_MODEL), seq)
    return xo.reshape(batch, seq, D_MODEL)
```

```python
import functools

import jax
import jax.numpy as jnp
from jax import lax
from jax.experimental import pallas as pl
from jax.experimental.pallas import tpu as pltpu

F32 = jnp.float32
BF16 = jnp.bfloat16

D_MODEL = 1024
HG_HEADS = 8
HG_DK = 128
HG_DV = 128
ATT_Q_HEADS = 16
ATT_KV_HEADS = 4
ATT_GROUP = ATT_Q_HEADS // ATT_KV_HEADS
ATT_DH = 64
KV_WIDTH = ATT_KV_HEADS * ATT_DH
ATT_SCALE = ATT_DH ** -0.5
assert ATT_SCALE == 0.125
WINDOW = 128
EPS = 1e-6
D_IN = 8 * D_MODEL + 2 * KV_WIDTH

NCH = 512
N_CHUNKS = D_IN // NCH
N_HG_CHUNKS = 4 * D_MODEL // NCH
N_PJ_CHUNKS = N_CHUNKS - N_HG_CHUNKS
PJ_AQ, PJ_AG, PJ_MH, PJ_MA, PJ_KV = 0, 2, 4, 6, 8
SLABS_PER_CHUNK = NCH // HG_DK
SEC_HQ, SEC_HF, SEC_HI, SEC_HG = range(4)

V7X_VMEM_BYTES = 64 * 1024 * 1024
VMEM_LIMIT = V7X_VMEM_BYTES * 7 // 8

HG_C = 64
HG_SUB = 16
FR_TT = 512
HG_GROUP = FR_TT // HG_C
HG_UNITS = HG_GROUP // 2
SUBLANES = 8
HG_SLOT_HEADS = 2
BK_TT = 512


def _nt(a, b):
    return lax.dot_general(a.astype(BF16), b.astype(BF16), (((1,), (1,)), ((), ())),
                           preferred_element_type=F32)


def _sigmoid(x):
    return 1.0 / (1.0 + jnp.exp(-x))


WP_ROWS = 256


def _wprep_kernel(w_ref, o_ref):
    kv_src = N_HG_CHUNKS + 2
    for c in range(N_CHUNKS):
        src = c if c < kv_src else (kv_src if c == N_CHUNKS - 1 else c + 1)
        blk = w_ref[:, src * NCH:(src + 1) * NCH]
        if c // 2 == N_HG_CHUNKS // 2:
            blk = blk * ATT_SCALE
        o_ref[c] = blk.astype(BF16)


def _wprep(w):
    return pl.pallas_call(
        _wprep_kernel,
        grid=(D_MODEL // WP_ROWS,),
        in_specs=[pl.BlockSpec((WP_ROWS, D_IN), lambda r: (r, 0))],
        out_specs=pl.BlockSpec((N_CHUNKS, WP_ROWS, NCH), lambda r: (0, r, 0)),
        out_shape=jax.ShapeDtypeStruct((N_CHUNKS, D_MODEL, NCH), BF16),
        compiler_params=pltpu.CompilerParams(
            dimension_semantics=("arbitrary",), vmem_limit_bytes=VMEM_LIMIT),
        name="wprep",
    )(w)


def _front_kernel(x0_ref, xnext_ref, nw_ref, w_ref, lb_ref, hnw_ref, pj_ref, gh_ref, xn0_ref, xn1_ref,
                  hb0_ref, hb1_ref, st_ref, p_sc, qb_sc, kb_sc, dec_sc, *, n_tiles, tiles_per_seq):
    C, SUB, NB = HG_C, HG_SUB, HG_C // HG_SUB
    assert NB == 4, "the three score levels below are written for four sub-blocks per chunk"
    sub_shift = SUB.bit_length() - 1
    HALF = NCH // 2
    i = pl.program_id(0)

    @pl.when(lax.rem(jnp.maximum(i - 1, 0), tiles_per_seq) == 0)
    def _():
        st_ref[...] = jnp.zeros_like(st_ref)

    def normalise(src_ref, dst_ref):
        x = src_ref[...]
        ms = jnp.mean(x * x, axis=-1, keepdims=True)
        dst_ref[...] = (x * lax.rsqrt(ms + EPS) * nw_ref[...]).astype(BF16)

    @pl.when(i == 0)
    def _():
        normalise(x0_ref, xn0_ref)

    row = lax.broadcasted_iota(jnp.int32, (C, C), 0)
    col = lax.broadcasted_iota(jnp.int32, (C, C), 1)
    rb, cb = row >> sub_shift, col >> sub_shift
    m0 = (rb == cb) & (col <= row)
    m1 = ((rb & 1) == 1) & (cb == rb - 1)
    m2 = (row >= 2 * SUB) & (col < 2 * SUB)
    rows = [slice(c * C, (c + 1) * C) for c in range(HG_GROUP)]
    urows = [slice(u * 2 * C, (u + 1) * 2 * C) for u in range(HG_UNITS)]
    sub_row = lax.broadcasted_iota(jnp.int32, (SUBLANES, HG_DK), 0)

    def cat(blocks):
        return jnp.concatenate(blocks, axis=0).astype(BF16)

    def cumsum_rows(g):
        groups = []
        for v in range(C // SUBLANES):
            xg = g[v * SUBLANES:(v + 1) * SUBLANES]
            shift = 1
            while shift < SUBLANES:
                xg = xg + jnp.where(sub_row >= shift, pltpu.roll(xg, shift, axis=0), 0.0)
                shift *= 2
            groups.append(xg)
        out, carry = [groups[0]], groups[0][SUBLANES - 1:SUBLANES]
        for xg in groups[1:]:
            out.append(xg + carry)
            carry = carry + xg[SUBLANES - 1:SUBLANES]
        return jnp.concatenate(out, axis=0)

    def slot(heads_a, abuf, heads_b, chunks, wbuf, rbuf, xn_ref):
        pieces = [(chunk, half, keep) for chunk, keep in chunks for half in range(2)]

        def project(flush=False):
            if not pieces:
                return
            chunk, half, keep = pieces.pop(0)
            acc = jnp.dot(xn_ref[...], w_ref[chunk, :, half * HALF:(half + 1) * HALF],
                          preferred_element_type=F32).astype(BF16)
            if not keep:
                pj_ref[chunk - N_HG_CHUNKS, :, half * HALF:(half + 1) * HALF] = acc
            else:
                for s in range(SLABS_PER_CHUNK // 2):
                    slab = chunk * SLABS_PER_CHUNK + half * (SLABS_PER_CHUNK // 2) + s
                    wbuf[slab] = acc[:, s * HG_DK:(s + 1) * HG_DK]
            if flush:
                project(True)

        def src(sec, head, buf=rbuf):
            return buf.at[sec * HG_HEADS + head]

        loaded = []
        for n, head in enumerate(heads_b):
            base = n * HG_UNITS
            loaded.append(dict(
                v=[src(SEC_HI, head)[urows[u], :] for u in range(HG_UNITS)],
                p=[p_sc[base + u] for u in range(HG_UNITS)],
                qb=[qb_sc[base + u] for u in range(HG_UNITS)],
                kb=[kb_sc[base + u] for u in range(HG_UNITS)],
                dec=[dec_sc[base + u] for u in range(HG_UNITS)]))

        project()

        gates = []
        for head in heads_a:
            lb = lb_ref[head]
            q, k, b = [], [], []
            for c in range(HG_GROUP):
                xq = src(SEC_HQ, head, abuf)[rows[c], :].astype(F32)
                xf = src(SEC_HF, head, abuf)[rows[c], :].astype(F32)
                f = lb + (1.0 - lb) * _sigmoid(xf)
                k.append(1.0 - f)
                q.append(xq * _sigmoid(xq))
                b.append(cumsum_rows(jnp.log(f)))
            gates.append((q, k, b))

        for ld in loaded:
            ld["o_intra"] = [jnp.dot(ld["p"][u], ld["v"][u], preferred_element_type=F32)
                             for u in range(HG_UNITS)]
            ld["kv"] = [lax.dot_general(ld["v"][u], ld["kb"][u], (((0,), (0,)), ((), ())),
                                        preferred_element_type=F32) for u in range(HG_UNITS)]
            project()

        for head, ld in zip(heads_b, loaded):
            st = st_ref[head]
            ld["o"] = []
            for u in range(HG_UNITS):
                ld["o"].append(ld["o_intra"][u] + _nt(ld["qb"][u], st))
                st = st * ld["dec"][u] + ld["kv"][u]
            st_ref[head] = st
            project()

        for n, (q, k, b) in enumerate(gates):
            unit_handover(n * HG_UNITS, *chunk_scores(q, k, b))
            project()

        project(flush=True)

        for head, ld in zip(heads_b, loaded):
            nw = hnw_ref[head]
            for u in range(HG_UNITS):
                xg = src(SEC_HG, head)[urows[u], :].astype(F32)
                o = ld["o"][u]
                ms_o = jnp.mean(o * o, axis=-1, keepdims=True)
                on = o * lax.rsqrt(ms_o + EPS) * nw
                gh_ref[head, urows[u], :] = (on * (xg * _sigmoid(xg))).astype(BF16)

    def chunk_scores(q, k, b):
        p_c, qb_c, kb_c, last_c = [], [], [], []
        for c in range(HG_GROUP):
            ends = [b[c][(j + 1) * SUB - 1:(j + 1) * SUB] for j in range(NB)]
            starts = [jnp.zeros_like(ends[0])] + ends[:-1]
            last = ends[-1]
            q1, k1, qd, kd, qb, kb = [], [], [], [], [], []
            for j in range(NB):
                sl = slice(j * SUB, (j + 1) * SUB)
                bj = b[c][sl]
                q1j = q[c][sl] * jnp.exp(bj - starts[j])
                k1j = k[c][sl] * jnp.exp(ends[j] - bj)
                half = jnp.exp(0.5 * (starts[j] - ends[j]))
                q1.append(q1j)
                k1.append(k1j)
                qd.append(q1j * half)
                kd.append(k1j * half)
                qb.append(q1j * jnp.exp(starts[j]))
                kb.append(k1j * jnp.exp(last - ends[j]))
            q2 = q1[:3] + [q1[3] * jnp.exp(ends[2] - ends[1])]
            k2 = [k1[0] * jnp.exp(ends[1] - ends[0])] + k1[1:]
            s0 = _nt(cat(qd), cat(kd))
            s1 = _nt(cat(q1), cat(k1))
            s2 = _nt(cat(q2), cat(k2))
            p_c.append(jnp.where(m0, s0, jnp.where(m1, s1, jnp.where(m2, s2, 0.0))).astype(BF16))
            qb_c.append(jnp.concatenate(qb, axis=0))
            kb_c.append(jnp.concatenate(kb, axis=0))
            last_c.append(last)
        return p_c, qb_c, kb_c, last_c

    def unit_handover(base, p_c, qb_c, kb_c, last_c):
        p_a, qb_a, kb_a, dec_a = [], [], [], []
        for u in range(HG_UNITS):
            c0, c1 = 2 * u, 2 * u + 1
            cross = _nt(qb_c[c1], kb_c[c0]).astype(BF16)
            p_a.append(jnp.concatenate(
                [jnp.concatenate([p_c[c0], jnp.zeros((C, C), BF16)], axis=1),
                 jnp.concatenate([cross, p_c[c1]], axis=1)], axis=0))
            qb_a.append(cat([qb_c[c0], qb_c[c1] * jnp.exp(last_c[c0])]))
            kb_a.append(cat([kb_c[c0] * jnp.exp(last_c[c1]), kb_c[c1]]))
            dec_a.append(jnp.exp(last_c[c0] + last_c[c1]))

        for u in range(HG_UNITS):
            p_sc[base + u] = p_a[u]
            qb_sc[base + u] = qb_a[u]
            kb_sc[base + u] = kb_a[u]
            dec_sc[base + u] = dec_a[u]

    def run(wbuf, rbuf, xn_ref, xn_next_ref, recur, proj):
        hp = HG_SLOT_HEADS
        n_slots = HG_HEADS // hp
        assert (hp, n_slots, N_HG_CHUNKS) == (2, 4, 8), "chunk schedule below is written for this"

        def mid(t, carry):
            heads_a = tuple(hp * t + n for n in range(hp)) if recur else ()
            heads_b = tuple(hp * (t - 1) + n for n in range(hp)) if recur else ()
            late = t == n_slots - 1
            keep0 = jnp.where(late, 4, t - 1)
            keep1 = jnp.where(late, 5, t + 1)
            out0 = N_HG_CHUNKS + 2 * (t - 1)
            chunks = [(keep0, True), (out0, False), (keep1, True), (out0 + 1, False)] if proj else []
            slot(heads_a, rbuf, heads_b, chunks, wbuf, rbuf, xn_ref)
            return carry

        lax.fori_loop(1, n_slots, mid, 0)
        if proj:
            normalise(xnext_ref, xn_next_ref)
        first_out = N_HG_CHUNKS + 2 * (n_slots - 1)
        last_chunks = ([(N_HG_CHUNKS - 2, True), (N_HG_CHUNKS - 1, True)]
                       + [(c, False) for c in range(first_out, N_CHUNKS)])
        slot(tuple(range(hp)) if proj else (), wbuf,
             tuple(HG_HEADS - hp + n for n in range(hp)) if recur else (),
             last_chunks if proj else [], wbuf, rbuf, xn_ref)

    for parity, bufs in ((0, (hb0_ref, hb1_ref, xn0_ref, xn1_ref)), (1, (hb1_ref, hb0_ref, xn1_ref, xn0_ref))):
        on_parity = lax.rem(i, 2) == parity
        if parity == 0:
            pl.when(i == 0)(functools.partial(run, *bufs, recur=False, proj=True))
        if (n_tiles % 2) == parity:
            pl.when(i == n_tiles)(functools.partial(run, *bufs, recur=True, proj=False))
        pl.when(on_parity & (i > 0) & (i < n_tiles))(functools.partial(run, *bufs, recur=True, proj=True))


def _front(x2d, norm_w, w3, lb, hnw, seq):
    m = x2d.shape[0]
    nt = m // FR_TT
    last = nt - 1
    return pl.pallas_call(
        functools.partial(_front_kernel, n_tiles=nt, tiles_per_seq=seq // FR_TT),
        grid=(nt + 1,),
        in_specs=[
            pl.BlockSpec((FR_TT, D_MODEL), lambda i: (0, 0), pipeline_mode=pl.Buffered(1)),
            pl.BlockSpec((FR_TT, D_MODEL), lambda i: (jnp.minimum(i + 1, last), 0)),
            pl.BlockSpec((1, D_MODEL), lambda i: (0, 0)),
            pl.BlockSpec((N_CHUNKS, D_MODEL, NCH), lambda i: (0, 0, 0), pipeline_mode=pl.Buffered(1)),
            pl.BlockSpec((HG_HEADS, 1, HG_DK), lambda i: (0, 0, 0)),
            pl.BlockSpec((HG_HEADS, 1, HG_DV), lambda i: (0, 0, 0)),
        ],
        out_specs=[
            pl.BlockSpec((N_PJ_CHUNKS, FR_TT, NCH), lambda i: (0, jnp.minimum(i, last), 0)),
            pl.BlockSpec((HG_HEADS, FR_TT, HG_DV), lambda i: (0, jnp.maximum(i - 1, 0), 0)),
        ],
        out_shape=[jax.ShapeDtypeStruct((N_PJ_CHUNKS, m, NCH), BF16),
                   jax.ShapeDtypeStruct((HG_HEADS, m, HG_DV), BF16)],
        scratch_shapes=[pltpu.VMEM((FR_TT, D_MODEL), BF16),
                        pltpu.VMEM((FR_TT, D_MODEL), BF16),
                        pltpu.VMEM((4 * HG_HEADS, FR_TT, HG_DK), BF16),
                        pltpu.VMEM((4 * HG_HEADS, FR_TT, HG_DK), BF16),
                        pltpu.VMEM((HG_HEADS, HG_DV, HG_DK), F32),
                        pltpu.VMEM((HG_SLOT_HEADS * HG_UNITS, 2 * HG_C, 2 * HG_C), BF16),
                        pltpu.VMEM((HG_SLOT_HEADS * HG_UNITS, 2 * HG_C, HG_DK), BF16),
                        pltpu.VMEM((HG_SLOT_HEADS * HG_UNITS, 2 * HG_C, HG_DK), BF16),
                        pltpu.VMEM((HG_SLOT_HEADS * HG_UNITS, 1, HG_DK), F32)],
        compiler_params=pltpu.CompilerParams(
            dimension_semantics=("arbitrary",), vmem_limit_bytes=VMEM_LIMIT),
        name="front",
    )(x2d, x2d, norm_w, w3, lb, hnw)


def _back_kernel(sink_ref, q0_ref, q1_ref, kvp_ref, kv_ref, ag0_ref, ag1_ref, gh_ref, mh_ref, ma_ref,
                 x_ref, wbh32_ref, wba32_ref, wo32_ref, fnw_ref, o_ref,
                 kc_ref, vx_ref, ga0_ref, ga1_ref, wbh_ref, wba_ref, wo_ref, acc_ref, mg_ref,
                 *, n_tiles, tiles_per_seq):
    W, DH = WINDOW, ATT_DH
    half_heads = ATT_Q_HEADS // 2
    HALF = D_MODEL // 2
    QUARTER = D_MODEL // 4
    j = pl.program_id(0)

    @pl.when(j == 0)
    def _():
        for hh in range(2):
            wbh_ref[hh] = wbh32_ref[:, hh * HALF:(hh + 1) * HALF].astype(BF16)
            wba_ref[hh] = wba32_ref[:, hh * HALF:(hh + 1) * HALF].astype(BF16)
            wo_ref[hh] = wo32_ref[hh * HALF:(hh + 1) * HALF, :].astype(BF16)
        acc_ref[...] = jnp.zeros_like(acc_ref)

    kc_ref[0:W, :] = kvp_ref[:, :KV_WIDTH]
    kc_ref[W:, :] = kv_ref[:, :KV_WIDTH]
    lane = lax.broadcasted_iota(jnp.int32, (1, 2 * DH), 1)
    low = lane < DH
    for rows, src in ((slice(0, W), kvp_ref), (slice(W, None), kv_ref)):
        for slab in range(ATT_KV_HEADS // 2):
            x = src[:, KV_WIDTH + slab * 2 * DH:KV_WIDTH + (slab + 1) * 2 * DH]
            xr = jnp.concatenate([x[:, DH:], x[:, :DH]], axis=1)
            zero = jnp.zeros_like(x)
            for t, piece in enumerate((jnp.where(low, x, zero), jnp.where(low, zero, xr),
                                       jnp.where(low, xr, zero), jnp.where(low, zero, x))):
                vx_ref[rows, (4 * slab + t) * 2 * DH:(4 * slab + t + 1) * 2 * DH] = piece
    ones_lo = jnp.broadcast_to(jnp.where(low, 1.0, 0.0).astype(BF16), (2 * W, 2 * DH))
    ones_hi = jnp.broadcast_to(jnp.where(low, 0.0, 1.0).astype(BF16), (2 * W, 2 * DH))
    low_rows = jnp.broadcast_to(low, (W, 2 * DH))

    first_tile = lax.rem(jnp.minimum(j, n_tiles - 1), tiles_per_seq) == 0
    qi = lax.broadcasted_iota(jnp.int32, (W, W), 0)
    kj = lax.broadcasted_iota(jnp.int32, (W, W), 1)
    upper = kj > qi

    def head_cols(ref_pair, hd, width):
        ref = ref_pair[hd // half_heads]
        hd = hd % half_heads
        return ref, slice(hd * DH, (hd + width) * DH)

    def half(hh, carry, ga_ref, gap_ref, attend, merge):
        gh = jnp.concatenate([gh_ref[h] for h in range(HG_HEADS)], axis=1) if merge else None
        y_parts = {}

        def branch_piece(name, lhs, w_ref, k):
            y_parts[name, k] = jnp.dot(lhs, w_ref[hh, :, k * QUARTER:(k + 1) * QUARTER],
                                       preferred_element_type=F32)

        def gate():
            yh = jnp.concatenate([y_parts["h", 0], y_parts["h", 1]], axis=1)
            ya = jnp.concatenate([y_parts["a", 0], y_parts["a", 1]], axis=1)
            mg_ref[...] = (_sigmoid(mh_ref[hh].astype(F32)) * yh
                           + _sigmoid(ma_ref[hh].astype(F32)) * ya).astype(BF16)

        def out_piece(k):
            cols = slice(k * QUARTER, (k + 1) * QUARTER)
            acc_ref[:, cols] = jnp.where(hh == 0, 0.0, acc_ref[:, cols]) + jnp.dot(
                mg_ref[...], wo_ref[hh, :, cols], preferred_element_type=F32)

        pieces = [lambda: branch_piece("h", gh, wbh_ref, 0), lambda: branch_piece("h", gh, wbh_ref, 1),
                  lambda: branch_piece("a", gap_ref[...], wba_ref, 0),
                  lambda: branch_piece("a", gap_ref[...], wba_ref, 1),
                  lambda: (gate(), out_piece(0)), lambda: out_piece(1), lambda: out_piece(2),
                  lambda: out_piece(3)]

        units = [(bi, h) for bi in range(2) for h in range(ATT_KV_HEADS)]

        def scores(bi, h):
            r = pl.multiple_of((2 * hh + bi) * W, W)
            kk = kc_ref[pl.ds(r, 2 * W), h * DH:(h + 1) * DH]
            parts = []
            for hd in range(h * ATT_GROUP, (h + 1) * ATT_GROUP):
                ref, cols = head_cols((q0_ref, q1_ref), hd, 1)
                parts.append(ref[pl.ds(r, W), cols])
            return _nt(jnp.concatenate(parts, axis=0), kk)

        s_next = scores(*units[0]) if attend else None
        for idx, (bi, h) in enumerate(units):
            if merge:
                pieces[idx]()
            if attend:
                s = s_next
                if idx + 1 < len(units):
                    s_next = scores(*units[idx + 1])
            if not attend:
                continue
            r = pl.multiple_of((2 * hh + bi) * W, W)
            prev_bias = jnp.where(first_tile & (2 * hh + bi == 0), -jnp.inf, 0.0)
            probs, sink_terms = [], []
            for jj, hd in enumerate(range(h * ATT_GROUP, (h + 1) * ATT_GROUP)):
                sj = s[jj * W:(jj + 1) * W]
                c = jnp.where(upper, sj[:, :W] + prev_bias, sj[:, W:])
                sink = sink_ref[hd]
                m = jnp.maximum(jnp.max(c, axis=-1, keepdims=True), sink)
                p = jnp.exp(c - m)
                sink_terms.append(jnp.exp(sink - m))
                probs.append(jnp.concatenate(
                    [jnp.where(upper, p, 0.0), jnp.where(upper, 0.0, p)], axis=1).astype(BF16))
            w_lo = jnp.concatenate(
                [vx_ref[pl.ds(r, 2 * W), (2 * h) * 2 * DH:(2 * h + 1) * 2 * DH], ones_lo], axis=1)
            w_hi = jnp.concatenate(
                [vx_ref[pl.ds(r, 2 * W), (2 * h + 1) * 2 * DH:(2 * h + 2) * 2 * DH], ones_hi], axis=1)
            res = (jnp.dot(jnp.concatenate(probs[0::2], axis=0), w_lo, preferred_element_type=F32)
                   + jnp.dot(jnp.concatenate(probs[1::2], axis=0), w_hi, preferred_element_type=F32))
            for pair in range(ATT_GROUP // 2):
                rp = res[pair * W:(pair + 1) * W]
                den = rp[:, 2 * DH:] + jnp.where(low_rows, sink_terms[2 * pair], sink_terms[2 * pair + 1])
                j0 = h * ATT_GROUP + 2 * pair
                ag_ref, cols = head_cols((ag0_ref, ag1_ref), j0, 2)
                ag = ag_ref[pl.ds(r, W), cols].astype(F32)
                ga_ref[pl.ds(r, W), j0 * DH:(j0 + 2) * DH] = (
                    rp[:, :2 * DH] / den * (ag * _sigmoid(ag))).astype(BF16)
        return carry

    def run(ga_ref, gap_ref, attend, merge):
        lax.fori_loop(0, 2, functools.partial(half, ga_ref=ga_ref, gap_ref=gap_ref, attend=attend,
                                              merge=merge), 0)
        if merge:
            xo = x_ref[...] + acc_ref[...]
            ms = jnp.mean(xo * xo, axis=-1, keepdims=True)
            o_ref[...] = xo * lax.rsqrt(ms + EPS) * fnw_ref[...]

    for parity, bufs in ((0, (ga0_ref, ga1_ref)), (1, (ga1_ref, ga0_ref))):
        if parity == 0:
            pl.when(j == 0)(functools.partial(run, *bufs, attend=True, merge=False))
        if (n_tiles % 2) == parity:
            pl.when(j == n_tiles)(functools.partial(run, *bufs, attend=False, merge=True))
        pl.when((lax.rem(j, 2) == parity) & (j > 0) & (j < n_tiles))(
            functools.partial(run, *bufs, attend=True, merge=True))


def _back(pj, gh3, x2d, sinks, wbh, wba, wo, fnw, seq):
    m = x2d.shape[0]
    nt = m // BK_TT
    last = nt - 1
    per = BK_TT // WINDOW

    def cur(jj):
        return jnp.minimum(jj, last)

    def prv(jj):
        return jnp.maximum(jj - 1, 0)

    def chunk(c):
        return pl.BlockSpec((None, BK_TT, NCH), lambda jj, s: (c, cur(jj), 0))

    def pair(c):
        return pl.BlockSpec((2, BK_TT, NCH), lambda jj, s: (c // 2, prv(jj), 0))

    prev_kv = pl.BlockSpec((None, WINDOW, NCH),
                           lambda jj, s: (PJ_KV, jnp.maximum(cur(jj) * per - 1, 0), 0))
    tile = pl.BlockSpec((BK_TT, D_MODEL), lambda jj, s: (prv(jj), 0))
    wspec = pl.BlockSpec((D_MODEL, D_MODEL), lambda jj, s: (0, 0), pipeline_mode=pl.Buffered(1))
    half = D_MODEL // 2
    grid_spec = pltpu.PrefetchScalarGridSpec(
        num_scalar_prefetch=1,
        grid=(nt + 1,),
        in_specs=[chunk(PJ_AQ), chunk(PJ_AQ + 1), prev_kv, chunk(PJ_KV), chunk(PJ_AG), chunk(PJ_AG + 1),
                  pl.BlockSpec((HG_HEADS, BK_TT, HG_DV), lambda jj, s: (0, prv(jj), 0)),
                  pair(PJ_MH), pair(PJ_MA), tile, wspec, wspec, wspec,
                  pl.BlockSpec((1, D_MODEL), lambda jj, s: (0, 0))],
        out_specs=tile,
        scratch_shapes=[pltpu.VMEM((BK_TT + WINDOW, KV_WIDTH), BF16),
                        pltpu.VMEM((BK_TT + WINDOW, ATT_KV_HEADS * 4 * ATT_DH), BF16),
                        pltpu.VMEM((BK_TT, D_MODEL), BF16),
                        pltpu.VMEM((BK_TT, D_MODEL), BF16),
                        pltpu.VMEM((2, D_MODEL, half), BF16),
                        pltpu.VMEM((2, D_MODEL, half), BF16),
                        pltpu.VMEM((2, half, D_MODEL), BF16),
                        pltpu.VMEM((BK_TT, D_MODEL), F32),
                        pltpu.VMEM((BK_TT, half), BF16)],
    )
    return pl.pallas_call(
        functools.partial(_back_kernel, n_tiles=nt, tiles_per_seq=seq // BK_TT),
        grid_spec=grid_spec,
        out_shape=jax.ShapeDtypeStruct((m, D_MODEL), F32),
        compiler_params=pltpu.CompilerParams(
            dimension_semantics=("arbitrary",), vmem_limit_bytes=VMEM_LIMIT),
        name="back",
    )(sinks, pj, pj, pj, pj, pj, pj, gh3, pj, pj, x2d, wbh, wba, wo, fnw)


def kernel(x, norm_w, w_in, hgrn_lower_bound, hgrn_norm_w, w_branch_hgrn, attn_sinks,
           w_branch_attn, w_out, final_norm_w):
    batch, seq, _ = x.shape
    depth = norm_w.shape[0]
    assert depth == 1, "the back kernel fuses the final RMSNorm into the single layer"
    assert seq % FR_TT == 0 and seq % BK_TT == 0
    lb_all = jnp.cumsum(jax.nn.softmax(hgrn_lower_bound.astype(F32), axis=0), axis=0)
    w3 = _wprep(w_in[0])
    x2d = x.reshape(batch * seq, D_MODEL)
    pj, gh3 = _front(x2d, norm_w[0].reshape(1, D_MODEL), w3,
                     lb_all[0].reshape(HG_HEADS, 1, HG_DK),
                     hgrn_norm_w[0].reshape(HG_HEADS, 1, HG_DV), seq)
    xo = _back(pj, gh3, x2d, attn_sinks[0].astype(F32), w_branch_hgrn[0], w_branch_attn[0], w_out[0],
               final_norm_w.reshape(1, D_MODEL), seq)
    return xo.reshape(batch, seq, D_MODEL)
```

```python
import functools

import jax
import jax.numpy as jnp
from jax import lax
from jax.experimental import pallas as pl
from jax.experimental.pallas import tpu as pltpu

F32 = jnp.float32
BF16 = jnp.bfloat16

D_MODEL = 1024
HG_HEADS = 8
HG_DK = 128
HG_DV = 128
ATT_Q_HEADS = 16
ATT_KV_HEADS = 4
ATT_GROUP = ATT_Q_HEADS // ATT_KV_HEADS
ATT_DH = 64
KV_WIDTH = ATT_KV_HEADS * ATT_DH
ATT_SCALE = ATT_DH ** -0.5
assert ATT_SCALE == 0.125
WINDOW = 128
EPS = 1e-6
D_IN = 8 * D_MODEL + 2 * KV_WIDTH

NCH = 512
N_CHUNKS = D_IN // NCH
N_HG_CHUNKS = 4 * D_MODEL // NCH
N_PJ_CHUNKS = N_CHUNKS - N_HG_CHUNKS
PJ_AQ, PJ_AG, PJ_MH, PJ_MA, PJ_KV = 0, 2, 4, 6, 8
SLABS_PER_CHUNK = NCH // HG_DK
SEC_HQ, SEC_HF, SEC_HI, SEC_HG = range(4)

V7X_VMEM_BYTES = 64 * 1024 * 1024
VMEM_LIMIT = V7X_VMEM_BYTES * 7 // 8

HG_C = 64
HG_SUB = 16
FR_TT = 512
HG_GROUP = FR_TT // HG_C
HG_UNITS = HG_GROUP // 2
SUBLANES = 8
HG_SLOT_HEADS = 2
BK_TT = 512


def _nt(a, b):
    return lax.dot_general(a.astype(BF16), b.astype(BF16), (((1,), (1,)), ((), ())),
                           preferred_element_type=F32)


def _sigmoid(x):
    return 1.0 / (1.0 + jnp.exp(-x))


WP_ROWS = 256


def _wprep_kernel(w_ref, o_ref):
    kv_src = N_HG_CHUNKS + 2
    for c in range(N_CHUNKS):
        src = c if c < kv_src else (kv_src if c == N_CHUNKS - 1 else c + 1)
        blk = w_ref[:, src * NCH:(src + 1) * NCH]
        if c // 2 == N_HG_CHUNKS // 2:
            blk = blk * ATT_SCALE
        o_ref[c] = blk.astype(BF16)


def _wprep(w):
    return pl.pallas_call(
        _wprep_kernel,
        grid=(D_MODEL // WP_ROWS,),
        in_specs=[pl.BlockSpec((WP_ROWS, D_IN), lambda r: (r, 0))],
        out_specs=pl.BlockSpec((N_CHUNKS, WP_ROWS, NCH), lambda r: (0, r, 0)),
        out_shape=jax.ShapeDtypeStruct((N_CHUNKS, D_MODEL, NCH), BF16),
        compiler_params=pltpu.CompilerParams(
            dimension_semantics=("arbitrary",), vmem_limit_bytes=VMEM_LIMIT),
        name="wprep",
    )(w)


def _front_kernel(x0_ref, xnext_ref, nw_ref, w_ref, lb_ref, hnw_ref, pj_ref, gh_ref, xn0_ref, xn1_ref,
                  hb0_ref, hb1_ref, st_ref, p_sc, qb_sc, kb_sc, dec_sc, *, n_tiles, tiles_per_seq):
    C, SUB, NB = HG_C, HG_SUB, HG_C // HG_SUB
    assert NB == 4, "the three score levels below are written for four sub-blocks per chunk"
    sub_shift = SUB.bit_length() - 1
    HALF = NCH // 2
    i = pl.program_id(0)

    @pl.when(lax.rem(jnp.maximum(i - 1, 0), tiles_per_seq) == 0)
    def _():
        st_ref[...] = jnp.zeros_like(st_ref)

    def normalise(src_ref, dst_ref):
        x = src_ref[...]
        ms = jnp.mean(x * x, axis=-1, keepdims=True)
        dst_ref[...] = (x * lax.rsqrt(ms + EPS) * nw_ref[...]).astype(BF16)

    @pl.when(i == 0)
    def _():
        normalise(x0_ref, xn0_ref)

    row = lax.broadcasted_iota(jnp.int32, (C, C), 0)
    col = lax.broadcasted_iota(jnp.int32, (C, C), 1)
    rb, cb = row >> sub_shift, col >> sub_shift
    m0 = (rb == cb) & (col <= row)
    m1 = ((rb & 1) == 1) & (cb == rb - 1)
    m2 = (row >= 2 * SUB) & (col < 2 * SUB)
    rows = [slice(c * C, (c + 1) * C) for c in range(HG_GROUP)]
    urows = [slice(u * 2 * C, (u + 1) * 2 * C) for u in range(HG_UNITS)]
    sub_row = lax.broadcasted_iota(jnp.int32, (SUBLANES, HG_DK), 0)

    def cat(blocks):
        return jnp.concatenate(blocks, axis=0).astype(BF16)

    def cumsum_rows(g):
        groups = []
        for v in range(C // SUBLANES):
            xg = g[v * SUBLANES:(v + 1) * SUBLANES]
            shift = 1
            while shift < SUBLANES:
                xg = xg + jnp.where(sub_row >= shift, pltpu.roll(xg, shift, axis=0), 0.0)
                shift *= 2
            groups.append(xg)
        out, carry = [groups[0]], groups[0][SUBLANES - 1:SUBLANES]
        for xg in groups[1:]:
            out.append(xg + carry)
            carry = carry + xg[SUBLANES - 1:SUBLANES]
        return jnp.concatenate(out, axis=0)

    def slot(heads_a, abuf, heads_b, chunks, wbuf, rbuf, xn_ref):
        pieces = [(chunk, half, keep) for chunk, keep in chunks for half in range(2)]

        def project(flush=False):
            if not pieces:
                return
            chunk, half, keep = pieces.pop(0)
            acc = jnp.dot(xn_ref[...], w_ref[chunk, :, half * HALF:(half + 1) * HALF],
                          preferred_element_type=F32).astype(BF16)
            if not keep:
                pj_ref[chunk - N_HG_CHUNKS, :, half * HALF:(half + 1) * HALF] = acc
            else:
                for s in range(SLABS_PER_CHUNK // 2):
                    slab = chunk * SLABS_PER_CHUNK + half * (SLABS_PER_CHUNK // 2) + s
                    wbuf[slab] = acc[:, s * HG_DK:(s + 1) * HG_DK]
            if flush:
                project(True)

        def src(sec, head, buf=rbuf):
            return buf.at[sec * HG_HEADS + head]

        loaded = []
        for n, head in enumerate(heads_b):
            base = n * HG_UNITS
            loaded.append(dict(
                v=[src(SEC_HI, head)[urows[u], :] for u in range(HG_UNITS)],
                p=[p_sc[base + u] for u in range(HG_UNITS)],
                qb=[qb_sc[base + u] for u in range(HG_UNITS)],
                kb=[kb_sc[base + u] for u in range(HG_UNITS)],
                dec=[dec_sc[base + u] for u in range(HG_UNITS)]))

        project()

        gates = []
        for head in heads_a:
            lb = lb_ref[head]
            q, k, b = [], [], []
            for c in range(HG_GROUP):
                xq = src(SEC_HQ, head, abuf)[rows[c], :].astype(F32)
                xf = src(SEC_HF, head, abuf)[rows[c], :].astype(F32)
                f = lb + (1.0 - lb) * _sigmoid(xf)
                k.append(1.0 - f)
                q.append(xq * _sigmoid(xq))
                b.append(cumsum_rows(jnp.log(f)))
            gates.append((q, k, b))

        for ld in loaded:
            ld["o_intra"] = [jnp.dot(ld["p"][u], ld["v"][u], preferred_element_type=F32)
                             for u in range(HG_UNITS)]
            ld["kv"] = [lax.dot_general(ld["v"][u], ld["kb"][u], (((0,), (0,)), ((), ())),
                                        preferred_element_type=F32) for u in range(HG_UNITS)]
            project()

        for head, ld in zip(heads_b, loaded):
            st = st_ref[head]
            ld["o"] = []
            for u in range(HG_UNITS):
                ld["o"].append(ld["o_intra"][u] + _nt(ld["qb"][u], st))
                st = st * ld["dec"][u] + ld["kv"][u]
            st_ref[head] = st
            project()

        for n, (q, k, b) in enumerate(gates):
            unit_handover(n * HG_UNITS, *chunk_scores(q, k, b))
            project()

        project(flush=True)

        for head, ld in zip(heads_b, loaded):
            nw = hnw_ref[head]
            for u in range(HG_UNITS):
                xg = src(SEC_HG, head)[urows[u], :].astype(F32)
                o = ld["o"][u]
                ms_o = jnp.mean(o * o, axis=-1, keepdims=True)
                on = o * lax.rsqrt(ms_o + EPS) * nw
                gh_ref[head, urows[u], :] = (on * (xg * _sigmoid(xg))).astype(BF16)

    def chunk_scores(q, k, b):
        p_c, qb_c, kb_c, last_c = [], [], [], []
        for c in range(HG_GROUP):
            ends = [b[c][(j + 1) * SUB - 1:(j + 1) * SUB] for j in range(NB)]
            starts = [jnp.zeros_like(ends[0])] + ends[:-1]
            last = ends[-1]
            q1, k1, qd, kd, qb, kb = [], [], [], [], [], []
            for j in range(NB):
                sl = slice(j * SUB, (j + 1) * SUB)
                bj = b[c][sl]
                q1j = q[c][sl] * jnp.exp(bj - starts[j])
                k1j = k[c][sl] * jnp.exp(ends[j] - bj)
                half = jnp.exp(0.5 * (starts[j] - ends[j]))
                q1.append(q1j)
                k1.append(k1j)
                qd.append(q1j * half)
                kd.append(k1j * half)
                qb.append(q1j * jnp.exp(starts[j]))
                kb.append(k1j * jnp.exp(last - ends[j]))
            q2 = q1[:3] + [q1[3] * jnp.exp(ends[2] - ends[1])]
            k2 = [k1[0] * jnp.exp(ends[1] - ends[0])] + k1[1:]
            s0 = _nt(cat(qd), cat(kd))
            s1 = _nt(cat(q1), cat(k1))
            s2 = _nt(cat(q2), cat(k2))
            p_c.append(jnp.where(m0, s0, jnp.where(m1, s1, jnp.where(m2, s2, 0.0))).astype(BF16))
            qb_c.append(jnp.concatenate(qb, axis=0))
            kb_c.append(jnp.concatenate(kb, axis=0))
            last_c.append(last)
        return p_c, qb_c, kb_c, last_c

    def unit_handover(base, p_c, qb_c, kb_c, last_c):
        p_a, qb_a, kb_a, dec_a = [], [], [], []
        for u in range(HG_UNITS):
            c0, c1 = 2 * u, 2 * u + 1
            cross = _nt(qb_c[c1], kb_c[c0]).astype(BF16)
            p_a.append(jnp.concatenate(
                [jnp.concatenate([p_c[c0], jnp.zeros((C, C), BF16)], axis=1),
                 jnp.concatenate([cross, p_c[c1]], axis=1)], axis=0))
            qb_a.append(cat([qb_c[c0], qb_c[c1] * jnp.exp(last_c[c0])]))
            kb_a.append(cat([kb_c[c0] * jnp.exp(last_c[c1]), kb_c[c1]]))
            dec_a.append(jnp.exp(last_c[c0] + last_c[c1]))

        for u in range(HG_UNITS):
            p_sc[base + u] = p_a[u]
            qb_sc[base + u] = qb_a[u]
            kb_sc[base + u] = kb_a[u]
            dec_sc[base + u] = dec_a[u]

    def run(wbuf, rbuf, xn_ref, xn_next_ref, recur, proj):
        hp = HG_SLOT_HEADS
        n_slots = HG_HEADS // hp
        assert (hp, n_slots, N_HG_CHUNKS) == (2, 4, 8), "chunk schedule below is written for this"

        def mid(t, carry):
            heads_a = tuple(hp * t + n for n in range(hp)) if recur else ()
            heads_b = tuple(hp * (t - 1) + n for n in range(hp)) if recur else ()
            late = t == n_slots - 1
            keep0 = jnp.where(late, 4, t - 1)
            keep1 = jnp.where(late, 5, t + 1)
            out0 = N_HG_CHUNKS + 2 * (t - 1)
            chunks = [(keep0, True), (out0, False), (keep1, True), (out0 + 1, False)] if proj else []
            slot(heads_a, rbuf, heads_b, chunks, wbuf, rbuf, xn_ref)
            return carry

        lax.fori_loop(1, n_slots, mid, 0)
        if proj:
            normalise(xnext_ref, xn_next_ref)
        first_out = N_HG_CHUNKS + 2 * (n_slots - 1)
        last_chunks = ([(N_HG_CHUNKS - 2, True), (N_HG_CHUNKS - 1, True)]
                       + [(c, False) for c in range(first_out, N_CHUNKS)])
        slot(tuple(range(hp)) if proj else (), wbuf,
             tuple(HG_HEADS - hp + n for n in range(hp)) if recur else (),
             last_chunks if proj else [], wbuf, rbuf, xn_ref)

    for parity, bufs in ((0, (hb0_ref, hb1_ref, xn0_ref, xn1_ref)), (1, (hb1_ref, hb0_ref, xn1_ref, xn0_ref))):
        on_parity = lax.rem(i, 2) == parity
        if parity == 0:
            pl.when(i == 0)(functools.partial(run, *bufs, recur=False, proj=True))
        if (n_tiles % 2) == parity:
            pl.when(i == n_tiles)(functools.partial(run, *bufs, recur=True, proj=False))
        pl.when(on_parity & (i > 0) & (i < n_tiles))(functools.partial(run, *bufs, recur=True, proj=True))


def _front(x2d, norm_w, w3, lb, hnw, seq):
    m = x2d.shape[0]
    nt = m // FR_TT
    last = nt - 1
    return pl.pallas_call(
        functools.partial(_front_kernel, n_tiles=nt, tiles_per_seq=seq // FR_TT),
        grid=(nt + 1,),
        in_specs=[
            pl.BlockSpec((FR_TT, D_MODEL), lambda i: (0, 0), pipeline_mode=pl.Buffered(1)),
            pl.BlockSpec((FR_TT, D_MODEL), lambda i: (jnp.minimum(i + 1, last), 0)),
            pl.BlockSpec((1, D_MODEL), lambda i: (0, 0)),
            pl.BlockSpec((N_CHUNKS, D_MODEL, NCH), lambda i: (0, 0, 0), pipeline_mode=pl.Buffered(1)),
            pl.BlockSpec((HG_HEADS, 1, HG_DK), lambda i: (0, 0, 0)),
            pl.BlockSpec((HG_HEADS, 1, HG_DV), lambda i: (0, 0, 0)),
        ],
        out_specs=[
            pl.BlockSpec((N_PJ_CHUNKS, FR_TT, NCH), lambda i: (0, jnp.minimum(i, last), 0)),
            pl.BlockSpec((HG_HEADS, FR_TT, HG_DV), lambda i: (0, jnp.maximum(i - 1, 0), 0)),
        ],
        out_shape=[jax.ShapeDtypeStruct((N_PJ_CHUNKS, m, NCH), BF16),
                   jax.ShapeDtypeStruct((HG_HEADS, m, HG_DV), BF16)],
        scratch_shapes=[pltpu.VMEM((FR_TT, D_MODEL), BF16),
                        pltpu.VMEM((FR_TT, D_MODEL), BF16),
                        pltpu.VMEM((4 * HG_HEADS, FR_TT, HG_DK), BF16),
                        pltpu.VMEM((4 * HG_HEADS, FR_TT, HG_DK), BF16),
                        pltpu.VMEM((HG_HEADS, HG_DV, HG_DK), F32),
                        pltpu.VMEM((HG_SLOT_HEADS * HG_UNITS, 2 * HG_C, 2 * HG_C), BF16),
                        pltpu.VMEM((HG_SLOT_HEADS * HG_UNITS, 2 * HG_C, HG_DK), BF16),
                        pltpu.VMEM((HG_SLOT_HEADS * HG_UNITS, 2 * HG_C, HG_DK), BF16),
                        pltpu.VMEM((HG_SLOT_HEADS * HG_UNITS, 1, HG_DK), F32)],
        compiler_params=pltpu.CompilerParams(
            dimension_semantics=("arbitrary",), vmem_limit_bytes=VMEM_LIMIT),
        name="front",
    )(x2d, x2d, norm_w, w3, lb, hnw)


def _back_kernel(sink_ref, q0_ref, q1_ref, kvp_ref, kv_ref, ag0_ref, ag1_ref, gh_ref, mh_ref, ma_ref,
                 x_ref, wbh32_ref, wba32_ref, wo32_ref, fnw_ref, o_ref,
                 kc_ref, vx_ref, ga0_ref, ga1_ref, wbh_ref, wba_ref, wo_ref, acc_ref, mg_ref,
                 *, n_tiles, tiles_per_seq):
    W, DH = WINDOW, ATT_DH
    half_heads = ATT_Q_HEADS // 2
    HALF = D_MODEL // 2
    QUARTER = D_MODEL // 4
    j = pl.program_id(0)

    @pl.when(j == 0)
    def _():
        for hh in range(2):
            wbh_ref[hh] = wbh32_ref[:, hh * HALF:(hh + 1) * HALF].astype(BF16)
            wba_ref[hh] = wba32_ref[:, hh * HALF:(hh + 1) * HALF].astype(BF16)
            wo_ref[hh] = wo32_ref[hh * HALF:(hh + 1) * HALF, :].astype(BF16)
        acc_ref[...] = jnp.zeros_like(acc_ref)

    kc_ref[0:W, :] = kvp_ref[:, :KV_WIDTH]
    kc_ref[W:, :] = kv_ref[:, :KV_WIDTH]
    lane = lax.broadcasted_iota(jnp.int32, (1, 2 * DH), 1)
    low = lane < DH
    for rows, src in ((slice(0, W), kvp_ref), (slice(W, None), kv_ref)):
        for slab in range(ATT_KV_HEADS // 2):
            x = src[:, KV_WIDTH + slab * 2 * DH:KV_WIDTH + (slab + 1) * 2 * DH]
            xr = jnp.concatenate([x[:, DH:], x[:, :DH]], axis=1)
            zero = jnp.zeros_like(x)
            for t, piece in enumerate((jnp.where(low, x, zero), jnp.where(low, zero, xr),
                                       jnp.where(low, xr, zero), jnp.where(low, zero, x))):
                vx_ref[rows, (4 * slab + t) * 2 * DH:(4 * slab + t + 1) * 2 * DH] = piece
    ones_lo = jnp.broadcast_to(jnp.where(low, 1.0, 0.0).astype(BF16), (2 * W, 2 * DH))
    ones_hi = jnp.broadcast_to(jnp.where(low, 0.0, 1.0).astype(BF16), (2 * W, 2 * DH))
    low_rows = jnp.broadcast_to(low, (W, 2 * DH))

    first_tile = lax.rem(jnp.minimum(j, n_tiles - 1), tiles_per_seq) == 0
    qi = lax.broadcasted_iota(jnp.int32, (W, W), 0)
    kj = lax.broadcasted_iota(jnp.int32, (W, W), 1)
    upper = kj > qi

    def head_cols(ref_pair, hd, width):
        ref = ref_pair[hd // half_heads]
        hd = hd % half_heads
        return ref, slice(hd * DH, (hd + width) * DH)

    def half(hh, carry, ga_ref, gap_ref, attend, merge):
        gh = jnp.concatenate([gh_ref[h] for h in range(HG_HEADS)], axis=1) if merge else None
        y_parts = {}

        def branch_piece(name, lhs, w_ref, k):
            y_parts[name, k] = jnp.dot(lhs, w_ref[hh, :, k * QUARTER:(k + 1) * QUARTER],
                                       preferred_element_type=F32)

        def gate():
            yh = jnp.concatenate([y_parts["h", 0], y_parts["h", 1]], axis=1)
            ya = jnp.concatenate([y_parts["a", 0], y_parts["a", 1]], axis=1)
            mg_ref[...] = (_sigmoid(mh_ref[hh].astype(F32)) * yh
                           + _sigmoid(ma_ref[hh].astype(F32)) * ya).astype(BF16)

        def out_piece(k):
            cols = slice(k * QUARTER, (k + 1) * QUARTER)
            part = jnp.dot(mg_ref[...], wo_ref[hh, :, cols], preferred_element_type=F32)
            acc_ref[:, cols] = part if hh == 0 else acc_ref[:, cols] + part

        pieces = [lambda: branch_piece("h", gh, wbh_ref, 0), lambda: branch_piece("h", gh, wbh_ref, 1),
                  lambda: branch_piece("a", gap_ref[...], wba_ref, 0),
                  lambda: branch_piece("a", gap_ref[...], wba_ref, 1),
                  lambda: (gate(), out_piece(0)), lambda: out_piece(1), lambda: out_piece(2),
                  lambda: out_piece(3)]

        units = [(bi, h) for bi in range(2) for h in range(ATT_KV_HEADS)]

        def scores(bi, h):
            r = pl.multiple_of((2 * hh + bi) * W, W)
            kk = kc_ref[pl.ds(r, 2 * W), h * DH:(h + 1) * DH]
            parts = []
            for hd in range(h * ATT_GROUP, (h + 1) * ATT_GROUP):
                ref, cols = head_cols((q0_ref, q1_ref), hd, 1)
                parts.append(ref[pl.ds(r, W), cols])
            return _nt(jnp.concatenate(parts, axis=0), kk)

        s_next = scores(*units[0]) if attend else None
        for idx, (bi, h) in enumerate(units):
            if merge:
                pieces[idx]()
            if attend:
                s = s_next
                if idx + 1 < len(units):
                    s_next = scores(*units[idx + 1])
            if not attend:
                continue
            r = pl.multiple_of((2 * hh + bi) * W, W)
            prev_bias = jnp.where(first_tile & (2 * hh + bi == 0), -jnp.inf, 0.0)
            probs, sink_terms = [], []
            for jj, hd in enumerate(range(h * ATT_GROUP, (h + 1) * ATT_GROUP)):
                sj = s[jj * W:(jj + 1) * W]
                c = jnp.where(upper, sj[:, :W] + prev_bias, sj[:, W:])
                sink = sink_ref[hd]
                m = jnp.maximum(jnp.max(c, axis=-1, keepdims=True), sink)
                p = jnp.exp(c - m)
                sink_terms.append(jnp.exp(sink - m))
                probs.append(jnp.concatenate(
                    [jnp.where(upper, p, 0.0), jnp.where(upper, 0.0, p)], axis=1).astype(BF16))
            w_lo = jnp.concatenate(
                [vx_ref[pl.ds(r, 2 * W), (2 * h) * 2 * DH:(2 * h + 1) * 2 * DH], ones_lo], axis=1)
            w_hi = jnp.concatenate(
                [vx_ref[pl.ds(r, 2 * W), (2 * h + 1) * 2 * DH:(2 * h + 2) * 2 * DH], ones_hi], axis=1)
            res = (jnp.dot(jnp.concatenate(probs[0::2], axis=0), w_lo, preferred_element_type=F32)
                   + jnp.dot(jnp.concatenate(probs[1::2], axis=0), w_hi, preferred_element_type=F32))
            for pair in range(ATT_GROUP // 2):
                rp = res[pair * W:(pair + 1) * W]
                den = rp[:, 2 * DH:] + jnp.where(low_rows, sink_terms[2 * pair], sink_terms[2 * pair + 1])
                j0 = h * ATT_GROUP + 2 * pair
                ag_ref, cols = head_cols((ag0_ref, ag1_ref), j0, 2)
                ag = ag_ref[pl.ds(r, W), cols].astype(F32)
                ga_ref[pl.ds(r, W), j0 * DH:(j0 + 2) * DH] = (
                    rp[:, :2 * DH] / den * (ag * _sigmoid(ag))).astype(BF16)
        return carry

    def run(ga_ref, gap_ref, attend, merge):
        for hh in range(2):
            half(hh, 0, ga_ref=ga_ref, gap_ref=gap_ref, attend=attend, merge=merge)
        if merge:
            xo = x_ref[...] + acc_ref[...]
            ms = jnp.mean(xo * xo, axis=-1, keepdims=True)
            o_ref[...] = xo * lax.rsqrt(ms + EPS) * fnw_ref[...]

    for parity, bufs in ((0, (ga0_ref, ga1_ref)), (1, (ga1_ref, ga0_ref))):
        if parity == 0:
            pl.when(j == 0)(functools.partial(run, *bufs, attend=True, merge=False))
        if (n_tiles % 2) == parity:
            pl.when(j == n_tiles)(functools.partial(run, *bufs, attend=False, merge=True))
        pl.when((lax.rem(j, 2) == parity) & (j > 0) & (j < n_tiles))(
            functools.partial(run, *bufs, attend=True, merge=True))


def _back(pj, gh3, x2d, sinks, wbh, wba, wo, fnw, seq):
    m = x2d.shape[0]
    nt = m // BK_TT
    last = nt - 1
    per = BK_TT // WINDOW

    def cur(jj):
        return jnp.minimum(jj, last)

    def prv(jj):
        return jnp.maximum(jj - 1, 0)

    def chunk(c):
        return pl.BlockSpec((None, BK_TT, NCH), lambda jj, s: (c, cur(jj), 0))

    def pair(c):
        return pl.BlockSpec((2, BK_TT, NCH), lambda jj, s: (c // 2, prv(jj), 0))

    prev_kv = pl.BlockSpec((None, WINDOW, NCH),
                           lambda jj, s: (PJ_KV, jnp.maximum(cur(jj) * per - 1, 0), 0))
    tile = pl.BlockSpec((BK_TT, D_MODEL), lambda jj, s: (prv(jj), 0))
    wspec = pl.BlockSpec((D_MODEL, D_MODEL), lambda jj, s: (0, 0), pipeline_mode=pl.Buffered(1))
    half = D_MODEL // 2
    grid_spec = pltpu.PrefetchScalarGridSpec(
        num_scalar_prefetch=1,
        grid=(nt + 1,),
        in_specs=[chunk(PJ_AQ), chunk(PJ_AQ + 1), prev_kv, chunk(PJ_KV), chunk(PJ_AG), chunk(PJ_AG + 1),
                  pl.BlockSpec((HG_HEADS, BK_TT, HG_DV), lambda jj, s: (0, prv(jj), 0)),
                  pair(PJ_MH), pair(PJ_MA), tile, wspec, wspec, wspec,
                  pl.BlockSpec((1, D_MODEL), lambda jj, s: (0, 0))],
        out_specs=tile,
        scratch_shapes=[pltpu.VMEM((BK_TT + WINDOW, KV_WIDTH), BF16),
                        pltpu.VMEM((BK_TT + WINDOW, ATT_KV_HEADS * 4 * ATT_DH), BF16),
                        pltpu.VMEM((BK_TT, D_MODEL), BF16),
                        pltpu.VMEM((BK_TT, D_MODEL), BF16),
                        pltpu.VMEM((2, D_MODEL, half), BF16),
                        pltpu.VMEM((2, D_MODEL, half), BF16),
                        pltpu.VMEM((2, half, D_MODEL), BF16),
                        pltpu.VMEM((BK_TT, D_MODEL), F32),
                        pltpu.VMEM((BK_TT, half), BF16)],
    )
    return pl.pallas_call(
        functools.partial(_back_kernel, n_tiles=nt, tiles_per_seq=seq // BK_TT),
        grid_spec=grid_spec,
        out_shape=jax.ShapeDtypeStruct((m, D_MODEL), F32),
        compiler_params=pltpu.CompilerParams(
            dimension_semantics=("arbitrary",), vmem_limit_bytes=VMEM_LIMIT),
        name="back",
    )(sinks, pj, pj, pj, pj, pj, pj, gh3, pj, pj, x2d, wbh, wba, wo, fnw)


def kernel(x, norm_w, w_in, hgrn_lower_bound, hgrn_norm_w, w_branch_hgrn, attn_sinks,
           w_branch_attn, w_out, final_norm_w):
    batch, seq, _ = x.shape
    depth = norm_w.shape[0]
    assert depth == 1, "the back kernel fuses the final RMSNorm into the single layer"
    assert seq % FR_TT == 0 and seq % BK_TT == 0
    lb_all = jnp.cumsum(jax.nn.softmax(hgrn_lower_bound.astype(F32), axis=0), axis=0)
    w3 = _wprep(w_in[0])
    x2d = x.reshape(batch * seq, D_MODEL)
    pj, gh3 = _front(x2d, norm_w[0].reshape(1, D_MODEL), w3,
                     lb_all[0].reshape(HG_HEADS, 1, HG_DK),
                     hgrn_norm_w[0].reshape(HG_HEADS, 1, HG_DV), seq)
    xo = _back(pj, gh3, x2d, attn_sinks[0].astype(F32), w_branch_hgrn[0], w_branch_attn[0], w_out[0],
               final_norm_w.reshape(1, D_MODEL), seq)
    return xo.reshape(batch, seq, D_MODEL)
```

```python
import functools

import jax
import jax.numpy as jnp
from jax import lax
from jax.experimental import pallas as pl
from jax.experimental.pallas import tpu as pltpu

F32 = jnp.float32
BF16 = jnp.bfloat16

D_MODEL = 1024
HG_HEADS = 8
HG_DK = 128
HG_DV = 128
ATT_Q_HEADS = 16
ATT_KV_HEADS = 4
ATT_GROUP = ATT_Q_HEADS // ATT_KV_HEADS
ATT_DH = 64
KV_WIDTH = ATT_KV_HEADS * ATT_DH
ATT_SCALE = ATT_DH ** -0.5
assert ATT_SCALE == 0.125
WINDOW = 128
EPS = 1e-6
D_IN = 8 * D_MODEL + 2 * KV_WIDTH

NCH = 512
N_CHUNKS = D_IN // NCH
N_HG_CHUNKS = 4 * D_MODEL // NCH
N_PJ_CHUNKS = N_CHUNKS - N_HG_CHUNKS
PJ_AQ, PJ_AG, PJ_MH, PJ_MA, PJ_KV = 0, 2, 4, 6, 8
SLABS_PER_CHUNK = NCH // HG_DK
SEC_HQ, SEC_HF, SEC_HI, SEC_HG = range(4)

V7X_VMEM_BYTES = 64 * 1024 * 1024
VMEM_LIMIT = V7X_VMEM_BYTES * 7 // 8

HG_C = 64
HG_SUB = 16
FR_TT = 512
HG_GROUP = FR_TT // HG_C
HG_UNITS = HG_GROUP // 2
SUBLANES = 8
HG_SLOT_HEADS = 2
BK_TT = 512


def _nt(a, b):
    return lax.dot_general(a.astype(BF16), b.astype(BF16), (((1,), (1,)), ((), ())),
                           preferred_element_type=F32)


def _sigmoid(x):
    return 1.0 / (1.0 + jnp.exp(-x))


WP_ROWS = 256


def _wprep_kernel(w_ref, o_ref):
    kv_src = N_HG_CHUNKS + 2
    for c in range(N_CHUNKS):
        src = c if c < kv_src else (kv_src if c == N_CHUNKS - 1 else c + 1)
        blk = w_ref[:, src * NCH:(src + 1) * NCH]
        if c // 2 == N_HG_CHUNKS // 2:
            blk = blk * ATT_SCALE
        o_ref[c] = blk.astype(BF16)


def _wprep(w):
    return pl.pallas_call(
        _wprep_kernel,
        grid=(D_MODEL // WP_ROWS,),
        in_specs=[pl.BlockSpec((WP_ROWS, D_IN), lambda r: (r, 0))],
        out_specs=pl.BlockSpec((N_CHUNKS, WP_ROWS, NCH), lambda r: (0, r, 0)),
        out_shape=jax.ShapeDtypeStruct((N_CHUNKS, D_MODEL, NCH), BF16),
        compiler_params=pltpu.CompilerParams(
            dimension_semantics=("arbitrary",), vmem_limit_bytes=VMEM_LIMIT),
        name="wprep",
    )(w)


def _front_kernel(x0_ref, xnext_ref, nw_ref, w_ref, lb_ref, hnw_ref, pj_ref, gh_ref, xn0_ref, xn1_ref,
                  hb0_ref, hb1_ref, st_ref, p_sc, qb_sc, kb_sc, dec_sc, *, n_tiles, tiles_per_seq):
    C, SUB, NB = HG_C, HG_SUB, HG_C // HG_SUB
    assert NB == 4, "the three score levels below are written for four sub-blocks per chunk"
    sub_shift = SUB.bit_length() - 1
    HALF = NCH // 2
    i = pl.program_id(0)

    @pl.when(lax.rem(jnp.maximum(i - 1, 0), tiles_per_seq) == 0)
    def _():
        st_ref[...] = jnp.zeros_like(st_ref)

    def normalise(src_ref, dst_ref):
        x = src_ref[...]
        ms = jnp.mean(x * x, axis=-1, keepdims=True)
        dst_ref[...] = (x * lax.rsqrt(ms + EPS) * nw_ref[...]).astype(BF16)

    @pl.when(i == 0)
    def _():
        normalise(x0_ref, xn0_ref)

    row = lax.broadcasted_iota(jnp.int32, (C, C), 0)
    col = lax.broadcasted_iota(jnp.int32, (C, C), 1)
    rb, cb = row >> sub_shift, col >> sub_shift
    m0 = (rb == cb) & (col <= row)
    m1 = ((rb & 1) == 1) & (cb == rb - 1)
    m2 = (row >= 2 * SUB) & (col < 2 * SUB)
    rows = [slice(c * C, (c + 1) * C) for c in range(HG_GROUP)]
    urows = [slice(u * 2 * C, (u + 1) * 2 * C) for u in range(HG_UNITS)]
    sub_row = lax.broadcasted_iota(jnp.int32, (SUBLANES, HG_DK), 0)

    def cat(blocks):
        return jnp.concatenate(blocks, axis=0).astype(BF16)

    def cumsum_rows(g):
        groups = []
        for v in range(C // SUBLANES):
            xg = g[v * SUBLANES:(v + 1) * SUBLANES]
            shift = 1
            while shift < SUBLANES:
                xg = xg + jnp.where(sub_row >= shift, pltpu.roll(xg, shift, axis=0), 0.0)
                shift *= 2
            groups.append(xg)
        out, carry = [groups[0]], groups[0][SUBLANES - 1:SUBLANES]
        for xg in groups[1:]:
            out.append(xg + carry)
            carry = carry + xg[SUBLANES - 1:SUBLANES]
        return jnp.concatenate(out, axis=0)

    def slot(heads_a, abuf, heads_b, chunks, wbuf, rbuf, xn_ref):
        pieces = [(chunk, half, keep) for chunk, keep in chunks for half in range(2)]

        def project(flush=False):
            if not pieces:
                return
            chunk, half, keep = pieces.pop(0)
            acc = jnp.dot(xn_ref[...], w_ref[chunk, :, half * HALF:(half + 1) * HALF],
                          preferred_element_type=F32).astype(BF16)
            if not keep:
                pj_ref[chunk - N_HG_CHUNKS, :, half * HALF:(half + 1) * HALF] = acc
            else:
                for s in range(SLABS_PER_CHUNK // 2):
                    slab = chunk * SLABS_PER_CHUNK + half * (SLABS_PER_CHUNK // 2) + s
                    wbuf[slab] = acc[:, s * HG_DK:(s + 1) * HG_DK]
            if flush:
                project(True)

        def src(sec, head, buf=rbuf):
            return buf.at[sec * HG_HEADS + head]

        loaded = []
        for n, head in enumerate(heads_b):
            base = n * HG_UNITS
            loaded.append(dict(
                v=[src(SEC_HI, head)[urows[u], :] for u in range(HG_UNITS)],
                p=[p_sc[base + u] for u in range(HG_UNITS)],
                qb=[qb_sc[base + u] for u in range(HG_UNITS)],
                kb=[kb_sc[base + u] for u in range(HG_UNITS)],
                dec=[dec_sc[base + u] for u in range(HG_UNITS)]))

        project()

        gates = []
        for head in heads_a:
            lb = lb_ref[head]
            q, k, b = [], [], []
            for c in range(HG_GROUP):
                xq = src(SEC_HQ, head, abuf)[rows[c], :].astype(F32)
                xf = src(SEC_HF, head, abuf)[rows[c], :].astype(F32)
                f = lb + (1.0 - lb) * _sigmoid(xf)
                k.append(1.0 - f)
                q.append(xq * _sigmoid(xq))
                b.append(cumsum_rows(jnp.log(f)))
            gates.append((q, k, b))

        for ld in loaded:
            ld["o_intra"] = [jnp.dot(ld["p"][u], ld["v"][u], preferred_element_type=F32)
                             for u in range(HG_UNITS)]
            ld["kv"] = [lax.dot_general(ld["v"][u], ld["kb"][u], (((0,), (0,)), ((), ())),
                                        preferred_element_type=F32) for u in range(HG_UNITS)]
            project()

        for head, ld in zip(heads_b, loaded):
            st = st_ref[head]
            ld["o"] = []
            for u in range(HG_UNITS):
                ld["o"].append(ld["o_intra"][u] + _nt(ld["qb"][u], st))
                st = st * ld["dec"][u] + ld["kv"][u]
            st_ref[head] = st
            project()

        for n, (q, k, b) in enumerate(gates):
            unit_handover(n * HG_UNITS, *chunk_scores(q, k, b))
            project()

        project(flush=True)

        for head, ld in zip(heads_b, loaded):
            nw = hnw_ref[head]
            for u in range(HG_UNITS):
                xg = src(SEC_HG, head)[urows[u], :].astype(F32)
                o = ld["o"][u]
                ms_o = jnp.mean(o * o, axis=-1, keepdims=True)
                on = o * lax.rsqrt(ms_o + EPS) * nw
                gh_ref[head, urows[u], :] = (on * (xg * _sigmoid(xg))).astype(BF16)

    def chunk_scores(q, k, b):
        p_c, qb_c, kb_c, last_c = [], [], [], []
        for c in range(HG_GROUP):
            ends = [b[c][(j + 1) * SUB - 1:(j + 1) * SUB] for j in range(NB)]
            starts = [jnp.zeros_like(ends[0])] + ends[:-1]
            last = ends[-1]
            q1, k1, qd, kd, qb, kb = [], [], [], [], [], []
            for j in range(NB):
                sl = slice(j * SUB, (j + 1) * SUB)
                bj = b[c][sl]
                q1j = q[c][sl] * jnp.exp(bj - starts[j])
                k1j = k[c][sl] * jnp.exp(ends[j] - bj)
                half = jnp.exp(0.5 * (starts[j] - ends[j]))
                q1.append(q1j)
                k1.append(k1j)
                qd.append(q1j * half)
                kd.append(k1j * half)
                qb.append(q1j * jnp.exp(starts[j]))
                kb.append(k1j * jnp.exp(last - ends[j]))
            q2 = q1[:3] + [q1[3] * jnp.exp(ends[2] - ends[1])]
            k2 = [k1[0] * jnp.exp(ends[1] - ends[0])] + k1[1:]
            s0 = _nt(cat(qd), cat(kd))
            s1 = _nt(cat(q1), cat(k1))
            s2 = _nt(cat(q2), cat(k2))
            p_c.append(jnp.where(m0, s0, jnp.where(m1, s1, jnp.where(m2, s2, 0.0))).astype(BF16))
            qb_c.append(jnp.concatenate(qb, axis=0))
            kb_c.append(jnp.concatenate(kb, axis=0))
            last_c.append(last)
        return p_c, qb_c, kb_c, last_c

    def unit_handover(base, p_c, qb_c, kb_c, last_c):
        p_a, qb_a, kb_a, dec_a = [], [], [], []
        for u in range(HG_UNITS):
            c0, c1 = 2 * u, 2 * u + 1
            cross = _nt(qb_c[c1], kb_c[c0]).astype(BF16)
            p_a.append(jnp.concatenate(
                [jnp.concatenate([p_c[c0], jnp.zeros((C, C), BF16)], axis=1),
                 jnp.concatenate([cross, p_c[c1]], axis=1)], axis=0))
            qb_a.append(cat([qb_c[c0], qb_c[c1] * jnp.exp(last_c[c0])]))
            kb_a.append(cat([kb_c[c0] * jnp.exp(last_c[c1]), kb_c[c1]]))
            dec_a.append(jnp.exp(last_c[c0] + last_c[c1]))

        for u in range(HG_UNITS):
            p_sc[base + u] = p_a[u]
            qb_sc[base + u] = qb_a[u]
            kb_sc[base + u] = kb_a[u]
            dec_sc[base + u] = dec_a[u]

    def run(wbuf, rbuf, xn_ref, xn_next_ref, recur, proj):
        hp = HG_SLOT_HEADS
        n_slots = HG_HEADS // hp
        assert (hp, n_slots, N_HG_CHUNKS) == (2, 4, 8), "chunk schedule below is written for this"

        def mid(t, carry):
            heads_a = tuple(hp * t + n for n in range(hp)) if recur else ()
            heads_b = tuple(hp * (t - 1) + n for n in range(hp)) if recur else ()
            late = t == n_slots - 1
            keep0 = jnp.where(late, 4, t - 1)
            keep1 = jnp.where(late, 5, t + 1)
            out0 = N_HG_CHUNKS + 2 * (t - 1)
            chunks = [(keep0, True), (out0, False), (keep1, True), (out0 + 1, False)] if proj else []
            slot(heads_a, rbuf, heads_b, chunks, wbuf, rbuf, xn_ref)
            return carry

        lax.fori_loop(1, n_slots, mid, 0)
        if proj:
            normalise(xnext_ref, xn_next_ref)
        first_out = N_HG_CHUNKS + 2 * (n_slots - 1)
        last_chunks = ([(N_HG_CHUNKS - 2, True), (N_HG_CHUNKS - 1, True)]
                       + [(c, False) for c in range(first_out, N_CHUNKS)])
        slot(tuple(range(hp)) if proj else (), wbuf,
             tuple(HG_HEADS - hp + n for n in range(hp)) if recur else (),
             last_chunks if proj else [], wbuf, rbuf, xn_ref)

    for parity, bufs in ((0, (hb0_ref, hb1_ref, xn0_ref, xn1_ref)), (1, (hb1_ref, hb0_ref, xn1_ref, xn0_ref))):
        on_parity = lax.rem(i, 2) == parity
        if parity == 0:
            pl.when(i == 0)(functools.partial(run, *bufs, recur=False, proj=True))
        if (n_tiles % 2) == parity:
            pl.when(i == n_tiles)(functools.partial(run, *bufs, recur=True, proj=False))
        pl.when(on_parity & (i > 0) & (i < n_tiles))(functools.partial(run, *bufs, recur=True, proj=True))


def _front(x2d, norm_w, w3, lb, hnw, seq):
    m = x2d.shape[0]
    nt = m // FR_TT
    last = nt - 1
    return pl.pallas_call(
        functools.partial(_front_kernel, n_tiles=nt, tiles_per_seq=seq // FR_TT),
        grid=(nt + 1,),
        in_specs=[
            pl.BlockSpec((FR_TT, D_MODEL), lambda i: (0, 0), pipeline_mode=pl.Buffered(1)),
            pl.BlockSpec((FR_TT, D_MODEL), lambda i: (jnp.minimum(i + 1, last), 0)),
            pl.BlockSpec((1, D_MODEL), lambda i: (0, 0)),
            pl.BlockSpec((N_CHUNKS, D_MODEL, NCH), lambda i: (0, 0, 0), pipeline_mode=pl.Buffered(1)),
            pl.BlockSpec((HG_HEADS, 1, HG_DK), lambda i: (0, 0, 0)),
            pl.BlockSpec((HG_HEADS, 1, HG_DV), lambda i: (0, 0, 0)),
        ],
        out_specs=[
            pl.BlockSpec((N_PJ_CHUNKS, FR_TT, NCH), lambda i: (0, jnp.minimum(i, last), 0)),
            pl.BlockSpec((HG_HEADS, FR_TT, HG_DV), lambda i: (0, jnp.maximum(i - 1, 0), 0)),
        ],
        out_shape=[jax.ShapeDtypeStruct((N_PJ_CHUNKS, m, NCH), BF16),
                   jax.ShapeDtypeStruct((HG_HEADS, m, HG_DV), BF16)],
        scratch_shapes=[pltpu.VMEM((FR_TT, D_MODEL), BF16),
                        pltpu.VMEM((FR_TT, D_MODEL), BF16),
                        pltpu.VMEM((4 * HG_HEADS, FR_TT, HG_DK), BF16),
                        pltpu.VMEM((4 * HG_HEADS, FR_TT, HG_DK), BF16),
                        pltpu.VMEM((HG_HEADS, HG_DV, HG_DK), F32),
                        pltpu.VMEM((HG_SLOT_HEADS * HG_UNITS, 2 * HG_C, 2 * HG_C), BF16),
                        pltpu.VMEM((HG_SLOT_HEADS * HG_UNITS, 2 * HG_C, HG_DK), BF16),
                        pltpu.VMEM((HG_SLOT_HEADS * HG_UNITS, 2 * HG_C, HG_DK), BF16),
                        pltpu.VMEM((HG_SLOT_HEADS * HG_UNITS, 1, HG_DK), F32)],
        compiler_params=pltpu.CompilerParams(
            dimension_semantics=("arbitrary",), vmem_limit_bytes=VMEM_LIMIT),
        name="front",
    )(x2d, x2d, norm_w, w3, lb, hnw)


def _back_kernel(sink_ref, q0_ref, q1_ref, kvp_ref, kv_ref, ag0_ref, ag1_ref, gh_ref, mh_ref, ma_ref,
                 x_ref, wbh32_ref, wba32_ref, wo32_ref, fnw_ref, o_ref,
                 kc_ref, vx_ref, ga0_ref, ga1_ref, wbh_ref, wba_ref, wo_ref, acc_ref, mg_ref,
                 *, n_tiles, tiles_per_seq):
    W, DH = WINDOW, ATT_DH
    half_heads = ATT_Q_HEADS // 2
    HALF = D_MODEL // 2
    QUARTER = D_MODEL // 4
    j = pl.program_id(0)

    @pl.when(j == 0)
    def _():
        for hh in range(2):
            wbh_ref[hh] = wbh32_ref[:, hh * HALF:(hh + 1) * HALF].astype(BF16)
            wba_ref[hh] = wba32_ref[:, hh * HALF:(hh + 1) * HALF].astype(BF16)
            wo_ref[hh] = wo32_ref[hh * HALF:(hh + 1) * HALF, :].astype(BF16)
        acc_ref[...] = jnp.zeros_like(acc_ref)

    kc_ref[0:W, :] = kvp_ref[:, :KV_WIDTH]
    kc_ref[W:, :] = kv_ref[:, :KV_WIDTH]
    lane = lax.broadcasted_iota(jnp.int32, (1, 2 * DH), 1)
    low = lane < DH
    for rows, src in ((slice(0, W), kvp_ref), (slice(W, None), kv_ref)):
        for slab in range(ATT_KV_HEADS // 2):
            x = src[:, KV_WIDTH + slab * 2 * DH:KV_WIDTH + (slab + 1) * 2 * DH]
            xr = jnp.concatenate([x[:, DH:], x[:, :DH]], axis=1)
            zero = jnp.zeros_like(x)
            for t, piece in enumerate((jnp.where(low, x, zero), jnp.where(low, zero, xr),
                                       jnp.where(low, xr, zero), jnp.where(low, zero, x))):
                vx_ref[rows, (4 * slab + t) * 2 * DH:(4 * slab + t + 1) * 2 * DH] = piece
    ones_lo = jnp.broadcast_to(jnp.where(low, 1.0, 0.0).astype(BF16), (2 * W, 2 * DH))
    ones_hi = jnp.broadcast_to(jnp.where(low, 0.0, 1.0).astype(BF16), (2 * W, 2 * DH))
    low_rows = jnp.broadcast_to(low, (W, 2 * DH))

    first_tile = lax.rem(jnp.minimum(j, n_tiles - 1), tiles_per_seq) == 0
    qi = lax.broadcasted_iota(jnp.int32, (W, W), 0)
    kj = lax.broadcasted_iota(jnp.int32, (W, W), 1)
    upper = kj > qi

    def head_cols(ref_pair, hd, width):
        ref = ref_pair[hd // half_heads]
        hd = hd % half_heads
        return ref, slice(hd * DH, (hd + width) * DH)

    def half(hh, carry, ga_ref, gap_ref, attend, merge):
        gh = jnp.concatenate([gh_ref[h] for h in range(HG_HEADS)], axis=1) if merge else None
        y_parts = {}

        def branch_piece(name, lhs, w_ref, k):
            y_parts[name, k] = jnp.dot(lhs, w_ref[hh, :, k * QUARTER:(k + 1) * QUARTER],
                                       preferred_element_type=F32)

        def gate():
            yh = jnp.concatenate([y_parts["h", 0], y_parts["h", 1]], axis=1)
            ya = jnp.concatenate([y_parts["a", 0], y_parts["a", 1]], axis=1)
            mg_ref[...] = (_sigmoid(mh_ref[hh].astype(F32)) * yh
                           + _sigmoid(ma_ref[hh].astype(F32)) * ya).astype(BF16)

        def out_piece(k):
            cols = slice(k * QUARTER, (k + 1) * QUARTER)
            acc_ref[:, cols] = jnp.where(hh == 0, 0.0, acc_ref[:, cols]) + jnp.dot(
                mg_ref[...], wo_ref[hh, :, cols], preferred_element_type=F32)

        pieces = [lambda: branch_piece("h", gh, wbh_ref, 0), lambda: branch_piece("h", gh, wbh_ref, 1),
                  lambda: branch_piece("a", gap_ref[...], wba_ref, 0),
                  lambda: branch_piece("a", gap_ref[...], wba_ref, 1),
                  lambda: (gate(), out_piece(0)), lambda: out_piece(1), lambda: out_piece(2),
                  lambda: out_piece(3)]

        units = [(bi, h) for bi in range(2) for h in range(ATT_KV_HEADS)]

        def scores(bi, h):
            r = pl.multiple_of((2 * hh + bi) * W, W)
            kk = kc_ref[pl.ds(r, 2 * W), h * DH:(h + 1) * DH]
            parts = []
            for hd in range(h * ATT_GROUP, (h + 1) * ATT_GROUP):
                ref, cols = head_cols((q0_ref, q1_ref), hd, 1)
                parts.append(ref[pl.ds(r, W), cols])
            return _nt(jnp.concatenate(parts, axis=0), kk)

        s_next = scores(*units[0]) if attend else None
        for idx, (bi, h) in enumerate(units):
            if merge:
                pieces[idx]()
            if attend:
                s = s_next
                if idx + 1 < len(units):
                    s_next = scores(*units[idx + 1])
            if not attend:
                continue
            r = pl.multiple_of((2 * hh + bi) * W, W)
            prev_bias = jnp.where(first_tile & (2 * hh + bi == 0), -jnp.inf, 0.0)
            probs, sink_terms = [], []
            for jj, hd in enumerate(range(h * ATT_GROUP, (h + 1) * ATT_GROUP)):
                sj = s[jj * W:(jj + 1) * W]
                c = jnp.where(upper, sj[:, :W] + prev_bias, sj[:, W:])
                sink = sink_ref[hd]
                m = jnp.maximum(jnp.max(c, axis=-1, keepdims=True), sink)
                p = jnp.exp(c - m)
                sink_terms.append(jnp.exp(sink - m))
                probs.append(jnp.concatenate(
                    [jnp.where(upper, p, 0.0), jnp.where(upper, 0.0, p)], axis=1).astype(BF16))
            w_lo = jnp.concatenate(
                [vx_ref[pl.ds(r, 2 * W), (2 * h) * 2 * DH:(2 * h + 1) * 2 * DH], ones_lo], axis=1)
            w_hi = jnp.concatenate(
                [vx_ref[pl.ds(r, 2 * W), (2 * h + 1) * 2 * DH:(2 * h + 2) * 2 * DH], ones_hi], axis=1)
            res = (jnp.dot(jnp.concatenate(probs[0::2], axis=0), w_lo, preferred_element_type=F32)
                   + jnp.dot(jnp.concatenate(probs[1::2], axis=0), w_hi, preferred_element_type=F32))
            for pair in range(ATT_GROUP // 2):
                rp = res[pair * W:(pair + 1) * W]
                den = rp[:, 2 * DH:] + jnp.where(low_rows, sink_terms[2 * pair], sink_terms[2 * pair + 1])
                j0 = h * ATT_GROUP + 2 * pair
                ag_ref, cols = head_cols((ag0_ref, ag1_ref), j0, 2)
                ag = ag_ref[pl.ds(r, W), cols].astype(F32)
                ga_ref[pl.ds(r, W), j0 * DH:(j0 + 2) * DH] = (
                    rp[:, :2 * DH] / den * (ag * _sigmoid(ag))).astype(BF16)
        return carry

    def run(ga_ref, gap_ref, attend, merge):
        lax.fori_loop(0, 2, functools.partial(half, ga_ref=ga_ref, gap_ref=gap_ref, attend=attend,
                                              merge=merge), 0)
        if merge:
            xo = x_ref[...] + acc_ref[...]
            ms = jnp.mean(xo * xo, axis=-1, keepdims=True)
            o_ref[...] = xo * lax.rsqrt(ms + EPS) * fnw_ref[...]

    for parity, bufs in ((0, (ga0_ref, ga1_ref)), (1, (ga1_ref, ga0_ref))):
        if parity == 0:
            pl.when(j == 0)(functools.partial(run, *bufs, attend=True, merge=False))
        if (n_tiles % 2) == parity:
            pl.when(j == n_tiles)(functools.partial(run, *bufs, attend=False, merge=True))
        pl.when((lax.rem(j, 2) == parity) & (j > 0) & (j < n_tiles))(
            functools.partial(run, *bufs, attend=True, merge=True))


def _back(pj, gh3, x2d, sinks, wbh, wba, wo, fnw, seq):
    m = x2d.shape[0]
    nt = m // BK_TT
    last = nt - 1
    per = BK_TT // WINDOW

    def cur(jj):
        return jnp.minimum(jj, last)

    def prv(jj):
        return jnp.maximum(jj - 1, 0)

    def chunk(c):
        return pl.BlockSpec((None, BK_TT, NCH), lambda jj, s: (c, cur(jj), 0))

    def pair(c):
        return pl.BlockSpec((2, BK_TT, NCH), lambda jj, s: (c // 2, prv(jj), 0))

    prev_kv = pl.BlockSpec((None, WINDOW, NCH),
                           lambda jj, s: (PJ_KV, jnp.maximum(cur(jj) * per - 1, 0), 0))
    tile = pl.BlockSpec((BK_TT, D_MODEL), lambda jj, s: (prv(jj), 0))
    wspec = pl.BlockSpec((D_MODEL, D_MODEL), lambda jj, s: (0, 0), pipeline_mode=pl.Buffered(1))
    half = D_MODEL // 2
    grid_spec = pltpu.PrefetchScalarGridSpec(
        num_scalar_prefetch=1,
        grid=(nt + 1,),
        in_specs=[chunk(PJ_AQ), chunk(PJ_AQ + 1), prev_kv, chunk(PJ_KV), chunk(PJ_AG), chunk(PJ_AG + 1),
                  pl.BlockSpec((HG_HEADS, BK_TT, HG_DV), lambda jj, s: (0, prv(jj), 0)),
                  pair(PJ_MH), pair(PJ_MA), tile, wspec, wspec, wspec,
                  pl.BlockSpec((1, D_MODEL), lambda jj, s: (0, 0))],
        out_specs=tile,
        scratch_shapes=[pltpu.VMEM((BK_TT + WINDOW, KV_WIDTH), BF16),
                        pltpu.VMEM((BK_TT + WINDOW, ATT_KV_HEADS * 4 * ATT_DH), BF16),
                        pltpu.VMEM((BK_TT, D_MODEL), BF16),
                        pltpu.VMEM((BK_TT, D_MODEL), BF16),
                        pltpu.VMEM((2, D_MODEL, half), BF16),
                        pltpu.VMEM((2, D_MODEL, half), BF16),
                        pltpu.VMEM((2, half, D_MODEL), BF16),
                        pltpu.VMEM((BK_TT, D_MODEL), F32),
                        pltpu.VMEM((BK_TT, half), BF16)],
    )
    return pl.pallas_call(
        functools.partial(_back_kernel, n_tiles=nt, tiles_per_seq=seq // BK_TT),
        grid_spec=grid_spec,
        out_shape=jax.ShapeDtypeStruct((m, D_MODEL), F32),
        compiler_params=pltpu.CompilerParams(
            dimension_semantics=("arbitrary",), vmem_limit_bytes=VMEM_LIMIT),
        name="back",
    )(sinks, pj, pj, pj, pj, pj, pj, gh3, pj, pj, x2d, wbh, wba, wo, fnw)


def kernel(x, norm_w, w_in, hgrn_lower_bound, hgrn_norm_w, w_branch_hgrn, attn_sinks,
           w_branch_attn, w_out, final_norm_w):
    batch, seq, _ = x.shape
    depth = norm_w.shape[0]
    assert depth == 1, "the back kernel fuses the final RMSNorm into the single layer"
    assert seq % FR_TT == 0 and seq % BK_TT == 0
    lb_all = jnp.cumsum(jax.nn.softmax(hgrn_lower_bound.astype(F32), axis=0), axis=0)
    w3 = _wprep(w_in[0])
    x2d = x.reshape(batch * seq, D_MODEL)
    pj, gh3 = _front(x2d, norm_w[0].reshape(1, D_MODEL), w3,
                     lb_all[0].reshape(HG_HEADS, 1, HG_DK),
                     hgrn_norm_w[0].reshape(HG_HEADS, 1, HG_DV), seq)
    xo = _back(pj, gh3, x2d, attn_sinks[0].astype(F32), w_branch_hgrn[0], w_branch_attn[0], w_out[0],
               final_norm_w.reshape(1, D_MODEL), seq)
    return xo.reshape(batch, seq, D_MODEL)
```

```python
import functools

import jax
import jax.numpy as jnp
from jax import lax
from jax.experimental import pallas as pl
from jax.experimental.pallas import tpu as pltpu

F32 = jnp.float32
BF16 = jnp.bfloat16

D_MODEL = 1024
HG_HEADS = 8
HG_DK = 128
HG_DV = 128
ATT_Q_HEADS = 16
ATT_KV_HEADS = 4
ATT_GROUP = ATT_Q_HEADS // ATT_KV_HEADS
ATT_DH = 64
KV_WIDTH = ATT_KV_HEADS * ATT_DH
ATT_SCALE = ATT_DH ** -0.5
assert ATT_SCALE == 0.125
WINDOW = 128
EPS = 1e-6
D_IN = 8 * D_MODEL + 2 * KV_WIDTH

NCH = 512
N_CHUNKS = D_IN // NCH
N_HG_CHUNKS = 4 * D_MODEL // NCH
N_PJ_CHUNKS = N_CHUNKS - N_HG_CHUNKS
PJ_AQ, PJ_AG, PJ_MH, PJ_MA, PJ_KV = 0, 2, 4, 6, 8
SLABS_PER_CHUNK = NCH // HG_DK
SEC_HQ, SEC_HF, SEC_HI, SEC_HG = range(4)

V7X_VMEM_BYTES = 64 * 1024 * 1024
VMEM_LIMIT = V7X_VMEM_BYTES * 7 // 8

HG_C = 64
HG_SUB = 16
FR_TT = 512
HG_GROUP = FR_TT // HG_C
HG_UNITS = HG_GROUP // 2
SUBLANES = 8
HG_SLOT_HEADS = 2
BK_TT = 512


def _nt(a, b):
    return lax.dot_general(a.astype(BF16), b.astype(BF16), (((1,), (1,)), ((), ())),
                           preferred_element_type=F32)


def _sigmoid(x):
    return 1.0 / (1.0 + jnp.exp(-x))


WP_ROWS = 256


def _wprep_kernel(w_ref, o_ref):
    kv_src = N_HG_CHUNKS + 2
    for c in range(N_CHUNKS):
        src = c if c < kv_src else (kv_src if c == N_CHUNKS - 1 else c + 1)
        blk = w_ref[:, src * NCH:(src + 1) * NCH]
        if c // 2 == N_HG_CHUNKS // 2:
            blk = blk * ATT_SCALE
        o_ref[c] = blk.astype(BF16)


def _wprep(w):
    return pl.pallas_call(
        _wprep_kernel,
        grid=(D_MODEL // WP_ROWS,),
        in_specs=[pl.BlockSpec((WP_ROWS, D_IN), lambda r: (r, 0))],
        out_specs=pl.BlockSpec((N_CHUNKS, WP_ROWS, NCH), lambda r: (0, r, 0)),
        out_shape=jax.ShapeDtypeStruct((N_CHUNKS, D_MODEL, NCH), BF16),
        compiler_params=pltpu.CompilerParams(
            dimension_semantics=("arbitrary",), vmem_limit_bytes=VMEM_LIMIT),
        name="wprep",
    )(w)


def _front_kernel(x0_ref, xnext_ref, nw_ref, w_ref, lb_ref, hnw_ref, pj_ref, gh_ref, xn0_ref, xn1_ref,
                  hb0_ref, hb1_ref, st_ref, p_sc, qb_sc, kb_sc, dec_sc, *, n_tiles, tiles_per_seq):
    C, SUB, NB = HG_C, HG_SUB, HG_C // HG_SUB
    assert NB == 4, "the three score levels below are written for four sub-blocks per chunk"
    sub_shift = SUB.bit_length() - 1
    HALF = NCH // 2
    i = pl.program_id(0)

    @pl.when(lax.rem(jnp.maximum(i - 1, 0), tiles_per_seq) == 0)
    def _():
        st_ref[...] = jnp.zeros_like(st_ref)

    def normalise(src_ref, dst_ref):
        x = src_ref[...]
        ms = jnp.mean(x * x, axis=-1, keepdims=True)
        dst_ref[...] = (x * lax.rsqrt(ms + EPS) * nw_ref[...]).astype(BF16)

    @pl.when(i == 0)
    def _():
        normalise(x0_ref, xn0_ref)

    row = lax.broadcasted_iota(jnp.int32, (C, C), 0)
    col = lax.broadcasted_iota(jnp.int32, (C, C), 1)
    rb, cb = row >> sub_shift, col >> sub_shift
    m0 = (rb == cb) & (col <= row)
    m1 = ((rb & 1) == 1) & (cb == rb - 1)
    m2 = (row >= 2 * SUB) & (col < 2 * SUB)
    rows = [slice(c * C, (c + 1) * C) for c in range(HG_GROUP)]
    urows = [slice(u * 2 * C, (u + 1) * 2 * C) for u in range(HG_UNITS)]
    sub_row = lax.broadcasted_iota(jnp.int32, (SUBLANES, HG_DK), 0)

    def cat(blocks):
        return jnp.concatenate(blocks, axis=0).astype(BF16)

    def cumsum_rows(g):
        groups = []
        for v in range(C // SUBLANES):
            xg = g[v * SUBLANES:(v + 1) * SUBLANES]
            shift = 1
            while shift < SUBLANES:
                xg = xg + jnp.where(sub_row >= shift, pltpu.roll(xg, shift, axis=0), 0.0)
                shift *= 2
            groups.append(xg)
        out, carry = [groups[0]], groups[0][SUBLANES - 1:SUBLANES]
        for xg in groups[1:]:
            out.append(xg + carry)
            carry = carry + xg[SUBLANES - 1:SUBLANES]
        return jnp.concatenate(out, axis=0)

    def slot(heads_a, abuf, heads_b, chunks, wbuf, rbuf, xn_ref):
        pieces = [(chunk, half, keep) for chunk, keep in chunks for half in range(2)]

        def project(flush=False):
            if not pieces:
                return
            chunk, half, keep = pieces.pop(0)
            acc = jnp.dot(xn_ref[...], w_ref[chunk, :, half * HALF:(half + 1) * HALF],
                          preferred_element_type=F32).astype(BF16)
            if not keep:
                pj_ref[chunk - N_HG_CHUNKS, :, half * HALF:(half + 1) * HALF] = acc
            else:
                for s in range(SLABS_PER_CHUNK // 2):
                    slab = chunk * SLABS_PER_CHUNK + half * (SLABS_PER_CHUNK // 2) + s
                    wbuf[slab] = acc[:, s * HG_DK:(s + 1) * HG_DK]
            if flush:
                project(True)

        def src(sec, head, buf=rbuf):
            return buf.at[sec * HG_HEADS + head]

        loaded = []
        for n, head in enumerate(heads_b):
            base = n * HG_UNITS
            loaded.append(dict(
                v=[src(SEC_HI, head)[urows[u], :] for u in range(HG_UNITS)],
                p=[p_sc[base + u] for u in range(HG_UNITS)],
                qb=[qb_sc[base + u] for u in range(HG_UNITS)],
                kb=[kb_sc[base + u] for u in range(HG_UNITS)],
                dec=[dec_sc[base + u] for u in range(HG_UNITS)]))

        project()

        gates = []
        for head in heads_a:
            lb = lb_ref[head]
            q, k, b = [], [], []
            for c in range(HG_GROUP):
                xq = src(SEC_HQ, head, abuf)[rows[c], :].astype(F32)
                xf = src(SEC_HF, head, abuf)[rows[c], :].astype(F32)
                f = lb + (1.0 - lb) * _sigmoid(xf)
                k.append(1.0 - f)
                q.append(xq * _sigmoid(xq))
                b.append(cumsum_rows(jnp.log(f)))
            gates.append((q, k, b))

        for ld in loaded:
            ld["o_intra"] = [jnp.dot(ld["p"][u], ld["v"][u], preferred_element_type=F32)
                             for u in range(HG_UNITS)]
            ld["kv"] = [lax.dot_general(ld["v"][u], ld["kb"][u], (((0,), (0,)), ((), ())),
                                        preferred_element_type=F32) for u in range(HG_UNITS)]
            project()

        for head, ld in zip(heads_b, loaded):
            st = st_ref[head]
            ld["o"] = []
            for u in range(HG_UNITS):
                ld["o"].append(ld["o_intra"][u] + _nt(ld["qb"][u], st))
                st = st * ld["dec"][u] + ld["kv"][u]
            st_ref[head] = st
            project()

        for n, (q, k, b) in enumerate(gates):
            unit_handover(n * HG_UNITS, *chunk_scores(q, k, b))
            project()

        project(flush=True)

        for head, ld in zip(heads_b, loaded):
            nw = hnw_ref[head]
            for u in range(HG_UNITS):
                xg = src(SEC_HG, head)[urows[u], :].astype(F32)
                o = ld["o"][u]
                ms_o = jnp.mean(o * o, axis=-1, keepdims=True)
                on = o * lax.rsqrt(ms_o + EPS) * nw
                gh_ref[head, urows[u], :] = (on * (xg * _sigmoid(xg))).astype(BF16)

    def chunk_scores(q, k, b):
        p_c, qb_c, kb_c, last_c = [], [], [], []
        for c in range(HG_GROUP):
            ends = [b[c][(j + 1) * SUB - 1:(j + 1) * SUB] for j in range(NB)]
            starts = [jnp.zeros_like(ends[0])] + ends[:-1]
            last = ends[-1]
            q1, k1, qd, kd, qb, kb = [], [], [], [], [], []
            for j in range(NB):
                sl = slice(j * SUB, (j + 1) * SUB)
                bj = b[c][sl]
                q1j = q[c][sl] * jnp.exp(bj - starts[j])
                k1j = k[c][sl] * jnp.exp(ends[j] - bj)
                half = jnp.exp(0.5 * (starts[j] - ends[j]))
                q1.append(q1j)
                k1.append(k1j)
                qd.append(q1j * half)
                kd.append(k1j * half)
                qb.append(q1j * jnp.exp(starts[j]))
                kb.append(k1j * jnp.exp(last - ends[j]))
            q2 = q1[:3] + [q1[3] * jnp.exp(ends[2] - ends[1])]
            k2 = [k1[0] * jnp.exp(ends[1] - ends[0])] + k1[1:]
            s0 = _nt(cat(qd), cat(kd))
            s1 = _nt(cat(q1), cat(k1))
            s2 = _nt(cat(q2), cat(k2))
            p_c.append(jnp.where(m0, s0, jnp.where(m1, s1, jnp.where(m2, s2, 0.0))).astype(BF16))
            qb_c.append(jnp.concatenate(qb, axis=0))
            kb_c.append(jnp.concatenate(kb, axis=0))
            last_c.append(last)
        return p_c, qb_c, kb_c, last_c

    def unit_handover(base, p_c, qb_c, kb_c, last_c):
        p_a, qb_a, kb_a, dec_a = [], [], [], []
        for u in range(HG_UNITS):
            c0, c1 = 2 * u, 2 * u + 1
            cross = _nt(qb_c[c1], kb_c[c0]).astype(BF16)
            p_a.append(jnp.concatenate(
                [jnp.concatenate([p_c[c0], jnp.zeros((C, C), BF16)], axis=1),
                 jnp.concatenate([cross, p_c[c1]], axis=1)], axis=0))
            qb_a.append(cat([qb_c[c0], qb_c[c1] * jnp.exp(last_c[c0])]))
            kb_a.append(cat([kb_c[c0] * jnp.exp(last_c[c1]), kb_c[c1]]))
            dec_a.append(jnp.exp(last_c[c0] + last_c[c1]))

        for u in range(HG_UNITS):
            p_sc[base + u] = p_a[u]
            qb_sc[base + u] = qb_a[u]
            kb_sc[base + u] = kb_a[u]
            dec_sc[base + u] = dec_a[u]

    def run(wbuf, rbuf, xn_ref, xn_next_ref, recur, proj):
        hp = HG_SLOT_HEADS
        n_slots = HG_HEADS // hp
        assert (hp, n_slots, N_HG_CHUNKS) == (2, 4, 8), "chunk schedule below is written for this"

        def mid(t, carry):
            heads_a = tuple(hp * t + n for n in range(hp)) if recur else ()
            heads_b = tuple(hp * (t - 1) + n for n in range(hp)) if recur else ()
            late = t == n_slots - 1
            keep0 = jnp.where(late, 4, t - 1)
            keep1 = jnp.where(late, 5, t + 1)
            out0 = N_HG_CHUNKS + 2 * (t - 1)
            chunks = [(keep0, True), (out0, False), (keep1, True), (out0 + 1, False)] if proj else []
            slot(heads_a, rbuf, heads_b, chunks, wbuf, rbuf, xn_ref)
            return carry

        lax.fori_loop(1, n_slots, mid, 0)
        if proj:
            normalise(xnext_ref, xn_next_ref)
        first_out = N_HG_CHUNKS + 2 * (n_slots - 1)
        last_chunks = ([(N_HG_CHUNKS - 2, True), (N_HG_CHUNKS - 1, True)]
                       + [(c, False) for c in range(first_out, N_CHUNKS)])
        slot(tuple(range(hp)) if proj else (), wbuf,
             tuple(HG_HEADS - hp + n for n in range(hp)) if recur else (),
             last_chunks if proj else [], wbuf, rbuf, xn_ref)

    for parity, bufs in ((0, (hb0_ref, hb1_ref, xn0_ref, xn1_ref)), (1, (hb1_ref, hb0_ref, xn1_ref, xn0_ref))):
        on_parity = lax.rem(i, 2) == parity
        if parity == 0:
            pl.when(i == 0)(functools.partial(run, *bufs, recur=False, proj=True))
        if (n_tiles % 2) == parity:
            pl.when(i == n_tiles)(functools.partial(run, *bufs, recur=True, proj=False))
        pl.when(on_parity & (i > 0) & (i < n_tiles))(functools.partial(run, *bufs, recur=True, proj=True))


def _front(x2d, norm_w, w3, lb, hnw, seq):
    m = x2d.shape[0]
    nt = m // FR_TT
    last = nt - 1
    return pl.pallas_call(
        functools.partial(_front_kernel, n_tiles=nt, tiles_per_seq=seq // FR_TT),
        grid=(nt + 1,),
        in_specs=[
            pl.BlockSpec((FR_TT, D_MODEL), lambda i: (0, 0), pipeline_mode=pl.Buffered(1)),
            pl.BlockSpec((FR_TT, D_MODEL), lambda i: (jnp.minimum(i + 1, last), 0)),
            pl.BlockSpec((1, D_MODEL), lambda i: (0, 0)),
            pl.BlockSpec((N_CHUNKS, D_MODEL, NCH), lambda i: (0, 0, 0), pipeline_mode=pl.Buffered(1)),
            pl.BlockSpec((HG_HEADS, 1, HG_DK), lambda i: (0, 0, 0)),
            pl.BlockSpec((HG_HEADS, 1, HG_DV), lambda i: (0, 0, 0)),
        ],
        out_specs=[
            pl.BlockSpec((N_PJ_CHUNKS, FR_TT, NCH), lambda i: (0, jnp.minimum(i, last), 0)),
            pl.BlockSpec((HG_HEADS, FR_TT, HG_DV), lambda i: (0, jnp.maximum(i - 1, 0), 0)),
        ],
        out_shape=[jax.ShapeDtypeStruct((N_PJ_CHUNKS, m, NCH), BF16),
                   jax.ShapeDtypeStruct((HG_HEADS, m, HG_DV), BF16)],
        scratch_shapes=[pltpu.VMEM((FR_TT, D_MODEL), BF16),
                        pltpu.VMEM((FR_TT, D_MODEL), BF16),
                        pltpu.VMEM((4 * HG_HEADS, FR_TT, HG_DK), BF16),
                        pltpu.VMEM((4 * HG_HEADS, FR_TT, HG_DK), BF16),
                        pltpu.VMEM((HG_HEADS, HG_DV, HG_DK), F32),
                        pltpu.VMEM((HG_SLOT_HEADS * HG_UNITS, 2 * HG_C, 2 * HG_C), BF16),
                        pltpu.VMEM((HG_SLOT_HEADS * HG_UNITS, 2 * HG_C, HG_DK), BF16),
                        pltpu.VMEM((HG_SLOT_HEADS * HG_UNITS, 2 * HG_C, HG_DK), BF16),
                        pltpu.VMEM((HG_SLOT_HEADS * HG_UNITS, 1, HG_DK), F32)],
        compiler_params=pltpu.CompilerParams(
            dimension_semantics=("arbitrary",), vmem_limit_bytes=VMEM_LIMIT),
        name="front",
    )(x2d, x2d, norm_w, w3, lb, hnw)


def _back_kernel(sink_ref, q0_ref, q1_ref, kvp_ref, kv_ref, ag0_ref, ag1_ref, gh_ref, mh_ref, ma_ref,
                 x_ref, wbh32_ref, wba32_ref, wo32_ref, fnw_ref, o_ref,
                 kvc_ref, ga0_ref, ga1_ref, wbh_ref, wba_ref, wo_ref, acc_ref, mg_ref,
                 *, n_tiles, tiles_per_seq):
    W, DH = WINDOW, ATT_DH
    half_heads = ATT_Q_HEADS // 2
    HALF = D_MODEL // 2
    QUARTER = D_MODEL // 4
    j = pl.program_id(0)

    @pl.when(j == 0)
    def _():
        for hh in range(2):
            wbh_ref[hh] = wbh32_ref[:, hh * HALF:(hh + 1) * HALF].astype(BF16)
            wba_ref[hh] = wba32_ref[:, hh * HALF:(hh + 1) * HALF].astype(BF16)
            wo_ref[hh] = wo32_ref[hh * HALF:(hh + 1) * HALF, :].astype(BF16)
        acc_ref[...] = jnp.zeros_like(acc_ref)

    kvc_ref[0:W, :] = kvp_ref[...]
    kvc_ref[W:, :] = kv_ref[...]
    lane = lax.broadcasted_iota(jnp.int32, (1, 2 * DH), 1)
    low = lane < DH
    ones_lo = jnp.broadcast_to(jnp.where(low, 1.0, 0.0).astype(BF16), (2 * W, 2 * DH))
    ones_hi = jnp.broadcast_to(jnp.where(low, 0.0, 1.0).astype(BF16), (2 * W, 2 * DH))
    low_rows = jnp.broadcast_to(low, (W, 2 * DH))

    first_tile = lax.rem(jnp.minimum(j, n_tiles - 1), tiles_per_seq) == 0
    qi = lax.broadcasted_iota(jnp.int32, (W, W), 0)
    kj = lax.broadcasted_iota(jnp.int32, (W, W), 1)
    upper = kj > qi

    def head_cols(ref_pair, hd, width):
        ref = ref_pair[hd // half_heads]
        hd = hd % half_heads
        return ref, slice(hd * DH, (hd + width) * DH)

    def half(hh, carry, ga_ref, gap_ref, attend, merge):
        gh = jnp.concatenate([gh_ref[h] for h in range(HG_HEADS)], axis=1) if merge else None
        y_parts = {}

        def branch_piece(name, lhs, w_ref, k):
            y_parts[name, k] = jnp.dot(lhs, w_ref[hh, :, k * QUARTER:(k + 1) * QUARTER],
                                       preferred_element_type=F32)

        def gate():
            yh = jnp.concatenate([y_parts["h", 0], y_parts["h", 1]], axis=1)
            ya = jnp.concatenate([y_parts["a", 0], y_parts["a", 1]], axis=1)
            mg_ref[...] = (_sigmoid(mh_ref[hh].astype(F32)) * yh
                           + _sigmoid(ma_ref[hh].astype(F32)) * ya).astype(BF16)

        def out_piece(k):
            cols = slice(k * QUARTER, (k + 1) * QUARTER)
            acc_ref[:, cols] = jnp.where(hh == 0, 0.0, acc_ref[:, cols]) + jnp.dot(
                mg_ref[...], wo_ref[hh, :, cols], preferred_element_type=F32)

        pieces = [lambda: branch_piece("h", gh, wbh_ref, 0), lambda: branch_piece("h", gh, wbh_ref, 1),
                  lambda: branch_piece("a", gap_ref[...], wba_ref, 0),
                  lambda: branch_piece("a", gap_ref[...], wba_ref, 1),
                  lambda: (gate(), out_piece(0)), lambda: out_piece(1), lambda: out_piece(2),
                  lambda: out_piece(3)]

        units = [(bi, h) for bi in range(2) for h in range(ATT_KV_HEADS)]

        def scores(bi, h):
            r = pl.multiple_of((2 * hh + bi) * W, W)
            kk = kvc_ref[pl.ds(r, 2 * W), h * DH:(h + 1) * DH]
            parts = []
            for hd in range(h * ATT_GROUP, (h + 1) * ATT_GROUP):
                ref, cols = head_cols((q0_ref, q1_ref), hd, 1)
                parts.append(ref[pl.ds(r, W), cols])
            return _nt(jnp.concatenate(parts, axis=0), kk)

        s_next = scores(*units[0]) if attend else None
        for idx, (bi, h) in enumerate(units):
            if merge:
                pieces[idx]()
            if attend:
                s = s_next
                if idx + 1 < len(units):
                    s_next = scores(*units[idx + 1])
            if not attend:
                continue
            r = pl.multiple_of((2 * hh + bi) * W, W)
            prev_bias = jnp.where(first_tile & (2 * hh + bi == 0), -jnp.inf, 0.0)
            probs, sink_terms = [], []
            for jj, hd in enumerate(range(h * ATT_GROUP, (h + 1) * ATT_GROUP)):
                sj = s[jj * W:(jj + 1) * W]
                c = jnp.where(upper, sj[:, :W] + prev_bias, sj[:, W:])
                sink = sink_ref[hd]
                m = jnp.maximum(jnp.max(c, axis=-1, keepdims=True), sink)
                p = jnp.exp(c - m)
                sink_terms.append(jnp.exp(sink - m))
                probs.append(jnp.concatenate(
                    [jnp.where(upper, p, 0.0), jnp.where(upper, 0.0, p)], axis=1).astype(BF16))
            vpair = kvc_ref[pl.ds(r, 2 * W), KV_WIDTH + (h // 2) * 2 * DH:
                            KV_WIDTH + (h // 2 + 1) * 2 * DH]
            vswap = jnp.concatenate([vpair[:, DH:], vpair[:, :DH]], axis=1)
            in_low, in_high = (vpair, vswap) if h % 2 == 0 else (vswap, vpair)
            zero = jnp.zeros_like(vpair)
            w_lo = jnp.concatenate([jnp.where(low, in_low, zero), ones_lo], axis=1)
            w_hi = jnp.concatenate([jnp.where(low, zero, in_high), ones_hi], axis=1)
            res = (jnp.dot(jnp.concatenate(probs[0::2], axis=0), w_lo, preferred_element_type=F32)
                   + jnp.dot(jnp.concatenate(probs[1::2], axis=0), w_hi, preferred_element_type=F32))
            for pair in range(ATT_GROUP // 2):
                rp = res[pair * W:(pair + 1) * W]
                den = rp[:, 2 * DH:] + jnp.where(low_rows, sink_terms[2 * pair], sink_terms[2 * pair + 1])
                j0 = h * ATT_GROUP + 2 * pair
                ag_ref, cols = head_cols((ag0_ref, ag1_ref), j0, 2)
                ag = ag_ref[pl.ds(r, W), cols].astype(F32)
                ga_ref[pl.ds(r, W), j0 * DH:(j0 + 2) * DH] = (
                    rp[:, :2 * DH] / den * (ag * _sigmoid(ag))).astype(BF16)
        return carry

    def run(ga_ref, gap_ref, attend, merge):
        lax.fori_loop(0, 2, functools.partial(half, ga_ref=ga_ref, gap_ref=gap_ref, attend=attend,
                                              merge=merge), 0)
        if merge:
            xo = x_ref[...] + acc_ref[...]
            ms = jnp.mean(xo * xo, axis=-1, keepdims=True)
            o_ref[...] = xo * lax.rsqrt(ms + EPS) * fnw_ref[...]

    for parity, bufs in ((0, (ga0_ref, ga1_ref)), (1, (ga1_ref, ga0_ref))):
        if parity == 0:
            pl.when(j == 0)(functools.partial(run, *bufs, attend=True, merge=False))
        if (n_tiles % 2) == parity:
            pl.when(j == n_tiles)(functools.partial(run, *bufs, attend=False, merge=True))
        pl.when((lax.rem(j, 2) == parity) & (j > 0) & (j < n_tiles))(
            functools.partial(run, *bufs, attend=True, merge=True))


def _back(pj, gh3, x2d, sinks, wbh, wba, wo, fnw, seq):
    m = x2d.shape[0]
    nt = m // BK_TT
    last = nt - 1
    per = BK_TT // WINDOW

    def cur(jj):
        return jnp.minimum(jj, last)

    def prv(jj):
        return jnp.maximum(jj - 1, 0)

    def chunk(c):
        return pl.BlockSpec((None, BK_TT, NCH), lambda jj, s: (c, cur(jj), 0))

    def pair(c):
        return pl.BlockSpec((2, BK_TT, NCH), lambda jj, s: (c // 2, prv(jj), 0))

    prev_kv = pl.BlockSpec((None, WINDOW, NCH),
                           lambda jj, s: (PJ_KV, jnp.maximum(cur(jj) * per - 1, 0), 0))
    tile = pl.BlockSpec((BK_TT, D_MODEL), lambda jj, s: (prv(jj), 0))
    wspec = pl.BlockSpec((D_MODEL, D_MODEL), lambda jj, s: (0, 0), pipeline_mode=pl.Buffered(1))
    half = D_MODEL // 2
    grid_spec = pltpu.PrefetchScalarGridSpec(
        num_scalar_prefetch=1,
        grid=(nt + 1,),
        in_specs=[chunk(PJ_AQ), chunk(PJ_AQ + 1), prev_kv, chunk(PJ_KV), chunk(PJ_AG), chunk(PJ_AG + 1),
                  pl.BlockSpec((HG_HEADS, BK_TT, HG_DV), lambda jj, s: (0, prv(jj), 0)),
                  pair(PJ_MH), pair(PJ_MA), tile, wspec, wspec, wspec,
                  pl.BlockSpec((1, D_MODEL), lambda jj, s: (0, 0))],
        out_specs=tile,
        scratch_shapes=[pltpu.VMEM((BK_TT + WINDOW, NCH), BF16),
                        pltpu.VMEM((BK_TT, D_MODEL), BF16),
                        pltpu.VMEM((BK_TT, D_MODEL), BF16),
                        pltpu.VMEM((2, D_MODEL, half), BF16),
                        pltpu.VMEM((2, D_MODEL, half), BF16),
                        pltpu.VMEM((2, half, D_MODEL), BF16),
                        pltpu.VMEM((BK_TT, D_MODEL), F32),
                        pltpu.VMEM((BK_TT, half), BF16)],
    )
    return pl.pallas_call(
        functools.partial(_back_kernel, n_tiles=nt, tiles_per_seq=seq // BK_TT),
        grid_spec=grid_spec,
        out_shape=jax.ShapeDtypeStruct((m, D_MODEL), F32),
        compiler_params=pltpu.CompilerParams(
            dimension_semantics=("arbitrary",), vmem_limit_bytes=VMEM_LIMIT),
        name="back",
    )(sinks, pj, pj, pj, pj, pj, pj, gh3, pj, pj, x2d, wbh, wba, wo, fnw)


def kernel(x, norm_w, w_in, hgrn_lower_bound, hgrn_norm_w, w_branch_hgrn, attn_sinks,
           w_branch_attn, w_out, final_norm_w):
    batch, seq, _ = x.shape
    depth = norm_w.shape[0]
    assert depth == 1, "the back kernel fuses the final RMSNorm into the single layer"
    assert seq % FR_TT == 0 and seq % BK_TT == 0
    lb_all = jnp.cumsum(jax.nn.softmax(hgrn_lower_bound.astype(F32), axis=0), axis=0)
    w3 = _wprep(w_in[0])
    x2d = x.reshape(batch * seq, D_MODEL)
    pj, gh3 = _front(x2d, norm_w[0].reshape(1, D_MODEL), w3,
                     lb_all[0].reshape(HG_HEADS, 1, HG_DK),
                     hgrn_norm_w[0].reshape(HG_HEADS, 1, HG_DV), seq)
    xo = _back(pj, gh3, x2d, attn_sinks[0].astype(F32), w_branch_hgrn[0], w_branch_attn[0], w_out[0],
               final_norm_w.reshape(1, D_MODEL), seq)
    return xo.reshape(batch, seq, D_MODEL)
```

```python
import functools

import jax
import jax.numpy as jnp
from jax import lax
from jax.experimental import pallas as pl
from jax.experimental.pallas import tpu as pltpu

F32 = jnp.float32
BF16 = jnp.bfloat16

D_MODEL = 1024
HG_HEADS = 8
HG_DK = 128
HG_DV = 128
ATT_Q_HEADS = 16
ATT_KV_HEADS = 4
ATT_GROUP = ATT_Q_HEADS // ATT_KV_HEADS
ATT_DH = 64
KV_WIDTH = ATT_KV_HEADS * ATT_DH
ATT_SCALE = ATT_DH ** -0.5
assert ATT_SCALE == 0.125
WINDOW = 128
EPS = 1e-6
D_IN = 8 * D_MODEL + 2 * KV_WIDTH

NCH = 512
N_CHUNKS = D_IN // NCH
N_HG_CHUNKS = 4 * D_MODEL // NCH
N_PJ_CHUNKS = N_CHUNKS - N_HG_CHUNKS
PJ_AQ, PJ_AG, PJ_MH, PJ_MA, PJ_KV = 0, 2, 4, 6, 8
SLABS_PER_CHUNK = NCH // HG_DK
SEC_HQ, SEC_HF, SEC_HI, SEC_HG = range(4)

V7X_VMEM_BYTES = 64 * 1024 * 1024
VMEM_LIMIT = V7X_VMEM_BYTES * 7 // 8

HG_C = 64
HG_SUB = 16
HG_LOG_DECAY_FLOOR = -10.0
assert HG_SUB // 2 * -HG_LOG_DECAY_FLOOR <= 80.0
FR_TT = 512
HG_GROUP = FR_TT // HG_C
HG_UNITS = HG_GROUP // 2
SUBLANES = 8
HG_SLOT_HEADS = 2
BK_TT = 512


def _nt(a, b):
    return lax.dot_general(a.astype(BF16), b.astype(BF16), (((1,), (1,)), ((), ())),
                           preferred_element_type=F32)


def _sigmoid(x):
    return 1.0 / (1.0 + jnp.exp(-x))


WP_ROWS = 256


def _wprep_kernel(w_ref, o_ref):
    kv_src = N_HG_CHUNKS + 2
    for c in range(N_CHUNKS):
        src = c if c < kv_src else (kv_src if c == N_CHUNKS - 1 else c + 1)
        blk = w_ref[:, src * NCH:(src + 1) * NCH]
        if c // 2 == N_HG_CHUNKS // 2:
            blk = blk * ATT_SCALE
        o_ref[c] = blk.astype(BF16)


def _wprep(w):
    return pl.pallas_call(
        _wprep_kernel,
        grid=(D_MODEL // WP_ROWS,),
        in_specs=[pl.BlockSpec((WP_ROWS, D_IN), lambda r: (r, 0))],
        out_specs=pl.BlockSpec((N_CHUNKS, WP_ROWS, NCH), lambda r: (0, r, 0)),
        out_shape=jax.ShapeDtypeStruct((N_CHUNKS, D_MODEL, NCH), BF16),
        compiler_params=pltpu.CompilerParams(
            dimension_semantics=("arbitrary",), vmem_limit_bytes=VMEM_LIMIT),
        name="wprep",
    )(w)


def _front_kernel(x0_ref, xnext_ref, nw_ref, w_ref, lb_ref, hnw_ref, pj_ref, gh_ref, xn0_ref, xn1_ref,
                  hb0_ref, hb1_ref, st_ref, p_sc, qb_sc, kb_sc, dec_sc, *, n_tiles, tiles_per_seq):
    C, SUB, NB = HG_C, HG_SUB, HG_C // HG_SUB
    assert NB == 4, "the three score levels below are written for four sub-blocks per chunk"
    sub_shift = SUB.bit_length() - 1
    HALF = NCH // 2
    i = pl.program_id(0)

    @pl.when(lax.rem(jnp.maximum(i - 1, 0), tiles_per_seq) == 0)
    def _():
        st_ref[...] = jnp.zeros_like(st_ref)

    def normalise(src_ref, dst_ref):
        x = src_ref[...]
        ms = jnp.mean(x * x, axis=-1, keepdims=True)
        dst_ref[...] = (x * lax.rsqrt(ms + EPS) * nw_ref[...]).astype(BF16)

    @pl.when(i == 0)
    def _():
        normalise(x0_ref, xn0_ref)

    row = lax.broadcasted_iota(jnp.int32, (C, C), 0)
    col = lax.broadcasted_iota(jnp.int32, (C, C), 1)
    rb, cb = row >> sub_shift, col >> sub_shift
    m0 = (rb == cb) & (col <= row)
    m1 = ((rb & 1) == 1) & (cb == rb - 1)
    m2 = (row >= 2 * SUB) & (col < 2 * SUB)
    rows = [slice(c * C, (c + 1) * C) for c in range(HG_GROUP)]
    urows = [slice(u * 2 * C, (u + 1) * 2 * C) for u in range(HG_UNITS)]
    sub_row = lax.broadcasted_iota(jnp.int32, (SUBLANES, HG_DK), 0)

    def cat(blocks):
        return jnp.concatenate(blocks, axis=0).astype(BF16)

    def cumsum_rows(g):
        groups = []
        for v in range(C // SUBLANES):
            xg = g[v * SUBLANES:(v + 1) * SUBLANES]
            shift = 1
            while shift < SUBLANES:
                xg = xg + jnp.where(sub_row >= shift, pltpu.roll(xg, shift, axis=0), 0.0)
                shift *= 2
            groups.append(xg)
        out, carry = [groups[0]], groups[0][SUBLANES - 1:SUBLANES]
        for xg in groups[1:]:
            out.append(xg + carry)
            carry = carry + xg[SUBLANES - 1:SUBLANES]
        return jnp.concatenate(out, axis=0)

    def slot(heads_a, abuf, heads_b, chunks, wbuf, rbuf, xn_ref):
        pieces = [(chunk, half, keep) for chunk, keep in chunks for half in range(2)]

        def project(flush=False):
            if not pieces:
                return
            chunk, half, keep = pieces.pop(0)
            acc = jnp.dot(xn_ref[...], w_ref[chunk, :, half * HALF:(half + 1) * HALF],
                          preferred_element_type=F32).astype(BF16)
            if not keep:
                pj_ref[chunk - N_HG_CHUNKS, :, half * HALF:(half + 1) * HALF] = acc
            else:
                for s in range(SLABS_PER_CHUNK // 2):
                    slab = chunk * SLABS_PER_CHUNK + half * (SLABS_PER_CHUNK // 2) + s
                    wbuf[slab] = acc[:, s * HG_DK:(s + 1) * HG_DK]
            if flush:
                project(True)

        def src(sec, head, buf=rbuf):
            return buf.at[sec * HG_HEADS + head]

        loaded = []
        for n, head in enumerate(heads_b):
            base = n * HG_UNITS
            loaded.append(dict(
                v=[src(SEC_HI, head)[urows[u], :] for u in range(HG_UNITS)],
                p=[p_sc[base + u] for u in range(HG_UNITS)],
                qb=[qb_sc[base + u] for u in range(HG_UNITS)],
                kb=[kb_sc[base + u] for u in range(HG_UNITS)],
                dec=[dec_sc[base + u] for u in range(HG_UNITS)]))

        project()

        gates = []
        for head in heads_a:
            lb = lb_ref[head]
            q, k, b = [], [], []
            for c in range(HG_GROUP):
                xq = src(SEC_HQ, head, abuf)[rows[c], :].astype(F32)
                xf = src(SEC_HF, head, abuf)[rows[c], :].astype(F32)
                f = lb + (1.0 - lb) * _sigmoid(xf)
                k.append(1.0 - f)
                q.append(xq * _sigmoid(xq))
                b.append(cumsum_rows(jnp.maximum(jnp.log(f), HG_LOG_DECAY_FLOOR)))
            gates.append((q, k, b))

        for ld in loaded:
            ld["o_intra"] = [jnp.dot(ld["p"][u], ld["v"][u], preferred_element_type=F32)
                             for u in range(HG_UNITS)]
            ld["kv"] = [lax.dot_general(ld["v"][u], ld["kb"][u], (((0,), (0,)), ((), ())),
                                        preferred_element_type=F32) for u in range(HG_UNITS)]
            project()

        for head, ld in zip(heads_b, loaded):
            st = st_ref[head]
            ld["o"] = []
            for u in range(HG_UNITS):
                ld["o"].append(ld["o_intra"][u] + _nt(ld["qb"][u], st))
                st = st * ld["dec"][u] + ld["kv"][u]
            st_ref[head] = st
            project()

        for n, (q, k, b) in enumerate(gates):
            unit_handover(n * HG_UNITS, *chunk_scores(q, k, b))
            project()

        project(flush=True)

        for head, ld in zip(heads_b, loaded):
            nw = hnw_ref[head]
            for u in range(HG_UNITS):
                xg = src(SEC_HG, head)[urows[u], :].astype(F32)
                o = ld["o"][u]
                ms_o = jnp.mean(o * o, axis=-1, keepdims=True)
                on = o * lax.rsqrt(ms_o + EPS) * nw
                gh_ref[head, urows[u], :] = (on * (xg * _sigmoid(xg))).astype(BF16)

    def chunk_scores(q, k, b):
        p_c, qb_c, kb_c, last_c = [], [], [], []
        for c in range(HG_GROUP):
            ends = [b[c][(j + 1) * SUB - 1:(j + 1) * SUB] for j in range(NB)]
            starts = [jnp.zeros_like(ends[0])] + ends[:-1]
            last = ends[-1]
            q1, k1, qd, kd, qb, kb = [], [], [], [], [], []
            for j in range(NB):
                sl = slice(j * SUB, (j + 1) * SUB)
                bj = b[c][sl]
                q1j = q[c][sl] * jnp.exp(bj - starts[j])
                k1j = k[c][sl] * jnp.exp(ends[j] - bj)
                mid = 0.5 * (starts[j] + ends[j])
                q1.append(q1j)
                k1.append(k1j)
                qd.append(q[c][sl] * jnp.exp(bj - mid))
                kd.append(k[c][sl] * jnp.exp(mid - bj))
                qb.append(q1j * jnp.exp(starts[j]))
                kb.append(k1j * jnp.exp(last - ends[j]))
            q2 = q1[:3] + [q1[3] * jnp.exp(ends[2] - ends[1])]
            k2 = [k1[0] * jnp.exp(ends[1] - ends[0])] + k1[1:]
            s0 = _nt(cat(qd), cat(kd))
            s1 = _nt(cat(q1), cat(k1))
            s2 = _nt(cat(q2), cat(k2))
            p_c.append(jnp.where(m0, s0, jnp.where(m1, s1, jnp.where(m2, s2, 0.0))).astype(BF16))
            qb_c.append(jnp.concatenate(qb, axis=0))
            kb_c.append(jnp.concatenate(kb, axis=0))
            last_c.append(last)
        return p_c, qb_c, kb_c, last_c

    def unit_handover(base, p_c, qb_c, kb_c, last_c):
        p_a, qb_a, kb_a, dec_a = [], [], [], []
        for u in range(HG_UNITS):
            c0, c1 = 2 * u, 2 * u + 1
            cross = _nt(qb_c[c1], kb_c[c0]).astype(BF16)
            p_a.append(jnp.concatenate(
                [jnp.concatenate([p_c[c0], jnp.zeros((C, C), BF16)], axis=1),
                 jnp.concatenate([cross, p_c[c1]], axis=1)], axis=0))
            qb_a.append(cat([qb_c[c0], qb_c[c1] * jnp.exp(last_c[c0])]))
            kb_a.append(cat([kb_c[c0] * jnp.exp(last_c[c1]), kb_c[c1]]))
            dec_a.append(jnp.exp(last_c[c0] + last_c[c1]))

        for u in range(HG_UNITS):
            p_sc[base + u] = p_a[u]
            qb_sc[base + u] = qb_a[u]
            kb_sc[base + u] = kb_a[u]
            dec_sc[base + u] = dec_a[u]

    def run(wbuf, rbuf, xn_ref, xn_next_ref, recur, proj):
        hp = HG_SLOT_HEADS
        n_slots = HG_HEADS // hp
        assert (hp, n_slots, N_HG_CHUNKS) == (2, 4, 8), "chunk schedule below is written for this"

        def mid(t, carry):
            heads_a = tuple(hp * t + n for n in range(hp)) if recur else ()
            heads_b = tuple(hp * (t - 1) + n for n in range(hp)) if recur else ()
            late = t == n_slots - 1
            keep0 = jnp.where(late, 4, t - 1)
            keep1 = jnp.where(late, 5, t + 1)
            out0 = N_HG_CHUNKS + 2 * (t - 1)
            chunks = [(keep0, True), (out0, False), (keep1, True), (out0 + 1, False)] if proj else []
            slot(heads_a, rbuf, heads_b, chunks, wbuf, rbuf, xn_ref)
            return carry

        lax.fori_loop(1, n_slots, mid, 0)
        if proj:
            normalise(xnext_ref, xn_next_ref)
        first_out = N_HG_CHUNKS + 2 * (n_slots - 1)
        last_chunks = ([(N_HG_CHUNKS - 2, True), (N_HG_CHUNKS - 1, True)]
                       + [(c, False) for c in range(first_out, N_CHUNKS)])
        slot(tuple(range(hp)) if proj else (), wbuf,
             tuple(HG_HEADS - hp + n for n in range(hp)) if recur else (),
             last_chunks if proj else [], wbuf, rbuf, xn_ref)

    for parity, bufs in ((0, (hb0_ref, hb1_ref, xn0_ref, xn1_ref)), (1, (hb1_ref, hb0_ref, xn1_ref, xn0_ref))):
        on_parity = lax.rem(i, 2) == parity
        if parity == 0:
            pl.when(i == 0)(functools.partial(run, *bufs, recur=False, proj=True))
        if (n_tiles % 2) == parity:
            pl.when(i == n_tiles)(functools.partial(run, *bufs, recur=True, proj=False))
        pl.when(on_parity & (i > 0) & (i < n_tiles))(functools.partial(run, *bufs, recur=True, proj=True))


def _front(x2d, norm_w, w3, lb, hnw, seq):
    m = x2d.shape[0]
    nt = m // FR_TT
    last = nt - 1
    return pl.pallas_call(
        functools.partial(_front_kernel, n_tiles=nt, tiles_per_seq=seq // FR_TT),
        grid=(nt + 1,),
        in_specs=[
            pl.BlockSpec((FR_TT, D_MODEL), lambda i: (0, 0), pipeline_mode=pl.Buffered(1)),
            pl.BlockSpec((FR_TT, D_MODEL), lambda i: (jnp.minimum(i + 1, last), 0)),
            pl.BlockSpec((1, D_MODEL), lambda i: (0, 0)),
            pl.BlockSpec((N_CHUNKS, D_MODEL, NCH), lambda i: (0, 0, 0), pipeline_mode=pl.Buffered(1)),
            pl.BlockSpec((HG_HEADS, 1, HG_DK), lambda i: (0, 0, 0)),
            pl.BlockSpec((HG_HEADS, 1, HG_DV), lambda i: (0, 0, 0)),
        ],
        out_specs=[
            pl.BlockSpec((N_PJ_CHUNKS, FR_TT, NCH), lambda i: (0, jnp.minimum(i, last), 0)),
            pl.BlockSpec((HG_HEADS, FR_TT, HG_DV), lambda i: (0, jnp.maximum(i - 1, 0), 0)),
        ],
        out_shape=[jax.ShapeDtypeStruct((N_PJ_CHUNKS, m, NCH), BF16),
                   jax.ShapeDtypeStruct((HG_HEADS, m, HG_DV), BF16)],
        scratch_shapes=[pltpu.VMEM((FR_TT, D_MODEL), BF16),
                        pltpu.VMEM((FR_TT, D_MODEL), BF16),
                        pltpu.VMEM((4 * HG_HEADS, FR_TT, HG_DK), BF16),
                        pltpu.VMEM((4 * HG_HEADS, FR_TT, HG_DK), BF16),
                        pltpu.VMEM((HG_HEADS, HG_DV, HG_DK), F32),
                        pltpu.VMEM((HG_SLOT_HEADS * HG_UNITS, 2 * HG_C, 2 * HG_C), BF16),
                        pltpu.VMEM((HG_SLOT_HEADS * HG_UNITS, 2 * HG_C, HG_DK), BF16),
                        pltpu.VMEM((HG_SLOT_HEADS * HG_UNITS, 2 * HG_C, HG_DK), BF16),
                        pltpu.VMEM((HG_SLOT_HEADS * HG_UNITS, 1, HG_DK), F32)],
        compiler_params=pltpu.CompilerParams(
            dimension_semantics=("arbitrary",), vmem_limit_bytes=VMEM_LIMIT),
        name="front",
    )(x2d, x2d, norm_w, w3, lb, hnw)


def _back_kernel(sink_ref, q0_ref, q1_ref, kvp_ref, kv_ref, ag0_ref, ag1_ref, gh_ref, mh_ref, ma_ref,
                 x_ref, wbh32_ref, wba32_ref, wo32_ref, fnw_ref, o_ref,
                 kvc_ref, ga0_ref, ga1_ref, wbh_ref, wba_ref, wo_ref, acc_ref, mg_ref,
                 *, n_tiles, tiles_per_seq):
    W, DH = WINDOW, ATT_DH
    half_heads = ATT_Q_HEADS // 2
    HALF = D_MODEL // 2
    QUARTER = D_MODEL // 4
    j = pl.program_id(0)

    @pl.when(j == 0)
    def _():
        for hh in range(2):
            wbh_ref[hh] = wbh32_ref[:, hh * HALF:(hh + 1) * HALF].astype(BF16)
            wba_ref[hh] = wba32_ref[:, hh * HALF:(hh + 1) * HALF].astype(BF16)
            wo_ref[hh] = wo32_ref[hh * HALF:(hh + 1) * HALF, :].astype(BF16)
        acc_ref[...] = jnp.zeros_like(acc_ref)

    kvc_ref[0:W, :] = kvp_ref[...]
    kvc_ref[W:, :] = kv_ref[...]
    lane = lax.broadcasted_iota(jnp.int32, (1, 2 * DH), 1)
    low = lane < DH
    ones_lo = jnp.broadcast_to(jnp.where(low, 1.0, 0.0).astype(BF16), (2 * W, 2 * DH))
    ones_hi = jnp.broadcast_to(jnp.where(low, 0.0, 1.0).astype(BF16), (2 * W, 2 * DH))
    low_rows = jnp.broadcast_to(low, (W, 2 * DH))

    first_tile = lax.rem(jnp.minimum(j, n_tiles - 1), tiles_per_seq) == 0
    qi = lax.broadcasted_iota(jnp.int32, (W, W), 0)
    kj = lax.broadcasted_iota(jnp.int32, (W, W), 1)
    upper = kj > qi

    def head_cols(ref_pair, hd, width):
        ref = ref_pair[hd // half_heads]
        hd = hd % half_heads
        return ref, slice(hd * DH, (hd + width) * DH)

    def half(hh, carry, ga_ref, gap_ref, attend, merge):
        gh = jnp.concatenate([gh_ref[h] for h in range(HG_HEADS)], axis=1) if merge else None
        y_parts = {}

        def branch_piece(name, lhs, w_ref, k):
            y_parts[name, k] = jnp.dot(lhs, w_ref[hh, :, k * QUARTER:(k + 1) * QUARTER],
                                       preferred_element_type=F32)

        def gate():
            yh = jnp.concatenate([y_parts["h", 0], y_parts["h", 1]], axis=1)
            ya = jnp.concatenate([y_parts["a", 0], y_parts["a", 1]], axis=1)
            mg_ref[...] = (_sigmoid(mh_ref[hh].astype(F32)) * yh
                           + _sigmoid(ma_ref[hh].astype(F32)) * ya).astype(BF16)

        def out_piece(k):
            cols = slice(k * QUARTER, (k + 1) * QUARTER)
            acc_ref[:, cols] = jnp.where(hh == 0, 0.0, acc_ref[:, cols]) + jnp.dot(
                mg_ref[...], wo_ref[hh, :, cols], preferred_element_type=F32)

        pieces = [lambda: branch_piece("h", gh, wbh_ref, 0), lambda: branch_piece("h", gh, wbh_ref, 1),
                  lambda: branch_piece("a", gap_ref[...], wba_ref, 0),
                  lambda: branch_piece("a", gap_ref[...], wba_ref, 1),
                  lambda: (gate(), out_piece(0)), lambda: out_piece(1), lambda: out_piece(2),
                  lambda: out_piece(3)]

        units = [(bi, h) for bi in range(2) for h in range(ATT_KV_HEADS)]

        def scores(bi, h):
            r = pl.multiple_of((2 * hh + bi) * W, W)
            kk = kvc_ref[pl.ds(r, 2 * W), h * DH:(h + 1) * DH]
            parts = []
            for hd in range(h * ATT_GROUP, (h + 1) * ATT_GROUP):
                ref, cols = head_cols((q0_ref, q1_ref), hd, 1)
                parts.append(ref[pl.ds(r, W), cols])
            return _nt(jnp.concatenate(parts, axis=0), kk)

        s_next = scores(*units[0]) if attend else None
        for idx, (bi, h) in enumerate(units):
            if merge:
                pieces[idx]()
            if attend:
                s = s_next
                if idx + 1 < len(units):
                    s_next = scores(*units[idx + 1])
            if not attend:
                continue
            r = pl.multiple_of((2 * hh + bi) * W, W)
            prev_bias = jnp.where(first_tile & (2 * hh + bi == 0), -jnp.inf, 0.0)
            probs, sink_terms = [], []
            for jj, hd in enumerate(range(h * ATT_GROUP, (h + 1) * ATT_GROUP)):
                sj = s[jj * W:(jj + 1) * W]
                c = jnp.where(upper, sj[:, :W] + prev_bias, sj[:, W:])
                sink = sink_ref[hd]
                m = jnp.maximum(jnp.max(c, axis=-1, keepdims=True), sink)
                p = jnp.exp(c - m)
                sink_terms.append(jnp.exp(sink - m))
                probs.append(jnp.concatenate(
                    [jnp.where(upper, p, 0.0), jnp.where(upper, 0.0, p)], axis=1).astype(BF16))
            vpair = kvc_ref[pl.ds(r, 2 * W), KV_WIDTH + (h // 2) * 2 * DH:
                            KV_WIDTH + (h // 2 + 1) * 2 * DH]
            vswap = jnp.concatenate([vpair[:, DH:], vpair[:, :DH]], axis=1)
            in_low, in_high = (vpair, vswap) if h % 2 == 0 else (vswap, vpair)
            zero = jnp.zeros_like(vpair)
            w_lo = jnp.concatenate([jnp.where(low, in_low, zero), ones_lo], axis=1)
            w_hi = jnp.concatenate([jnp.where(low, zero, in_high), ones_hi], axis=1)
            res = (jnp.dot(jnp.concatenate(probs[0::2], axis=0), w_lo, preferred_element_type=F32)
                   + jnp.dot(jnp.concatenate(probs[1::2], axis=0), w_hi, preferred_element_type=F32))
            for pair in range(ATT_GROUP // 2):
                rp = res[pair * W:(pair + 1) * W]
                den = rp[:, 2 * DH:] + jnp.where(low_rows, sink_terms[2 * pair], sink_terms[2 * pair + 1])
                j0 = h * ATT_GROUP + 2 * pair
                ag_ref, cols = head_cols((ag0_ref, ag1_ref), j0, 2)
                ag = ag_ref[pl.ds(r, W), cols].astype(F32)
                ga_ref[pl.ds(r, W), j0 * DH:(j0 + 2) * DH] = (
                    rp[:, :2 * DH] / den * (ag * _sigmoid(ag))).astype(BF16)
        return carry

    def run(ga_ref, gap_ref, attend, merge):
        lax.fori_loop(0, 2, functools.partial(half, ga_ref=ga_ref, gap_ref=gap_ref, attend=attend,
                                              merge=merge), 0)
        if merge:
            xo = x_ref[...] + acc_ref[...]
            ms = jnp.mean(xo * xo, axis=-1, keepdims=True)
            o_ref[...] = xo * lax.rsqrt(ms + EPS) * fnw_ref[...]

    for parity, bufs in ((0, (ga0_ref, ga1_ref)), (1, (ga1_ref, ga0_ref))):
        if parity == 0:
            pl.when(j == 0)(functools.partial(run, *bufs, attend=True, merge=False))
        if (n_tiles % 2) == parity:
            pl.when(j == n_tiles)(functools.partial(run, *bufs, attend=False, merge=True))
        pl.when((lax.rem(j, 2) == parity) & (j > 0) & (j < n_tiles))(
            functools.partial(run, *bufs, attend=True, merge=True))


def _back(pj, gh3, x2d, sinks, wbh, wba, wo, fnw, seq):
    m = x2d.shape[0]
    nt = m // BK_TT
    last = nt - 1
    per = BK_TT // WINDOW

    def cur(jj):
        return jnp.minimum(jj, last)

    def prv(jj):
        return jnp.maximum(jj - 1, 0)

    def chunk(c):
        return pl.BlockSpec((None, BK_TT, NCH), lambda jj, s: (c, cur(jj), 0))

    def pair(c):
        return pl.BlockSpec((2, BK_TT, NCH), lambda jj, s: (c // 2, prv(jj), 0))

    prev_kv = pl.BlockSpec((None, WINDOW, NCH),
                           lambda jj, s: (PJ_KV, jnp.maximum(cur(jj) * per - 1, 0), 0))
    tile = pl.BlockSpec((BK_TT, D_MODEL), lambda jj, s: (prv(jj), 0))
    wspec = pl.BlockSpec((D_MODEL, D_MODEL), lambda jj, s: (0, 0), pipeline_mode=pl.Buffered(1))
    half = D_MODEL // 2
    grid_spec = pltpu.PrefetchScalarGridSpec(
        num_scalar_prefetch=1,
        grid=(nt + 1,),
        in_specs=[chunk(PJ_AQ), chunk(PJ_AQ + 1), prev_kv, chunk(PJ_KV), chunk(PJ_AG), chunk(PJ_AG + 1),
                  pl.BlockSpec((HG_HEADS, BK_TT, HG_DV), lambda jj, s: (0, prv(jj), 0)),
                  pair(PJ_MH), pair(PJ_MA), tile, wspec, wspec, wspec,
                  pl.BlockSpec((1, D_MODEL), lambda jj, s: (0, 0))],
        out_specs=tile,
        scratch_shapes=[pltpu.VMEM((BK_TT + WINDOW, NCH), BF16),
                        pltpu.VMEM((BK_TT, D_MODEL), BF16),
                        pltpu.VMEM((BK_TT, D_MODEL), BF16),
                        pltpu.VMEM((2, D_MODEL, half), BF16),
                        pltpu.VMEM((2, D_MODEL, half), BF16),
                        pltpu.VMEM((2, half, D_MODEL), BF16),
                        pltpu.VMEM((BK_TT, D_MODEL), F32),
                        pltpu.VMEM((BK_TT, half), BF16)],
    )
    return pl.pallas_call(
        functools.partial(_back_kernel, n_tiles=nt, tiles_per_seq=seq // BK_TT),
        grid_spec=grid_spec,
        out_shape=jax.ShapeDtypeStruct((m, D_MODEL), F32),
        compiler_params=pltpu.CompilerParams(
            dimension_semantics=("arbitrary",), vmem_limit_bytes=VMEM_LIMIT),
        name="back",
    )(sinks, pj, pj, pj, pj, pj, pj, gh3, pj, pj, x2d, wbh, wba, wo, fnw)


def kernel(x, norm_w, w_in, hgrn_lower_bound, hgrn_norm_w, w_branch_hgrn, attn_sinks,
           w_branch_attn, w_out, final_norm_w):
    batch, seq, _ = x.shape
    depth = norm_w.shape[0]
    assert depth == 1, "the back kernel fuses the final RMSNorm into the single layer"
    assert seq % FR_TT == 0 and seq % BK_TT == 0
    lb_all = jnp.cumsum(jax.nn.softmax(hgrn_lower_bound.astype(F32), axis=0), axis=0)
    w3 = _wprep(w_in[0])
    x2d = x.reshape(batch * seq, D_MODEL)
    pj, gh3 = _front(x2d, norm_w[0].reshape(1, D_MODEL), w3,
                     lb_all[0].reshape(HG_HEADS, 1, HG_DK),
                     hgrn_norm_w[0].reshape(HG_HEADS, 1, HG_DV), seq)
    xo = _back(pj, gh3, x2d, attn_sinks[0].astype(F32), w_branch_hgrn[0], w_branch_attn[0], w_out[0],
               final_norm_w.reshape(1, D_MODEL), seq)
    return xo.reshape(batch, seq, D_MODEL)
```

```python
import functools

import jax
import jax.numpy as jnp
from jax import lax
from jax.experimental import pallas as pl
from jax.experimental.pallas import tpu as pltpu

F32 = jnp.float32
BF16 = jnp.bfloat16

D_MODEL = 1024
HG_HEADS = 8
HG_DK = 128
HG_DV = 128
ATT_Q_HEADS = 16
ATT_KV_HEADS = 4
ATT_GROUP = ATT_Q_HEADS // ATT_KV_HEADS
ATT_DH = 64
KV_WIDTH = ATT_KV_HEADS * ATT_DH
ATT_SCALE = ATT_DH ** -0.5
assert ATT_SCALE == 0.125
WINDOW = 128
EPS = 1e-6
D_IN = 8 * D_MODEL + 2 * KV_WIDTH

NCH = 512
N_CHUNKS = D_IN // NCH
N_HG_CHUNKS = 4 * D_MODEL // NCH
N_PJ_CHUNKS = N_CHUNKS - N_HG_CHUNKS
PJ_AQ, PJ_AG, PJ_MH, PJ_MA, PJ_KV = 0, 2, 4, 6, 8
SLABS_PER_CHUNK = NCH // HG_DK
SEC_HQ, SEC_HF, SEC_HI, SEC_HG = range(4)

V7X_VMEM_BYTES = 64 * 1024 * 1024
VMEM_LIMIT = V7X_VMEM_BYTES * 7 // 8

HG_C = 64
HG_SUB = 16
HG_LOG_DECAY_FLOOR = -10.0
assert HG_SUB // 2 * -HG_LOG_DECAY_FLOOR <= 80.0
FR_TT = 512
HG_GROUP = FR_TT // HG_C
HG_UNITS = HG_GROUP // 2
SUBLANES = 8
HG_SLOT_HEADS = 2
FIRST_STEP_CHUNK_ORDER = (0, 8, 2, 9, 1, 10, 3, 11, 4, 12, 5, 13, 6, 7, 14, 15, 16)
assert sorted(FIRST_STEP_CHUNK_ORDER) == list(range(N_CHUNKS))
BK_TT = 512


def _nt(a, b):
    return lax.dot_general(a.astype(BF16), b.astype(BF16), (((1,), (1,)), ((), ())),
                           preferred_element_type=F32)


def _sigmoid(x):
    return 1.0 / (1.0 + jnp.exp(-x))


WP_ROWS = 256


def _wprep_kernel(w_ref, o_ref):
    kv_src = N_HG_CHUNKS + 2
    for c in range(N_CHUNKS):
        src = c if c < kv_src else (kv_src if c == N_CHUNKS - 1 else c + 1)
        blk = w_ref[:, src * NCH:(src + 1) * NCH]
        if c // 2 == N_HG_CHUNKS // 2:
            blk = blk * ATT_SCALE
        o_ref[c] = blk.astype(BF16)


def _wprep(w):
    return pl.pallas_call(
        _wprep_kernel,
        grid=(D_MODEL // WP_ROWS,),
        in_specs=[pl.BlockSpec((WP_ROWS, D_IN), lambda r: (r, 0))],
        out_specs=pl.BlockSpec((N_CHUNKS, WP_ROWS, NCH), lambda r: (0, r, 0)),
        out_shape=jax.ShapeDtypeStruct((N_CHUNKS, D_MODEL, NCH), BF16),
        compiler_params=pltpu.CompilerParams(
            dimension_semantics=("arbitrary",), vmem_limit_bytes=VMEM_LIMIT),
        name="wprep",
    )(w)


def _front_kernel(x0_ref, xnext_ref, nw_ref, w_hbm, lb_ref, hnw_ref, pj_ref, gh_ref, xn0_ref, xn1_ref,
                  hb0_ref, hb1_ref, st_ref, p_sc, qb_sc, kb_sc, dec_sc, w_ref, w_sem,
                  *, n_tiles, tiles_per_seq):
    C, SUB, NB = HG_C, HG_SUB, HG_C // HG_SUB
    assert NB == 4, "the three score levels below are written for four sub-blocks per chunk"
    sub_shift = SUB.bit_length() - 1
    HALF = NCH // 2
    i = pl.program_id(0)

    @pl.when(lax.rem(jnp.maximum(i - 1, 0), tiles_per_seq) == 0)
    def _():
        st_ref[...] = jnp.zeros_like(st_ref)

    def normalise(src_ref, dst_ref):
        x = src_ref[...]
        ms = jnp.mean(x * x, axis=-1, keepdims=True)
        dst_ref[...] = (x * lax.rsqrt(ms + EPS) * nw_ref[...]).astype(BF16)

    def weight_copy(chunk):
        return pltpu.make_async_copy(w_hbm.at[chunk], w_ref.at[chunk], w_sem.at[chunk])

    @pl.when(i == 0)
    def _():
        for chunk in FIRST_STEP_CHUNK_ORDER:
            weight_copy(chunk).start()
        normalise(x0_ref, xn0_ref)

    row = lax.broadcasted_iota(jnp.int32, (C, C), 0)
    col = lax.broadcasted_iota(jnp.int32, (C, C), 1)
    rb, cb = row >> sub_shift, col >> sub_shift
    m0 = (rb == cb) & (col <= row)
    m1 = ((rb & 1) == 1) & (cb == rb - 1)
    m2 = (row >= 2 * SUB) & (col < 2 * SUB)
    rows = [slice(c * C, (c + 1) * C) for c in range(HG_GROUP)]
    urows = [slice(u * 2 * C, (u + 1) * 2 * C) for u in range(HG_UNITS)]
    sub_row = lax.broadcasted_iota(jnp.int32, (SUBLANES, HG_DK), 0)

    def cat(blocks):
        return jnp.concatenate(blocks, axis=0).astype(BF16)

    def cumsum_rows(g):
        groups = []
        for v in range(C // SUBLANES):
            xg = g[v * SUBLANES:(v + 1) * SUBLANES]
            shift = 1
            while shift < SUBLANES:
                xg = xg + jnp.where(sub_row >= shift, pltpu.roll(xg, shift, axis=0), 0.0)
                shift *= 2
            groups.append(xg)
        out, carry = [groups[0]], groups[0][SUBLANES - 1:SUBLANES]
        for xg in groups[1:]:
            out.append(xg + carry)
            carry = carry + xg[SUBLANES - 1:SUBLANES]
        return jnp.concatenate(out, axis=0)

    def slot(heads_a, abuf, heads_b, chunks, wbuf, rbuf, xn_ref, wait_weights):
        pieces = [(chunk, half, keep) for chunk, keep in chunks for half in range(2)]

        def project(flush=False):
            if not pieces:
                return
            chunk, half, keep = pieces.pop(0)
            if wait_weights and half == 0:
                weight_copy(chunk).wait()
            acc = jnp.dot(xn_ref[...], w_ref[chunk, :, half * HALF:(half + 1) * HALF],
                          preferred_element_type=F32).astype(BF16)
            if not keep:
                pj_ref[chunk - N_HG_CHUNKS, :, half * HALF:(half + 1) * HALF] = acc
            else:
                for s in range(SLABS_PER_CHUNK // 2):
                    slab = chunk * SLABS_PER_CHUNK + half * (SLABS_PER_CHUNK // 2) + s
                    wbuf[slab] = acc[:, s * HG_DK:(s + 1) * HG_DK]
            if flush:
                project(True)

        def src(sec, head, buf=rbuf):
            return buf.at[sec * HG_HEADS + head]

        loaded = []
        for n, head in enumerate(heads_b):
            base = n * HG_UNITS
            loaded.append(dict(
                v=[src(SEC_HI, head)[urows[u], :] for u in range(HG_UNITS)],
                p=[p_sc[base + u] for u in range(HG_UNITS)],
                qb=[qb_sc[base + u] for u in range(HG_UNITS)],
                kb=[kb_sc[base + u] for u in range(HG_UNITS)],
                dec=[dec_sc[base + u] for u in range(HG_UNITS)]))

        project()

        gates = []
        for head in heads_a:
            lb = lb_ref[head]
            q, k, b = [], [], []
            for c in range(HG_GROUP):
                xq = src(SEC_HQ, head, abuf)[rows[c], :].astype(F32)
                xf = src(SEC_HF, head, abuf)[rows[c], :].astype(F32)
                f = lb + (1.0 - lb) * _sigmoid(xf)
                k.append(1.0 - f)
                q.append(xq * _sigmoid(xq))
                b.append(cumsum_rows(jnp.maximum(jnp.log(f), HG_LOG_DECAY_FLOOR)))
            gates.append((q, k, b))

        for ld in loaded:
            ld["o_intra"] = [jnp.dot(ld["p"][u], ld["v"][u], preferred_element_type=F32)
                             for u in range(HG_UNITS)]
            ld["kv"] = [lax.dot_general(ld["v"][u], ld["kb"][u], (((0,), (0,)), ((), ())),
                                        preferred_element_type=F32) for u in range(HG_UNITS)]
            project()

        for head, ld in zip(heads_b, loaded):
            st = st_ref[head]
            ld["o"] = []
            for u in range(HG_UNITS):
                ld["o"].append(ld["o_intra"][u] + _nt(ld["qb"][u], st))
                st = st * ld["dec"][u] + ld["kv"][u]
            st_ref[head] = st
            project()

        for n, (q, k, b) in enumerate(gates):
            unit_handover(n * HG_UNITS, *chunk_scores(q, k, b))
            project()

        project(flush=True)

        for head, ld in zip(heads_b, loaded):
            nw = hnw_ref[head]
            for u in range(HG_UNITS):
                xg = src(SEC_HG, head)[urows[u], :].astype(F32)
                o = ld["o"][u]
                ms_o = jnp.mean(o * o, axis=-1, keepdims=True)
                on = o * lax.rsqrt(ms_o + EPS) * nw
                gh_ref[head, urows[u], :] = (on * (xg * _sigmoid(xg))).astype(BF16)

    def chunk_scores(q, k, b):
        p_c, qb_c, kb_c, last_c = [], [], [], []
        for c in range(HG_GROUP):
            ends = [b[c][(j + 1) * SUB - 1:(j + 1) * SUB] for j in range(NB)]
            starts = [jnp.zeros_like(ends[0])] + ends[:-1]
            last = ends[-1]
            q1, k1, qd, kd, qb, kb = [], [], [], [], [], []
            for j in range(NB):
                sl = slice(j * SUB, (j + 1) * SUB)
                bj = b[c][sl]
                q1j = q[c][sl] * jnp.exp(bj - starts[j])
                k1j = k[c][sl] * jnp.exp(ends[j] - bj)
                mid = 0.5 * (starts[j] + ends[j])
                q1.append(q1j)
                k1.append(k1j)
                qd.append(q[c][sl] * jnp.exp(bj - mid))
                kd.append(k[c][sl] * jnp.exp(mid - bj))
                qb.append(q1j * jnp.exp(starts[j]))
                kb.append(k1j * jnp.exp(last - ends[j]))
            q2 = q1[:3] + [q1[3] * jnp.exp(ends[2] - ends[1])]
            k2 = [k1[0] * jnp.exp(ends[1] - ends[0])] + k1[1:]
            s0 = _nt(cat(qd), cat(kd))
            s1 = _nt(cat(q1), cat(k1))
            s2 = _nt(cat(q2), cat(k2))
            p_c.append(jnp.where(m0, s0, jnp.where(m1, s1, jnp.where(m2, s2, 0.0))).astype(BF16))
            qb_c.append(jnp.concatenate(qb, axis=0))
            kb_c.append(jnp.concatenate(kb, axis=0))
            last_c.append(last)
        return p_c, qb_c, kb_c, last_c

    def unit_handover(base, p_c, qb_c, kb_c, last_c):
        p_a, qb_a, kb_a, dec_a = [], [], [], []
        for u in range(HG_UNITS):
            c0, c1 = 2 * u, 2 * u + 1
            cross = _nt(qb_c[c1], kb_c[c0]).astype(BF16)
            p_a.append(jnp.concatenate(
                [jnp.concatenate([p_c[c0], jnp.zeros((C, C), BF16)], axis=1),
                 jnp.concatenate([cross, p_c[c1]], axis=1)], axis=0))
            qb_a.append(cat([qb_c[c0], qb_c[c1] * jnp.exp(last_c[c0])]))
            kb_a.append(cat([kb_c[c0] * jnp.exp(last_c[c1]), kb_c[c1]]))
            dec_a.append(jnp.exp(last_c[c0] + last_c[c1]))

        for u in range(HG_UNITS):
            p_sc[base + u] = p_a[u]
            qb_sc[base + u] = qb_a[u]
            kb_sc[base + u] = kb_a[u]
            dec_sc[base + u] = dec_a[u]

    def run(wbuf, rbuf, xn_ref, xn_next_ref, recur, proj):
        hp = HG_SLOT_HEADS
        n_slots = HG_HEADS // hp
        assert (hp, n_slots, N_HG_CHUNKS) == (2, 4, 8), "chunk schedule below is written for this"

        def mid(t, carry):
            heads_a = tuple(hp * t + n for n in range(hp)) if recur else ()
            heads_b = tuple(hp * (t - 1) + n for n in range(hp)) if recur else ()
            late = t == n_slots - 1
            keep0 = jnp.where(late, 4, t - 1)
            keep1 = jnp.where(late, 5, t + 1)
            out0 = N_HG_CHUNKS + 2 * (t - 1)
            chunks = [(keep0, True), (out0, False), (keep1, True), (out0 + 1, False)] if proj else []
            slot(heads_a, rbuf, heads_b, chunks, wbuf, rbuf, xn_ref, wait_weights=not recur)
            return carry

        lax.fori_loop(1, n_slots, mid, 0)
        if proj:
            normalise(xnext_ref, xn_next_ref)
        first_out = N_HG_CHUNKS + 2 * (n_slots - 1)
        last_chunks = ([(N_HG_CHUNKS - 2, True), (N_HG_CHUNKS - 1, True)]
                       + [(c, False) for c in range(first_out, N_CHUNKS)])
        slot(tuple(range(hp)) if proj else (), wbuf,
             tuple(HG_HEADS - hp + n for n in range(hp)) if recur else (),
             last_chunks if proj else [], wbuf, rbuf, xn_ref, wait_weights=not recur)

    for parity, bufs in ((0, (hb0_ref, hb1_ref, xn0_ref, xn1_ref)), (1, (hb1_ref, hb0_ref, xn1_ref, xn0_ref))):
        on_parity = lax.rem(i, 2) == parity
        if parity == 0:
            pl.when(i == 0)(functools.partial(run, *bufs, recur=False, proj=True))
        if (n_tiles % 2) == parity:
            pl.when(i == n_tiles)(functools.partial(run, *bufs, recur=True, proj=False))
        pl.when(on_parity & (i > 0) & (i < n_tiles))(functools.partial(run, *bufs, recur=True, proj=True))


def _front(x2d, norm_w, w3, lb, hnw, seq):
    m = x2d.shape[0]
    nt = m // FR_TT
    last = nt - 1
    return pl.pallas_call(
        functools.partial(_front_kernel, n_tiles=nt, tiles_per_seq=seq // FR_TT),
        grid=(nt + 1,),
        in_specs=[
            pl.BlockSpec((FR_TT, D_MODEL), lambda i: (0, 0), pipeline_mode=pl.Buffered(1)),
            pl.BlockSpec((FR_TT, D_MODEL), lambda i: (jnp.minimum(i + 1, last), 0)),
            pl.BlockSpec((1, D_MODEL), lambda i: (0, 0)),
            pl.BlockSpec(memory_space=pl.ANY),
            pl.BlockSpec((HG_HEADS, 1, HG_DK), lambda i: (0, 0, 0)),
            pl.BlockSpec((HG_HEADS, 1, HG_DV), lambda i: (0, 0, 0)),
        ],
        out_specs=[
            pl.BlockSpec((N_PJ_CHUNKS, FR_TT, NCH), lambda i: (0, jnp.minimum(i, last), 0)),
            pl.BlockSpec((HG_HEADS, FR_TT, HG_DV), lambda i: (0, jnp.maximum(i - 1, 0), 0)),
        ],
        out_shape=[jax.ShapeDtypeStruct((N_PJ_CHUNKS, m, NCH), BF16),
                   jax.ShapeDtypeStruct((HG_HEADS, m, HG_DV), BF16)],
        scratch_shapes=[pltpu.VMEM((FR_TT, D_MODEL), BF16),
                        pltpu.VMEM((FR_TT, D_MODEL), BF16),
                        pltpu.VMEM((4 * HG_HEADS, FR_TT, HG_DK), BF16),
                        pltpu.VMEM((4 * HG_HEADS, FR_TT, HG_DK), BF16),
                        pltpu.VMEM((HG_HEADS, HG_DV, HG_DK), F32),
                        pltpu.VMEM((HG_SLOT_HEADS * HG_UNITS, 2 * HG_C, 2 * HG_C), BF16),
                        pltpu.VMEM((HG_SLOT_HEADS * HG_UNITS, 2 * HG_C, HG_DK), BF16),
                        pltpu.VMEM((HG_SLOT_HEADS * HG_UNITS, 2 * HG_C, HG_DK), BF16),
                        pltpu.VMEM((HG_SLOT_HEADS * HG_UNITS, 1, HG_DK), F32),
                        pltpu.VMEM((N_CHUNKS, D_MODEL, NCH), BF16),
                        pltpu.SemaphoreType.DMA((N_CHUNKS,))],
        compiler_params=pltpu.CompilerParams(
            dimension_semantics=("arbitrary",), vmem_limit_bytes=VMEM_LIMIT),
        name="front",
    )(x2d, x2d, norm_w, w3, lb, hnw)


def _back_kernel(sink_ref, q0_ref, q1_ref, kvp_ref, kv_ref, ag0_ref, ag1_ref, gh_ref, mh_ref, ma_ref,
                 x_ref, wbh_hbm, wba_hbm, wo_hbm, fnw_ref, o_ref,
                 kvc_ref, ga0_ref, ga1_ref, wbh_ref, wba_ref, wo_ref, acc_ref, mg_ref,
                 wbh32_ref, wba32_ref, wo32_ref, w_sem,
                 *, n_tiles, tiles_per_seq):
    W, DH = WINDOW, ATT_DH
    half_heads = ATT_Q_HEADS // 2
    HALF = D_MODEL // 2
    QUARTER = D_MODEL // 4
    j = pl.program_id(0)

    weight_copies = [pltpu.make_async_copy(src, dst, w_sem.at[n]) for n, (src, dst) in enumerate(
        ((wbh_hbm, wbh32_ref), (wba_hbm, wba32_ref), (wo_hbm, wo32_ref)))]

    @pl.when(j == 0)
    def _():
        for copy in weight_copies:
            copy.start()
        acc_ref[...] = jnp.zeros_like(acc_ref)

    kvc_ref[0:W, :] = kvp_ref[...]
    kvc_ref[W:, :] = kv_ref[...]
    lane = lax.broadcasted_iota(jnp.int32, (1, 2 * DH), 1)
    low = lane < DH
    ones_lo = jnp.broadcast_to(jnp.where(low, 1.0, 0.0).astype(BF16), (2 * W, 2 * DH))
    ones_hi = jnp.broadcast_to(jnp.where(low, 0.0, 1.0).astype(BF16), (2 * W, 2 * DH))
    low_rows = jnp.broadcast_to(low, (W, 2 * DH))

    first_tile = lax.rem(jnp.minimum(j, n_tiles - 1), tiles_per_seq) == 0
    qi = lax.broadcasted_iota(jnp.int32, (W, W), 0)
    kj = lax.broadcasted_iota(jnp.int32, (W, W), 1)
    upper = kj > qi

    def head_cols(ref_pair, hd, width):
        ref = ref_pair[hd // half_heads]
        hd = hd % half_heads
        return ref, slice(hd * DH, (hd + width) * DH)

    def half(hh, carry, ga_ref, gap_ref, attend, merge):
        gh = jnp.concatenate([gh_ref[h] for h in range(HG_HEADS)], axis=1) if merge else None
        y_parts = {}

        def branch_piece(name, lhs, w_ref, k):
            y_parts[name, k] = jnp.dot(lhs, w_ref[hh, :, k * QUARTER:(k + 1) * QUARTER],
                                       preferred_element_type=F32)

        def gate():
            yh = jnp.concatenate([y_parts["h", 0], y_parts["h", 1]], axis=1)
            ya = jnp.concatenate([y_parts["a", 0], y_parts["a", 1]], axis=1)
            mg_ref[...] = (_sigmoid(mh_ref[hh].astype(F32)) * yh
                           + _sigmoid(ma_ref[hh].astype(F32)) * ya).astype(BF16)

        def out_piece(k):
            cols = slice(k * QUARTER, (k + 1) * QUARTER)
            acc_ref[:, cols] = jnp.where(hh == 0, 0.0, acc_ref[:, cols]) + jnp.dot(
                mg_ref[...], wo_ref[hh, :, cols], preferred_element_type=F32)

        pieces = [lambda: branch_piece("h", gh, wbh_ref, 0), lambda: branch_piece("h", gh, wbh_ref, 1),
                  lambda: branch_piece("a", gap_ref[...], wba_ref, 0),
                  lambda: branch_piece("a", gap_ref[...], wba_ref, 1),
                  lambda: (gate(), out_piece(0)), lambda: out_piece(1), lambda: out_piece(2),
                  lambda: out_piece(3)]

        units = [(bi, h) for bi in range(2) for h in range(ATT_KV_HEADS)]

        def scores(bi, h):
            r = pl.multiple_of((2 * hh + bi) * W, W)
            kk = kvc_ref[pl.ds(r, 2 * W), h * DH:(h + 1) * DH]
            parts = []
            for hd in range(h * ATT_GROUP, (h + 1) * ATT_GROUP):
                ref, cols = head_cols((q0_ref, q1_ref), hd, 1)
                parts.append(ref[pl.ds(r, W), cols])
            return _nt(jnp.concatenate(parts, axis=0), kk)

        s_next = scores(*units[0]) if attend else None
        for idx, (bi, h) in enumerate(units):
            if merge:
                pieces[idx]()
            if attend:
                s = s_next
                if idx + 1 < len(units):
                    s_next = scores(*units[idx + 1])
            if not attend:
                continue
            r = pl.multiple_of((2 * hh + bi) * W, W)
            prev_bias = jnp.where(first_tile & (2 * hh + bi == 0), -jnp.inf, 0.0)
            probs, sink_terms = [], []
            for jj, hd in enumerate(range(h * ATT_GROUP, (h + 1) * ATT_GROUP)):
                sj = s[jj * W:(jj + 1) * W]
                c = jnp.where(upper, sj[:, :W] + prev_bias, sj[:, W:])
                sink = sink_ref[hd]
                m = jnp.maximum(jnp.max(c, axis=-1, keepdims=True), sink)
                p = jnp.exp(c - m)
                sink_terms.append(jnp.exp(sink - m))
                probs.append(jnp.concatenate(
                    [jnp.where(upper, p, 0.0), jnp.where(upper, 0.0, p)], axis=1).astype(BF16))
            vpair = kvc_ref[pl.ds(r, 2 * W), KV_WIDTH + (h // 2) * 2 * DH:
                            KV_WIDTH + (h // 2 + 1) * 2 * DH]
            vswap = jnp.concatenate([vpair[:, DH:], vpair[:, :DH]], axis=1)
            in_low, in_high = (vpair, vswap) if h % 2 == 0 else (vswap, vpair)
            zero = jnp.zeros_like(vpair)
            w_lo = jnp.concatenate([jnp.where(low, in_low, zero), ones_lo], axis=1)
            w_hi = jnp.concatenate([jnp.where(low, zero, in_high), ones_hi], axis=1)
            res = (jnp.dot(jnp.concatenate(probs[0::2], axis=0), w_lo, preferred_element_type=F32)
                   + jnp.dot(jnp.concatenate(probs[1::2], axis=0), w_hi, preferred_element_type=F32))
            for pair in range(ATT_GROUP // 2):
                rp = res[pair * W:(pair + 1) * W]
                den = rp[:, 2 * DH:] + jnp.where(low_rows, sink_terms[2 * pair], sink_terms[2 * pair + 1])
                j0 = h * ATT_GROUP + 2 * pair
                ag_ref, cols = head_cols((ag0_ref, ag1_ref), j0, 2)
                ag = ag_ref[pl.ds(r, W), cols].astype(F32)
                ga_ref[pl.ds(r, W), j0 * DH:(j0 + 2) * DH] = (
                    rp[:, :2 * DH] / den * (ag * _sigmoid(ag))).astype(BF16)
        return carry

    def run(ga_ref, gap_ref, attend, merge):
        lax.fori_loop(0, 2, functools.partial(half, ga_ref=ga_ref, gap_ref=gap_ref, attend=attend,
                                              merge=merge), 0)
        if merge:
            xo = x_ref[...] + acc_ref[...]
            ms = jnp.mean(xo * xo, axis=-1, keepdims=True)
            o_ref[...] = xo * lax.rsqrt(ms + EPS) * fnw_ref[...]

    for parity, bufs in ((0, (ga0_ref, ga1_ref)), (1, (ga1_ref, ga0_ref))):
        if parity == 0:
            pl.when(j == 0)(functools.partial(run, *bufs, attend=True, merge=False))
        if (n_tiles % 2) == parity:
            pl.when(j == n_tiles)(functools.partial(run, *bufs, attend=False, merge=True))
        pl.when((lax.rem(j, 2) == parity) & (j > 0) & (j < n_tiles))(
            functools.partial(run, *bufs, attend=True, merge=True))

    @pl.when(j == 0)
    def _():
        for copy in weight_copies:
            copy.wait()
        for hh in range(2):
            wbh_ref[hh] = wbh32_ref[:, hh * HALF:(hh + 1) * HALF].astype(BF16)
            wba_ref[hh] = wba32_ref[:, hh * HALF:(hh + 1) * HALF].astype(BF16)
            wo_ref[hh] = wo32_ref[hh * HALF:(hh + 1) * HALF, :].astype(BF16)


def _back(pj, gh3, x2d, sinks, wbh, wba, wo, fnw, seq):
    m = x2d.shape[0]
    nt = m // BK_TT
    last = nt - 1
    per = BK_TT // WINDOW

    def cur(jj):
        return jnp.minimum(jj, last)

    def prv(jj):
        return jnp.maximum(jj - 1, 0)

    def chunk(c):
        return pl.BlockSpec((None, BK_TT, NCH), lambda jj, s: (c, cur(jj), 0))

    def pair(c):
        return pl.BlockSpec((2, BK_TT, NCH), lambda jj, s: (c // 2, prv(jj), 0))

    prev_kv = pl.BlockSpec((None, WINDOW, NCH),
                           lambda jj, s: (PJ_KV, jnp.maximum(cur(jj) * per - 1, 0), 0))
    tile = pl.BlockSpec((BK_TT, D_MODEL), lambda jj, s: (prv(jj), 0))
    wspec = pl.BlockSpec(memory_space=pl.ANY)
    half = D_MODEL // 2
    grid_spec = pltpu.PrefetchScalarGridSpec(
        num_scalar_prefetch=1,
        grid=(nt + 1,),
        in_specs=[chunk(PJ_AQ), chunk(PJ_AQ + 1), prev_kv, chunk(PJ_KV), chunk(PJ_AG), chunk(PJ_AG + 1),
                  pl.BlockSpec((HG_HEADS, BK_TT, HG_DV), lambda jj, s: (0, prv(jj), 0)),
                  pair(PJ_MH), pair(PJ_MA), tile, wspec, wspec, wspec,
                  pl.BlockSpec((1, D_MODEL), lambda jj, s: (0, 0))],
        out_specs=tile,
        scratch_shapes=[pltpu.VMEM((BK_TT + WINDOW, NCH), BF16),
                        pltpu.VMEM((BK_TT, D_MODEL), BF16),
                        pltpu.VMEM((BK_TT, D_MODEL), BF16),
                        pltpu.VMEM((2, D_MODEL, half), BF16),
                        pltpu.VMEM((2, D_MODEL, half), BF16),
                        pltpu.VMEM((2, half, D_MODEL), BF16),
                        pltpu.VMEM((BK_TT, D_MODEL), F32),
                        pltpu.VMEM((BK_TT, half), BF16),
                        pltpu.VMEM((D_MODEL, D_MODEL), F32),
                        pltpu.VMEM((D_MODEL, D_MODEL), F32),
                        pltpu.VMEM((D_MODEL, D_MODEL), F32),
                        pltpu.SemaphoreType.DMA((3,))],
    )
    return pl.pallas_call(
        functools.partial(_back_kernel, n_tiles=nt, tiles_per_seq=seq // BK_TT),
        grid_spec=grid_spec,
        out_shape=jax.ShapeDtypeStruct((m, D_MODEL), F32),
        compiler_params=pltpu.CompilerParams(
            dimension_semantics=("arbitrary",), vmem_limit_bytes=VMEM_LIMIT),
        name="back",
    )(sinks, pj, pj, pj, pj, pj, pj, gh3, pj, pj, x2d, wbh, wba, wo, fnw)


def kernel(x, norm_w, w_in, hgrn_lower_bound, hgrn_norm_w, w_branch_hgrn, attn_sinks,
           w_branch_attn, w_out, final_norm_w):
    batch, seq, _ = x.shape
    depth = norm_w.shape[0]
    assert depth == 1, "the back kernel fuses the final RMSNorm into the single layer"
    assert seq % FR_TT == 0 and seq % BK_TT == 0
    lb_all = jnp.cumsum(jax.nn.softmax(hgrn_lower_bound.astype(F32), axis=0), axis=0)
    w3 = _wprep(w_in[0])
    x2d = x.reshape(batch * seq, D_MODEL)
    pj, gh3 = _front(x2d, norm_w[0].reshape(1, D_MODEL), w3,
                     lb_all[0].reshape(HG_HEADS, 1, HG_DK),
                     hgrn_norm_w[0].reshape(HG_HEADS, 1, HG_DV), seq)
    xo = _back(pj, gh3, x2d, attn_sinks[0].astype(F32), w_branch_hgrn[0], w_branch_attn[0], w_out[0],
               final_norm_w.reshape(1, D_MODEL), seq)
    return xo.reshape(batch, seq, D_MODEL)
```

```python
import functools

import jax
import jax.numpy as jnp
from jax import lax
from jax.experimental import pallas as pl
from jax.experimental.pallas import tpu as pltpu

F32 = jnp.float32
BF16 = jnp.bfloat16

D_MODEL = 1024
HG_HEADS = 8
HG_DK = 128
HG_DV = 128
ATT_Q_HEADS = 16
ATT_KV_HEADS = 4
ATT_GROUP = ATT_Q_HEADS // ATT_KV_HEADS
ATT_DH = 64
KV_WIDTH = ATT_KV_HEADS * ATT_DH
ATT_SCALE = ATT_DH ** -0.5
assert ATT_SCALE == 0.125
WINDOW = 128
EPS = 1e-6
D_IN = 8 * D_MODEL + 2 * KV_WIDTH

NCH = 512
N_CHUNKS = D_IN // NCH
N_HG_CHUNKS = 4 * D_MODEL // NCH
N_PJ_CHUNKS = N_CHUNKS - N_HG_CHUNKS
PJ_AQ, PJ_AG, PJ_MH, PJ_MA, PJ_KV = 0, 2, 4, 6, 8
SLABS_PER_CHUNK = NCH // HG_DK
SEC_HQ, SEC_HF, SEC_HI, SEC_HG = range(4)

V7X_VMEM_BYTES = 64 * 1024 * 1024
VMEM_LIMIT = V7X_VMEM_BYTES * 7 // 8

HG_C = 64
HG_SUB = 16
HG_LOG_DECAY_FLOOR = -10.0
assert HG_SUB // 2 * -HG_LOG_DECAY_FLOOR <= 80.0
FR_TT = 512
HG_GROUP = FR_TT // HG_C
HG_UNITS = HG_GROUP // 2
SUBLANES = 8
HG_SLOT_HEADS = 2
BK_TT = 512


def _nt(a, b):
    return lax.dot_general(a.astype(BF16), b.astype(BF16), (((1,), (1,)), ((), ())),
                           preferred_element_type=F32)


def _sigmoid(x):
    return 1.0 / (1.0 + jnp.exp(-x))


WP_ROWS = 256


def _wprep_kernel(w_ref, o_ref):
    kv_src = N_HG_CHUNKS + 2
    for c in range(N_CHUNKS):
        src = c if c < kv_src else (kv_src if c == N_CHUNKS - 1 else c + 1)
        blk = w_ref[:, src * NCH:(src + 1) * NCH]
        if c // 2 == N_HG_CHUNKS // 2:
            blk = blk * ATT_SCALE
        o_ref[c] = blk.astype(BF16)


def _wprep(w):
    return pl.pallas_call(
        _wprep_kernel,
        grid=(D_MODEL // WP_ROWS,),
        in_specs=[pl.BlockSpec((WP_ROWS, D_IN), lambda r: (r, 0))],
        out_specs=pl.BlockSpec((N_CHUNKS, WP_ROWS, NCH), lambda r: (0, r, 0)),
        out_shape=jax.ShapeDtypeStruct((N_CHUNKS, D_MODEL, NCH), BF16),
        compiler_params=pltpu.CompilerParams(
            dimension_semantics=("arbitrary",), vmem_limit_bytes=VMEM_LIMIT),
        name="wprep",
    )(w)


def _front_kernel(x0_ref, xnext_ref, nw_ref, w_ref, lb_ref, hnw_ref, pj_ref, gh_ref, xn0_ref, xn1_ref,
                  hb0_ref, hb1_ref, st_ref, p_sc, qb_sc, kb_sc, dec_sc, *, n_tiles, tiles_per_seq):
    C, SUB, NB = HG_C, HG_SUB, HG_C // HG_SUB
    assert NB == 4, "the three score levels below are written for four sub-blocks per chunk"
    sub_shift = SUB.bit_length() - 1
    HALF = NCH // 2
    i = pl.program_id(0)

    @pl.when(lax.rem(jnp.maximum(i - 1, 0), tiles_per_seq) == 0)
    def _():
        st_ref[...] = jnp.zeros_like(st_ref)

    def normalise(src_ref, dst_ref):
        x = src_ref[...]
        ms = jnp.mean(x * x, axis=-1, keepdims=True)
        dst_ref[...] = (x * lax.rsqrt(ms + EPS) * nw_ref[...]).astype(BF16)

    @pl.when(i == 0)
    def _():
        normalise(x0_ref, xn0_ref)

    row = lax.broadcasted_iota(jnp.int32, (C, C), 0)
    col = lax.broadcasted_iota(jnp.int32, (C, C), 1)
    rb, cb = row >> sub_shift, col >> sub_shift
    m0 = (rb == cb) & (col <= row)
    m1 = ((rb & 1) == 1) & (cb == rb - 1)
    m2 = (row >= 2 * SUB) & (col < 2 * SUB)
    rows = [slice(c * C, (c + 1) * C) for c in range(HG_GROUP)]
    urows = [slice(u * 2 * C, (u + 1) * 2 * C) for u in range(HG_UNITS)]
    sub_row = lax.broadcasted_iota(jnp.int32, (SUBLANES, HG_DK), 0)

    def cat(blocks):
        return jnp.concatenate(blocks, axis=0).astype(BF16)

    def cumsum_rows(g):
        groups = []
        for v in range(C // SUBLANES):
            xg = g[v * SUBLANES:(v + 1) * SUBLANES]
            shift = 1
            while shift < SUBLANES:
                xg = xg + jnp.where(sub_row >= shift, pltpu.roll(xg, shift, axis=0), 0.0)
                shift *= 2
            groups.append(xg)
        out, carry = [groups[0]], groups[0][SUBLANES - 1:SUBLANES]
        for xg in groups[1:]:
            out.append(xg + carry)
            carry = carry + xg[SUBLANES - 1:SUBLANES]
        return jnp.concatenate(out, axis=0)

    def slot(heads_a, abuf, heads_b, chunks, wbuf, rbuf, xn_ref):
        pieces = [(chunk, half, keep) for chunk, keep in chunks for half in range(2)]

        def project(flush=False):
            if not pieces:
                return
            chunk, half, keep = pieces.pop(0)
            acc = jnp.dot(xn_ref[...], w_ref[chunk, :, half * HALF:(half + 1) * HALF],
                          preferred_element_type=F32).astype(BF16)
            if not keep:
                pj_ref[chunk - N_HG_CHUNKS, :, half * HALF:(half + 1) * HALF] = acc
            else:
                for s in range(SLABS_PER_CHUNK // 2):
                    slab = chunk * SLABS_PER_CHUNK + half * (SLABS_PER_CHUNK // 2) + s
                    wbuf[slab] = acc[:, s * HG_DK:(s + 1) * HG_DK]
            if flush:
                project(True)

        def src(sec, head, buf=rbuf):
            return buf.at[sec * HG_HEADS + head]

        loaded = []
        for n, head in enumerate(heads_b):
            base = n * HG_UNITS
            loaded.append(dict(
                v=[src(SEC_HI, head)[urows[u], :] for u in range(HG_UNITS)],
                p=[p_sc[base + u] for u in range(HG_UNITS)],
                qb=[qb_sc[base + u] for u in range(HG_UNITS)],
                kb=[kb_sc[base + u] for u in range(HG_UNITS)],
                dec=[dec_sc[base + u] for u in range(HG_UNITS)]))

        project()

        gates = []
        for head in heads_a:
            lb = lb_ref[head]
            q, k, b = [], [], []
            for c in range(HG_GROUP):
                xq = src(SEC_HQ, head, abuf)[rows[c], :].astype(F32)
                xf = src(SEC_HF, head, abuf)[rows[c], :].astype(F32)
                f = lb + (1.0 - lb) * _sigmoid(xf)
                k.append(1.0 - f)
                q.append(xq * _sigmoid(xq))
                b.append(cumsum_rows(jnp.maximum(jnp.log(f), HG_LOG_DECAY_FLOOR)))
            gates.append((q, k, b))

        for ld in loaded:
            ld["o_intra"] = [jnp.dot(ld["p"][u], ld["v"][u], preferred_element_type=F32)
                             for u in range(HG_UNITS)]
            ld["kv"] = [lax.dot_general(ld["v"][u], ld["kb"][u], (((0,), (0,)), ((), ())),
                                        preferred_element_type=F32) for u in range(HG_UNITS)]
            project()

        for head, ld in zip(heads_b, loaded):
            st = st_ref[head]
            ld["o"] = []
            for u in range(HG_UNITS):
                ld["o"].append(ld["o_intra"][u] + _nt(ld["qb"][u], st))
                st = st * ld["dec"][u] + ld["kv"][u]
            st_ref[head] = st
            project()

        for n, (q, k, b) in enumerate(gates):
            unit_handover(n * HG_UNITS, *chunk_scores(q, k, b))
            project()

        project(flush=True)

        for head, ld in zip(heads_b, loaded):
            nw = hnw_ref[head]
            for u in range(HG_UNITS):
                xg = src(SEC_HG, head)[urows[u], :].astype(F32)
                o = ld["o"][u]
                ms_o = jnp.mean(o * o, axis=-1, keepdims=True)
                on = o * lax.rsqrt(ms_o + EPS) * nw
                gh_ref[head, urows[u], :] = (on * (xg * _sigmoid(xg))).astype(BF16)

    def chunk_scores(q, k, b):
        p_c, qb_c, kb_c, last_c = [], [], [], []
        for c in range(HG_GROUP):
            ends = [b[c][(j + 1) * SUB - 1:(j + 1) * SUB] for j in range(NB)]
            starts = [jnp.zeros_like(ends[0])] + ends[:-1]
            last = ends[-1]
            q1, k1, qd, kd, qb, kb = [], [], [], [], [], []
            for j in range(NB):
                sl = slice(j * SUB, (j + 1) * SUB)
                bj = b[c][sl]
                q1j = q[c][sl] * jnp.exp(bj - starts[j])
                k1j = k[c][sl] * jnp.exp(ends[j] - bj)
                mid = 0.5 * (starts[j] + ends[j])
                q1.append(q1j)
                k1.append(k1j)
                qd.append(q[c][sl] * jnp.exp(bj - mid))
                kd.append(k[c][sl] * jnp.exp(mid - bj))
                qb.append(q1j * jnp.exp(starts[j]))
                kb.append(k1j * jnp.exp(last - ends[j]))
            q2 = q1[:3] + [q1[3] * jnp.exp(ends[2] - ends[1])]
            k2 = [k1[0] * jnp.exp(ends[1] - ends[0])] + k1[1:]
            s0 = _nt(cat(qd), cat(kd))
            s1 = _nt(cat(q1), cat(k1))
            s2 = _nt(cat(q2), cat(k2))
            p_c.append(jnp.where(m0, s0, jnp.where(m1, s1, jnp.where(m2, s2, 0.0))).astype(BF16))
            qb_c.append(jnp.concatenate(qb, axis=0))
            kb_c.append(jnp.concatenate(kb, axis=0))
            last_c.append(last)
        return p_c, qb_c, kb_c, last_c

    def unit_handover(base, p_c, qb_c, kb_c, last_c):
        p_a, qb_a, kb_a, dec_a = [], [], [], []
        for u in range(HG_UNITS):
            c0, c1 = 2 * u, 2 * u + 1
            cross = _nt(qb_c[c1], kb_c[c0]).astype(BF16)
            p_a.append(jnp.concatenate(
                [jnp.concatenate([p_c[c0], jnp.zeros((C, C), BF16)], axis=1),
                 jnp.concatenate([cross, p_c[c1]], axis=1)], axis=0))
            qb_a.append(cat([qb_c[c0], qb_c[c1] * jnp.exp(last_c[c0])]))
            kb_a.append(cat([kb_c[c0] * jnp.exp(last_c[c1]), kb_c[c1]]))
            dec_a.append(jnp.exp(last_c[c0] + last_c[c1]))

        for u in range(HG_UNITS):
            p_sc[base + u] = p_a[u]
            qb_sc[base + u] = qb_a[u]
            kb_sc[base + u] = kb_a[u]
            dec_sc[base + u] = dec_a[u]

    def run(wbuf, rbuf, xn_ref, xn_next_ref, recur, proj):
        hp = HG_SLOT_HEADS
        n_slots = HG_HEADS // hp
        assert (hp, n_slots, N_HG_CHUNKS) == (2, 4, 8), "chunk schedule below is written for this"

        def mid(t, carry):
            heads_a = tuple(hp * t + n for n in range(hp)) if recur else ()
            heads_b = tuple(hp * (t - 1) + n for n in range(hp)) if recur else ()
            late = t == n_slots - 1
            keep0 = jnp.where(late, 4, t - 1)
            keep1 = jnp.where(late, 5, t + 1)
            out0 = N_HG_CHUNKS + 2 * (t - 1)
            chunks = [(keep0, True), (out0, False), (keep1, True), (out0 + 1, False)] if proj else []
            slot(heads_a, rbuf, heads_b, chunks, wbuf, rbuf, xn_ref)
            return carry

        lax.fori_loop(1, n_slots, mid, 0)
        if proj:
            normalise(xnext_ref, xn_next_ref)
        first_out = N_HG_CHUNKS + 2 * (n_slots - 1)
        last_chunks = ([(N_HG_CHUNKS - 2, True), (N_HG_CHUNKS - 1, True)]
                       + [(c, False) for c in range(first_out, N_CHUNKS)])
        slot(tuple(range(hp)) if proj else (), wbuf,
             tuple(HG_HEADS - hp + n for n in range(hp)) if recur else (),
             last_chunks if proj else [], wbuf, rbuf, xn_ref)

    for parity, bufs in ((0, (hb0_ref, hb1_ref, xn0_ref, xn1_ref)), (1, (hb1_ref, hb0_ref, xn1_ref, xn0_ref))):
        on_parity = lax.rem(i, 2) == parity
        if parity == 0:
            pl.when(i == 0)(functools.partial(run, *bufs, recur=False, proj=True))
        if (n_tiles % 2) == parity:
            pl.when(i == n_tiles)(functools.partial(run, *bufs, recur=True, proj=False))
        pl.when(on_parity & (i > 0) & (i < n_tiles))(functools.partial(run, *bufs, recur=True, proj=True))


def _front(x2d, norm_w, w3, lb, hnw, seq):
    m = x2d.shape[0]
    nt = m // FR_TT
    last = nt - 1
    return pl.pallas_call(
        functools.partial(_front_kernel, n_tiles=nt, tiles_per_seq=seq // FR_TT),
        grid=(nt + 1,),
        in_specs=[
            pl.BlockSpec((FR_TT, D_MODEL), lambda i: (0, 0), pipeline_mode=pl.Buffered(1)),
            pl.BlockSpec((FR_TT, D_MODEL), lambda i: (jnp.minimum(i + 1, last), 0)),
            pl.BlockSpec((1, D_MODEL), lambda i: (0, 0)),
            pl.BlockSpec((N_CHUNKS, D_MODEL, NCH), lambda i: (0, 0, 0), pipeline_mode=pl.Buffered(1)),
            pl.BlockSpec((HG_HEADS, 1, HG_DK), lambda i: (0, 0, 0)),
            pl.BlockSpec((HG_HEADS, 1, HG_DV), lambda i: (0, 0, 0)),
        ],
        out_specs=[
            pl.BlockSpec((N_PJ_CHUNKS, FR_TT, NCH), lambda i: (0, jnp.minimum(i, last), 0)),
            pl.BlockSpec((HG_HEADS, FR_TT, HG_DV), lambda i: (0, jnp.maximum(i - 1, 0), 0)),
        ],
        out_shape=[jax.ShapeDtypeStruct((N_PJ_CHUNKS, m, NCH), BF16),
                   jax.ShapeDtypeStruct((HG_HEADS, m, HG_DV), BF16)],
        scratch_shapes=[pltpu.VMEM((FR_TT, D_MODEL), BF16),
                        pltpu.VMEM((FR_TT, D_MODEL), BF16),
                        pltpu.VMEM((4 * HG_HEADS, FR_TT, HG_DK), BF16),
                        pltpu.VMEM((4 * HG_HEADS, FR_TT, HG_DK), BF16),
                        pltpu.VMEM((HG_HEADS, HG_DV, HG_DK), F32),
                        pltpu.VMEM((HG_SLOT_HEADS * HG_UNITS, 2 * HG_C, 2 * HG_C), BF16),
                        pltpu.VMEM((HG_SLOT_HEADS * HG_UNITS, 2 * HG_C, HG_DK), BF16),
                        pltpu.VMEM((HG_SLOT_HEADS * HG_UNITS, 2 * HG_C, HG_DK), BF16),
                        pltpu.VMEM((HG_SLOT_HEADS * HG_UNITS, 1, HG_DK), F32)],
        compiler_params=pltpu.CompilerParams(
            dimension_semantics=("arbitrary",), vmem_limit_bytes=VMEM_LIMIT),
        name="front",
    )(x2d, x2d, norm_w, w3, lb, hnw)


def _back_kernel(sink_ref, q0_ref, q1_ref, kvp_ref, kv_ref, ag0_ref, ag1_ref, gh_ref, mh_ref, ma_ref,
                 x_ref, wbh_hbm, wba_hbm, wo_hbm, fnw_ref, o_ref,
                 kvc_ref, ga0_ref, ga1_ref, wbh_ref, wba_ref, wo_ref, acc_ref, mg_ref,
                 wbh32_ref, wba32_ref, wo32_ref, w_sem,
                 *, n_tiles, tiles_per_seq):
    W, DH = WINDOW, ATT_DH
    half_heads = ATT_Q_HEADS // 2
    HALF = D_MODEL // 2
    QUARTER = D_MODEL // 4
    j = pl.program_id(0)

    weight_copies = [pltpu.make_async_copy(src, dst, w_sem.at[n]) for n, (src, dst) in enumerate(
        ((wbh_hbm, wbh32_ref), (wba_hbm, wba32_ref), (wo_hbm, wo32_ref)))]

    @pl.when(j == 0)
    def _():
        for copy in weight_copies:
            copy.start()
        acc_ref[...] = jnp.zeros_like(acc_ref)

    kvc_ref[0:W, :] = kvp_ref[...]
    kvc_ref[W:, :] = kv_ref[...]
    lane = lax.broadcasted_iota(jnp.int32, (1, 2 * DH), 1)
    low = lane < DH
    ones_lo = jnp.broadcast_to(jnp.where(low, 1.0, 0.0).astype(BF16), (2 * W, 2 * DH))
    ones_hi = jnp.broadcast_to(jnp.where(low, 0.0, 1.0).astype(BF16), (2 * W, 2 * DH))
    low_rows = jnp.broadcast_to(low, (W, 2 * DH))

    first_tile = lax.rem(jnp.minimum(j, n_tiles - 1), tiles_per_seq) == 0
    qi = lax.broadcasted_iota(jnp.int32, (W, W), 0)
    kj = lax.broadcasted_iota(jnp.int32, (W, W), 1)
    upper = kj > qi

    def head_cols(ref_pair, hd, width):
        ref = ref_pair[hd // half_heads]
        hd = hd % half_heads
        return ref, slice(hd * DH, (hd + width) * DH)

    def half(hh, carry, ga_ref, gap_ref, attend, merge):
        gh = jnp.concatenate([gh_ref[h] for h in range(HG_HEADS)], axis=1) if merge else None
        y_parts = {}

        def branch_piece(name, lhs, w_ref, k):
            y_parts[name, k] = jnp.dot(lhs, w_ref[hh, :, k * QUARTER:(k + 1) * QUARTER],
                                       preferred_element_type=F32)

        def gate():
            yh = jnp.concatenate([y_parts["h", 0], y_parts["h", 1]], axis=1)
            ya = jnp.concatenate([y_parts["a", 0], y_parts["a", 1]], axis=1)
            mg_ref[...] = (_sigmoid(mh_ref[hh].astype(F32)) * yh
                           + _sigmoid(ma_ref[hh].astype(F32)) * ya).astype(BF16)

        def out_piece(k):
            cols = slice(k * QUARTER, (k + 1) * QUARTER)
            acc_ref[:, cols] = jnp.where(hh == 0, 0.0, acc_ref[:, cols]) + jnp.dot(
                mg_ref[...], wo_ref[hh, :, cols], preferred_element_type=F32)

        pieces = [lambda: branch_piece("h", gh, wbh_ref, 0), lambda: branch_piece("h", gh, wbh_ref, 1),
                  lambda: branch_piece("a", gap_ref[...], wba_ref, 0),
                  lambda: branch_piece("a", gap_ref[...], wba_ref, 1),
                  lambda: (gate(), out_piece(0)), lambda: out_piece(1), lambda: out_piece(2),
                  lambda: out_piece(3)]

        units = [(bi, h) for bi in range(2) for h in range(ATT_KV_HEADS)]

        def scores(bi, h):
            r = pl.multiple_of((2 * hh + bi) * W, W)
            kk = kvc_ref[pl.ds(r, 2 * W), h * DH:(h + 1) * DH]
            parts = []
            for hd in range(h * ATT_GROUP, (h + 1) * ATT_GROUP):
                ref, cols = head_cols((q0_ref, q1_ref), hd, 1)
                parts.append(ref[pl.ds(r, W), cols])
            return _nt(jnp.concatenate(parts, axis=0), kk)

        s_next = scores(*units[0]) if attend else None
        for idx, (bi, h) in enumerate(units):
            if merge:
                pieces[idx]()
            if attend:
                s = s_next
                if idx + 1 < len(units):
                    s_next = scores(*units[idx + 1])
            if not attend:
                continue
            r = pl.multiple_of((2 * hh + bi) * W, W)
            prev_bias = jnp.where(first_tile & (2 * hh + bi == 0), -jnp.inf, 0.0)
            probs, sink_terms = [], []
            for jj, hd in enumerate(range(h * ATT_GROUP, (h + 1) * ATT_GROUP)):
                sj = s[jj * W:(jj + 1) * W]
                c = jnp.where(upper, sj[:, :W] + prev_bias, sj[:, W:])
                sink = sink_ref[hd]
                m = jnp.maximum(jnp.max(c, axis=-1, keepdims=True), sink)
                p = jnp.exp(c - m)
                sink_terms.append(jnp.exp(sink - m))
                probs.append(jnp.concatenate(
                    [jnp.where(upper, p, 0.0), jnp.where(upper, 0.0, p)], axis=1).astype(BF16))
            vpair = kvc_ref[pl.ds(r, 2 * W), KV_WIDTH + (h // 2) * 2 * DH:
                            KV_WIDTH + (h // 2 + 1) * 2 * DH]
            vswap = jnp.concatenate([vpair[:, DH:], vpair[:, :DH]], axis=1)
            in_low, in_high = (vpair, vswap) if h % 2 == 0 else (vswap, vpair)
            zero = jnp.zeros_like(vpair)
            w_lo = jnp.concatenate([jnp.where(low, in_low, zero), ones_lo], axis=1)
            w_hi = jnp.concatenate([jnp.where(low, zero, in_high), ones_hi], axis=1)
            res = (jnp.dot(jnp.concatenate(probs[0::2], axis=0), w_lo, preferred_element_type=F32)
                   + jnp.dot(jnp.concatenate(probs[1::2], axis=0), w_hi, preferred_element_type=F32))
            for pair in range(ATT_GROUP // 2):
                rp = res[pair * W:(pair + 1) * W]
                den = rp[:, 2 * DH:] + jnp.where(low_rows, sink_terms[2 * pair], sink_terms[2 * pair + 1])
                j0 = h * ATT_GROUP + 2 * pair
                ag_ref, cols = head_cols((ag0_ref, ag1_ref), j0, 2)
                ag = ag_ref[pl.ds(r, W), cols].astype(F32)
                ga_ref[pl.ds(r, W), j0 * DH:(j0 + 2) * DH] = (
                    rp[:, :2 * DH] / den * (ag * _sigmoid(ag))).astype(BF16)
        return carry

    def run(ga_ref, gap_ref, attend, merge):
        lax.fori_loop(0, 2, functools.partial(half, ga_ref=ga_ref, gap_ref=gap_ref, attend=attend,
                                              merge=merge), 0)
        if merge:
            xo = x_ref[...] + acc_ref[...]
            ms = jnp.mean(xo * xo, axis=-1, keepdims=True)
            o_ref[...] = xo * lax.rsqrt(ms + EPS) * fnw_ref[...]

    for parity, bufs in ((0, (ga0_ref, ga1_ref)), (1, (ga1_ref, ga0_ref))):
        if parity == 0:
            pl.when(j == 0)(functools.partial(run, *bufs, attend=True, merge=False))
        if (n_tiles % 2) == parity:
            pl.when(j == n_tiles)(functools.partial(run, *bufs, attend=False, merge=True))
        pl.when((lax.rem(j, 2) == parity) & (j > 0) & (j < n_tiles))(
            functools.partial(run, *bufs, attend=True, merge=True))

    @pl.when(j == 0)
    def _():
        for copy in weight_copies:
            copy.wait()
        for hh in range(2):
            wbh_ref[hh] = wbh32_ref[:, hh * HALF:(hh + 1) * HALF].astype(BF16)
            wba_ref[hh] = wba32_ref[:, hh * HALF:(hh + 1) * HALF].astype(BF16)
            wo_ref[hh] = wo32_ref[hh * HALF:(hh + 1) * HALF, :].astype(BF16)


def _back(pj, gh3, x2d, sinks, wbh, wba, wo, fnw, seq):
    m = x2d.shape[0]
    nt = m // BK_TT
    last = nt - 1
    per = BK_TT // WINDOW

    def cur(jj):
        return jnp.minimum(jj, last)

    def prv(jj):
        return jnp.maximum(jj - 1, 0)

    def chunk(c):
        return pl.BlockSpec((None, BK_TT, NCH), lambda jj, s: (c, cur(jj), 0))

    def pair(c):
        return pl.BlockSpec((2, BK_TT, NCH), lambda jj, s: (c // 2, prv(jj), 0))

    prev_kv = pl.BlockSpec((None, WINDOW, NCH),
                           lambda jj, s: (PJ_KV, jnp.maximum(cur(jj) * per - 1, 0), 0))
    tile = pl.BlockSpec((BK_TT, D_MODEL), lambda jj, s: (prv(jj), 0))
    wspec = pl.BlockSpec(memory_space=pl.ANY)
    half = D_MODEL // 2
    grid_spec = pltpu.PrefetchScalarGridSpec(
        num_scalar_prefetch=1,
        grid=(nt + 1,),
        in_specs=[chunk(PJ_AQ), chunk(PJ_AQ + 1), prev_kv, chunk(PJ_KV), chunk(PJ_AG), chunk(PJ_AG + 1),
                  pl.BlockSpec((HG_HEADS, BK_TT, HG_DV), lambda jj, s: (0, prv(jj), 0)),
                  pair(PJ_MH), pair(PJ_MA), tile, wspec, wspec, wspec,
                  pl.BlockSpec((1, D_MODEL), lambda jj, s: (0, 0))],
        out_specs=tile,
        scratch_shapes=[pltpu.VMEM((BK_TT + WINDOW, NCH), BF16),
                        pltpu.VMEM((BK_TT, D_MODEL), BF16),
                        pltpu.VMEM((BK_TT, D_MODEL), BF16),
                        pltpu.VMEM((2, D_MODEL, half), BF16),
                        pltpu.VMEM((2, D_MODEL, half), BF16),
                        pltpu.VMEM((2, half, D_MODEL), BF16),
                        pltpu.VMEM((BK_TT, D_MODEL), F32),
                        pltpu.VMEM((BK_TT, half), BF16),
                        pltpu.VMEM((D_MODEL, D_MODEL), F32),
                        pltpu.VMEM((D_MODEL, D_MODEL), F32),
                        pltpu.VMEM((D_MODEL, D_MODEL), F32),
                        pltpu.SemaphoreType.DMA((3,))],
    )
    return pl.pallas_call(
        functools.partial(_back_kernel, n_tiles=nt, tiles_per_seq=seq // BK_TT),
        grid_spec=grid_spec,
        out_shape=jax.ShapeDtypeStruct((m, D_MODEL), F32),
        compiler_params=pltpu.CompilerParams(
            dimension_semantics=("arbitrary",), vmem_limit_bytes=VMEM_LIMIT),
        name="back",
    )(sinks, pj, pj, pj, pj, pj, pj, gh3, pj, pj, x2d, wbh, wba, wo, fnw)


def kernel(x, norm_w, w_in, hgrn_lower_bound, hgrn_norm_w, w_branch_hgrn, attn_sinks,
           w_branch_attn, w_out, final_norm_w):
    batch, seq, _ = x.shape
    depth = norm_w.shape[0]
    assert depth == 1, "the back kernel fuses the final RMSNorm into the single layer"
    assert seq % FR_TT == 0 and seq % BK_TT == 0
    lb_all = jnp.cumsum(jax.nn.softmax(hgrn_lower_bound.astype(F32), axis=0), axis=0)
    w3 = _wprep(w_in[0])
    x2d = x.reshape(batch * seq, D_MODEL)
    pj, gh3 = _front(x2d, norm_w[0].reshape(1, D_MODEL), w3,
                     lb_all[0].reshape(HG_HEADS, 1, HG_DK),
                     hgrn_norm_w[0].reshape(HG_HEADS, 1, HG_DV), seq)
    xo = _back(pj, gh3, x2d, attn_sinks[0].astype(F32), w_branch_hgrn[0], w_branch_attn[0], w_out[0],
               final_norm_w.reshape(1, D_MODEL), seq)
    return xo.reshape(batch, seq, D_MODEL)
```

```python
import functools

import jax
import jax.numpy as jnp
from jax import lax
from jax.experimental import pallas as pl
from jax.experimental.pallas import tpu as pltpu

F32 = jnp.float32
BF16 = jnp.bfloat16

D_MODEL = 1024
HG_HEADS = 8
HG_DK = 128
HG_DV = 128
ATT_Q_HEADS = 16
ATT_KV_HEADS = 4
ATT_GROUP = ATT_Q_HEADS // ATT_KV_HEADS
ATT_DH = 64
KV_WIDTH = ATT_KV_HEADS * ATT_DH
ATT_SCALE = ATT_DH ** -0.5
assert ATT_SCALE == 0.125
WINDOW = 128
EPS = 1e-6
D_IN = 8 * D_MODEL + 2 * KV_WIDTH

NCH = 512
N_CHUNKS = D_IN // NCH
N_HG_CHUNKS = 4 * D_MODEL // NCH
N_PJ_CHUNKS = N_CHUNKS - N_HG_CHUNKS
PJ_AQ, PJ_AG, PJ_MH, PJ_MA, PJ_KV = 0, 2, 4, 6, 8
SLABS_PER_CHUNK = NCH // HG_DK
SEC_HQ, SEC_HF, SEC_HI, SEC_HG = range(4)

V7X_VMEM_BYTES = 64 * 1024 * 1024
VMEM_LIMIT = V7X_VMEM_BYTES * 7 // 8

HG_C = 64
HG_SUB = 16
HG_LOG_DECAY_FLOOR = -10.0
assert HG_SUB // 2 * -HG_LOG_DECAY_FLOOR <= 80.0
FR_TT = 512
HG_GROUP = FR_TT // HG_C
HG_UNITS = HG_GROUP // 2
SUBLANES = 8
HG_SLOT_HEADS = 2
BK_TT = 512


def _nt(a, b):
    return lax.dot_general(a.astype(BF16), b.astype(BF16), (((1,), (1,)), ((), ())),
                           preferred_element_type=F32)


def _sigmoid(x):
    return 1.0 / (1.0 + jnp.exp(-x))


WP_ROWS = 256


def _wprep_kernel(w_ref, o_ref):
    kv_src = N_HG_CHUNKS + 2
    for c in range(N_CHUNKS):
        src = c if c < kv_src else (kv_src if c == N_CHUNKS - 1 else c + 1)
        blk = w_ref[:, src * NCH:(src + 1) * NCH]
        if c // 2 == N_HG_CHUNKS // 2:
            blk = blk * ATT_SCALE
        o_ref[c] = blk.astype(BF16)


def _wprep(w):
    return pl.pallas_call(
        _wprep_kernel,
        grid=(D_MODEL // WP_ROWS,),
        in_specs=[pl.BlockSpec((WP_ROWS, D_IN), lambda r: (r, 0))],
        out_specs=pl.BlockSpec((N_CHUNKS, WP_ROWS, NCH), lambda r: (0, r, 0)),
        out_shape=jax.ShapeDtypeStruct((N_CHUNKS, D_MODEL, NCH), BF16),
        compiler_params=pltpu.CompilerParams(
            dimension_semantics=("arbitrary",), vmem_limit_bytes=VMEM_LIMIT),
        name="wprep",
    )(w)


def _front_kernel(x0_ref, xnext_ref, nw_ref, w_ref, lb_ref, hnw_ref, pj_ref, gh_ref, xn0_ref, xn1_ref,
                  hb0_ref, hb1_ref, st_ref, p_sc, qb_sc, kb_sc, dec_sc, *, n_tiles, tiles_per_seq):
    C, SUB, NB = HG_C, HG_SUB, HG_C // HG_SUB
    assert NB == 4, "the three score levels below are written for four sub-blocks per chunk"
    sub_shift = SUB.bit_length() - 1
    HALF = NCH // 2
    i = pl.program_id(0)

    @pl.when(lax.rem(jnp.maximum(i - 1, 0), tiles_per_seq) == 0)
    def _():
        st_ref[...] = jnp.zeros_like(st_ref)

    def normalise(src_ref, dst_ref):
        x = src_ref[...]
        ms = jnp.mean(x * x, axis=-1, keepdims=True)
        dst_ref[...] = (x * lax.rsqrt(ms + EPS) * nw_ref[...]).astype(BF16)

    @pl.when(i == 0)
    def _():
        normalise(x0_ref, xn0_ref)

    row = lax.broadcasted_iota(jnp.int32, (C, C), 0)
    col = lax.broadcasted_iota(jnp.int32, (C, C), 1)
    rb, cb = row >> sub_shift, col >> sub_shift
    m0 = (rb == cb) & (col <= row)
    m1 = ((rb & 1) == 1) & (cb == rb - 1)
    m2 = (row >= 2 * SUB) & (col < 2 * SUB)
    rows = [slice(c * C, (c + 1) * C) for c in range(HG_GROUP)]
    urows = [slice(u * 2 * C, (u + 1) * 2 * C) for u in range(HG_UNITS)]
    sub_row = lax.broadcasted_iota(jnp.int32, (SUBLANES, HG_DK), 0)

    def cat(blocks):
        return jnp.concatenate(blocks, axis=0).astype(BF16)

    def cumsum_rows(g):
        groups = []
        for v in range(C // SUBLANES):
            xg = g[v * SUBLANES:(v + 1) * SUBLANES]
            shift = 1
            while shift < SUBLANES:
                xg = xg + jnp.where(sub_row >= shift, pltpu.roll(xg, shift, axis=0), 0.0)
                shift *= 2
            groups.append(xg)
        out, carry = [groups[0]], groups[0][SUBLANES - 1:SUBLANES]
        for xg in groups[1:]:
            out.append(xg + carry)
            carry = carry + xg[SUBLANES - 1:SUBLANES]
        return jnp.concatenate(out, axis=0)

    def slot(heads_a, abuf, heads_b, chunks, wbuf, rbuf, xn_ref):
        pieces = [(chunk, half, keep) for chunk, keep in chunks for half in range(2)]

        def project(flush=False):
            if not pieces:
                return
            chunk, half, keep = pieces.pop(0)
            acc = jnp.dot(xn_ref[...], w_ref[chunk, :, half * HALF:(half + 1) * HALF],
                          preferred_element_type=F32).astype(BF16)
            if not keep:
                pj_ref[chunk - N_HG_CHUNKS, :, half * HALF:(half + 1) * HALF] = acc
            else:
                for s in range(SLABS_PER_CHUNK // 2):
                    slab = chunk * SLABS_PER_CHUNK + half * (SLABS_PER_CHUNK // 2) + s
                    wbuf[slab] = acc[:, s * HG_DK:(s + 1) * HG_DK]
            if flush:
                project(True)

        def src(sec, head, buf=rbuf):
            return buf.at[sec * HG_HEADS + head]

        loaded = []
        for n, head in enumerate(heads_b):
            base = n * HG_UNITS
            loaded.append(dict(
                v=[src(SEC_HI, head)[urows[u], :] for u in range(HG_UNITS)],
                p=[p_sc[base + u] for u in range(HG_UNITS)],
                qb=[qb_sc[base + u] for u in range(HG_UNITS)],
                kb=[kb_sc[base + u] for u in range(HG_UNITS)],
                dec=[dec_sc[base + u] for u in range(HG_UNITS)]))

        project()

        gates = []
        for head in heads_a:
            lb = lb_ref[head]
            q, k, b = [], [], []
            for c in range(HG_GROUP):
                xq = src(SEC_HQ, head, abuf)[rows[c], :].astype(F32)
                xf = src(SEC_HF, head, abuf)[rows[c], :].astype(F32)
                f = lb + (1.0 - lb) * _sigmoid(xf)
                k.append(1.0 - f)
                q.append(xq * _sigmoid(xq))
                b.append(cumsum_rows(jnp.maximum(jnp.log(f), HG_LOG_DECAY_FLOOR)))
            gates.append((q, k, b))

        for ld in loaded:
            ld["o_intra"] = [jnp.dot(ld["p"][u], ld["v"][u], preferred_element_type=F32)
                             for u in range(HG_UNITS)]
            ld["kv"] = [lax.dot_general(ld["v"][u], ld["kb"][u], (((0,), (0,)), ((), ())),
                                        preferred_element_type=F32) for u in range(HG_UNITS)]
            project()

        for head, ld in zip(heads_b, loaded):
            st = st_ref[head]
            ld["o"] = []
            for u in range(HG_UNITS):
                ld["o"].append(ld["o_intra"][u] + _nt(ld["qb"][u], st))
                st = st * ld["dec"][u] + ld["kv"][u]
            st_ref[head] = st
            project()

        for n, (q, k, b) in enumerate(gates):
            unit_handover(n * HG_UNITS, *chunk_scores(q, k, b))
            project()

        project(flush=True)

        for head, ld in zip(heads_b, loaded):
            nw = hnw_ref[head]
            for u in range(HG_UNITS):
                xg = src(SEC_HG, head)[urows[u], :].astype(F32)
                o = ld["o"][u]
                ms_o = jnp.mean(o * o, axis=-1, keepdims=True)
                on = o * lax.rsqrt(ms_o + EPS) * nw
                gh_ref[head, urows[u], :] = (on * (xg * _sigmoid(xg))).astype(BF16)

    def chunk_scores(q, k, b):
        p_c, qb_c, kb_c, last_c = [], [], [], []
        for c in range(HG_GROUP):
            ends = [b[c][(j + 1) * SUB - 1:(j + 1) * SUB] for j in range(NB)]
            starts = [jnp.zeros_like(ends[0])] + ends[:-1]
            last = ends[-1]
            q1, k1, qd, kd, qb, kb = [], [], [], [], [], []
            for j in range(NB):
                sl = slice(j * SUB, (j + 1) * SUB)
                bj = b[c][sl]
                q1j = q[c][sl] * jnp.exp(bj - starts[j])
                k1j = k[c][sl] * jnp.exp(ends[j] - bj)
                mid = 0.5 * (starts[j] + ends[j])
                q1.append(q1j)
                k1.append(k1j)
                qd.append(q[c][sl] * jnp.exp(bj - mid))
                kd.append(k[c][sl] * jnp.exp(mid - bj))
                qb.append(q1j * jnp.exp(starts[j]))
                kb.append(k1j * jnp.exp(last - ends[j]))
            q2 = q1[:3] + [q1[3] * jnp.exp(ends[2] - ends[1])]
            k2 = [k1[0] * jnp.exp(ends[1] - ends[0])] + k1[1:]
            s0 = _nt(cat(qd), cat(kd))
            s1 = _nt(cat(q1), cat(k1))
            s2 = _nt(cat(q2), cat(k2))
            p_c.append(jnp.where(m0, s0, jnp.where(m1, s1, jnp.where(m2, s2, 0.0))).astype(BF16))
            qb_c.append(jnp.concatenate(qb, axis=0))
            kb_c.append(jnp.concatenate(kb, axis=0))
            last_c.append(last)
        return p_c, qb_c, kb_c, last_c

    def unit_handover(base, p_c, qb_c, kb_c, last_c):
        p_a, qb_a, kb_a, dec_a = [], [], [], []
        for u in range(HG_UNITS):
            c0, c1 = 2 * u, 2 * u + 1
            cross = _nt(qb_c[c1], kb_c[c0]).astype(BF16)
            p_a.append(jnp.concatenate(
                [jnp.concatenate([p_c[c0], jnp.zeros((C, C), BF16)], axis=1),
                 jnp.concatenate([cross, p_c[c1]], axis=1)], axis=0))
            qb_a.append(cat([qb_c[c0], qb_c[c1] * jnp.exp(last_c[c0])]))
            kb_a.append(cat([kb_c[c0] * jnp.exp(last_c[c1]), kb_c[c1]]))
            dec_a.append(jnp.exp(last_c[c0] + last_c[c1]))

        for u in range(HG_UNITS):
            p_sc[base + u] = p_a[u]
            qb_sc[base + u] = qb_a[u]
            kb_sc[base + u] = kb_a[u]
            dec_sc[base + u] = dec_a[u]

    def run(wbuf, rbuf, xn_ref, xn_next_ref, recur, proj):
        hp = HG_SLOT_HEADS
        n_slots = HG_HEADS // hp
        assert (hp, n_slots, N_HG_CHUNKS) == (2, 4, 8), "chunk schedule below is written for this"

        def mid(t, carry):
            heads_a = tuple(hp * t + n for n in range(hp)) if recur else ()
            heads_b = tuple(hp * (t - 1) + n for n in range(hp)) if recur else ()
            late = t == n_slots - 1
            keep0 = jnp.where(late, 4, t - 1)
            keep1 = jnp.where(late, 5, t + 1)
            out0 = N_HG_CHUNKS + 2 * (t - 1)
            chunks = [(keep0, True), (out0, False), (keep1, True), (out0 + 1, False)] if proj else []
            slot(heads_a, rbuf, heads_b, chunks, wbuf, rbuf, xn_ref)
            return carry

        lax.fori_loop(1, n_slots, mid, 0)
        if proj:
            normalise(xnext_ref, xn_next_ref)
        first_out = N_HG_CHUNKS + 2 * (n_slots - 1)
        last_chunks = ([(N_HG_CHUNKS - 2, True), (N_HG_CHUNKS - 1, True)]
                       + [(c, False) for c in range(first_out, N_CHUNKS)])
        slot(tuple(range(hp)) if proj else (), wbuf,
             tuple(HG_HEADS - hp + n for n in range(hp)) if recur else (),
             last_chunks if proj else [], wbuf, rbuf, xn_ref)

    for parity, bufs in ((0, (hb0_ref, hb1_ref, xn0_ref, xn1_ref)), (1, (hb1_ref, hb0_ref, xn1_ref, xn0_ref))):
        on_parity = lax.rem(i, 2) == parity
        if parity == 0:
            pl.when(i == 0)(functools.partial(run, *bufs, recur=False, proj=True))
        if (n_tiles % 2) == parity:
            pl.when(i == n_tiles)(functools.partial(run, *bufs, recur=True, proj=False))
        pl.when(on_parity & (i > 0) & (i < n_tiles))(functools.partial(run, *bufs, recur=True, proj=True))


def _front(x2d, norm_w, w3, lb, hnw, seq):
    m = x2d.shape[0]
    nt = m // FR_TT
    last = nt - 1
    return pl.pallas_call(
        functools.partial(_front_kernel, n_tiles=nt, tiles_per_seq=seq // FR_TT),
        grid=(nt + 1,),
        in_specs=[
            pl.BlockSpec((FR_TT, D_MODEL), lambda i: (0, 0), pipeline_mode=pl.Buffered(1)),
            pl.BlockSpec((FR_TT, D_MODEL), lambda i: (jnp.minimum(i + 1, last), 0)),
            pl.BlockSpec((1, D_MODEL), lambda i: (0, 0)),
            pl.BlockSpec((N_CHUNKS, D_MODEL, NCH), lambda i: (0, 0, 0), pipeline_mode=pl.Buffered(1)),
            pl.BlockSpec((HG_HEADS, 1, HG_DK), lambda i: (0, 0, 0)),
            pl.BlockSpec((HG_HEADS, 1, HG_DV), lambda i: (0, 0, 0)),
        ],
        out_specs=[
            pl.BlockSpec((N_PJ_CHUNKS, FR_TT, NCH), lambda i: (0, jnp.minimum(i, last), 0)),
            pl.BlockSpec((HG_HEADS, FR_TT, HG_DV), lambda i: (0, jnp.maximum(i - 1, 0), 0)),
        ],
        out_shape=[jax.ShapeDtypeStruct((N_PJ_CHUNKS, m, NCH), BF16),
                   jax.ShapeDtypeStruct((HG_HEADS, m, HG_DV), BF16)],
        scratch_shapes=[pltpu.VMEM((FR_TT, D_MODEL), BF16),
                        pltpu.VMEM((FR_TT, D_MODEL), BF16),
                        pltpu.VMEM((4 * HG_HEADS, FR_TT, HG_DK), BF16),
                        pltpu.VMEM((4 * HG_HEADS, FR_TT, HG_DK), BF16),
                        pltpu.VMEM((HG_HEADS, HG_DV, HG_DK), F32),
                        pltpu.VMEM((HG_SLOT_HEADS * HG_UNITS, 2 * HG_C, 2 * HG_C), BF16),
                        pltpu.VMEM((HG_SLOT_HEADS * HG_UNITS, 2 * HG_C, HG_DK), BF16),
                        pltpu.VMEM((HG_SLOT_HEADS * HG_UNITS, 2 * HG_C, HG_DK), BF16),
                        pltpu.VMEM((HG_SLOT_HEADS * HG_UNITS, 1, HG_DK), F32)],
        compiler_params=pltpu.CompilerParams(
            dimension_semantics=("arbitrary",), vmem_limit_bytes=VMEM_LIMIT),
        name="front",
    )(x2d, x2d, norm_w, w3, lb, hnw)


def _back_kernel(sink_ref, q_ref, kvp_ref, kv_ref, ag_pair_ref, gh_ref, mh_ref, ma_ref,
                 x_ref, wbh32_ref, wba32_ref, wo32_ref, fnw_ref, o_ref,
                 kvc_ref, ga0_ref, ga1_ref, wbh_ref, wba_ref, wo_ref, acc_ref, mg_ref,
                 *, n_tiles, tiles_per_seq):
    W, DH = WINDOW, ATT_DH
    half_heads = ATT_Q_HEADS // 2
    HALF = D_MODEL // 2
    QUARTER = D_MODEL // 4
    j = pl.program_id(0)

    @pl.when(j == 0)
    def _():
        for hh in range(2):
            wbh_ref[hh] = wbh32_ref[:, hh * HALF:(hh + 1) * HALF].astype(BF16)
            wba_ref[hh] = wba32_ref[:, hh * HALF:(hh + 1) * HALF].astype(BF16)
            wo_ref[hh] = wo32_ref[hh * HALF:(hh + 1) * HALF, :].astype(BF16)
        acc_ref[...] = jnp.zeros_like(acc_ref)

    kvc_ref[0:W, :] = kvp_ref[...]
    kvc_ref[W:, :] = kv_ref[...]
    lane = lax.broadcasted_iota(jnp.int32, (1, 2 * DH), 1)
    low = lane < DH
    ones_lo = jnp.broadcast_to(jnp.where(low, 1.0, 0.0).astype(BF16), (2 * W, 2 * DH))
    ones_hi = jnp.broadcast_to(jnp.where(low, 0.0, 1.0).astype(BF16), (2 * W, 2 * DH))
    low_rows = jnp.broadcast_to(low, (W, 2 * DH))

    first_tile = lax.rem(jnp.minimum(j, n_tiles - 1), tiles_per_seq) == 0
    qi = lax.broadcasted_iota(jnp.int32, (W, W), 0)
    kj = lax.broadcasted_iota(jnp.int32, (W, W), 1)
    upper = kj > qi

    def head_cols(ref_pair, hd, width):
        ref = ref_pair.at[hd // half_heads]
        hd = hd % half_heads
        return ref, slice(hd * DH, (hd + width) * DH)

    def half(hh, carry, ga_ref, gap_ref, attend, merge):
        gh = jnp.concatenate([gh_ref[h] for h in range(HG_HEADS)], axis=1) if merge else None
        y_parts = {}

        def branch_piece(name, lhs, w_ref, k):
            y_parts[name, k] = jnp.dot(lhs, w_ref[hh, :, k * QUARTER:(k + 1) * QUARTER],
                                       preferred_element_type=F32)

        def gate():
            yh = jnp.concatenate([y_parts["h", 0], y_parts["h", 1]], axis=1)
            ya = jnp.concatenate([y_parts["a", 0], y_parts["a", 1]], axis=1)
            mg_ref[...] = (_sigmoid(mh_ref[hh].astype(F32)) * yh
                           + _sigmoid(ma_ref[hh].astype(F32)) * ya).astype(BF16)

        def out_piece(k):
            cols = slice(k * QUARTER, (k + 1) * QUARTER)
            acc_ref[:, cols] = jnp.where(hh == 0, 0.0, acc_ref[:, cols]) + jnp.dot(
                mg_ref[...], wo_ref[hh, :, cols], preferred_element_type=F32)

        pieces = [lambda: branch_piece("h", gh, wbh_ref, 0), lambda: branch_piece("h", gh, wbh_ref, 1),
                  lambda: branch_piece("a", gap_ref[...], wba_ref, 0),
                  lambda: branch_piece("a", gap_ref[...], wba_ref, 1),
                  lambda: (gate(), out_piece(0)), lambda: out_piece(1), lambda: out_piece(2),
                  lambda: out_piece(3)]

        units = [(bi, h) for bi in range(2) for h in range(ATT_KV_HEADS)]

        def scores(bi, h):
            r = pl.multiple_of((2 * hh + bi) * W, W)
            kk = kvc_ref[pl.ds(r, 2 * W), h * DH:(h + 1) * DH]
            parts = []
            for hd in range(h * ATT_GROUP, (h + 1) * ATT_GROUP):
                ref, cols = head_cols(q_ref, hd, 1)
                parts.append(ref[pl.ds(r, W), cols])
            return _nt(jnp.concatenate(parts, axis=0), kk)

        s_next = scores(*units[0]) if attend else None
        for idx, (bi, h) in enumerate(units):
            if merge:
                pieces[idx]()
            if attend:
                s = s_next
                if idx + 1 < len(units):
                    s_next = scores(*units[idx + 1])
            if not attend:
                continue
            r = pl.multiple_of((2 * hh + bi) * W, W)
            prev_bias = jnp.where(first_tile & (2 * hh + bi == 0), -jnp.inf, 0.0)
            probs, sink_terms = [], []
            for jj, hd in enumerate(range(h * ATT_GROUP, (h + 1) * ATT_GROUP)):
                sj = s[jj * W:(jj + 1) * W]
                c = jnp.where(upper, sj[:, :W] + prev_bias, sj[:, W:])
                sink = sink_ref[hd]
                m = jnp.maximum(jnp.max(c, axis=-1, keepdims=True), sink)
                p = jnp.exp(c - m)
                sink_terms.append(jnp.exp(sink - m))
                probs.append(jnp.concatenate(
                    [jnp.where(upper, p, 0.0), jnp.where(upper, 0.0, p)], axis=1).astype(BF16))
            vpair = kvc_ref[pl.ds(r, 2 * W), KV_WIDTH + (h // 2) * 2 * DH:
                            KV_WIDTH + (h // 2 + 1) * 2 * DH]
            vswap = jnp.concatenate([vpair[:, DH:], vpair[:, :DH]], axis=1)
            in_low, in_high = (vpair, vswap) if h % 2 == 0 else (vswap, vpair)
            zero = jnp.zeros_like(vpair)
            w_lo = jnp.concatenate([jnp.where(low, in_low, zero), ones_lo], axis=1)
            w_hi = jnp.concatenate([jnp.where(low, zero, in_high), ones_hi], axis=1)
            res = (jnp.dot(jnp.concatenate(probs[0::2], axis=0), w_lo, preferred_element_type=F32)
                   + jnp.dot(jnp.concatenate(probs[1::2], axis=0), w_hi, preferred_element_type=F32))
            for pair in range(ATT_GROUP // 2):
                rp = res[pair * W:(pair + 1) * W]
                den = rp[:, 2 * DH:] + jnp.where(low_rows, sink_terms[2 * pair], sink_terms[2 * pair + 1])
                j0 = h * ATT_GROUP + 2 * pair
                ag_ref, cols = head_cols(ag_pair_ref, j0, 2)
                ag = ag_ref[pl.ds(r, W), cols].astype(F32)
                ga_ref[pl.ds(r, W), j0 * DH:(j0 + 2) * DH] = (
                    rp[:, :2 * DH] / den * (ag * _sigmoid(ag))).astype(BF16)
        return carry

    def run(ga_ref, gap_ref, attend, merge):
        lax.fori_loop(0, 2, functools.partial(half, ga_ref=ga_ref, gap_ref=gap_ref, attend=attend,
                                              merge=merge), 0)
        if merge:
            xo = x_ref[...] + acc_ref[...]
            ms = jnp.mean(xo * xo, axis=-1, keepdims=True)
            o_ref[...] = xo * lax.rsqrt(ms + EPS) * fnw_ref[...]

    for parity, bufs in ((0, (ga0_ref, ga1_ref)), (1, (ga1_ref, ga0_ref))):
        if parity == 0:
            pl.when(j == 0)(functools.partial(run, *bufs, attend=True, merge=False))
        if (n_tiles % 2) == parity:
            pl.when(j == n_tiles)(functools.partial(run, *bufs, attend=False, merge=True))
        pl.when((lax.rem(j, 2) == parity) & (j > 0) & (j < n_tiles))(
            functools.partial(run, *bufs, attend=True, merge=True))


def _back(pj, gh3, x2d, sinks, wbh, wba, wo, fnw, seq):
    m = x2d.shape[0]
    nt = m // BK_TT
    last = nt - 1
    per = BK_TT // WINDOW

    def cur(jj):
        return jnp.minimum(jj, last)

    def prv(jj):
        return jnp.maximum(jj - 1, 0)

    def chunk(c):
        return pl.BlockSpec((None, BK_TT, NCH), lambda jj, s: (c, cur(jj), 0))

    def pair(c, when):
        return pl.BlockSpec((2, BK_TT, NCH), lambda jj, s: (c // 2, when(jj), 0))

    prev_kv = pl.BlockSpec((None, WINDOW, NCH),
                           lambda jj, s: (PJ_KV, jnp.maximum(cur(jj) * per - 1, 0), 0))
    tile = pl.BlockSpec((BK_TT, D_MODEL), lambda jj, s: (prv(jj), 0))
    wspec = pl.BlockSpec((D_MODEL, D_MODEL), lambda jj, s: (0, 0), pipeline_mode=pl.Buffered(1))
    half = D_MODEL // 2
    grid_spec = pltpu.PrefetchScalarGridSpec(
        num_scalar_prefetch=1,
        grid=(nt + 1,),
        in_specs=[pair(PJ_AQ, cur), prev_kv, chunk(PJ_KV), pair(PJ_AG, cur),
                  pl.BlockSpec((HG_HEADS, BK_TT, HG_DV), lambda jj, s: (0, prv(jj), 0)),
                  pair(PJ_MH, prv), pair(PJ_MA, prv), tile, wspec, wspec, wspec,
                  pl.BlockSpec((1, D_MODEL), lambda jj, s: (0, 0))],
        out_specs=tile,
        scratch_shapes=[pltpu.VMEM((BK_TT + WINDOW, NCH), BF16),
                        pltpu.VMEM((BK_TT, D_MODEL), BF16),
                        pltpu.VMEM((BK_TT, D_MODEL), BF16),
                        pltpu.VMEM((2, D_MODEL, half), BF16),
                        pltpu.VMEM((2, D_MODEL, half), BF16),
                        pltpu.VMEM((2, half, D_MODEL), BF16),
                        pltpu.VMEM((BK_TT, D_MODEL), F32),
                        pltpu.VMEM((BK_TT, half), BF16)],
    )
    return pl.pallas_call(
        functools.partial(_back_kernel, n_tiles=nt, tiles_per_seq=seq // BK_TT),
        grid_spec=grid_spec,
        out_shape=jax.ShapeDtypeStruct((m, D_MODEL), F32),
        compiler_params=pltpu.CompilerParams(
            dimension_semantics=("arbitrary",), vmem_limit_bytes=VMEM_LIMIT),
        name="back",
    )(sinks, pj, pj, pj, pj, gh3, pj, pj, x2d, wbh, wba, wo, fnw)


def kernel(x, norm_w, w_in, hgrn_lower_bound, hgrn_norm_w, w_branch_hgrn, attn_sinks,
           w_branch_attn, w_out, final_norm_w):
    batch, seq, _ = x.shape
    depth = norm_w.shape[0]
    assert depth == 1, "the back kernel fuses the final RMSNorm into the single layer"
    assert seq % FR_TT == 0 and seq % BK_TT == 0
    lb_all = jnp.cumsum(jax.nn.softmax(hgrn_lower_bound.astype(F32), axis=0), axis=0)
    w3 = _wprep(w_in[0])
    x2d = x.reshape(batch * seq, D_MODEL)
    pj, gh3 = _front(x2d, norm_w[0].reshape(1, D_MODEL), w3,
                     lb_all[0].reshape(HG_HEADS, 1, HG_DK),
                     hgrn_norm_w[0].reshape(HG_HEADS, 1, HG_DV), seq)
    xo = _back(pj, gh3, x2d, attn_sinks[0].astype(F32), w_branch_hgrn[0], w_branch_attn[0], w_out[0],
               final_norm_w.reshape(1, D_MODEL), seq)
    return xo.reshape(batch, seq, D_MODEL)
```

```python
import functools

import jax
import jax.numpy as jnp
from jax import lax
from jax.experimental import pallas as pl
from jax.experimental.pallas import tpu as pltpu

F32 = jnp.float32
BF16 = jnp.bfloat16

D_MODEL = 1024
HG_HEADS = 8
HG_DK = 128
HG_DV = 128
ATT_Q_HEADS = 16
ATT_KV_HEADS = 4
ATT_GROUP = ATT_Q_HEADS // ATT_KV_HEADS
ATT_DH = 64
KV_WIDTH = ATT_KV_HEADS * ATT_DH
ATT_SCALE = ATT_DH ** -0.5
assert ATT_SCALE == 0.125
WINDOW = 128
EPS = 1e-6
D_IN = 8 * D_MODEL + 2 * KV_WIDTH

NCH = 512
N_CHUNKS = D_IN // NCH
N_HG_CHUNKS = 4 * D_MODEL // NCH
N_PJ_CHUNKS = N_CHUNKS - N_HG_CHUNKS
PJ_AQ, PJ_AG, PJ_MH, PJ_MA, PJ_KV = 0, 2, 4, 6, 8
SLABS_PER_CHUNK = NCH // HG_DK
SEC_HQ, SEC_HF, SEC_HI, SEC_HG = range(4)

V7X_VMEM_BYTES = 64 * 1024 * 1024
VMEM_LIMIT = V7X_VMEM_BYTES * 7 // 8

HG_C = 64
HG_SUB = 16
HG_LOG_DECAY_FLOOR = -10.0
assert HG_SUB // 2 * -HG_LOG_DECAY_FLOOR <= 80.0
FR_TT = 512
HG_GROUP = FR_TT // HG_C
HG_UNITS = HG_GROUP // 2
SUBLANES = 8
HG_SLOT_HEADS = 2
BK_TT = 512


def _nt(a, b):
    return lax.dot_general(a.astype(BF16), b.astype(BF16), (((1,), (1,)), ((), ())),
                           preferred_element_type=F32)


def _sigmoid(x):
    return 1.0 / (1.0 + jnp.exp(-x))


WP_ROWS = 256


def _wprep_kernel(w_ref, o_ref):
    kv_src = N_HG_CHUNKS + 2
    for c in range(N_CHUNKS):
        src = c if c < kv_src else (kv_src if c == N_CHUNKS - 1 else c + 1)
        blk = w_ref[:, src * NCH:(src + 1) * NCH]
        if c // 2 == N_HG_CHUNKS // 2:
            blk = blk * ATT_SCALE
        o_ref[c] = blk.astype(BF16)


def _wprep(w):
    return pl.pallas_call(
        _wprep_kernel,
        grid=(D_MODEL // WP_ROWS,),
        in_specs=[pl.BlockSpec((WP_ROWS, D_IN), lambda r: (r, 0))],
        out_specs=pl.BlockSpec((N_CHUNKS, WP_ROWS, NCH), lambda r: (0, r, 0)),
        out_shape=jax.ShapeDtypeStruct((N_CHUNKS, D_MODEL, NCH), BF16),
        compiler_params=pltpu.CompilerParams(
            dimension_semantics=("arbitrary",), vmem_limit_bytes=VMEM_LIMIT),
        name="wprep",
    )(w)


def _front_kernel(x0_ref, xnext_ref, nw_ref, w_ref, lb_ref, hnw_ref, pj_ref, gh_ref, xn0_ref, xn1_ref,
                  hb0_ref, hb1_ref, st_ref, p_sc, qb_sc, kb_sc, dec_sc, *, n_tiles, tiles_per_seq):
    C, SUB, NB = HG_C, HG_SUB, HG_C // HG_SUB
    assert NB == 4, "the three score levels below are written for four sub-blocks per chunk"
    sub_shift = SUB.bit_length() - 1
    HALF = NCH // 2
    i = pl.program_id(0)

    @pl.when(lax.rem(jnp.maximum(i - 1, 0), tiles_per_seq) == 0)
    def _():
        st_ref[...] = jnp.zeros_like(st_ref)

    def normalise(src_ref, dst_ref):
        x = src_ref[...]
        ms = jnp.mean(x * x, axis=-1, keepdims=True)
        dst_ref[...] = (x * lax.rsqrt(ms + EPS) * nw_ref[...]).astype(BF16)

    @pl.when(i == 0)
    def _():
        normalise(x0_ref, xn0_ref)

    row = lax.broadcasted_iota(jnp.int32, (C, C), 0)
    col = lax.broadcasted_iota(jnp.int32, (C, C), 1)
    rb, cb = row >> sub_shift, col >> sub_shift
    m0 = (rb == cb) & (col <= row)
    m1 = ((rb & 1) == 1) & (cb == rb - 1)
    m2 = (row >= 2 * SUB) & (col < 2 * SUB)
    rows = [slice(c * C, (c + 1) * C) for c in range(HG_GROUP)]
    urows = [slice(u * 2 * C, (u + 1) * 2 * C) for u in range(HG_UNITS)]
    sub_row = lax.broadcasted_iota(jnp.int32, (SUBLANES, HG_DK), 0)

    def cat(blocks):
        return jnp.concatenate(blocks, axis=0).astype(BF16)

    def cumsum_rows(g):
        groups = []
        for v in range(C // SUBLANES):
            xg = g[v * SUBLANES:(v + 1) * SUBLANES]
            shift = 1
            while shift < SUBLANES:
                xg = xg + jnp.where(sub_row >= shift, pltpu.roll(xg, shift, axis=0), 0.0)
                shift *= 2
            groups.append(xg)
        out, carry = [groups[0]], groups[0][SUBLANES - 1:SUBLANES]
        for xg in groups[1:]:
            out.append(xg + carry)
            carry = carry + xg[SUBLANES - 1:SUBLANES]
        return jnp.concatenate(out, axis=0)

    def slot(heads_a, abuf, heads_b, chunks, wbuf, rbuf, xn_ref):
        pieces = [(chunk, half, keep) for chunk, keep in chunks for half in range(2)]

        def project(flush=False):
            if not pieces:
                return
            chunk, half, keep = pieces.pop(0)
            acc = jnp.dot(xn_ref[...], w_ref[chunk, :, half * HALF:(half + 1) * HALF],
                          preferred_element_type=F32).astype(BF16)
            if not keep:
                pj_ref[chunk - N_HG_CHUNKS, :, half * HALF:(half + 1) * HALF] = acc
            else:
                for s in range(SLABS_PER_CHUNK // 2):
                    slab = chunk * SLABS_PER_CHUNK + half * (SLABS_PER_CHUNK // 2) + s
                    wbuf[slab] = acc[:, s * HG_DK:(s + 1) * HG_DK]
            if flush:
                project(True)

        def src(sec, head, buf=rbuf):
            return buf.at[sec * HG_HEADS + head]

        loaded = []
        for n, head in enumerate(heads_b):
            base = n * HG_UNITS
            loaded.append(dict(
                v=[src(SEC_HI, head)[urows[u], :] for u in range(HG_UNITS)],
                p=[p_sc[base + u] for u in range(HG_UNITS)],
                qb=[qb_sc[base + u] for u in range(HG_UNITS)],
                kb=[kb_sc[base + u] for u in range(HG_UNITS)],
                dec=[dec_sc[base + u] for u in range(HG_UNITS)]))

        project()

        gates = []
        for head in heads_a:
            lb = lb_ref[head]
            q, k, b = [], [], []
            for c in range(HG_GROUP):
                xq = src(SEC_HQ, head, abuf)[rows[c], :].astype(F32)
                xf = src(SEC_HF, head, abuf)[rows[c], :].astype(F32)
                f = lb + (1.0 - lb) * _sigmoid(xf)
                k.append(1.0 - f)
                q.append(xq * _sigmoid(xq))
                b.append(cumsum_rows(jnp.maximum(jnp.log(f), HG_LOG_DECAY_FLOOR)))
            gates.append((q, k, b))

        for ld in loaded:
            ld["o_intra"] = [jnp.dot(ld["p"][u], ld["v"][u], preferred_element_type=F32)
                             for u in range(HG_UNITS)]
            ld["kv"] = [lax.dot_general(ld["v"][u], ld["kb"][u], (((0,), (0,)), ((), ())),
                                        preferred_element_type=F32) for u in range(HG_UNITS)]
            project()

        for head, ld in zip(heads_b, loaded):
            st = st_ref[head]
            ld["o"] = []
            for u in range(HG_UNITS):
                ld["o"].append(ld["o_intra"][u] + _nt(ld["qb"][u], st))
                st = st * ld["dec"][u] + ld["kv"][u]
            st_ref[head] = st
            project()

        for n, (q, k, b) in enumerate(gates):
            unit_handover(n * HG_UNITS, *chunk_scores(q, k, b))
            project()

        project(flush=True)

        for head, ld in zip(heads_b, loaded):
            nw = hnw_ref[head]
            for u in range(HG_UNITS):
                xg = src(SEC_HG, head)[urows[u], :].astype(F32)
                o = ld["o"][u]
                ms_o = jnp.mean(o * o, axis=-1, keepdims=True)
                on = o * lax.rsqrt(ms_o + EPS) * nw
                gh_ref[head, urows[u], :] = (on * (xg * _sigmoid(xg))).astype(BF16)

    def chunk_scores(q, k, b):
        p_c, qb_c, kb_c, last_c = [], [], [], []
        for c in range(HG_GROUP):
            ends = [b[c][(j + 1) * SUB - 1:(j + 1) * SUB] for j in range(NB)]
            starts = [jnp.zeros_like(ends[0])] + ends[:-1]
            last = ends[-1]
            q1, k1, qd, kd, qb, kb = [], [], [], [], [], []
            for j in range(NB):
                sl = slice(j * SUB, (j + 1) * SUB)
                bj = b[c][sl]
                q1j = q[c][sl] * jnp.exp(bj - starts[j])
                k1j = k[c][sl] * jnp.exp(ends[j] - bj)
                mid = 0.5 * (starts[j] + ends[j])
                q1.append(q1j)
                k1.append(k1j)
                qd.append(q[c][sl] * jnp.exp(bj - mid))
                kd.append(k[c][sl] * jnp.exp(mid - bj))
                qb.append(q1j * jnp.exp(starts[j]))
                kb.append(k1j * jnp.exp(last - ends[j]))
            q2 = q1[:3] + [q1[3] * jnp.exp(ends[2] - ends[1])]
            k2 = [k1[0] * jnp.exp(ends[1] - ends[0])] + k1[1:]
            s0 = _nt(cat(qd), cat(kd))
            s1 = _nt(cat(q1), cat(k1))
            s2 = _nt(cat(q2), cat(k2))
            p_c.append(jnp.where(m0, s0, jnp.where(m1, s1, jnp.where(m2, s2, 0.0))).astype(BF16))
            qb_c.append(jnp.concatenate(qb, axis=0))
            kb_c.append(jnp.concatenate(kb, axis=0))
            last_c.append(last)
        return p_c, qb_c, kb_c, last_c

    def unit_handover(base, p_c, qb_c, kb_c, last_c):
        p_a, qb_a, kb_a, dec_a = [], [], [], []
        for u in range(HG_UNITS):
            c0, c1 = 2 * u, 2 * u + 1
            cross = _nt(qb_c[c1], kb_c[c0]).astype(BF16)
            p_a.append(jnp.concatenate(
                [jnp.concatenate([p_c[c0], jnp.zeros((C, C), BF16)], axis=1),
                 jnp.concatenate([cross, p_c[c1]], axis=1)], axis=0))
            qb_a.append(cat([qb_c[c0], qb_c[c1] * jnp.exp(last_c[c0])]))
            kb_a.append(cat([kb_c[c0] * jnp.exp(last_c[c1]), kb_c[c1]]))
            dec_a.append(jnp.exp(last_c[c0] + last_c[c1]))

        for u in range(HG_UNITS):
            p_sc[base + u] = p_a[u]
            qb_sc[base + u] = qb_a[u]
            kb_sc[base + u] = kb_a[u]
            dec_sc[base + u] = dec_a[u]

    def run(wbuf, rbuf, xn_ref, xn_next_ref, recur, proj):
        hp = HG_SLOT_HEADS
        n_slots = HG_HEADS // hp
        assert (hp, n_slots, N_HG_CHUNKS) == (2, 4, 8), "chunk schedule below is written for this"

        def mid(t, carry):
            heads_a = tuple(hp * t + n for n in range(hp)) if recur else ()
            heads_b = tuple(hp * (t - 1) + n for n in range(hp)) if recur else ()
            late = t == n_slots - 1
            keep0 = jnp.where(late, 4, t - 1)
            keep1 = jnp.where(late, 5, t + 1)
            out0 = N_HG_CHUNKS + 2 * (t - 1)
            chunks = [(keep0, True), (out0, False), (keep1, True), (out0 + 1, False)] if proj else []
            slot(heads_a, rbuf, heads_b, chunks, wbuf, rbuf, xn_ref)
            return carry

        lax.fori_loop(1, n_slots, mid, 0)
        if proj:
            normalise(xnext_ref, xn_next_ref)
        first_out = N_HG_CHUNKS + 2 * (n_slots - 1)
        last_chunks = ([(N_HG_CHUNKS - 2, True), (N_HG_CHUNKS - 1, True)]
                       + [(c, False) for c in range(first_out, N_CHUNKS)])
        def last_slot(_, carry):
            slot(tuple(range(hp)) if proj else (), wbuf,
                 tuple(HG_HEADS - hp + n for n in range(hp)) if recur else (),
                 last_chunks if proj else [], wbuf, rbuf, xn_ref)
            return carry

        lax.fori_loop(0, jnp.minimum(i + 1, 1), last_slot, 0)

    for parity, bufs in ((0, (hb0_ref, hb1_ref, xn0_ref, xn1_ref)), (1, (hb1_ref, hb0_ref, xn1_ref, xn0_ref))):
        on_parity = lax.rem(i, 2) == parity
        if parity == 0:
            pl.when(i == 0)(functools.partial(run, *bufs, recur=False, proj=True))
        if (n_tiles % 2) == parity:
            pl.when(i == n_tiles)(functools.partial(run, *bufs, recur=True, proj=False))
        pl.when(on_parity & (i > 0) & (i < n_tiles))(functools.partial(run, *bufs, recur=True, proj=True))


def _front(x2d, norm_w, w3, lb, hnw, seq):
    m = x2d.shape[0]
    nt = m // FR_TT
    last = nt - 1
    return pl.pallas_call(
        functools.partial(_front_kernel, n_tiles=nt, tiles_per_seq=seq // FR_TT),
        grid=(nt + 1,),
        in_specs=[
            pl.BlockSpec((FR_TT, D_MODEL), lambda i: (0, 0), pipeline_mode=pl.Buffered(1)),
            pl.BlockSpec((FR_TT, D_MODEL), lambda i: (jnp.minimum(i + 1, last), 0)),
            pl.BlockSpec((1, D_MODEL), lambda i: (0, 0)),
            pl.BlockSpec((N_CHUNKS, D_MODEL, NCH), lambda i: (0, 0, 0), pipeline_mode=pl.Buffered(1)),
            pl.BlockSpec((HG_HEADS, 1, HG_DK), lambda i: (0, 0, 0)),
            pl.BlockSpec((HG_HEADS, 1, HG_DV), lambda i: (0, 0, 0)),
        ],
        out_specs=[
            pl.BlockSpec((N_PJ_CHUNKS, FR_TT, NCH), lambda i: (0, jnp.minimum(i, last), 0)),
            pl.BlockSpec((HG_HEADS, FR_TT, HG_DV), lambda i: (0, jnp.maximum(i - 1, 0), 0)),
        ],
        out_shape=[jax.ShapeDtypeStruct((N_PJ_CHUNKS, m, NCH), BF16),
                   jax.ShapeDtypeStruct((HG_HEADS, m, HG_DV), BF16)],
        scratch_shapes=[pltpu.VMEM((FR_TT, D_MODEL), BF16),
                        pltpu.VMEM((FR_TT, D_MODEL), BF16),
                        pltpu.VMEM((4 * HG_HEADS, FR_TT, HG_DK), BF16),
                        pltpu.VMEM((4 * HG_HEADS, FR_TT, HG_DK), BF16),
                        pltpu.VMEM((HG_HEADS, HG_DV, HG_DK), F32),
                        pltpu.VMEM((HG_SLOT_HEADS * HG_UNITS, 2 * HG_C, 2 * HG_C), BF16),
                        pltpu.VMEM((HG_SLOT_HEADS * HG_UNITS, 2 * HG_C, HG_DK), BF16),
                        pltpu.VMEM((HG_SLOT_HEADS * HG_UNITS, 2 * HG_C, HG_DK), BF16),
                        pltpu.VMEM((HG_SLOT_HEADS * HG_UNITS, 1, HG_DK), F32)],
        compiler_params=pltpu.CompilerParams(
            dimension_semantics=("arbitrary",), vmem_limit_bytes=VMEM_LIMIT),
        name="front",
    )(x2d, x2d, norm_w, w3, lb, hnw)


def _back_kernel(sink_ref, q0_ref, q1_ref, kvp_ref, kv_ref, ag0_ref, ag1_ref, gh_ref, mh_ref, ma_ref,
                 x_ref, wbh32_ref, wba32_ref, wo32_ref, fnw_ref, o_ref,
                 kvc_ref, ga0_ref, ga1_ref, wbh_ref, wba_ref, wo_ref, acc_ref, mg_ref,
                 *, n_tiles, tiles_per_seq):
    W, DH = WINDOW, ATT_DH
    half_heads = ATT_Q_HEADS // 2
    HALF = D_MODEL // 2
    QUARTER = D_MODEL // 4
    j = pl.program_id(0)

    @pl.when(j == 0)
    def _():
        for hh in range(2):
            wbh_ref[hh] = wbh32_ref[:, hh * HALF:(hh + 1) * HALF].astype(BF16)
            wba_ref[hh] = wba32_ref[:, hh * HALF:(hh + 1) * HALF].astype(BF16)
            wo_ref[hh] = wo32_ref[hh * HALF:(hh + 1) * HALF, :].astype(BF16)
        acc_ref[...] = jnp.zeros_like(acc_ref)

    kvc_ref[0:W, :] = kvp_ref[...]
    kvc_ref[W:, :] = kv_ref[...]
    lane = lax.broadcasted_iota(jnp.int32, (1, 2 * DH), 1)
    low = lane < DH
    ones_lo = jnp.broadcast_to(jnp.where(low, 1.0, 0.0).astype(BF16), (2 * W, 2 * DH))
    ones_hi = jnp.broadcast_to(jnp.where(low, 0.0, 1.0).astype(BF16), (2 * W, 2 * DH))
    low_rows = jnp.broadcast_to(low, (W, 2 * DH))

    first_tile = lax.rem(jnp.minimum(j, n_tiles - 1), tiles_per_seq) == 0
    qi = lax.broadcasted_iota(jnp.int32, (W, W), 0)
    kj = lax.broadcasted_iota(jnp.int32, (W, W), 1)
    upper = kj > qi

    def head_cols(ref_pair, hd, width):
        ref = ref_pair[hd // half_heads]
        hd = hd % half_heads
        return ref, slice(hd * DH, (hd + width) * DH)

    def half(hh, carry, ga_ref, gap_ref, attend, merge):
        gh = jnp.concatenate([gh_ref[h] for h in range(HG_HEADS)], axis=1) if merge else None
        y_parts = {}

        def branch_piece(name, lhs, w_ref, k):
            y_parts[name, k] = jnp.dot(lhs, w_ref[hh, :, k * QUARTER:(k + 1) * QUARTER],
                                       preferred_element_type=F32)

        def gate():
            yh = jnp.concatenate([y_parts["h", 0], y_parts["h", 1]], axis=1)
            ya = jnp.concatenate([y_parts["a", 0], y_parts["a", 1]], axis=1)
            mg_ref[...] = (_sigmoid(mh_ref[hh].astype(F32)) * yh
                           + _sigmoid(ma_ref[hh].astype(F32)) * ya).astype(BF16)

        def out_piece(k):
            cols = slice(k * QUARTER, (k + 1) * QUARTER)
            acc_ref[:, cols] = jnp.where(hh == 0, 0.0, acc_ref[:, cols]) + jnp.dot(
                mg_ref[...], wo_ref[hh, :, cols], preferred_element_type=F32)

        pieces = [lambda: branch_piece("h", gh, wbh_ref, 0), lambda: branch_piece("h", gh, wbh_ref, 1),
                  lambda: branch_piece("a", gap_ref[...], wba_ref, 0),
                  lambda: branch_piece("a", gap_ref[...], wba_ref, 1),
                  lambda: (gate(), out_piece(0)), lambda: out_piece(1), lambda: out_piece(2),
                  lambda: out_piece(3)]

        units = [(bi, h) for bi in range(2) for h in range(ATT_KV_HEADS)]

        def scores(bi, h):
            r = pl.multiple_of((2 * hh + bi) * W, W)
            kk = kvc_ref[pl.ds(r, 2 * W), h * DH:(h + 1) * DH]
            parts = []
            for hd in range(h * ATT_GROUP, (h + 1) * ATT_GROUP):
                ref, cols = head_cols((q0_ref, q1_ref), hd, 1)
                parts.append(ref[pl.ds(r, W), cols])
            return _nt(jnp.concatenate(parts, axis=0), kk)

        s_next = scores(*units[0]) if attend else None
        for idx, (bi, h) in enumerate(units):
            if merge:
                pieces[idx]()
            if attend:
                s = s_next
                if idx + 1 < len(units):
                    s_next = scores(*units[idx + 1])
            if not attend:
                continue
            r = pl.multiple_of((2 * hh + bi) * W, W)
            prev_bias = jnp.where(first_tile & (2 * hh + bi == 0), -jnp.inf, 0.0)
            probs, sink_terms = [], []
            for jj, hd in enumerate(range(h * ATT_GROUP, (h + 1) * ATT_GROUP)):
                sj = s[jj * W:(jj + 1) * W]
                c = jnp.where(upper, sj[:, :W] + prev_bias, sj[:, W:])
                sink = sink_ref[hd]
                m = jnp.maximum(jnp.max(c, axis=-1, keepdims=True), sink)
                p = jnp.exp(c - m)
                sink_terms.append(jnp.exp(sink - m))
                probs.append(jnp.concatenate(
                    [jnp.where(upper, p, 0.0), jnp.where(upper, 0.0, p)], axis=1).astype(BF16))
            vpair = kvc_ref[pl.ds(r, 2 * W), KV_WIDTH + (h // 2) * 2 * DH:
                            KV_WIDTH + (h // 2 + 1) * 2 * DH]
            vswap = jnp.concatenate([vpair[:, DH:], vpair[:, :DH]], axis=1)
            in_low, in_high = (vpair, vswap) if h % 2 == 0 else (vswap, vpair)
            zero = jnp.zeros_like(vpair)
            w_lo = jnp.concatenate([jnp.where(low, in_low, zero), ones_lo], axis=1)
            w_hi = jnp.concatenate([jnp.where(low, zero, in_high), ones_hi], axis=1)
            res = (jnp.dot(jnp.concatenate(probs[0::2], axis=0), w_lo, preferred_element_type=F32)
                   + jnp.dot(jnp.concatenate(probs[1::2], axis=0), w_hi, preferred_element_type=F32))
            for pair in range(ATT_GROUP // 2):
                rp = res[pair * W:(pair + 1) * W]
                den = rp[:, 2 * DH:] + jnp.where(low_rows, sink_terms[2 * pair], sink_terms[2 * pair + 1])
                j0 = h * ATT_GROUP + 2 * pair
                ag_ref, cols = head_cols((ag0_ref, ag1_ref), j0, 2)
                ag = ag_ref[pl.ds(r, W), cols].astype(F32)
                ga_ref[pl.ds(r, W), j0 * DH:(j0 + 2) * DH] = (
                    rp[:, :2 * DH] / den * (ag * _sigmoid(ag))).astype(BF16)
        return carry

    def run(ga_ref, gap_ref, attend, merge):
        lax.fori_loop(0, 2, functools.partial(half, ga_ref=ga_ref, gap_ref=gap_ref, attend=attend,
                                              merge=merge), 0)
        if merge:
            xo = x_ref[...] + acc_ref[...]
            ms = jnp.mean(xo * xo, axis=-1, keepdims=True)
            o_ref[...] = xo * lax.rsqrt(ms + EPS) * fnw_ref[...]

    for parity, bufs in ((0, (ga0_ref, ga1_ref)), (1, (ga1_ref, ga0_ref))):
        if parity == 0:
            pl.when(j == 0)(functools.partial(run, *bufs, attend=True, merge=False))
        if (n_tiles % 2) == parity:
            pl.when(j == n_tiles)(functools.partial(run, *bufs, attend=False, merge=True))
        pl.when((lax.rem(j, 2) == parity) & (j > 0) & (j < n_tiles))(
            functools.partial(run, *bufs, attend=True, merge=True))


def _back(pj, gh3, x2d, sinks, wbh, wba, wo, fnw, seq):
    m = x2d.shape[0]
    nt = m // BK_TT
    last = nt - 1
    per = BK_TT // WINDOW

    def cur(jj):
        return jnp.minimum(jj, last)

    def prv(jj):
        return jnp.maximum(jj - 1, 0)

    def chunk(c):
        return pl.BlockSpec((None, BK_TT, NCH), lambda jj, s: (c, cur(jj), 0))

    def pair(c):
        return pl.BlockSpec((2, BK_TT, NCH), lambda jj, s: (c // 2, prv(jj), 0))

    prev_kv = pl.BlockSpec((None, WINDOW, NCH),
                           lambda jj, s: (PJ_KV, jnp.maximum(cur(jj) * per - 1, 0), 0))
    tile = pl.BlockSpec((BK_TT, D_MODEL), lambda jj, s: (prv(jj), 0))
    wspec = pl.BlockSpec((D_MODEL, D_MODEL), lambda jj, s: (0, 0), pipeline_mode=pl.Buffered(1))
    half = D_MODEL // 2
    grid_spec = pltpu.PrefetchScalarGridSpec(
        num_scalar_prefetch=1,
        grid=(nt + 1,),
        in_specs=[chunk(PJ_AQ), chunk(PJ_AQ + 1), prev_kv, chunk(PJ_KV), chunk(PJ_AG), chunk(PJ_AG + 1),
                  pl.BlockSpec((HG_HEADS, BK_TT, HG_DV), lambda jj, s: (0, prv(jj), 0)),
                  pair(PJ_MH), pair(PJ_MA), tile, wspec, wspec, wspec,
                  pl.BlockSpec((1, D_MODEL), lambda jj, s: (0, 0))],
        out_specs=tile,
        scratch_shapes=[pltpu.VMEM((BK_TT + WINDOW, NCH), BF16),
                        pltpu.VMEM((BK_TT, D_MODEL), BF16),
                        pltpu.VMEM((BK_TT, D_MODEL), BF16),
                        pltpu.VMEM((2, D_MODEL, half), BF16),
                        pltpu.VMEM((2, D_MODEL, half), BF16),
                        pltpu.VMEM((2, half, D_MODEL), BF16),
                        pltpu.VMEM((BK_TT, D_MODEL), F32),
                        pltpu.VMEM((BK_TT, half), BF16)],
    )
    return pl.pallas_call(
        functools.partial(_back_kernel, n_tiles=nt, tiles_per_seq=seq // BK_TT),
        grid_spec=grid_spec,
        out_shape=jax.ShapeDtypeStruct((m, D_MODEL), F32),
        compiler_params=pltpu.CompilerParams(
            dimension_semantics=("arbitrary",), vmem_limit_bytes=VMEM_LIMIT),
        name="back",
    )(sinks, pj, pj, pj, pj, pj, pj, gh3, pj, pj, x2d, wbh, wba, wo, fnw)


def kernel(x, norm_w, w_in, hgrn_lower_bound, hgrn_norm_w, w_branch_hgrn, attn_sinks,
           w_branch_attn, w_out, final_norm_w):
    batch, seq, _ = x.shape
    depth = norm_w.shape[0]
    assert depth == 1, "the back kernel fuses the final RMSNorm into the single layer"
    assert seq % FR_TT == 0 and seq % BK_TT == 0
    lb_all = jnp.cumsum(jax.nn.softmax(hgrn_lower_bound.astype(F32), axis=0), axis=0)
    w3 = _wprep(w_in[0])
    x2d = x.reshape(batch * seq, D_MODEL)
    pj, gh3 = _front(x2d, norm_w[0].reshape(1, D_MODEL), w3,
                     lb_all[0].reshape(HG_HEADS, 1, HG_DK),
                     hgrn_norm_w[0].reshape(HG_HEADS, 1, HG_DV), seq)
    xo = _back(pj, gh3, x2d, attn_sinks[0].astype(F32), w_branch_hgrn[0], w_branch_attn[0], w_out[0],
               final_norm_w.reshape(1, D_MODEL), seq)
    return xo.reshape(batch, seq, D_MODEL)
```

```python
import functools

import jax
import jax.numpy as jnp
from jax import lax
from jax.experimental import pallas as pl
from jax.experimental.pallas import tpu as pltpu

F32 = jnp.float32
BF16 = jnp.bfloat16

D_MODEL = 1024
HG_HEADS = 8
HG_DK = 128
HG_DV = 128
ATT_Q_HEADS = 16
ATT_KV_HEADS = 4
ATT_GROUP = ATT_Q_HEADS // ATT_KV_HEADS
ATT_DH = 64
KV_WIDTH = ATT_KV_HEADS * ATT_DH
ATT_SCALE = ATT_DH ** -0.5
assert ATT_SCALE == 0.125
WINDOW = 128
EPS = 1e-6
D_IN = 8 * D_MODEL + 2 * KV_WIDTH

NCH = 512
N_CHUNKS = D_IN // NCH
N_HG_CHUNKS = 4 * D_MODEL // NCH
N_PJ_CHUNKS = N_CHUNKS - N_HG_CHUNKS
PJ_AQ, PJ_AG, PJ_MH, PJ_MA, PJ_KV = 0, 2, 4, 6, 8
SLABS_PER_CHUNK = NCH // HG_DK
SEC_HQ, SEC_HF, SEC_HI, SEC_HG = range(4)

V7X_VMEM_BYTES = 64 * 1024 * 1024
VMEM_LIMIT = V7X_VMEM_BYTES * 7 // 8

HG_C = 64
HG_SUB = 16
HG_LOG_DECAY_FLOOR = -10.0
assert HG_SUB // 2 * -HG_LOG_DECAY_FLOOR <= 80.0
FR_TT = 512
HG_GROUP = FR_TT // HG_C
HG_UNITS = HG_GROUP // 2
SUBLANES = 8
HG_SLOT_HEADS = 2
BK_TT = 512


def _nt(a, b):
    return lax.dot_general(a.astype(BF16), b.astype(BF16), (((1,), (1,)), ((), ())),
                           preferred_element_type=F32)


def _sigmoid(x):
    return 1.0 / (1.0 + jnp.exp(-x))


WP_ROWS = 256


def _wprep_kernel(w_ref, o_ref):
    kv_src = N_HG_CHUNKS + 2
    for c in range(N_CHUNKS):
        src = c if c < kv_src else (kv_src if c == N_CHUNKS - 1 else c + 1)
        blk = w_ref[:, src * NCH:(src + 1) * NCH]
        if c // 2 == N_HG_CHUNKS // 2:
            blk = blk * ATT_SCALE
        o_ref[c] = blk.astype(BF16)


def _wprep(w):
    return pl.pallas_call(
        _wprep_kernel,
        grid=(D_MODEL // WP_ROWS,),
        in_specs=[pl.BlockSpec((WP_ROWS, D_IN), lambda r: (r, 0))],
        out_specs=pl.BlockSpec((N_CHUNKS, WP_ROWS, NCH), lambda r: (0, r, 0)),
        out_shape=jax.ShapeDtypeStruct((N_CHUNKS, D_MODEL, NCH), BF16),
        compiler_params=pltpu.CompilerParams(
            dimension_semantics=("arbitrary",), vmem_limit_bytes=VMEM_LIMIT),
        name="wprep",
    )(w)


def _front_kernel(x0_ref, xnext_ref, nw_ref, w_ref, lb_ref, hnw_ref, pj_ref, gh_ref, xn0_ref, xn1_ref,
                  hb0_ref, hb1_ref, st_ref, p_sc, qb_sc, kb_sc, dec_sc, *, n_tiles, tiles_per_seq):
    C, SUB, NB = HG_C, HG_SUB, HG_C // HG_SUB
    assert NB == 4, "the three score levels below are written for four sub-blocks per chunk"
    sub_shift = SUB.bit_length() - 1
    HALF = NCH // 2
    i = pl.program_id(0)

    @pl.when(lax.rem(jnp.maximum(i - 1, 0), tiles_per_seq) == 0)
    def _():
        st_ref[...] = jnp.zeros_like(st_ref)

    def normalise(src_ref, dst_ref):
        x = src_ref[...]
        ms = jnp.mean(x * x, axis=-1, keepdims=True)
        dst_ref[...] = (x * lax.rsqrt(ms + EPS) * nw_ref[...]).astype(BF16)

    @pl.when(i == 0)
    def _():
        normalise(x0_ref, xn0_ref)

    row = lax.broadcasted_iota(jnp.int32, (C, C), 0)
    col = lax.broadcasted_iota(jnp.int32, (C, C), 1)
    rb, cb = row >> sub_shift, col >> sub_shift
    m0 = (rb == cb) & (col <= row)
    m1 = ((rb & 1) == 1) & (cb == rb - 1)
    m2 = (row >= 2 * SUB) & (col < 2 * SUB)
    rows = [slice(c * C, (c + 1) * C) for c in range(HG_GROUP)]
    urows = [slice(u * 2 * C, (u + 1) * 2 * C) for u in range(HG_UNITS)]
    sub_row = lax.broadcasted_iota(jnp.int32, (SUBLANES, HG_DK), 0)

    def cat(blocks):
        return jnp.concatenate(blocks, axis=0).astype(BF16)

    def cumsum_rows(g):
        groups = []
        for v in range(C // SUBLANES):
            xg = g[v * SUBLANES:(v + 1) * SUBLANES]
            shift = 1
            while shift < SUBLANES:
                xg = xg + jnp.where(sub_row >= shift, pltpu.roll(xg, shift, axis=0), 0.0)
                shift *= 2
            groups.append(xg)
        out, carry = [groups[0]], groups[0][SUBLANES - 1:SUBLANES]
        for xg in groups[1:]:
            out.append(xg + carry)
            carry = carry + xg[SUBLANES - 1:SUBLANES]
        return jnp.concatenate(out, axis=0)

    def slot(heads_a, abuf, heads_b, chunks, wbuf, rbuf, xn_ref):
        pieces = [(chunk, half, keep) for chunk, keep in chunks for half in range(2)]

        def project(flush=False):
            if not pieces:
                return
            chunk, half, keep = pieces.pop(0)
            acc = jnp.dot(xn_ref[...], w_ref[chunk, :, half * HALF:(half + 1) * HALF],
                          preferred_element_type=F32).astype(BF16)
            if not keep:
                pj_ref[chunk - N_HG_CHUNKS, :, half * HALF:(half + 1) * HALF] = acc
            else:
                for s in range(SLABS_PER_CHUNK // 2):
                    slab = chunk * SLABS_PER_CHUNK + half * (SLABS_PER_CHUNK // 2) + s
                    wbuf[slab] = acc[:, s * HG_DK:(s + 1) * HG_DK]
            if flush:
                project(True)

        def src(sec, head, buf=rbuf):
            return buf.at[sec * HG_HEADS + head]

        loaded = []
        for n, head in enumerate(heads_b):
            base = n * HG_UNITS
            loaded.append(dict(
                v=[src(SEC_HI, head)[urows[u], :] for u in range(HG_UNITS)],
                p=[p_sc[base + u] for u in range(HG_UNITS)],
                qb=[qb_sc[base + u] for u in range(HG_UNITS)],
                kb=[kb_sc[base + u] for u in range(HG_UNITS)],
                dec=[dec_sc[base + u] for u in range(HG_UNITS)]))

        project()

        gates = []
        for head in heads_a:
            lb = lb_ref[head]
            q, k, b = [], [], []
            for c in range(HG_GROUP):
                xq = src(SEC_HQ, head, abuf)[rows[c], :].astype(F32)
                xf = src(SEC_HF, head, abuf)[rows[c], :].astype(F32)
                f = lb + (1.0 - lb) * _sigmoid(xf)
                k.append(1.0 - f)
                q.append(xq * _sigmoid(xq))
                b.append(cumsum_rows(jnp.maximum(jnp.log(f), HG_LOG_DECAY_FLOOR)))
            gates.append((q, k, b))

        for ld in loaded:
            ld["o_intra"] = [jnp.dot(ld["p"][u], ld["v"][u], preferred_element_type=F32)
                             for u in range(HG_UNITS)]
            ld["kv"] = [lax.dot_general(ld["v"][u], ld["kb"][u], (((0,), (0,)), ((), ())),
                                        preferred_element_type=F32) for u in range(HG_UNITS)]
            project()

        for head, ld in zip(heads_b, loaded):
            st = st_ref[head]
            ld["o"] = []
            for u in range(HG_UNITS):
                ld["o"].append(ld["o_intra"][u] + _nt(ld["qb"][u], st))
                st = st * ld["dec"][u] + ld["kv"][u]
            st_ref[head] = st
            project()

        for n, (q, k, b) in enumerate(gates):
            unit_handover(n * HG_UNITS, *chunk_scores(q, k, b))
            project()

        project(flush=True)

        for head, ld in zip(heads_b, loaded):
            nw = hnw_ref[head]
            for u in range(HG_UNITS):
                xg = src(SEC_HG, head)[urows[u], :].astype(F32)
                o = ld["o"][u]
                ms_o = jnp.mean(o * o, axis=-1, keepdims=True)
                on = o * lax.rsqrt(ms_o + EPS) * nw
                gh_ref[head, urows[u], :] = (on * (xg * _sigmoid(xg))).astype(BF16)

    def chunk_scores(q, k, b):
        p_c, qb_c, kb_c, last_c = [], [], [], []
        for c in range(HG_GROUP):
            ends = [b[c][(j + 1) * SUB - 1:(j + 1) * SUB] for j in range(NB)]
            starts = [jnp.zeros_like(ends[0])] + ends[:-1]
            last = ends[-1]
            q1, k1, qd, kd, qb, kb = [], [], [], [], [], []
            for j in range(NB):
                sl = slice(j * SUB, (j + 1) * SUB)
                bj = b[c][sl]
                q1j = q[c][sl] * jnp.exp(bj - starts[j])
                k1j = k[c][sl] * jnp.exp(ends[j] - bj)
                mid = 0.5 * (starts[j] + ends[j])
                q1.append(q1j)
                k1.append(k1j)
                qd.append(q[c][sl] * jnp.exp(bj - mid))
                kd.append(k[c][sl] * jnp.exp(mid - bj))
                qb.append(q1j * jnp.exp(starts[j]))
                kb.append(k1j * jnp.exp(last - ends[j]))
            q2 = q1[:3] + [q1[3] * jnp.exp(ends[2] - ends[1])]
            k2 = [k1[0] * jnp.exp(ends[1] - ends[0])] + k1[1:]
            s0 = _nt(cat(qd), cat(kd))
            s1 = _nt(cat(q1), cat(k1))
            s2 = _nt(cat(q2), cat(k2))
            p_c.append(jnp.where(m0, s0, jnp.where(m1, s1, jnp.where(m2, s2, 0.0))).astype(BF16))
            qb_c.append(jnp.concatenate(qb, axis=0))
            kb_c.append(jnp.concatenate(kb, axis=0))
            last_c.append(last)
        return p_c, qb_c, kb_c, last_c

    def unit_handover(base, p_c, qb_c, kb_c, last_c):
        p_a, qb_a, kb_a, dec_a = [], [], [], []
        for u in range(HG_UNITS):
            c0, c1 = 2 * u, 2 * u + 1
            cross = _nt(qb_c[c1], kb_c[c0]).astype(BF16)
            p_a.append(jnp.concatenate(
                [jnp.concatenate([p_c[c0], jnp.zeros((C, C), BF16)], axis=1),
                 jnp.concatenate([cross, p_c[c1]], axis=1)], axis=0))
            qb_a.append(cat([qb_c[c0], qb_c[c1] * jnp.exp(last_c[c0])]))
            kb_a.append(cat([kb_c[c0] * jnp.exp(last_c[c1]), kb_c[c1]]))
            dec_a.append(jnp.exp(last_c[c0] + last_c[c1]))

        for u in range(HG_UNITS):
            p_sc[base + u] = p_a[u]
            qb_sc[base + u] = qb_a[u]
            kb_sc[base + u] = kb_a[u]
            dec_sc[base + u] = dec_a[u]

    def run(wbuf, rbuf, xn_ref, xn_next_ref, recur, proj):
        hp = HG_SLOT_HEADS
        n_slots = HG_HEADS // hp
        assert (hp, n_slots, N_HG_CHUNKS) == (2, 4, 8), "chunk schedule below is written for this"

        def mid(t, carry):
            heads_a = tuple(hp * t + n for n in range(hp)) if recur else ()
            heads_b = tuple(hp * (t - 1) + n for n in range(hp)) if recur else ()
            late = t == n_slots - 1
            keep0 = jnp.where(late, 4, t - 1)
            keep1 = jnp.where(late, 5, t + 1)
            out0 = N_HG_CHUNKS + 2 * (t - 1)
            chunks = [(keep0, True), (out0, False), (keep1, True), (out0 + 1, False)] if proj else []
            slot(heads_a, rbuf, heads_b, chunks, wbuf, rbuf, xn_ref)
            return carry

        lax.fori_loop(1, n_slots, mid, 0)
        if proj:
            normalise(xnext_ref, xn_next_ref)
        first_out = N_HG_CHUNKS + 2 * (n_slots - 1)
        last_chunks = ([(N_HG_CHUNKS - 2, True), (N_HG_CHUNKS - 1, True)]
                       + [(c, False) for c in range(first_out, N_CHUNKS)])
        slot(tuple(range(hp)) if proj else (), wbuf,
             tuple(HG_HEADS - hp + n for n in range(hp)) if recur else (),
             last_chunks if proj else [], wbuf, rbuf, xn_ref)

    for parity, bufs in ((0, (hb0_ref, hb1_ref, xn0_ref, xn1_ref)), (1, (hb1_ref, hb0_ref, xn1_ref, xn0_ref))):
        on_parity = lax.rem(i, 2) == parity
        if parity == 0:
            pl.when(i == 0)(functools.partial(run, *bufs, recur=False, proj=True))
        if (n_tiles % 2) == parity:
            pl.when(i == n_tiles)(functools.partial(run, *bufs, recur=True, proj=False))
        pl.when(on_parity & (i > 0) & (i < n_tiles))(functools.partial(run, *bufs, recur=True, proj=True))


def _front(x2d, norm_w, w3, lb, hnw, seq):
    m = x2d.shape[0]
    nt = m // FR_TT
    last = nt - 1
    return pl.pallas_call(
        functools.partial(_front_kernel, n_tiles=nt, tiles_per_seq=seq // FR_TT),
        grid=(nt + 1,),
        in_specs=[
            pl.BlockSpec((FR_TT, D_MODEL), lambda i: (0, 0), pipeline_mode=pl.Buffered(1)),
            pl.BlockSpec((FR_TT, D_MODEL), lambda i: (jnp.minimum(i + 1, last), 0)),
            pl.BlockSpec((1, D_MODEL), lambda i: (0, 0)),
            pl.BlockSpec((N_CHUNKS, D_MODEL, NCH), lambda i: (0, 0, 0), pipeline_mode=pl.Buffered(1)),
            pl.BlockSpec((HG_HEADS, 1, HG_DK), lambda i: (0, 0, 0)),
            pl.BlockSpec((HG_HEADS, 1, HG_DV), lambda i: (0, 0, 0)),
        ],
        out_specs=[
            pl.BlockSpec((N_PJ_CHUNKS, FR_TT, NCH), lambda i: (0, jnp.minimum(i, last), 0)),
            pl.BlockSpec((HG_HEADS, FR_TT, HG_DV), lambda i: (0, jnp.maximum(i - 1, 0), 0)),
        ],
        out_shape=[jax.ShapeDtypeStruct((N_PJ_CHUNKS, m, NCH), BF16),
                   jax.ShapeDtypeStruct((HG_HEADS, m, HG_DV), BF16)],
        scratch_shapes=[pltpu.VMEM((FR_TT, D_MODEL), BF16),
                        pltpu.VMEM((FR_TT, D_MODEL), BF16),
                        pltpu.VMEM((4 * HG_HEADS, FR_TT, HG_DK), BF16),
                        pltpu.VMEM((4 * HG_HEADS, FR_TT, HG_DK), BF16),
                        pltpu.VMEM((HG_HEADS, HG_DV, HG_DK), F32),
                        pltpu.VMEM((HG_SLOT_HEADS * HG_UNITS, 2 * HG_C, 2 * HG_C), BF16),
                        pltpu.VMEM((HG_SLOT_HEADS * HG_UNITS, 2 * HG_C, HG_DK), BF16),
                        pltpu.VMEM((HG_SLOT_HEADS * HG_UNITS, 2 * HG_C, HG_DK), BF16),
                        pltpu.VMEM((HG_SLOT_HEADS * HG_UNITS, 1, HG_DK), F32)],
        compiler_params=pltpu.CompilerParams(
            dimension_semantics=("arbitrary",), vmem_limit_bytes=VMEM_LIMIT),
        name="front",
    )(x2d, x2d, norm_w, w3, lb, hnw)


def _back_kernel(sink_ref, q0_ref, q1_ref, kvp_ref, kv_ref, ag0_ref, ag1_ref, gh_ref, mh_ref, ma_ref,
                 x_ref, wbh32_ref, wba32_ref, wo32_ref, fnw_ref, o_ref,
                 kvc_ref, ga0_ref, ga1_ref, wbh_ref, wba_ref, wo_ref, acc_ref, mg_ref,
                 *, n_tiles, tiles_per_seq):
    W, DH = WINDOW, ATT_DH
    half_heads = ATT_Q_HEADS // 2
    HALF = D_MODEL // 2
    QUARTER = D_MODEL // 4
    j = pl.program_id(0)

    @pl.when(j == 0)
    def _():
        for hh in range(2):
            wbh_ref[hh] = wbh32_ref[:, hh * HALF:(hh + 1) * HALF].astype(BF16)
            wba_ref[hh] = wba32_ref[:, hh * HALF:(hh + 1) * HALF].astype(BF16)
            wo_ref[hh] = wo32_ref[hh * HALF:(hh + 1) * HALF, :].astype(BF16)
        acc_ref[...] = jnp.zeros_like(acc_ref)

    kvc_ref[0:W, :] = kvp_ref[...]
    kvc_ref[W:, :] = kv_ref[...]
    lane = lax.broadcasted_iota(jnp.int32, (1, 2 * DH), 1)
    low = lane < DH
    ones_lo = jnp.broadcast_to(jnp.where(low, 1.0, 0.0).astype(BF16), (2 * W, 2 * DH))
    ones_hi = jnp.broadcast_to(jnp.where(low, 0.0, 1.0).astype(BF16), (2 * W, 2 * DH))
    low_rows = jnp.broadcast_to(low, (W, 2 * DH))

    first_tile = lax.rem(jnp.minimum(j, n_tiles - 1), tiles_per_seq) == 0
    qi = lax.broadcasted_iota(jnp.int32, (W, W), 0)
    kj = lax.broadcasted_iota(jnp.int32, (W, W), 1)
    upper = kj > qi

    def head_cols(ref_pair, hd, width):
        ref = ref_pair[hd // half_heads]
        hd = hd % half_heads
        return ref, slice(hd * DH, (hd + width) * DH)

    def half(hh, carry, ga_ref, gap_ref, attend, merge):
        gh = jnp.concatenate([gh_ref[h] for h in range(HG_HEADS)], axis=1) if merge else None
        y_parts = {}

        def branch_piece(name, lhs, w_ref, k):
            y_parts[name, k] = jnp.dot(lhs, w_ref[hh, :, k * QUARTER:(k + 1) * QUARTER],
                                       preferred_element_type=F32)

        def gate():
            yh = jnp.concatenate([y_parts["h", 0], y_parts["h", 1]], axis=1)
            ya = jnp.concatenate([y_parts["a", 0], y_parts["a", 1]], axis=1)
            mg_ref[...] = (_sigmoid(mh_ref[hh].astype(F32)) * yh
                           + _sigmoid(ma_ref[hh].astype(F32)) * ya).astype(BF16)

        def out_piece(k):
            cols = slice(k * QUARTER, (k + 1) * QUARTER)
            acc_ref[:, cols] = jnp.where(hh == 0, x_ref[:, cols], acc_ref[:, cols]) + jnp.dot(
                mg_ref[...], wo_ref[hh, :, cols], preferred_element_type=F32)

        pieces = [lambda: branch_piece("h", gh, wbh_ref, 0), lambda: branch_piece("h", gh, wbh_ref, 1),
                  lambda: branch_piece("a", gap_ref[...], wba_ref, 0),
                  lambda: branch_piece("a", gap_ref[...], wba_ref, 1),
                  lambda: (gate(), out_piece(0)), lambda: out_piece(1), lambda: out_piece(2),
                  lambda: out_piece(3)]

        units = [(bi, h) for bi in range(2) for h in range(ATT_KV_HEADS)]

        def scores(bi, h):
            r = pl.multiple_of((2 * hh + bi) * W, W)
            kk = kvc_ref[pl.ds(r, 2 * W), h * DH:(h + 1) * DH]
            parts = []
            for hd in range(h * ATT_GROUP, (h + 1) * ATT_GROUP):
                ref, cols = head_cols((q0_ref, q1_ref), hd, 1)
                parts.append(ref[pl.ds(r, W), cols])
            return _nt(jnp.concatenate(parts, axis=0), kk)

        s_next = scores(*units[0]) if attend else None
        for idx, (bi, h) in enumerate(units):
            if merge:
                pieces[idx]()
            if attend:
                s = s_next
                if idx + 1 < len(units):
                    s_next = scores(*units[idx + 1])
            if not attend:
                continue
            r = pl.multiple_of((2 * hh + bi) * W, W)
            prev_bias = jnp.where(first_tile & (2 * hh + bi == 0), -jnp.inf, 0.0)
            probs, sink_terms = [], []
            for jj, hd in enumerate(range(h * ATT_GROUP, (h + 1) * ATT_GROUP)):
                sj = s[jj * W:(jj + 1) * W]
                c = jnp.where(upper, sj[:, :W] + prev_bias, sj[:, W:])
                sink = sink_ref[hd]
                m = jnp.maximum(jnp.max(c, axis=-1, keepdims=True), sink)
                p = jnp.exp(c - m)
                sink_terms.append(jnp.exp(sink - m))
                probs.append(jnp.concatenate(
                    [jnp.where(upper, p, 0.0), jnp.where(upper, 0.0, p)], axis=1).astype(BF16))
            vpair = kvc_ref[pl.ds(r, 2 * W), KV_WIDTH + (h // 2) * 2 * DH:
                            KV_WIDTH + (h // 2 + 1) * 2 * DH]
            vswap = jnp.concatenate([vpair[:, DH:], vpair[:, :DH]], axis=1)
            in_low, in_high = (vpair, vswap) if h % 2 == 0 else (vswap, vpair)
            zero = jnp.zeros_like(vpair)
            w_lo = jnp.concatenate([jnp.where(low, in_low, zero), ones_lo], axis=1)
            w_hi = jnp.concatenate([jnp.where(low, zero, in_high), ones_hi], axis=1)
            res = (jnp.dot(jnp.concatenate(probs[0::2], axis=0), w_lo, preferred_element_type=F32)
                   + jnp.dot(jnp.concatenate(probs[1::2], axis=0), w_hi, preferred_element_type=F32))
            for pair in range(ATT_GROUP // 2):
                rp = res[pair * W:(pair + 1) * W]
                den = rp[:, 2 * DH:] + jnp.where(low_rows, sink_terms[2 * pair], sink_terms[2 * pair + 1])
                j0 = h * ATT_GROUP + 2 * pair
                ag_ref, cols = head_cols((ag0_ref, ag1_ref), j0, 2)
                ag = ag_ref[pl.ds(r, W), cols].astype(F32)
                ga_ref[pl.ds(r, W), j0 * DH:(j0 + 2) * DH] = (
                    rp[:, :2 * DH] / den * (ag * _sigmoid(ag))).astype(BF16)
        return carry

    def run(ga_ref, gap_ref, attend, merge):
        lax.fori_loop(0, 2, functools.partial(half, ga_ref=ga_ref, gap_ref=gap_ref, attend=attend,
                                              merge=merge), 0)
        if merge:
            xo = acc_ref[...]
            ms = jnp.mean(xo * xo, axis=-1, keepdims=True)
            o_ref[...] = xo * lax.rsqrt(ms + EPS) * fnw_ref[...]

    for parity, bufs in ((0, (ga0_ref, ga1_ref)), (1, (ga1_ref, ga0_ref))):
        if parity == 0:
            pl.when(j == 0)(functools.partial(run, *bufs, attend=True, merge=False))
        if (n_tiles % 2) == parity:
            pl.when(j == n_tiles)(functools.partial(run, *bufs, attend=False, merge=True))
        pl.when((lax.rem(j, 2) == parity) & (j > 0) & (j < n_tiles))(
            functools.partial(run, *bufs, attend=True, merge=True))


def _back(pj, gh3, x2d, sinks, wbh, wba, wo, fnw, seq):
    m = x2d.shape[0]
    nt = m // BK_TT
    last = nt - 1
    per = BK_TT // WINDOW

    def cur(jj):
        return jnp.minimum(jj, last)

    def prv(jj):
        return jnp.maximum(jj - 1, 0)

    def chunk(c):
        return pl.BlockSpec((None, BK_TT, NCH), lambda jj, s: (c, cur(jj), 0))

    def pair(c):
        return pl.BlockSpec((2, BK_TT, NCH), lambda jj, s: (c // 2, prv(jj), 0))

    prev_kv = pl.BlockSpec((None, WINDOW, NCH),
                           lambda jj, s: (PJ_KV, jnp.maximum(cur(jj) * per - 1, 0), 0))
    tile = pl.BlockSpec((BK_TT, D_MODEL), lambda jj, s: (prv(jj), 0))
    wspec = pl.BlockSpec((D_MODEL, D_MODEL), lambda jj, s: (0, 0), pipeline_mode=pl.Buffered(1))
    half = D_MODEL // 2
    grid_spec = pltpu.PrefetchScalarGridSpec(
        num_scalar_prefetch=1,
        grid=(nt + 1,),
        in_specs=[chunk(PJ_AQ), chunk(PJ_AQ + 1), prev_kv, chunk(PJ_KV), chunk(PJ_AG), chunk(PJ_AG + 1),
                  pl.BlockSpec((HG_HEADS, BK_TT, HG_DV), lambda jj, s: (0, prv(jj), 0)),
                  pair(PJ_MH), pair(PJ_MA), tile, wspec, wspec, wspec,
                  pl.BlockSpec((1, D_MODEL), lambda jj, s: (0, 0))],
        out_specs=tile,
        scratch_shapes=[pltpu.VMEM((BK_TT + WINDOW, NCH), BF16),
                        pltpu.VMEM((BK_TT, D_MODEL), BF16),
                        pltpu.VMEM((BK_TT, D_MODEL), BF16),
                        pltpu.VMEM((2, D_MODEL, half), BF16),
                        pltpu.VMEM((2, D_MODEL, half), BF16),
                        pltpu.VMEM((2, half, D_MODEL), BF16),
                        pltpu.VMEM((BK_TT, D_MODEL), F32),
                        pltpu.VMEM((BK_TT, half), BF16)],
    )
    return pl.pallas_call(
        functools.partial(_back_kernel, n_tiles=nt, tiles_per_seq=seq // BK_TT),
        grid_spec=grid_spec,
        out_shape=jax.ShapeDtypeStruct((m, D_MODEL), F32),
        compiler_params=pltpu.CompilerParams(
            dimension_semantics=("arbitrary",), vmem_limit_bytes=VMEM_LIMIT),
        name="back",
    )(sinks, pj, pj, pj, pj, pj, pj, gh3, pj, pj, x2d, wbh, wba, wo, fnw)


def kernel(x, norm_w, w_in, hgrn_lower_bound, hgrn_norm_w, w_branch_hgrn, attn_sinks,
           w_branch_attn, w_out, final_norm_w):
    batch, seq, _ = x.shape
    depth = norm_w.shape[0]
    assert depth == 1, "the back kernel fuses the final RMSNorm into the single layer"
    assert seq % FR_TT == 0 and seq % BK_TT == 0
    lb_all = jnp.cumsum(jax.nn.softmax(hgrn_lower_bound.astype(F32), axis=0), axis=0)
    w3 = _wprep(w_in[0])
    x2d = x.reshape(batch * seq, D_MODEL)
    pj, gh3 = _front(x2d, norm_w[0].reshape(1, D_MODEL), w3,
                     lb_all[0].reshape(HG_HEADS, 1, HG_DK),
                     hgrn_norm_w[0].reshape(HG_HEADS, 1, HG_DV), seq)
    xo = _back(pj, gh3, x2d, attn_sinks[0].astype(F32), w_branch_hgrn[0], w_branch_attn[0], w_out[0],
               final_norm_w.reshape(1, D_MODEL), seq)
    return xo.reshape(batch, seq, D_MODEL)
```
